```python
import math
import jax
import jax.numpy as jnp
from jax import lax
import numpy as np

D_MODEL = 2048
BATCH = 4
SEQ = 2048
DEPTH = 2

HALF = D_MODEL // 2
MLA_NOPE = 128
MLA_ROPE = 64
MLA_V = 128
MLA_HEADS = HALF // MLA_V
MLA_Q_RANK = D_MODEL // 4
MLA_KV_RANK = D_MODEL // 4
LRU_WIDTH = HALF
LRU_BLOCKS = 8
LRU_CONV = 4
LRU_C = 8.0
RET_DK = 256
RET_DV = 256
RET_HEADS = HALF // RET_DV
RET_CHUNK = 128
SWA_HD = 128
SWA_HEADS = HALF // SWA_HD
SWA_KV_HEADS = 2
WINDOW = 128
REL_BUCKETS = 32
REL_MAX_DIST = 128
N_EXPERTS = 64
TOP_K = 8
N_GROUPS = 8
TOP_GROUPS = 4
EXPERT_FF = D_MODEL // 4
SHARED_FF = D_MODEL // 4
ROUTE_SCALE = 2.5
ROPE_THETA = 10000.0
EPS = 1e-6
Q_BLOCK = 128
AB_SIZES = (MLA_Q_RANK, MLA_KV_RANK, MLA_ROPE, LRU_WIDTH, LRU_WIDTH)
CD_SIZES = (RET_HEADS * RET_DK, RET_HEADS * RET_DK, RET_HEADS * RET_DV, RET_HEADS * RET_DV,
            SWA_HEADS * SWA_HD, SWA_KV_HEADS * SWA_HD, SWA_KV_HEADS * SWA_HD)

kernel_name = 'hybrid_mla_rglru_retention_swa_moe_encoder'


def rms_norm(x, g):
    xf = x.astype(jnp.float32)
    y = xf * lax.rsqrt(jnp.mean(xf * xf, axis=-1, keepdims=True) + EPS)
    return y.astype(x.dtype) * g


def modulate(x, g, shift, scale):
    return rms_norm(x, g) * (1 + scale) + shift


def split_cols(t, sizes):
    return jnp.split(t, np.cumsum(sizes)[:-1].tolist(), axis=-1)


def rope_tables(seq, dim):
    inv = ROPE_THETA ** (-jnp.arange(0, dim, 2, dtype=jnp.float32) / dim)
    ang = jnp.arange(seq, dtype=jnp.float32)[:, None] * inv[None, :]
    return jnp.cos(ang), jnp.sin(ang)


def apply_rope(x, cos, sin):
    x1, x2 = jnp.split(x, 2, axis=-1)
    cs = cos[:, None, :].astype(x.dtype)
    sn = sin[:, None, :].astype(x.dtype)
    return jnp.concatenate([x1 * cs - x2 * sn, x2 * cs + x1 * sn], axis=-1)


def t5_bucket(rel):
    half = REL_BUCKETS // 2
    max_exact = half // 2
    ret = (rel > 0).astype(jnp.int32) * half
    n = jnp.abs(rel)
    nf = jnp.maximum(n, 1).astype(jnp.float32)
    large = max_exact + (jnp.log(nf / max_exact) / math.log(REL_MAX_DIST / max_exact)
                         * (half - max_exact)).astype(jnp.int32)
    large = jnp.minimum(large, half - 1)
    return ret + jnp.where(n < max_exact, n, large)


def mla_mixer(q_lat, kv_lat, k_rope, q_norm, kv_norm, w_uq, w_ukv, cos, sin):
    B, S, _ = q_lat.shape
    H = MLA_HEADS
    q = (rms_norm(q_lat, q_norm) @ w_uq).reshape(B, S, H, MLA_NOPE + MLA_ROPE)
    q_nope = q[..., :MLA_NOPE]
    q_rope = apply_rope(q[..., MLA_NOPE:], cos, sin)
    kv = (rms_norm(kv_lat, kv_norm) @ w_ukv).reshape(B, S, H, MLA_NOPE + MLA_V)
    k_nope, v = kv[..., :MLA_NOPE], kv[..., MLA_NOPE:]
    k_rope = apply_rope(k_rope[:, :, None, :], cos, sin)[:, :, 0]
    scale = (MLA_NOPE + MLA_ROPE) ** -0.5
    nb = S // Q_BLOCK
    qn = q_nope.reshape(B, nb, Q_BLOCK, H, MLA_NOPE).transpose(1, 0, 2, 3, 4)
    qr = q_rope.reshape(B, nb, Q_BLOCK, H, MLA_ROPE).transpose(1, 0, 2, 3, 4)

    def block(args):
        qn_b, qr_b = args
        s = (jnp.einsum('bqhd,bkhd->bhqk', qn_b, k_nope, preferred_element_type=jnp.float32)
             + jnp.einsum('bqhr,bkr->bhqk', qr_b, k_rope, preferred_element_type=jnp.float32))
        p = jax.nn.softmax(s * scale, axis=-1).astype(v.dtype)
        return jnp.einsum('bhqk,bkhd->bqhd', p, v)

    o = lax.map(block, (qn, qr))
    return o.transpose(1, 0, 2, 3, 4).reshape(B, S, H * MLA_V)


def conv_centred(x, w, b):
    C = x.shape[-1]
    left = LRU_CONV // 2
    y = lax.conv_general_dilated(x, w[:, None, :].astype(x.dtype), window_strides=(1,),
                                 padding=[(left, LRU_CONV - 1 - left)],
                                 dimension_numbers=('NWC', 'WIO', 'NWC'),
                                 feature_group_count=C)
    return y + b


def rglru_scan(x, w_a, b_a, w_x, b_x, lam, reverse):
    B, S, C = x.shape
    xb = x.reshape(B, S, LRU_BLOCKS, C // LRU_BLOCKS)
    r = jax.nn.sigmoid(jnp.einsum('bsgi,gij->bsgj', xb, w_a).reshape(B, S, C) + b_a)
    i = jax.nn.sigmoid(jnp.einsum('bsgi,gij->bsgj', xb, w_x).reshape(B, S, C) + b_x)
    log_a = -LRU_C * r.astype(jnp.float32) * jax.nn.softplus(-lam.astype(jnp.float32))
    a = jnp.exp(log_a)
    u = jnp.sqrt(-jnp.expm1(2.0 * log_a)) * (i * x).astype(jnp.float32)

    def combine(left, right):
        a1, b1 = left
        a2, b2 = right
        return a1 * a2, a2 * b1 + b2

    _, h = lax.associative_scan(combine, (a, u), reverse=reverse, axis=1)
    return h.astype(x.dtype)


def retention_chunkwise(q, k, v, log_gamma, include_diag):
    B, S, H, dk = q.shape
    dv = v.shape[-1]
    C = RET_CHUNK
    nc = S // C
    pos = jnp.arange(C, dtype=jnp.float32)
    diff = pos[:, None] - pos[None, :]
    mask = (diff >= 0) if include_diag else (diff > 0)
    decay = jnp.where(mask[None], jnp.exp(log_gamma[:, None, None] * jnp.maximum(diff, 0.0)[None]), 0.0)
    xi = jnp.exp(log_gamma[None, :] * (pos[:, None] + 1.0))
    zeta = jnp.exp(log_gamma[None, :] * (C - 1.0 - pos[:, None]))
    gamma_c = jnp.exp(log_gamma * C)

    def to_chunks(t):
        return t.reshape(B, nc, C, H, t.shape[-1]).transpose(1, 0, 2, 3, 4)

    def step(R, inp):
        qi, ki, vi = inp
        qf, kf, vf = qi.astype(jnp.float32), ki.astype(jnp.float32), vi.astype(jnp.float32)
        s = jnp.einsum('bnhd,bmhd->bhnm', qf, kf) * decay
        inner = jnp.einsum('bhnm,bmhv->bnhv', s, vf)
        cross = jnp.einsum('bnhd,bhdv->bnhv', qf, R) * xi[None, :, :, None]
        R = gamma_c[None, :, None, None] * R + jnp.einsum('bmhd,bmhv->bhdv', kf * zeta[None, :, :, None], vf)
        return R, inner + cross

    R0 = jnp.zeros((B, H, dk, dv), jnp.float32)
    _, o = lax.scan(step, R0, (to_chunks(q), to_chunks(k), to_chunks(v)))
    return o.transpose(1, 0, 2, 3, 4).reshape(B, S, H, dv)


def retention_mixer(q, k, v, g, gn_gain, cos, sin):
    B, S, _ = q.shape
    H = RET_HEADS
    q = apply_rope(q.reshape(B, S, H, RET_DK), cos, sin)
    k = apply_rope(k.reshape(B, S, H, RET_DK), cos, sin) * (RET_DK ** -0.5)
    v = v.reshape(B, S, H, RET_DV)
    lg_fwd = jnp.log1p(-(2.0 ** (-5.0 - jnp.arange(H, dtype=jnp.float32))))
    lg_bwd = lg_fwd[::-1]
    fwd = retention_chunkwise(q, k, v, lg_fwd, True)
    bwd = jnp.flip(retention_chunkwise(jnp.flip(q, 1), jnp.flip(k, 1), jnp.flip(v, 1), lg_bwd, False), 1)
    y = rms_norm(fwd + bwd, gn_gain.astype(jnp.float32))
    return jax.nn.silu(g) * y.reshape(B, S, H * RET_DV).astype(g.dtype)


def swa_mixer(q, k, v, rel_table, sinks):
    B, S, _ = q.shape
    W = WINDOW
    nb = S // W
    KV = SWA_KV_HEADS
    G = SWA_HEADS // SWA_KV_HEADS
    qb = q.reshape(B, nb, W, KV, G, SWA_HD)

    def windows(t):
        tp = jnp.pad(t.reshape(B, S, KV, SWA_HD), ((0, 0), (W, W), (0, 0), (0, 0)))
        tp = tp.reshape(B, nb + 2, W, KV, SWA_HD)
        return jnp.concatenate([tp[:, :-2], tp[:, 1:-1], tp[:, 2:]], axis=2)

    kw, vw = windows(k), windows(v)
    s = jnp.einsum('bnqkgd,bnjkd->bnkgqj', qb, kw, preferred_element_type=jnp.float32) * (SWA_HD ** -0.5)
    qi = jnp.arange(W)[:, None]
    kj = jnp.arange(3 * W)[None, :]
    rel = kj - W - qi
    bias = rel_table[t5_bucket(rel)].astype(jnp.float32)
    bias = bias.transpose(2, 0, 1).reshape(KV, G, W, 3 * W)
    key_pos = jnp.arange(nb)[:, None] * W + kj - W
    valid = (jnp.abs(rel) <= W)[None] & ((key_pos >= 0) & (key_pos < S))[:, None, :]
    s = jnp.where(valid[None, :, None, None], s + bias, -jnp.inf)
    sink = jnp.broadcast_to(sinks.astype(jnp.float32).reshape(KV, G, 1, 1), s.shape[:-1] + (1,))
    p = jax.nn.softmax(jnp.concatenate([s, sink], axis=-1), axis=-1)[..., :-1].astype(v.dtype)
    o = jnp.einsum('bnkgqj,bnjkd->bnqkgd', p, vw)
    return o.reshape(B, S, SWA_HEADS * SWA_HD)


def moe(h, w_router, router_bias, w_gate, w_up, w_down, ws_gate, ws_up, ws_down):
    B, S, D = h.shape
    N = B * S
    E = N_EXPERTS
    t = h.reshape(N, D)
    scores = jax.nn.sigmoid(jnp.matmul(t, w_router, preferred_element_type=jnp.float32))
    biased = scores + router_bias.astype(jnp.float32)
    grp = biased.reshape(N, N_GROUPS, E // N_GROUPS)
    grp_score = lax.top_k(grp, 2)[0].sum(-1)
    _, gidx = lax.top_k(grp_score, TOP_GROUPS)
    gmask = jnp.sum(jax.nn.one_hot(gidx, N_GROUPS, dtype=jnp.float32), axis=1) > 0
    emask = jnp.repeat(gmask, E // N_GROUPS, axis=1)
    _, eidx = lax.top_k(jnp.where(emask, biased, -jnp.inf), TOP_K)
    w = jnp.take_along_axis(scores, eidx, axis=1)
    w = w / jnp.sum(w, axis=-1, keepdims=True) * ROUTE_SCALE
    gates = jnp.sum(jax.nn.one_hot(eidx, E, dtype=jnp.float32) * w[..., None], axis=1)

    def expert(acc, p):
        wg, wu, wd, ge = p
        a = jax.nn.silu(t @ wg) * (t @ wu)
        return acc + ge[:, None] * jnp.matmul(a, wd, preferred_element_type=jnp.float32), None

    y, _ = lax.scan(expert, jnp.zeros((N, D), jnp.float32), (w_gate, w_up, w_down, gates.T))
    shared = (jax.nn.silu(t @ ws_gate) * (t @ ws_up)) @ ws_down
    return (y.astype(h.dtype) + shared).reshape(B, S, D)


def setup_inputs(seed: int = 0) -> dict:
    key = jax.random.key(seed)
    keys = jax.random.split(key, 48)
    counter = [0]

    def nxt():
        k = keys[counter[0]]
        counter[0] += 1
        return k

    def nrm(shape, scale):
        return jax.random.normal(nxt(), shape, jnp.float32) * scale

    def gain(shape):
        return 1.0 + nrm(shape, 0.02)

    D = D_MODEL
    ne = (DEPTH + 1) // 2
    no = DEPTH // 2
    bs = LRU_WIDTH // LRU_BLOCKS
    x = nrm((BATCH, SEQ, D), 1.0)
    c = nrm((BATCH, D), 1.0)
    w_mod = nrm((DEPTH, D, 6 * D), 0.5 * D ** -0.5)
    b_mod = nrm((DEPTH, 6 * D), 0.01)
    norm_mix = gain((DEPTH, D))
    norm_ffn = gain((DEPTH, D))
    final_norm = gain((D,))
    w_in_ab = nrm((ne, D, sum(AB_SIZES)), D ** -0.5)
    q_lat_norm = gain((ne, MLA_Q_RANK))
    kv_lat_norm = gain((ne, MLA_KV_RANK))
    w_uq = nrm((ne, MLA_Q_RANK, MLA_HEADS * (MLA_NOPE + MLA_ROPE)), MLA_Q_RANK ** -0.5)
    w_ukv = nrm((ne, MLA_KV_RANK, MLA_HEADS * (MLA_NOPE + MLA_V)), MLA_KV_RANK ** -0.5)
    conv_w = nrm((ne, LRU_CONV, LRU_WIDTH), LRU_CONV ** -0.5)
    conv_b = nrm((ne, LRU_WIDTH), 0.01)
    lru_w_a = nrm((ne, 2, LRU_BLOCKS, bs, bs), bs ** -0.5)
    lru_b_a = nrm((ne, 2, LRU_WIDTH), 0.01)
    lru_w_x = nrm((ne, 2, LRU_BLOCKS, bs, bs), bs ** -0.5)
    lru_b_x = nrm((ne, 2, LRU_WIDTH), 0.01)
    u = jax.random.uniform(nxt(), (ne, 2, LRU_WIDTH), jnp.float32, 0.9, 0.999)
    a0 = u ** (1.0 / LRU_C)
    lru_lambda = jnp.log(a0) - jnp.log1p(-a0)
    w_out_ab = nrm((ne, MLA_HEADS * MLA_V + LRU_WIDTH, D), D ** -0.5)
    w_in_cd = nrm((no, D, sum(CD_SIZES)), D ** -0.5)
    ret_gn = gain((no, RET_HEADS, RET_DV))
    swa_sinks = nrm((no, SWA_HEADS), 1.0)
    w_out_cd = nrm((no, RET_HEADS * RET_DV + SWA_HEADS * SWA_HD, D), D ** -0.5)
    rel_bias = nrm((REL_BUCKETS, SWA_HEADS), 0.5)
    w_router = nrm((DEPTH, D, N_EXPERTS), D ** -0.5)
    router_bias = nrm((DEPTH, N_EXPERTS), 0.01)
    w_gate = nrm((DEPTH, N_EXPERTS, D, EXPERT_FF), D ** -0.5)
    w_up = nrm((DEPTH, N_EXPERTS, D, EXPERT_FF), D ** -0.5)
    w_down = nrm((DEPTH, N_EXPERTS, EXPERT_FF, D), EXPERT_FF ** -0.5)
    ws_gate = nrm((DEPTH, D, SHARED_FF), D ** -0.5)
    ws_up = nrm((DEPTH, D, SHARED_FF), D ** -0.5)
    ws_down = nrm((DEPTH, SHARED_FF, D), SHARED_FF ** -0.5)
    return {'x': x, 'c': c, 'w_mod': w_mod, 'b_mod': b_mod, 'norm_mix': norm_mix,
            'norm_ffn': norm_ffn, 'final_norm': final_norm, 'w_in_ab': w_in_ab,
            'q_lat_norm': q_lat_norm, 'kv_lat_norm': kv_lat_norm, 'w_uq': w_uq, 'w_ukv': w_ukv,
            'conv_w': conv_w, 'conv_b': conv_b, 'lru_w_a': lru_w_a, 'lru_b_a': lru_b_a,
            'lru_w_x': lru_w_x, 'lru_b_x': lru_b_x, 'lru_lambda': lru_lambda, 'w_out_ab': w_out_ab,
            'w_in_cd': w_in_cd, 'ret_gn': ret_gn, 'swa_sinks': swa_sinks, 'w_out_cd': w_out_cd,
            'rel_bias': rel_bias, 'w_router': w_router, 'router_bias': router_bias,
            'w_gate': w_gate, 'w_up': w_up, 'w_down': w_down,
            'ws_gate': ws_gate, 'ws_up': ws_up, 'ws_down': ws_down}


def reference(x, c, w_mod, b_mod, norm_mix, norm_ffn, final_norm, w_in_ab, q_lat_norm, kv_lat_norm,
              w_uq, w_ukv, conv_w, conv_b, lru_w_a, lru_b_a, lru_w_x, lru_b_x, lru_lambda, w_out_ab,
              w_in_cd, ret_gn, swa_sinks, w_out_cd, rel_bias, w_router, router_bias,
              w_gate, w_up, w_down, ws_gate, ws_up, ws_down):
    B, S, D = x.shape
    cos_r, sin_r = rope_tables(S, MLA_ROPE)
    cos_t, sin_t = rope_tables(S, RET_DK)
    c_act = jax.nn.silu(c)
    for layer in range(DEPTH):
        mod = c_act @ w_mod[layer] + b_mod[layer]
        sh_m, sc_m, g_m, sh_f, sc_f, g_f = [m[:, None, :] for m in jnp.split(mod, 6, axis=-1)]
        hm = modulate(x, norm_mix[layer], sh_m, sc_m)
        if layer % 2 == 0:
            i = layer // 2
            q_lat, kv_lat, k_rope, lru_x, lru_gate = split_cols(hm @ w_in_ab[i], AB_SIZES)
            a_out = mla_mixer(q_lat, kv_lat, k_rope, q_lat_norm[i], kv_lat_norm[i],
                              w_uq[i], w_ukv[i], cos_r, sin_r)
            xc = conv_centred(lru_x, conv_w[i], conv_b[i])
            h_lru = (rglru_scan(xc, lru_w_a[i, 0], lru_b_a[i, 0], lru_w_x[i, 0], lru_b_x[i, 0], lru_lambda[i, 0], False)
                     + rglru_scan(xc, lru_w_a[i, 1], lru_b_a[i, 1], lru_w_x[i, 1], lru_b_x[i, 1], lru_lambda[i, 1], True))
            b_out = jax.nn.gelu(lru_gate) * h_lru
            mixed = jnp.concatenate([a_out, b_out], axis=-1) @ w_out_ab[i]
        else:
            i = layer // 2
            rq, rk, rv, rg, sq, sk, sv = split_cols(hm @ w_in_cd[i], CD_SIZES)
            c_out = retention_mixer(rq, rk, rv, rg, ret_gn[i], cos_t, sin_t)
            d_out = swa_mixer(sq, sk, sv, rel_bias, swa_sinks[i])
            mixed = jnp.concatenate([c_out, d_out], axis=-1) @ w_out_cd[i]
        x = x + g_m * mixed
        hf = modulate(x, norm_ffn[layer], sh_f, sc_f)
        x = x + g_f * moe(hf, w_router[layer], router_bias[layer], w_gate[layer], w_up[layer],
                          w_down[layer], ws_gate[layer], ws_up[layer], ws_down[layer])
    return rms_norm(x, final_norm)
```

```python
import functools
import math

import numpy as np
import jax
import jax.numpy as jnp
from jax import lax
from jax.experimental import pallas as pl
from jax.experimental.pallas import tpu as pltpu

F32 = jnp.float32
BF16 = jnp.bfloat16
I32 = jnp.int32
U32 = jnp.uint32

D_MODEL = 2048
BATCH = 4
SEQ = 2048
DEPTH = 2
N_TOK = BATCH * SEQ
HALF = D_MODEL // 2
MLA_NOPE = 128
MLA_ROPE = 64
MLA_V = 128
MLA_HEADS = HALF // MLA_V
MLA_Q_RANK = D_MODEL // 4
MLA_KV_RANK = D_MODEL // 4
MLA_QK = 256
LRU_WIDTH = HALF
LRU_BLOCKS = 8
LRU_BS = LRU_WIDTH // LRU_BLOCKS
LRU_CONV = 4
LRU_C = 8.0
RET_DK = 256
RET_DV = 256
RET_HEADS = HALF // RET_DV
SWA_HD = 128
SWA_HEADS = HALF // SWA_HD
SWA_KV_HEADS = 2
SWA_G = SWA_HEADS // SWA_KV_HEADS
WINDOW = 128
REL_BUCKETS = 32
REL_MAX_DIST = 128
N_EXPERTS = 64
TOP_K = 8
N_GROUPS = 8
GROUP_SIZE = N_EXPERTS // N_GROUPS
TOP_GROUPS = 4
EXPERT_FF = D_MODEL // 4
SHARED_FF = D_MODEL // 4
ROUTE_SCALE = 2.5
ROPE_THETA = 10000.0
EPS = 1e-6
NEG_BIG = -1e30

LANES = 128
SUBLANES = 8
VMEM_LIMIT = 52 * 2**20

TM_PROJ = 1024
TM_UP = 512
TQ_ATT = 512
TM_OUT = 512
T_ROUTE = 512
T_DISP = 256
TM_EXP = 256
N_SLOTS = N_TOK * TOP_K
NB_EXP = N_SLOTS // TM_EXP + N_EXPERTS
P_ROWS = NB_EXP * TM_EXP
PACK_W = D_MODEL // 2

LRU_SEG = 260
LRU_ROWS = SUBLANES * LRU_SEG
assert LRU_ROWS >= SEQ and LRU_SEG % 8 == 4


def _cparams(sem, vmem=VMEM_LIMIT):
    return pltpu.CompilerParams(dimension_semantics=sem, vmem_limit_bytes=vmem)


def _silu(x):
    return x * jax.nn.sigmoid(x)


def _rms(x, g):
    return x * lax.rsqrt(jnp.mean(x * x, axis=-1, keepdims=True) + EPS) * g


def _pack_bf16_pair(lo, hi):
    lo_b = lax.bitcast_convert_type(lo.astype(BF16).astype(F32), U32)
    hi_b = lax.bitcast_convert_type(hi.astype(BF16).astype(F32), U32)
    return (hi_b & jnp.uint32(0xFFFF0000)) | (lo_b >> 16)


def _unpack_bf16_pair(w):
    lo = lax.bitcast_convert_type(w << 16, F32)
    hi = lax.bitcast_convert_type(w & jnp.uint32(0xFFFF0000), F32)
    return lo, hi


def _mod_kernel(c_ref, w_ref, b_ref, o_ref):
    c = c_ref[...]
    ca = _silu(c).astype(BF16)
    o_ref[0] = jnp.dot(ca, w_ref[0].astype(BF16), preferred_element_type=F32) + b_ref[0]


def _modulation(c, w_mod, b_mod):
    tn = 1024
    cp = jnp.pad(c, ((0, SUBLANES - BATCH), (0, 0)))
    out = pl.pallas_call(
        _mod_kernel,
        grid=(DEPTH, 6 * D_MODEL // tn),
        in_specs=[pl.BlockSpec((SUBLANES, D_MODEL), lambda l, j: (0, 0)),
                  pl.BlockSpec((1, D_MODEL, tn), lambda l, j: (l, 0, j)),
                  pl.BlockSpec((1, 1, tn), lambda l, j: (l, 0, j))],
        out_specs=pl.BlockSpec((1, SUBLANES, tn), lambda l, j: (l, 0, j)),
        out_shape=jax.ShapeDtypeStruct((DEPTH, SUBLANES, 6 * D_MODEL), F32),
        compiler_params=_cparams(("parallel", "parallel")),
        name="adaln_mod",
    )(cp, w_mod, b_mod.reshape(DEPTH, 1, 6 * D_MODEL))
    return out[:, :BATCH]


def _inproj_kernel(x_ref, g_ref, sc_ref, sh_ref, w_ref, o_ref, h_ref):
    @pl.when(pl.program_id(1) == 0)
    def _():
        y = _rms(x_ref[...], g_ref[...])
        h_ref[...] = (y * (1.0 + sc_ref[0]) + sh_ref[0]).astype(BF16)

    o_ref[...] = jnp.dot(h_ref[...], w_ref[...], preferred_element_type=F32)


def _in_projection(x, gain, scale, shift, w_bf16, tn, name):
    p = w_bf16.shape[1]
    per_b = SEQ // TM_PROJ
    return pl.pallas_call(
        _inproj_kernel,
        grid=(N_TOK // TM_PROJ, p // tn),
        in_specs=[pl.BlockSpec((TM_PROJ, D_MODEL), lambda i, j: (i, 0)),
                  pl.BlockSpec((1, D_MODEL), lambda i, j: (0, 0)),
                  pl.BlockSpec((1, 1, D_MODEL), lambda i, j: (i // per_b, 0, 0)),
                  pl.BlockSpec((1, 1, D_MODEL), lambda i, j: (i // per_b, 0, 0)),
                  pl.BlockSpec((D_MODEL, tn), lambda i, j: (0, j))],
        out_specs=pl.BlockSpec((TM_PROJ, tn), lambda i, j: (i, j)),
        out_shape=jax.ShapeDtypeStruct((N_TOK, p), F32),
        scratch_shapes=[pltpu.VMEM((TM_PROJ, D_MODEL), BF16)],
        compiler_params=_cparams(("parallel", "arbitrary")),
        name=name,
    )(x, gain.reshape(1, D_MODEL), scale, shift, w_bf16)


def _mla_up_kernel(ql_ref, kvl_ref, kr_ref, qn_ref, kvn_ref, wq_ref, wkv_ref, cs_ref, q_ref, k_ref, v_ref):
    scale = (MLA_NOPE + MLA_ROPE) ** -0.5
    hq = _rms(ql_ref[...], qn_ref[...]).astype(BF16)
    hkv = _rms(kvl_ref[...], kvn_ref[...]).astype(BF16)
    yq = jnp.dot(hq, wq_ref[...], preferred_element_type=F32) * scale
    ykv = jnp.dot(hkv, wkv_ref[...], preferred_element_type=F32)
    cs = cs_ref[...]
    lane = lax.broadcasted_iota(I32, cs.shape, 1)

    def rope_sum(blk):
        z = blk * cs
        return z + pltpu.roll(z, MLA_ROPE, 1)

    kr = jnp.where(lane < MLA_ROPE, rope_sum(kr_ref[...]), 0.0).astype(BF16)
    for h in range(MLA_HEADS):
        c0 = h * MLA_QK
        q_ref[0, h, :, 0:MLA_NOPE] = yq[:, c0:c0 + MLA_NOPE].astype(BF16)
        q_ref[0, h, :, MLA_NOPE:MLA_QK] = rope_sum(yq[:, c0 + MLA_NOPE:c0 + MLA_QK]).astype(BF16)
        k_ref[0, h, :, 0:MLA_NOPE] = ykv[:, c0:c0 + MLA_NOPE].astype(BF16)
        k_ref[0, h, :, MLA_NOPE:MLA_QK] = kr
        v_ref[0, h] = ykv[:, c0 + MLA_NOPE:c0 + MLA_QK].astype(BF16)


def _mla_up(p0, q_norm, kv_norm, wq, wkv, cs_tab):
    per_b = SEQ // TM_UP
    qk_shape = jax.ShapeDtypeStruct((BATCH, MLA_HEADS, SEQ, MLA_QK), BF16)
    return pl.pallas_call(
        _mla_up_kernel,
        grid=(N_TOK // TM_UP,),
        in_specs=[pl.BlockSpec((TM_UP, MLA_Q_RANK), lambda i: (i, 0)),
                  pl.BlockSpec((TM_UP, MLA_KV_RANK), lambda i: (i, 1)),
                  pl.BlockSpec((TM_UP, LANES), lambda i: (i, 24)),
                  pl.BlockSpec((1, MLA_Q_RANK), lambda i: (0, 0)),
                  pl.BlockSpec((1, MLA_KV_RANK), lambda i: (0, 0)),
                  pl.BlockSpec((MLA_Q_RANK, MLA_HEADS * MLA_QK), lambda i: (0, 0)),
                  pl.BlockSpec((MLA_KV_RANK, MLA_HEADS * MLA_QK), lambda i: (0, 0)),
                  pl.BlockSpec((TM_UP, LANES), lambda i: (i % per_b, 0))],
        out_specs=[pl.BlockSpec((1, MLA_HEADS, TM_UP, MLA_QK), lambda i: (i // per_b, 0, i % per_b, 0)),
                   pl.BlockSpec((1, MLA_HEADS, TM_UP, MLA_QK), lambda i: (i // per_b, 0, i % per_b, 0)),
                   pl.BlockSpec((1, MLA_HEADS, TM_UP, MLA_V), lambda i: (i // per_b, 0, i % per_b, 0))],
        out_shape=[qk_shape, qk_shape, jax.ShapeDtypeStruct((BATCH, MLA_HEADS, SEQ, MLA_V), BF16)],
        compiler_params=_cparams(("parallel",)),
        name="mla_up",
    )(p0, p0, p0, q_norm.reshape(1, -1), kv_norm.reshape(1, -1), wq, wkv, cs_tab)


def _mla_attn_kernel(q_ref, k_ref, v_ref, o_ref):
    s = lax.dot_general(q_ref[0, 0], k_ref[0, 0], (((1,), (1,)), ((), ())), preferred_element_type=F32)
    m = jnp.max(s, axis=-1, keepdims=True)
    p = jnp.exp(s - m)
    l = jnp.sum(p, axis=-1, keepdims=True)
    o = jnp.dot(p.astype(BF16), v_ref[0, 0], preferred_element_type=F32)
    o_ref[...] = (o / l).astype(BF16)


def _mla_attention(q, k, v):
    nq = SEQ // TQ_ATT
    return pl.pallas_call(
        _mla_attn_kernel,
        grid=(BATCH, MLA_HEADS, nq),
        in_specs=[pl.BlockSpec((1, 1, TQ_ATT, MLA_QK), lambda b, h, i: (b, h, i, 0)),
                  pl.BlockSpec((1, 1, SEQ, MLA_QK), lambda b, h, i: (b, h, 0, 0)),
                  pl.BlockSpec((1, 1, SEQ, MLA_V), lambda b, h, i: (b, h, 0, 0))],
        out_specs=pl.BlockSpec((TQ_ATT, MLA_V), lambda b, h, i: (b * nq + i, h)),
        out_shape=jax.ShapeDtypeStruct((N_TOK, HALF), BF16),
        compiler_params=_cparams(("parallel", "parallel", "parallel")),
        name="mla_attn",
    )(q, k, v)


def _lru_kernel(x_ref, gate_ref, cw_ref, cb_ref, wg_ref, bg_ref, lam_ref, o_ref,
                af_ref, uf_ref, ab_ref, ub_ref, hf_ref, pf_ref, hb_ref, pb_ref, hs_ref):
    x = x_ref[...]
    row = lax.broadcasted_iota(I32, x.shape, 0)

    def shifted(d):
        r = pltpu.roll(x, (-d) % SEQ, 0)
        return jnp.where((row + d >= 0) & (row + d < SEQ), r, 0.0)

    cw = cw_ref[...]
    left = LRU_CONV // 2
    xc = cb_ref[...]
    for kk in range(LRU_CONV):
        d = kk - left
        xc = xc + cw[kk:kk + 1] * (x if d == 0 else shifted(d))

    gates = jnp.dot(xc.astype(BF16), wg_ref[0], preferred_element_type=F32) + bg_ref[0]
    lam = lam_ref[...]
    z = -lam
    sp = jnp.maximum(z, 0.0) + jnp.log1p(jnp.exp(-jnp.abs(z)))
    pad_rows = LRU_ROWS - SEQ
    for d, (a_ref, u_ref) in enumerate(((af_ref, uf_ref), (ab_ref, ub_ref))):
        r = jax.nn.sigmoid(gates[:, d * 256:d * 256 + LRU_BS])
        i = jax.nn.sigmoid(gates[:, d * 256 + LRU_BS:(d + 1) * 256])
        log_a = -LRU_C * r * sp[d:d + 1]
        a_ref[0:SEQ] = jnp.exp(log_a)
        u_ref[0:SEQ] = jnp.sqrt(1.0 - jnp.exp(2.0 * log_a)) * (i * xc)
        a_ref[SEQ:LRU_ROWS] = jnp.zeros((pad_rows, LANES), F32)
        u_ref[SEQ:LRU_ROWS] = jnp.zeros((pad_rows, LANES), F32)

    ones = jnp.ones((SUBLANES, LANES), F32)
    zeros = jnp.zeros((SUBLANES, LANES), F32)

    def seg(t):
        return pl.ds(t, SUBLANES, stride=LRU_SEG)

    def local_scan(s, carry):
        p_f, h_f, p_b, h_b = carry
        tf = s
        tb = LRU_SEG - 1 - s
        a = af_ref[seg(tf)]
        h_f = a * h_f + uf_ref[seg(tf)]
        p_f = a * p_f
        hf_ref[seg(tf)] = h_f
        pf_ref[seg(tf)] = p_f
        a = ab_ref[seg(tb)]
        h_b = a * h_b + ub_ref[seg(tb)]
        p_b = a * p_b
        hb_ref[seg(tb)] = h_b
        pb_ref[seg(tb)] = p_b
        return p_f, h_f, p_b, h_b

    p_f, h_f, p_b, h_b = lax.fori_loop(0, LRU_SEG, local_scan, (ones, zeros, ones, zeros), unroll=4)

    rows_f = []
    c = jnp.zeros((1, LANES), F32)
    for j in range(SUBLANES):
        rows_f.append(c)
        c = p_f[j:j + 1] * c + h_f[j:j + 1]
    rows_b = [None] * SUBLANES
    c = jnp.zeros((1, LANES), F32)
    for j in range(SUBLANES - 1, -1, -1):
        rows_b[j] = c
        c = p_b[j:j + 1] * c + h_b[j:j + 1]
    sub = lax.broadcasted_iota(I32, (SUBLANES, LANES), 0)
    c_f = zeros
    c_b = zeros
    for j in range(SUBLANES):
        c_f = jnp.where(sub == j, rows_f[j], c_f)
        c_b = jnp.where(sub == j, rows_b[j], c_b)

    def fixup(t, _):
        hs_ref[seg(t)] = (hf_ref[seg(t)] + pf_ref[seg(t)] * c_f) + (hb_ref[seg(t)] + pb_ref[seg(t)] * c_b)
        return 0

    lax.fori_loop(0, LRU_SEG, fixup, 0, unroll=4)

    g = gate_ref[...]
    gelu = 0.5 * g * (1.0 + jnp.tanh(math.sqrt(2.0 / math.pi) * (g + 0.044715 * (g * g * g))))
    o_ref[...] = (gelu * hs_ref[0:SEQ]).astype(BF16)


def _rglru(p0, conv_w, conv_b, w_gates, b_gates, lam):
    scan_buf = pltpu.VMEM((LRU_ROWS, LANES), F32)
    return pl.pallas_call(
        _lru_kernel,
        grid=(BATCH, LRU_BLOCKS),
        in_specs=[pl.BlockSpec((SEQ, LRU_BS), lambda b, g: (b, 8 + g)),
                  pl.BlockSpec((SEQ, LRU_BS), lambda b, g: (b, 16 + g)),
                  pl.BlockSpec((LRU_CONV, LRU_BS), lambda b, g: (0, g)),
                  pl.BlockSpec((1, LRU_BS), lambda b, g: (0, g)),
                  pl.BlockSpec((1, LRU_BS, 4 * LRU_BS), lambda b, g: (g, 0, 0)),
                  pl.BlockSpec((1, 1, 4 * LRU_BS), lambda b, g: (g, 0, 0)),
                  pl.BlockSpec((2, LRU_BS), lambda b, g: (0, g))],
        out_specs=pl.BlockSpec((SEQ, LRU_BS), lambda b, g: (b, g)),
        out_shape=jax.ShapeDtypeStruct((N_TOK, LRU_WIDTH), BF16),
        scratch_shapes=[scan_buf] * 9,
        compiler_params=_cparams(("parallel", "parallel")),
        name="rglru",
    )(p0, p0, conv_w, conv_b.reshape(1, -1), w_gates, b_gates, lam)


def _ret_kernel(lg_ref, q_ref, k_ref, v_ref, g_ref, cq_ref, sq_ref, ck_ref, sk_ref, gn_ref, o_ref, ks_ref, vs_ref):
    h = pl.program_id(1)
    qi = pl.program_id(2)
    half = RET_DK // 2

    def rope(t, c, s):
        t1, t2 = t[:, :half], t[:, half:]
        return jnp.concatenate([t1 * c - t2 * s, t2 * c + t1 * s], axis=1)

    @pl.when(qi == 0)
    def _():
        ks_ref[...] = (rope(k_ref[...], ck_ref[...], sk_ref[...]) * (RET_DK ** -0.5)).astype(BF16)
        vs_ref[...] = v_ref[...].astype(BF16)

    q = rope(q_ref[...], cq_ref[...], sq_ref[...]).astype(BF16)
    s = lax.dot_general(q, ks_ref[...], (((1,), (1,)), ((), ())), preferred_element_type=F32)
    n = qi * TQ_ATT + lax.broadcasted_iota(I32, s.shape, 0)
    m = lax.broadcasted_iota(I32, s.shape, 1)
    d = (n - m).astype(F32)
    lg_f = lg_ref[h]
    lg_b = lg_ref[RET_HEADS - 1 - h]
    dec = jnp.exp(jnp.where(d >= 0.0, lg_f * d, -lg_b * d))
    o = jnp.dot((s * dec).astype(BF16), vs_ref[...], preferred_element_type=F32)
    y = _rms(o, gn_ref[0])
    o_ref[...] = (_silu(g_ref[...]) * y).astype(BF16)


def _retention(p1, lg, cos_t, sin_t, ret_gn):
    nq = SEQ // TQ_ATT
    half = RET_DK // 2
    return pl.pallas_call(
        _ret_kernel,
        grid=(BATCH, RET_HEADS, nq),
        in_specs=[pl.BlockSpec(memory_space=pltpu.SMEM),
                  pl.BlockSpec((TQ_ATT, RET_DK), lambda b, h, i: (b * nq + i, h)),
                  pl.BlockSpec((SEQ, RET_DK), lambda b, h, i: (b, RET_HEADS + h)),
                  pl.BlockSpec((SEQ, RET_DV), lambda b, h, i: (b, 2 * RET_HEADS + h)),
                  pl.BlockSpec((TQ_ATT, RET_DV), lambda b, h, i: (b * nq + i, 3 * RET_HEADS + h)),
                  pl.BlockSpec((TQ_ATT, half), lambda b, h, i: (i, 0)),
                  pl.BlockSpec((TQ_ATT, half), lambda b, h, i: (i, 0)),
                  pl.BlockSpec((SEQ, half), lambda b, h, i: (0, 0)),
                  pl.BlockSpec((SEQ, half), lambda b, h, i: (0, 0)),
                  pl.BlockSpec((1, 1, RET_DV), lambda b, h, i: (h, 0, 0))],
        out_specs=pl.BlockSpec((TQ_ATT, RET_DV), lambda b, h, i: (b * nq + i, h)),
        out_shape=jax.ShapeDtypeStruct((N_TOK, HALF), BF16),
        scratch_shapes=[pltpu.VMEM((SEQ, RET_DK), BF16), pltpu.VMEM((SEQ, RET_DV), BF16)],
        compiler_params=_cparams(("parallel", "parallel", "arbitrary")),
        name="retention",
    )(lg, p1, p1, p1, p1, cos_t, sin_t, cos_t, sin_t, ret_gn.reshape(RET_HEADS, 1, RET_DV))


def _swa_kernel(sink_ref, q_ref, k_ref, v_ref, bias_ref, o_ref):
    kv = pl.program_id(1)
    n = pl.program_id(2)
    nb = SEQ // WINDOW
    w = WINDOW
    prev = jnp.maximum(n - 1, 0)
    nxt = jnp.minimum(n + 1, nb - 1)

    def rows(ref, blk):
        return ref[pl.ds(pl.multiple_of(blk * w, w), w), :].astype(BF16)

    kw = jnp.concatenate([rows(k_ref, prev), rows(k_ref, n), rows(k_ref, nxt)], axis=0)
    vw = jnp.concatenate([rows(v_ref, prev), rows(v_ref, n), rows(v_ref, nxt)], axis=0)
    qb = q_ref[...]
    q4 = jnp.concatenate([qb[:, g * SWA_HD:(g + 1) * SWA_HD] for g in range(SWA_G)], axis=0).astype(BF16)
    s = lax.dot_general(q4, kw, (((1,), (1,)), ((), ())), preferred_element_type=F32) * (SWA_HD ** -0.5)
    col = lax.broadcasted_iota(I32, (w, 3 * w), 1)
    outside = ((col < w) & (n == 0)) | ((col >= 2 * w) & (n == nb - 1))
    for g in range(SWA_G):
        sg = jnp.where(outside, NEG_BIG, s[g * w:(g + 1) * w] + bias_ref[g])
        sink = sink_ref[kv * SWA_G + g]
        m = jnp.maximum(jnp.max(sg, axis=-1, keepdims=True), sink)
        p = jnp.exp(sg - m)
        denom = jnp.sum(p, axis=-1, keepdims=True) + jnp.exp(sink - m)
        o = jnp.dot((p / denom).astype(BF16), vw, preferred_element_type=F32)
        o_ref[:, g * SWA_HD:(g + 1) * SWA_HD] = o.astype(BF16)


def _swa(p1, sinks, bias):
    nb = SEQ // WINDOW
    qcols = SWA_G * SWA_HD
    q_blk0 = (4 * RET_HEADS * RET_DK) // qcols
    k_blk0 = (4 * RET_HEADS * RET_DK + SWA_HEADS * SWA_HD) // SWA_HD
    v_blk0 = k_blk0 + SWA_KV_HEADS
    return pl.pallas_call(
        _swa_kernel,
        grid=(BATCH, SWA_KV_HEADS, nb),
        in_specs=[pl.BlockSpec(memory_space=pltpu.SMEM),
                  pl.BlockSpec((WINDOW, qcols), lambda b, kv, n: (b * nb + n, q_blk0 + kv)),
                  pl.BlockSpec((SEQ, SWA_HD), lambda b, kv, n: (b, k_blk0 + kv)),
                  pl.BlockSpec((SEQ, SWA_HD), lambda b, kv, n: (b, v_blk0 + kv)),
                  pl.BlockSpec((SWA_G, WINDOW, 3 * WINDOW), lambda b, kv, n: (kv, 0, 0))],
        out_specs=pl.BlockSpec((WINDOW, qcols), lambda b, kv, n: (b * nb + n, kv)),
        out_shape=jax.ShapeDtypeStruct((N_TOK, HALF), BF16),
        compiler_params=_cparams(("parallel", "parallel", "parallel")),
        name="swa",
    )(sinks, p1, p1, p1, bias)


def _outproj_kernel(a_ref, b_ref, wa_ref, wb_ref, x_ref, gm_ref, g_ref, sc_ref, sh_ref, wr_ref,
                    x1_ref, hp_ref, lg_ref):
    mixed = (jnp.dot(a_ref[...], wa_ref[...], preferred_element_type=F32)
             + jnp.dot(b_ref[...], wb_ref[...], preferred_element_type=F32))
    x1 = x_ref[...] + gm_ref[0] * mixed
    x1_ref[...] = x1
    hf = _rms(x1, g_ref[...]) * (1.0 + sc_ref[0]) + sh_ref[0]
    hp_ref[...] = _pack_bf16_pair(hf[:, :PACK_W], hf[:, PACK_W:])
    lg_ref[...] = jnp.dot(hf, wr_ref[...], preferred_element_type=F32, precision=lax.Precision.HIGHEST)


def _out_projection(a, b, w_out_bf16, x, g_m, gain, scale, shift, w_router_pad):
    per_b = SEQ // TM_OUT
    vec = pl.BlockSpec((1, 1, D_MODEL), lambda i: (i // per_b, 0, 0))
    return pl.pallas_call(
        _outproj_kernel,
        grid=(N_TOK // TM_OUT,),
        in_specs=[pl.BlockSpec((TM_OUT, HALF), lambda i: (i, 0)),
                  pl.BlockSpec((TM_OUT, HALF), lambda i: (i, 0)),
                  pl.BlockSpec((HALF, D_MODEL), lambda i: (0, 0)),
                  pl.BlockSpec((HALF, D_MODEL), lambda i: (1, 0)),
                  pl.BlockSpec((TM_OUT, D_MODEL), lambda i: (i, 0)),
                  vec,
                  pl.BlockSpec((1, D_MODEL), lambda i: (0, 0)),
                  vec, vec,
                  pl.BlockSpec((D_MODEL, LANES), lambda i: (0, 0))],
        out_specs=[pl.BlockSpec((TM_OUT, D_MODEL), lambda i: (i, 0)),
                   pl.BlockSpec((TM_OUT, PACK_W), lambda i: (i, 0)),
                   pl.BlockSpec((TM_OUT, LANES), lambda i: (i, 0))],
        out_shape=[jax.ShapeDtypeStruct((N_TOK, D_MODEL), F32),
                   jax.ShapeDtypeStruct((N_TOK, PACK_W), U32),
                   jax.ShapeDtypeStruct((N_TOK, LANES), F32)],
        compiler_params=_cparams(("parallel",)),
        name="out_proj",
    )(a, b, w_out_bf16, w_out_bf16, x, g_m, gain.reshape(1, D_MODEL), scale, shift, w_router_pad)


def _route_kernel(lg_ref, bias_ref, eidx_ref, w_ref, rank_ref, cnt_ref):
    step = pl.program_id(0)

    @pl.when(step == 0)
    def _():
        cnt_ref[...] = jnp.zeros(cnt_ref.shape, F32)

    t = T_ROUTE
    scores = jax.nn.sigmoid(lg_ref[...].T[:N_EXPERTS])
    biased = scores + bias_ref[...]
    sub = lax.broadcasted_iota(I32, (GROUP_SIZE, t), 0).astype(F32)
    ninf = -jnp.inf

    def first_argmax(v, idx, n):
        m = jnp.max(v, axis=0, keepdims=True)
        return m, jnp.min(jnp.where(v == m, idx, float(n)), axis=0, keepdims=True)

    gs = []
    for g in range(N_GROUPS):
        bg = biased[g * GROUP_SIZE:(g + 1) * GROUP_SIZE]
        m1, i1 = first_argmax(bg, sub, GROUP_SIZE)
        m2 = jnp.max(jnp.where(sub == i1, ninf, bg), axis=0, keepdims=True)
        gs.append(m1 + m2)
    cur = jnp.concatenate(gs, axis=0)

    gmask = jnp.zeros((N_GROUPS, t), F32)
    for _ in range(TOP_GROUPS):
        _, i = first_argmax(cur, sub, N_GROUPS)
        pick = sub == i
        gmask = jnp.where(pick, 1.0, gmask)
        cur = jnp.where(pick, ninf, cur)

    eid = lax.broadcasted_iota(I32, (N_EXPERTS, t), 0).astype(F32)
    emask = jnp.concatenate([jnp.broadcast_to(gmask[g:g + 1], (GROUP_SIZE, t)) for g in range(N_GROUPS)], axis=0)
    cur = jnp.where(emask > 0.5, biased, ninf)
    sels, idxs, ws = [], [], []
    onehot = jnp.zeros((N_EXPERTS, t), F32)
    for _ in range(TOP_K):
        _, i = first_argmax(cur, eid, N_EXPERTS)
        pick = eid == i
        sels.append(pick)
        idxs.append(i)
        ws.append(jnp.sum(jnp.where(pick, scores, 0.0), axis=0, keepdims=True))
        onehot = jnp.where(pick, 1.0, onehot)
        cur = jnp.where(pick, ninf, cur)
    wsum = ws[0]
    for k in range(1, TOP_K):
        wsum = wsum + ws[k]

    r = lax.broadcasted_iota(I32, (t, t), 0)
    c = lax.broadcasted_iota(I32, (t, t), 1)
    tri = (r < c).astype(BF16)
    before = jnp.dot(onehot.astype(BF16), tri, preferred_element_type=F32) + cnt_ref[:, 0:1]
    ranks = [jnp.sum(jnp.where(sels[k], before, 0.0), axis=0, keepdims=True) for k in range(TOP_K)]

    eidx_ref[...] = jnp.concatenate(idxs, axis=0).astype(I32)
    rank_ref[...] = jnp.concatenate(ranks, axis=0).astype(I32)
    wk = jnp.concatenate([w / wsum * ROUTE_SCALE for w in ws], axis=0)
    wfull = jnp.concatenate([wk, jnp.zeros((LANES - TOP_K, t), F32)], axis=0)
    w_ref[...] = wfull.T
    cnt_ref[...] = cnt_ref[...] + jnp.sum(onehot, axis=1, keepdims=True)


def _route(logits, router_bias):
    return pl.pallas_call(
        _route_kernel,
        grid=(N_TOK // T_ROUTE,),
        in_specs=[pl.BlockSpec((T_ROUTE, LANES), lambda i: (i, 0)),
                  pl.BlockSpec((N_EXPERTS, 1), lambda i: (0, 0))],
        out_specs=[pl.BlockSpec((TOP_K, T_ROUTE), lambda i: (0, i)),
                   pl.BlockSpec((T_ROUTE, LANES), lambda i: (i, 0)),
                   pl.BlockSpec((TOP_K, T_ROUTE), lambda i: (0, i)),
                   pl.BlockSpec((N_EXPERTS, LANES), lambda i: (0, 0))],
        out_shape=[jax.ShapeDtypeStruct((TOP_K, N_TOK), I32),
                   jax.ShapeDtypeStruct((N_TOK, LANES), F32),
                   jax.ShapeDtypeStruct((TOP_K, N_TOK), I32),
                   jax.ShapeDtypeStruct((N_EXPERTS, LANES), F32)],
        compiler_params=_cparams(("arbitrary",)),
        name="route",
    )(logits, router_bias.reshape(N_EXPERTS, 1))


def _slot_kernel(eidx_ref, rank_ref, off_ref, slot_ref):
    eidx = eidx_ref[...]
    off = off_ref[...]
    t = eidx.shape[1]
    eid = lax.broadcasted_iota(I32, (N_EXPERTS, t), 0)
    base = [jnp.sum(jnp.where(eid == eidx[k:k + 1], off, 0.0), axis=0, keepdims=True) for k in range(TOP_K)]
    slot_ref[...] = jnp.concatenate(base, axis=0).astype(I32) + rank_ref[...]


def _slots(eidx, rank, offsets):
    return pl.pallas_call(
        _slot_kernel,
        grid=(N_TOK // T_ROUTE,),
        in_specs=[pl.BlockSpec((TOP_K, T_ROUTE), lambda i: (0, i)),
                  pl.BlockSpec((TOP_K, T_ROUTE), lambda i: (0, i)),
                  pl.BlockSpec((N_EXPERTS, 1), lambda i: (0, 0))],
        out_specs=pl.BlockSpec((TOP_K, T_ROUTE), lambda i: (0, i)),
        out_shape=jax.ShapeDtypeStruct((TOP_K, N_TOK), I32),
        compiler_params=_cparams(("parallel",)),
        name="slots",
    )(eidx, rank, offsets.astype(F32).reshape(N_EXPERTS, 1))


def _dispatch_kernel(slot_ref, hp_ref, xs_in_ref, xs_ref, sem):
    del xs_in_ref
    base = pl.program_id(0) * T_DISP

    def issue(t, _):
        for k in range(TOP_K):
            dst = slot_ref[k * N_TOK + base + t]
            pltpu.make_async_copy(hp_ref.at[pl.ds(t, 1)], xs_ref.at[pl.ds(dst, 1)], sem).start()
        return 0

    lax.fori_loop(0, T_DISP, issue, 0)

    def drain(t, _):
        for k in range(TOP_K):
            pltpu.make_async_copy(hp_ref.at[pl.ds(0, 1)], xs_ref.at[pl.ds(0, 1)], sem).wait()
        return 0

    lax.fori_loop(0, T_DISP, drain, 0)


def _dispatch(slot_flat, hp):
    zeros = jnp.zeros((P_ROWS, PACK_W), U32)
    return pl.pallas_call(
        _dispatch_kernel,
        grid_spec=pltpu.PrefetchScalarGridSpec(
            num_scalar_prefetch=1,
            grid=(N_TOK // T_DISP,),
            in_specs=[pl.BlockSpec((T_DISP, PACK_W), lambda i, s: (i, 0)),
                      pl.BlockSpec(memory_space=pl.ANY)],
            out_specs=pl.BlockSpec(memory_space=pl.ANY),
            scratch_shapes=[pltpu.SemaphoreType.DMA]),
        out_shape=jax.ShapeDtypeStruct((P_ROWS, PACK_W), U32),
        input_output_aliases={2: 0},
        compiler_params=_cparams(("arbitrary",)),
        name="dispatch",
    )(slot_flat, hp, zeros)


def _expert_kernel(be_ref, bv_ref, xs_ref, wg_ref, wu_ref, wd_ref, ys_ref, wgb_ref, wub_ref, wdb_ref):
    i = pl.program_id(0)
    changed = jnp.logical_or(i == 0, be_ref[i] != be_ref[jnp.maximum(i - 1, 0)])

    @pl.when(changed)
    def _():
        wgb_ref[...] = wg_ref[0].astype(BF16)
        wub_ref[...] = wu_ref[0].astype(BF16)
        wdb_ref[...] = wd_ref[0].astype(BF16)

    @pl.when(bv_ref[i] > 0)
    def _():
        lo, hi = _unpack_bf16_pair(xs_ref[...])
        lo = lo.astype(BF16)
        hi = hi.astype(BF16)
        hg = (jnp.dot(lo, wgb_ref[0:PACK_W], preferred_element_type=F32)
              + jnp.dot(hi, wgb_ref[PACK_W:D_MODEL], preferred_element_type=F32))
        hu = (jnp.dot(lo, wub_ref[0:PACK_W], preferred_element_type=F32)
              + jnp.dot(hi, wub_ref[PACK_W:D_MODEL], preferred_element_type=F32))
        act = (_silu(hg) * hu).astype(BF16)
        y = jnp.dot(act, wdb_ref[...], preferred_element_type=F32)
        ys_ref[...] = _pack_bf16_pair(y[:, :PACK_W], y[:, PACK_W:])

    @pl.when(bv_ref[i] == 0)
    def _():
        ys_ref[...] = jnp.zeros(ys_ref.shape, U32)


def _experts(block_expert, block_valid, xs, w_gate, w_up, w_down):
    return pl.pallas_call(
        _expert_kernel,
        grid_spec=pltpu.PrefetchScalarGridSpec(
            num_scalar_prefetch=2,
            grid=(NB_EXP,),
            in_specs=[pl.BlockSpec((TM_EXP, PACK_W), lambda i, be, bv: (i, 0)),
                      pl.BlockSpec((1, D_MODEL, EXPERT_FF), lambda i, be, bv: (be[i], 0, 0)),
                      pl.BlockSpec((1, D_MODEL, EXPERT_FF), lambda i, be, bv: (be[i], 0, 0)),
                      pl.BlockSpec((1, EXPERT_FF, D_MODEL), lambda i, be, bv: (be[i], 0, 0))],
            out_specs=pl.BlockSpec((TM_EXP, PACK_W), lambda i, be, bv: (i, 0)),
            scratch_shapes=[pltpu.VMEM((D_MODEL, EXPERT_FF), BF16),
                            pltpu.VMEM((D_MODEL, EXPERT_FF), BF16),
                            pltpu.VMEM((EXPERT_FF, D_MODEL), BF16)]),
        out_shape=jax.ShapeDtypeStruct((P_ROWS, PACK_W), U32),
        compiler_params=_cparams(("arbitrary",)),
        name="experts",
    )(block_expert, block_valid, xs, w_gate, w_up, w_down)


def _combine_kernel(slot_ref, ys_ref, w_ref, hp_ref, x1_ref, gf_ref, wsg_ref, wsu_ref, wsd_ref, fn_ref, o_ref,
                    buf_ref, sem, *, final_norm):
    i = pl.program_id(0)
    nsteps = pl.num_programs(0)

    def issue(step, slot):
        base = step * T_DISP

        def body(t, _):
            for k in range(TOP_K):
                src = slot_ref[k * N_TOK + base + t]
                pltpu.make_async_copy(ys_ref.at[pl.ds(src, 1)], buf_ref.at[slot, k, pl.ds(t, 1)], sem.at[slot]).start()
            return 0

        lax.fori_loop(0, T_DISP, body, 0)

    @pl.when(i == 0)
    def _():
        issue(0, 0)

    @pl.when(i + 1 < nsteps)
    def _():
        issue(i + 1, (i + 1) % 2)

    cur = i % 2

    def drain(t, _):
        for k in range(TOP_K):
            pltpu.make_async_copy(ys_ref.at[pl.ds(0, 1)], buf_ref.at[cur, k, pl.ds(0, 1)], sem.at[cur]).wait()
        return 0

    lax.fori_loop(0, T_DISP, drain, 0)

    w = w_ref[...]
    moe_lo = jnp.zeros((T_DISP, PACK_W), F32)
    moe_hi = jnp.zeros((T_DISP, PACK_W), F32)
    for k in range(TOP_K):
        lo, hi = _unpack_bf16_pair(buf_ref[cur, k])
        wk = w[:, k:k + 1]
        moe_lo = moe_lo + wk * lo
        moe_hi = moe_hi + wk * hi

    hlo, hhi = _unpack_bf16_pair(hp_ref[...])
    hlo = hlo.astype(BF16)
    hhi = hhi.astype(BF16)
    sg = (jnp.dot(hlo, wsg_ref[0:PACK_W], preferred_element_type=F32)
          + jnp.dot(hhi, wsg_ref[PACK_W:D_MODEL], preferred_element_type=F32))
    su = (jnp.dot(hlo, wsu_ref[0:PACK_W], preferred_element_type=F32)
          + jnp.dot(hhi, wsu_ref[PACK_W:D_MODEL], preferred_element_type=F32))
    shared = jnp.dot((_silu(sg) * su).astype(BF16), wsd_ref[...], preferred_element_type=F32)
    moe = jnp.concatenate([moe_lo, moe_hi], axis=1)
    out = x1_ref[...] + gf_ref[0] * (moe + shared)
    if final_norm:
        out = _rms(out, fn_ref[...])
    o_ref[...] = out


def _combine(slot_flat, ys, w_tok, hp, x1, g_f, wsg, wsu, wsd, final_gain, final_norm):
    per_b = SEQ // T_DISP
    return pl.pallas_call(
        functools.partial(_combine_kernel, final_norm=final_norm),
        grid_spec=pltpu.PrefetchScalarGridSpec(
            num_scalar_prefetch=1,
            grid=(N_TOK // T_DISP,),
            in_specs=[pl.BlockSpec(memory_space=pl.ANY),
                      pl.BlockSpec((T_DISP, LANES), lambda i, s: (i, 0)),
                      pl.BlockSpec((T_DISP, PACK_W), lambda i, s: (i, 0)),
                      pl.BlockSpec((T_DISP, D_MODEL), lambda i, s: (i, 0)),
                      pl.BlockSpec((1, 1, D_MODEL), lambda i, s: (i // per_b, 0, 0)),
                      pl.BlockSpec((D_MODEL, SHARED_FF), lambda i, s: (0, 0)),
                      pl.BlockSpec((D_MODEL, SHARED_FF), lambda i, s: (0, 0)),
                      pl.BlockSpec((SHARED_FF, D_MODEL), lambda i, s: (0, 0)),
                      pl.BlockSpec((1, D_MODEL), lambda i, s: (0, 0))],
            out_specs=pl.BlockSpec((T_DISP, D_MODEL), lambda i, s: (i, 0)),
            scratch_shapes=[pltpu.VMEM((2, TOP_K, T_DISP, PACK_W), U32),
                            pltpu.SemaphoreType.DMA((2,))]),
        out_shape=jax.ShapeDtypeStruct((N_TOK, D_MODEL), F32),
        compiler_params=_cparams(("arbitrary",)),
        name="combine",
    )(slot_flat, ys, w_tok, hp, x1, g_f, wsg, wsu, wsd, final_gain.reshape(1, D_MODEL))


def _moe_layer(hp, logits, x1, g_f, router_bias, w_gate, w_up, w_down, ws_gate, ws_up, ws_down,
               final_gain, final_norm):
    eidx, w_tok, rank, counts = _route(logits, router_bias)
    cnt = counts[:, 0].astype(I32)
    padded = ((cnt + TM_EXP - 1) // TM_EXP) * TM_EXP
    ends = jnp.cumsum(padded)
    offsets = ends - padded
    slot = _slots(eidx, rank, offsets)
    slot_flat = slot.reshape(N_SLOTS)
    blk_start = jnp.arange(NB_EXP, dtype=I32) * TM_EXP
    block_expert = jnp.minimum(jnp.sum((blk_start[:, None] >= ends[None, :]).astype(I32), axis=1), N_EXPERTS - 1)
    block_valid = (blk_start < ends[-1]).astype(I32)
    xs = _dispatch(slot_flat, hp)
    ys = _experts(block_expert, block_valid, xs, w_gate, w_up, w_down)
    return _combine(slot_flat, ys, w_tok, hp, x1, g_f, ws_gate.astype(BF16), ws_up.astype(BF16),
                    ws_down.astype(BF16), final_gain, final_norm)


def _rope_tables(dim):
    inv = ROPE_THETA ** (-jnp.arange(0, dim, 2, dtype=F32) / dim)
    ang = jnp.arange(SEQ, dtype=F32)[:, None] * inv[None, :]
    return jnp.cos(ang), jnp.sin(ang)


def _rot_half_cols(w):
    half = w.shape[-1] // 2
    return jnp.concatenate([-w[..., half:], w[..., :half]], axis=-1)


def _t5_bucket(rel):
    half = REL_BUCKETS // 2
    max_exact = half // 2
    ret = (rel > 0).astype(I32) * half
    n = jnp.abs(rel)
    nf = jnp.maximum(n, 1).astype(F32)
    large = max_exact + (jnp.log(nf / max_exact) / math.log(REL_MAX_DIST / max_exact)
                         * (half - max_exact)).astype(I32)
    large = jnp.minimum(large, half - 1)
    return ret + jnp.where(n < max_exact, n, large)


def _swa_bias_table(rel_bias):
    qi = jnp.arange(WINDOW)[:, None]
    kj = jnp.arange(3 * WINDOW)[None, :]
    rel = kj - WINDOW - qi
    bias = rel_bias[_t5_bucket(rel)].astype(F32)
    bias = jnp.where((jnp.abs(rel) <= WINDOW)[:, :, None], bias, NEG_BIG)
    return bias.transpose(2, 0, 1)


def kernel(x, c, w_mod, b_mod, norm_mix, norm_ffn, final_norm, w_in_ab, q_lat_norm, kv_lat_norm, w_uq, w_ukv, conv_w, conv_b, lru_w_a, lru_b_a, lru_w_x, lru_b_x, lru_lambda, w_out_ab, w_in_cd, ret_gn, swa_sinks, w_out_cd, rel_bias, w_router, router_bias, w_gate, w_up, w_down, ws_gate, ws_up, ws_down):
    xf = x.reshape(N_TOK, D_MODEL)
    mod = _modulation(c, w_mod, b_mod)
    cos_r, sin_r = _rope_tables(MLA_ROPE)
    cs_tab = jnp.concatenate([cos_r, cos_r, sin_r, sin_r], axis=1)
    cos_t, sin_t = _rope_tables(RET_DK)
    lg_ret = jnp.log1p(-(2.0 ** (-5.0 - jnp.arange(RET_HEADS, dtype=F32))))

    for layer in range(DEPTH):
        sh_m, sc_m, g_m, sh_f, sc_f, g_f = [m.reshape(BATCH, 1, D_MODEL) for m in jnp.split(mod[layer], 6, axis=-1)]
        i = layer // 2
        if layer % 2 == 0:
            w = w_in_ab[i]
            o1, o2, o3, o4 = np.cumsum((MLA_Q_RANK, MLA_KV_RANK, MLA_ROPE, LRU_WIDTH)).tolist()
            w_kr = w[:, o2:o3]
            w_in = jnp.concatenate([w[:, :o2], w[:, o3:], w_kr, _rot_half_cols(w_kr)], axis=1).astype(BF16)
            p0 = _in_projection(xf, norm_mix[layer], sc_m, sh_m, w_in, 640, "in_proj_ab")
            wq = w_uq[i].reshape(MLA_Q_RANK, MLA_HEADS, MLA_NOPE + MLA_ROPE)
            wq_r = wq[:, :, MLA_NOPE:]
            wq = jnp.concatenate([wq, _rot_half_cols(wq_r)], axis=-1).reshape(MLA_Q_RANK, MLA_HEADS * MLA_QK)
            q, k, v = _mla_up(p0, q_lat_norm[i], kv_lat_norm[i], wq.astype(BF16), w_ukv[i].astype(BF16), cs_tab)
            a_out = _mla_attention(q, k, v)
            w_gates = jnp.concatenate([lru_w_a[i, 0], lru_w_x[i, 0], lru_w_a[i, 1], lru_w_x[i, 1]], axis=-1).astype(BF16)
            b_gates = jnp.concatenate([b.reshape(LRU_BLOCKS, 1, LRU_BS) for b in
                                       (lru_b_a[i, 0], lru_b_x[i, 0], lru_b_a[i, 1], lru_b_x[i, 1])], axis=-1)
            b_out = _rglru(p0, conv_w[i], conv_b[i], w_gates, b_gates, lru_lambda[i])
            w_out = w_out_ab[i].astype(BF16)
        else:
            p1 = _in_projection(xf, norm_mix[layer], sc_m, sh_m, w_in_cd[i].astype(BF16), 512, "in_proj_cd")
            a_out = _retention(p1, lg_ret, cos_t, sin_t, ret_gn[i])
            b_out = _swa(p1, swa_sinks[i], _swa_bias_table(rel_bias))
            w_out = w_out_cd[i].astype(BF16)
        w_router_pad = jnp.pad(w_router[layer], ((0, 0), (0, LANES - N_EXPERTS)))
        x1, hp, logits = _out_projection(a_out, b_out, w_out, xf, g_m, norm_ffn[layer], sc_f, sh_f, w_router_pad)
        xf = _moe_layer(hp, logits, x1, g_f, router_bias[layer], w_gate[layer], w_up[layer], w_down[layer],
                        ws_gate[layer], ws_up[layer], ws_down[layer], final_norm, layer == DEPTH - 1)
    return xf.reshape(BATCH, SEQ, D_MODEL)
```

```python
import functools
import math

import numpy as np
import jax
import jax.numpy as jnp
from jax import lax
from jax.experimental import pallas as pl
from jax.experimental.pallas import tpu as pltpu

F32 = jnp.float32
BF16 = jnp.bfloat16
I32 = jnp.int32
U32 = jnp.uint32

D_MODEL = 2048
BATCH = 4
SEQ = 2048
DEPTH = 2
N_TOK = BATCH * SEQ
HALF = D_MODEL // 2
MLA_NOPE = 128
MLA_ROPE = 64
MLA_V = 128
MLA_HEADS = HALF // MLA_V
MLA_Q_RANK = D_MODEL // 4
MLA_KV_RANK = D_MODEL // 4
MLA_QK = 256
LRU_WIDTH = HALF
LRU_BLOCKS = 8
LRU_BS = LRU_WIDTH // LRU_BLOCKS
LRU_CONV = 4
LRU_C = 8.0
RET_DK = 256
RET_DV = 256
RET_HEADS = HALF // RET_DV
SWA_HD = 128
SWA_HEADS = HALF // SWA_HD
SWA_KV_HEADS = 2
SWA_G = SWA_HEADS // SWA_KV_HEADS
WINDOW = 128
REL_BUCKETS = 32
REL_MAX_DIST = 128
N_EXPERTS = 64
TOP_K = 8
N_GROUPS = 8
GROUP_SIZE = N_EXPERTS // N_GROUPS
TOP_GROUPS = 4
EXPERT_FF = D_MODEL // 4
SHARED_FF = D_MODEL // 4
ROUTE_SCALE = 2.5
ROPE_THETA = 10000.0
EPS = 1e-6
NEG_BIG = -1e30

LANES = 128
SUBLANES = 8
VMEM_LIMIT = 52 * 2**20

TM_PROJ = 1024
TM_UP = 512
TQ_ATT = 512
TQ_MLA = 2048
TQ_SUB = 256
TM_OUT = 512
TM_OUT_SUB = 256
T_ROUTE = 512
T_DISP = 256
TM_EXP = 512
N_SLOTS = N_TOK * TOP_K
NB_EXP = N_SLOTS // TM_EXP + N_EXPERTS
P_ROWS = NB_EXP * TM_EXP
PACK_W = D_MODEL // 2
TOK_ROWS = PACK_W // LANES
assert TOK_ROWS == SUBLANES

LRU_SEG = 260
LRU_ROWS = SUBLANES * LRU_SEG
assert LRU_ROWS >= SEQ and LRU_SEG % 8 == 4


def _cparams(sem, vmem=VMEM_LIMIT):
    return pltpu.CompilerParams(dimension_semantics=sem, vmem_limit_bytes=vmem)


def _silu(x):
    return x * jax.nn.sigmoid(x)


def _rms(x, g):
    return x * lax.rsqrt(jnp.mean(x * x, axis=-1, keepdims=True) + EPS) * g


def _pack_bf16_pair(lo, hi):
    lo_b = lax.bitcast_convert_type(lo.astype(BF16).astype(F32), U32)
    hi_b = lax.bitcast_convert_type(hi.astype(BF16).astype(F32), U32)
    return (hi_b & jnp.uint32(0xFFFF0000)) | (lo_b >> 16)


def _unpack_bf16_pair(w):
    lo = lax.bitcast_convert_type(w << 16, F32)
    hi = lax.bitcast_convert_type(w & jnp.uint32(0xFFFF0000), F32)
    return lo, hi


def _store_token_tiles(ref, packed, tok0=0):
    t = packed.shape[0]
    for s in range(TOK_ROWS):
        ref[pl.ds(tok0 * TOK_ROWS + s, t, stride=TOK_ROWS), :] = packed[:, s * LANES:(s + 1) * LANES]


def _load_token_tiles(ref, t):
    return jnp.concatenate([ref[pl.ds(s, t, stride=TOK_ROWS), :] for s in range(TOK_ROWS)], axis=1)


def _mod_kernel(c_ref, w_ref, b_ref, o_ref):
    c = c_ref[...]
    ca = _silu(c).astype(BF16)
    o_ref[0] = jnp.dot(ca, w_ref[0].astype(BF16), preferred_element_type=F32) + b_ref[0]


def _modulation(c, w_mod, b_mod):
    tn = 1024
    cp = jnp.pad(c, ((0, SUBLANES - BATCH), (0, 0)))
    out = pl.pallas_call(
        _mod_kernel,
        grid=(DEPTH, 6 * D_MODEL // tn),
        in_specs=[pl.BlockSpec((SUBLANES, D_MODEL), lambda l, j: (0, 0)),
                  pl.BlockSpec((1, D_MODEL, tn), lambda l, j: (l, 0, j)),
                  pl.BlockSpec((1, 1, tn), lambda l, j: (l, 0, j))],
        out_specs=pl.BlockSpec((1, SUBLANES, tn), lambda l, j: (l, 0, j)),
        out_shape=jax.ShapeDtypeStruct((DEPTH, SUBLANES, 6 * D_MODEL), F32),
        compiler_params=_cparams(("parallel", "parallel")),
        name="adaln_mod",
    )(cp, w_mod, b_mod.reshape(DEPTH, 1, 6 * D_MODEL))
    return out[:, :BATCH]


def _inproj_kernel(x_ref, g_ref, sc_ref, sh_ref, w_ref, o_ref, h_ref):
    @pl.when(pl.program_id(1) == 0)
    def _():
        y = _rms(x_ref[...], g_ref[...])
        h_ref[...] = (y * (1.0 + sc_ref[0]) + sh_ref[0]).astype(BF16)

    o_ref[...] = jnp.dot(h_ref[...], w_ref[...], preferred_element_type=F32)


def _in_projection(x, gain, scale, shift, w_bf16, tn, name):
    p = w_bf16.shape[1]
    per_b = SEQ // TM_PROJ
    return pl.pallas_call(
        _inproj_kernel,
        grid=(N_TOK // TM_PROJ, p // tn),
        in_specs=[pl.BlockSpec((TM_PROJ, D_MODEL), lambda i, j: (i, 0)),
                  pl.BlockSpec((1, D_MODEL), lambda i, j: (0, 0)),
                  pl.BlockSpec((1, 1, D_MODEL), lambda i, j: (i // per_b, 0, 0)),
                  pl.BlockSpec((1, 1, D_MODEL), lambda i, j: (i // per_b, 0, 0)),
                  pl.BlockSpec((D_MODEL, tn), lambda i, j: (0, j))],
        out_specs=pl.BlockSpec((TM_PROJ, tn), lambda i, j: (i, j)),
        out_shape=jax.ShapeDtypeStruct((N_TOK, p), F32),
        scratch_shapes=[pltpu.VMEM((TM_PROJ, D_MODEL), BF16)],
        compiler_params=_cparams(("parallel", "arbitrary")),
        name=name,
    )(x, gain.reshape(1, D_MODEL), scale, shift, w_bf16)


def _mla_up_kernel(ql_ref, kvl_ref, kr_ref, qn_ref, kvn_ref, wq_ref, wkv_ref, cs_ref, q_ref, k_ref, v_ref):
    scale = (MLA_NOPE + MLA_ROPE) ** -0.5 * math.log2(math.e)
    hq = _rms(ql_ref[...], qn_ref[...]).astype(BF16)
    hkv = _rms(kvl_ref[...], kvn_ref[...]).astype(BF16)
    yq = jnp.dot(hq, wq_ref[...], preferred_element_type=F32) * scale
    ykv = jnp.dot(hkv, wkv_ref[...], preferred_element_type=F32)
    cs = cs_ref[...]
    lane = lax.broadcasted_iota(I32, cs.shape, 1)

    def rope_sum(blk):
        z = blk * cs
        return z + pltpu.roll(z, MLA_ROPE, 1)

    kr = jnp.where(lane < MLA_ROPE, rope_sum(kr_ref[...]), 0.0).astype(BF16)
    ones_col = jnp.where(lane == 0, 1.0, 0.0).astype(BF16)
    for h in range(MLA_HEADS):
        c0 = h * MLA_QK
        q_ref[0, h, :, 0:MLA_NOPE] = yq[:, c0:c0 + MLA_NOPE].astype(BF16)
        q_ref[0, h, :, MLA_NOPE:MLA_QK] = rope_sum(yq[:, c0 + MLA_NOPE:c0 + MLA_QK]).astype(BF16)
        k_ref[0, h, :, 0:MLA_NOPE] = ykv[:, c0:c0 + MLA_NOPE].astype(BF16)
        k_ref[0, h, :, MLA_NOPE:MLA_QK] = kr
        v_ref[0, h, :, 0:MLA_V] = ykv[:, c0 + MLA_NOPE:c0 + MLA_QK].astype(BF16)
        v_ref[0, h, :, MLA_V:2 * MLA_V] = ones_col


def _mla_up(p0, q_norm, kv_norm, wq, wkv, cs_tab):
    per_b = SEQ // TM_UP
    qk_shape = jax.ShapeDtypeStruct((BATCH, MLA_HEADS, SEQ, MLA_QK), BF16)
    return pl.pallas_call(
        _mla_up_kernel,
        grid=(N_TOK // TM_UP,),
        in_specs=[pl.BlockSpec((TM_UP, MLA_Q_RANK), lambda i: (i, 0)),
                  pl.BlockSpec((TM_UP, MLA_KV_RANK), lambda i: (i, 1)),
                  pl.BlockSpec((TM_UP, LANES), lambda i: (i, 24)),
                  pl.BlockSpec((1, MLA_Q_RANK), lambda i: (0, 0)),
                  pl.BlockSpec((1, MLA_KV_RANK), lambda i: (0, 0)),
                  pl.BlockSpec((MLA_Q_RANK, MLA_HEADS * MLA_QK), lambda i: (0, 0)),
                  pl.BlockSpec((MLA_KV_RANK, MLA_HEADS * MLA_QK), lambda i: (0, 0)),
                  pl.BlockSpec((TM_UP, LANES), lambda i: (i % per_b, 0))],
        out_specs=[pl.BlockSpec((1, MLA_HEADS, TM_UP, MLA_QK), lambda i: (i // per_b, 0, i % per_b, 0)),
                   pl.BlockSpec((1, MLA_HEADS, TM_UP, MLA_QK), lambda i: (i // per_b, 0, i % per_b, 0)),
                   pl.BlockSpec((1, MLA_HEADS, TM_UP, 2 * MLA_V), lambda i: (i // per_b, 0, i % per_b, 0))],
        out_shape=[qk_shape, qk_shape, jax.ShapeDtypeStruct((BATCH, MLA_HEADS, SEQ, 2 * MLA_V), BF16)],
        compiler_params=_cparams(("parallel",)),
        name="mla_up",
    )(p0, p0, p0, q_norm.reshape(1, -1), kv_norm.reshape(1, -1), wq, wkv, cs_tab)


def _mla_attn_kernel(q_ref, k_ref, v_ref, o_ref):
    k = k_ref[0, 0]
    v = v_ref[0, 0]
    for r in range(TQ_MLA // TQ_SUB):
        rows = pl.ds(r * TQ_SUB, TQ_SUB)
        s = lax.dot_general(q_ref[0, 0, rows, :], k, (((1,), (1,)), ((), ())), preferred_element_type=F32)
        p = jnp.exp2(s - jnp.max(s, axis=-1, keepdims=True))
        o = jnp.dot(p.astype(BF16), v, preferred_element_type=F32)
        o_ref[rows, :] = (o[:, 0:MLA_V] / o[:, MLA_V:MLA_V + 1]).astype(BF16)


def _mla_attention(q, k, v):
    nq = SEQ // TQ_MLA
    return pl.pallas_call(
        _mla_attn_kernel,
        grid=(BATCH, MLA_HEADS, nq),
        in_specs=[pl.BlockSpec((1, 1, TQ_MLA, MLA_QK), lambda b, h, i: (b, h, i, 0)),
                  pl.BlockSpec((1, 1, SEQ, MLA_QK), lambda b, h, i: (b, h, 0, 0)),
                  pl.BlockSpec((1, 1, SEQ, 2 * MLA_V), lambda b, h, i: (b, h, 0, 0))],
        out_specs=pl.BlockSpec((TQ_MLA, MLA_V), lambda b, h, i: (b * nq + i, h)),
        out_shape=jax.ShapeDtypeStruct((N_TOK, HALF), BF16),
        compiler_params=_cparams(("parallel", "parallel", "parallel")),
        name="mla_attn",
    )(q, k, v)


def _lru_kernel(x_ref, gate_ref, cw_ref, cb_ref, wg_ref, bg_ref, lam_ref, o_ref,
                af_ref, uf_ref, ab_ref, ub_ref, hf_ref, pf_ref, hb_ref, pb_ref, hs_ref):
    x = x_ref[...]
    row = lax.broadcasted_iota(I32, x.shape, 0)

    def shifted(d):
        r = pltpu.roll(x, (-d) % SEQ, 0)
        return jnp.where((row + d >= 0) & (row + d < SEQ), r, 0.0)

    cw = cw_ref[...]
    left = LRU_CONV // 2
    xc = cb_ref[...]
    for kk in range(LRU_CONV):
        d = kk - left
        xc = xc + cw[kk:kk + 1] * (x if d == 0 else shifted(d))

    gates = jnp.dot(xc.astype(BF16), wg_ref[0], preferred_element_type=F32) + bg_ref[0]
    lam = lam_ref[...]
    z = -lam
    sp = jnp.maximum(z, 0.0) + jnp.log1p(jnp.exp(-jnp.abs(z)))
    pad_rows = LRU_ROWS - SEQ
    for d, (a_ref, u_ref) in enumerate(((af_ref, uf_ref), (ab_ref, ub_ref))):
        r = jax.nn.sigmoid(gates[:, d * 256:d * 256 + LRU_BS])
        i = jax.nn.sigmoid(gates[:, d * 256 + LRU_BS:(d + 1) * 256])
        log_a = -LRU_C * r * sp[d:d + 1]
        a_ref[0:SEQ] = jnp.exp(log_a)
        u_ref[0:SEQ] = jnp.sqrt(1.0 - jnp.exp(2.0 * log_a)) * (i * xc)
        a_ref[SEQ:LRU_ROWS] = jnp.zeros((pad_rows, LANES), F32)
        u_ref[SEQ:LRU_ROWS] = jnp.zeros((pad_rows, LANES), F32)

    ones = jnp.ones((SUBLANES, LANES), F32)
    zeros = jnp.zeros((SUBLANES, LANES), F32)

    def seg(t):
        return pl.ds(t, SUBLANES, stride=LRU_SEG)

    def local_scan(s, carry):
        p_f, h_f, p_b, h_b = carry
        tf = s
        tb = LRU_SEG - 1 - s
        a = af_ref[seg(tf)]
        h_f = a * h_f + uf_ref[seg(tf)]
        p_f = a * p_f
        hf_ref[seg(tf)] = h_f
        pf_ref[seg(tf)] = p_f
        a = ab_ref[seg(tb)]
        h_b = a * h_b + ub_ref[seg(tb)]
        p_b = a * p_b
        hb_ref[seg(tb)] = h_b
        pb_ref[seg(tb)] = p_b
        return p_f, h_f, p_b, h_b

    p_f, h_f, p_b, h_b = lax.fori_loop(0, LRU_SEG, local_scan, (ones, zeros, ones, zeros), unroll=4)

    rows_f = []
    c = jnp.zeros((1, LANES), F32)
    for j in range(SUBLANES):
        rows_f.append(c)
        c = p_f[j:j + 1] * c + h_f[j:j + 1]
    rows_b = [None] * SUBLANES
    c = jnp.zeros((1, LANES), F32)
    for j in range(SUBLANES - 1, -1, -1):
        rows_b[j] = c
        c = p_b[j:j + 1] * c + h_b[j:j + 1]
    sub = lax.broadcasted_iota(I32, (SUBLANES, LANES), 0)
    c_f = zeros
    c_b = zeros
    for j in range(SUBLANES):
        c_f = jnp.where(sub == j, rows_f[j], c_f)
        c_b = jnp.where(sub == j, rows_b[j], c_b)

    def fixup(t, _):
        hs_ref[seg(t)] = (hf_ref[seg(t)] + pf_ref[seg(t)] * c_f) + (hb_ref[seg(t)] + pb_ref[seg(t)] * c_b)
        return 0

    lax.fori_loop(0, LRU_SEG, fixup, 0, unroll=4)

    g = gate_ref[...]
    gelu = 0.5 * g * (1.0 + jnp.tanh(math.sqrt(2.0 / math.pi) * (g + 0.044715 * (g * g * g))))
    o_ref[...] = (gelu * hs_ref[0:SEQ]).astype(BF16)


def _rglru(p0, conv_w, conv_b, w_gates, b_gates, lam):
    scan_buf = pltpu.VMEM((LRU_ROWS, LANES), F32)
    return pl.pallas_call(
        _lru_kernel,
        grid=(BATCH, LRU_BLOCKS),
        in_specs=[pl.BlockSpec((SEQ, LRU_BS), lambda b, g: (b, 8 + g)),
                  pl.BlockSpec((SEQ, LRU_BS), lambda b, g: (b, 16 + g)),
                  pl.BlockSpec((LRU_CONV, LRU_BS), lambda b, g: (0, g)),
                  pl.BlockSpec((1, LRU_BS), lambda b, g: (0, g)),
                  pl.BlockSpec((1, LRU_BS, 4 * LRU_BS), lambda b, g: (g, 0, 0)),
                  pl.BlockSpec((1, 1, 4 * LRU_BS), lambda b, g: (g, 0, 0)),
                  pl.BlockSpec((2, LRU_BS), lambda b, g: (0, g))],
        out_specs=pl.BlockSpec((SEQ, LRU_BS), lambda b, g: (b, g)),
        out_shape=jax.ShapeDtypeStruct((N_TOK, LRU_WIDTH), BF16),
        scratch_shapes=[scan_buf] * 9,
        compiler_params=_cparams(("parallel", "parallel")),
        name="rglru",
    )(p0, p0, conv_w, conv_b.reshape(1, -1), w_gates, b_gates, lam)


def _ret_kernel(lg_ref, q_ref, k_ref, v_ref, g_ref, cq_ref, sq_ref, ck_ref, sk_ref, gn_ref, o_ref, ks_ref, vs_ref):
    h = pl.program_id(1)
    qi = pl.program_id(2)
    half = RET_DK // 2

    def rope(t, c, s):
        t1, t2 = t[:, :half], t[:, half:]
        return jnp.concatenate([t1 * c - t2 * s, t2 * c + t1 * s], axis=1)

    @pl.when(qi == 0)
    def _():
        ks_ref[...] = (rope(k_ref[...], ck_ref[...], sk_ref[...]) * (RET_DK ** -0.5)).astype(BF16)
        vs_ref[...] = v_ref[...].astype(BF16)

    q = rope(q_ref[...], cq_ref[...], sq_ref[...]).astype(BF16)
    s = lax.dot_general(q, ks_ref[...], (((1,), (1,)), ((), ())), preferred_element_type=F32)
    n = qi * TQ_ATT + lax.broadcasted_iota(I32, s.shape, 0)
    m = lax.broadcasted_iota(I32, s.shape, 1)
    d = (n - m).astype(F32)
    lg_f = lg_ref[h]
    lg_b = lg_ref[RET_HEADS - 1 - h]
    dec = jnp.exp(jnp.where(d >= 0.0, lg_f * d, -lg_b * d))
    o = jnp.dot((s * dec).astype(BF16), vs_ref[...], preferred_element_type=F32)
    y = _rms(o, gn_ref[0])
    o_ref[...] = (_silu(g_ref[...]) * y).astype(BF16)


def _retention(p1, lg, cos_t, sin_t, ret_gn):
    nq = SEQ // TQ_ATT
    half = RET_DK // 2
    return pl.pallas_call(
        _ret_kernel,
        grid=(BATCH, RET_HEADS, nq),
        in_specs=[pl.BlockSpec(memory_space=pltpu.SMEM),
                  pl.BlockSpec((TQ_ATT, RET_DK), lambda b, h, i: (b * nq + i, h)),
                  pl.BlockSpec((SEQ, RET_DK), lambda b, h, i: (b, RET_HEADS + h)),
                  pl.BlockSpec((SEQ, RET_DV), lambda b, h, i: (b, 2 * RET_HEADS + h)),
                  pl.BlockSpec((TQ_ATT, RET_DV), lambda b, h, i: (b * nq + i, 3 * RET_HEADS + h)),
                  pl.BlockSpec((TQ_ATT, half), lambda b, h, i: (i, 0)),
                  pl.BlockSpec((TQ_ATT, half), lambda b, h, i: (i, 0)),
                  pl.BlockSpec((SEQ, half), lambda b, h, i: (0, 0)),
                  pl.BlockSpec((SEQ, half), lambda b, h, i: (0, 0)),
                  pl.BlockSpec((1, 1, RET_DV), lambda b, h, i: (h, 0, 0))],
        out_specs=pl.BlockSpec((TQ_ATT, RET_DV), lambda b, h, i: (b * nq + i, h)),
        out_shape=jax.ShapeDtypeStruct((N_TOK, HALF), BF16),
        scratch_shapes=[pltpu.VMEM((SEQ, RET_DK), BF16), pltpu.VMEM((SEQ, RET_DV), BF16)],
        compiler_params=_cparams(("parallel", "parallel", "arbitrary")),
        name="retention",
    )(lg, p1, p1, p1, p1, cos_t, sin_t, cos_t, sin_t, ret_gn.reshape(RET_HEADS, 1, RET_DV))


def _swa_kernel(sink_ref, q_ref, k_ref, v_ref, bias_ref, o_ref):
    kv = pl.program_id(1)
    n = pl.program_id(2)
    nb = SEQ // WINDOW
    w = WINDOW
    prev = jnp.maximum(n - 1, 0)
    nxt = jnp.minimum(n + 1, nb - 1)

    def rows(ref, blk):
        return ref[pl.ds(pl.multiple_of(blk * w, w), w), :].astype(BF16)

    kw = jnp.concatenate([rows(k_ref, prev), rows(k_ref, n), rows(k_ref, nxt)], axis=0)
    vw = jnp.concatenate([rows(v_ref, prev), rows(v_ref, n), rows(v_ref, nxt)], axis=0)
    qb = q_ref[...]
    q4 = jnp.concatenate([qb[:, g * SWA_HD:(g + 1) * SWA_HD] for g in range(SWA_G)], axis=0).astype(BF16)
    s = lax.dot_general(q4, kw, (((1,), (1,)), ((), ())), preferred_element_type=F32) * (SWA_HD ** -0.5)
    col = lax.broadcasted_iota(I32, (w, 3 * w), 1)
    outside = ((col < w) & (n == 0)) | ((col >= 2 * w) & (n == nb - 1))
    for g in range(SWA_G):
        sg = jnp.where(outside, NEG_BIG, s[g * w:(g + 1) * w] + bias_ref[g])
        sink = sink_ref[kv * SWA_G + g]
        m = jnp.maximum(jnp.max(sg, axis=-1, keepdims=True), sink)
        p = jnp.exp(sg - m)
        denom = jnp.sum(p, axis=-1, keepdims=True) + jnp.exp(sink - m)
        o = jnp.dot((p / denom).astype(BF16), vw, preferred_element_type=F32)
        o_ref[:, g * SWA_HD:(g + 1) * SWA_HD] = o.astype(BF16)


def _swa(p1, sinks, bias):
    nb = SEQ // WINDOW
    qcols = SWA_G * SWA_HD
    q_blk0 = (4 * RET_HEADS * RET_DK) // qcols
    k_blk0 = (4 * RET_HEADS * RET_DK + SWA_HEADS * SWA_HD) // SWA_HD
    v_blk0 = k_blk0 + SWA_KV_HEADS
    return pl.pallas_call(
        _swa_kernel,
        grid=(BATCH, SWA_KV_HEADS, nb),
        in_specs=[pl.BlockSpec(memory_space=pltpu.SMEM),
                  pl.BlockSpec((WINDOW, qcols), lambda b, kv, n: (b * nb + n, q_blk0 + kv)),
                  pl.BlockSpec((SEQ, SWA_HD), lambda b, kv, n: (b, k_blk0 + kv)),
                  pl.BlockSpec((SEQ, SWA_HD), lambda b, kv, n: (b, v_blk0 + kv)),
                  pl.BlockSpec((SWA_G, WINDOW, 3 * WINDOW), lambda b, kv, n: (kv, 0, 0))],
        out_specs=pl.BlockSpec((WINDOW, qcols), lambda b, kv, n: (b * nb + n, kv)),
        out_shape=jax.ShapeDtypeStruct((N_TOK, HALF), BF16),
        compiler_params=_cparams(("parallel", "parallel", "parallel")),
        name="swa",
    )(sinks, p1, p1, p1, bias)


def _outproj_kernel(a_ref, b_ref, wa_ref, wb_ref, x_ref, gm_ref, g_ref, sc_ref, sh_ref, wr_ref,
                    x1_ref, hp_ref, lg_ref):
    wr = wr_ref[...]
    for r in range(TM_OUT // TM_OUT_SUB):
        rows = pl.ds(r * TM_OUT_SUB, TM_OUT_SUB)
        mixed = (jnp.dot(a_ref[rows, :], wa_ref[...], preferred_element_type=F32)
                 + jnp.dot(b_ref[rows, :], wb_ref[...], preferred_element_type=F32))
        x1 = x_ref[rows, :] + gm_ref[0] * mixed
        x1_ref[rows, :] = x1
        hf = _rms(x1, g_ref[...]) * (1.0 + sc_ref[0]) + sh_ref[0]
        _store_token_tiles(hp_ref, _pack_bf16_pair(hf[:, :PACK_W], hf[:, PACK_W:]), r * TM_OUT_SUB)
        h_hi = hf.astype(BF16)
        h_lo = (hf - h_hi.astype(F32)).astype(BF16)
        t_hi = jnp.dot(h_hi, wr, preferred_element_type=F32)
        t_lo = jnp.dot(h_lo, wr, preferred_element_type=F32)
        lg_ref[rows, :] = (t_hi[:, :LANES] + t_hi[:, LANES:]) + (t_lo[:, :LANES] + t_lo[:, LANES:])


def _out_projection(a, b, w_out_bf16, x, g_m, gain, scale, shift, w_router_pad):
    per_b = SEQ // TM_OUT
    vec = pl.BlockSpec((1, 1, D_MODEL), lambda i: (i // per_b, 0, 0))
    return pl.pallas_call(
        _outproj_kernel,
        grid=(N_TOK // TM_OUT,),
        in_specs=[pl.BlockSpec((TM_OUT, HALF), lambda i: (i, 0)),
                  pl.BlockSpec((TM_OUT, HALF), lambda i: (i, 0)),
                  pl.BlockSpec((HALF, D_MODEL), lambda i: (0, 0)),
                  pl.BlockSpec((HALF, D_MODEL), lambda i: (1, 0)),
                  pl.BlockSpec((TM_OUT, D_MODEL), lambda i: (i, 0)),
                  vec,
                  pl.BlockSpec((1, D_MODEL), lambda i: (0, 0)),
                  vec, vec,
                  pl.BlockSpec((D_MODEL, 2 * LANES), lambda i: (0, 0))],
        out_specs=[pl.BlockSpec((TM_OUT, D_MODEL), lambda i: (i, 0)),
                   pl.BlockSpec((TM_OUT * TOK_ROWS, LANES), lambda i: (i, 0)),
                   pl.BlockSpec((TM_OUT, LANES), lambda i: (i, 0))],
        out_shape=[jax.ShapeDtypeStruct((N_TOK, D_MODEL), F32),
                   jax.ShapeDtypeStruct((N_TOK * TOK_ROWS, LANES), U32),
                   jax.ShapeDtypeStruct((N_TOK, LANES), F32)],
        compiler_params=_cparams(("parallel",)),
        name="out_proj",
    )(a, b, w_out_bf16, w_out_bf16, x, g_m, gain.reshape(1, D_MODEL), scale, shift, w_router_pad)


def _route_kernel(lg_ref, bias_ref, eidx_ref, w_ref, rank_ref, cnt_ref):
    step = pl.program_id(0)

    @pl.when(step == 0)
    def _():
        cnt_ref[...] = jnp.zeros(cnt_ref.shape, F32)

    t = T_ROUTE
    scores = jax.nn.sigmoid(lg_ref[...].T[:N_EXPERTS])
    biased = scores + bias_ref[...]
    sub = lax.broadcasted_iota(I32, (GROUP_SIZE, t), 0).astype(F32)
    ninf = -jnp.inf

    def first_argmax(v, idx, n):
        m = jnp.max(v, axis=0, keepdims=True)
        return m, jnp.min(jnp.where(v == m, idx, float(n)), axis=0, keepdims=True)

    gs = []
    for g in range(N_GROUPS):
        bg = biased[g * GROUP_SIZE:(g + 1) * GROUP_SIZE]
        m1, i1 = first_argmax(bg, sub, GROUP_SIZE)
        m2 = jnp.max(jnp.where(sub == i1, ninf, bg), axis=0, keepdims=True)
        gs.append(m1 + m2)
    cur = jnp.concatenate(gs, axis=0)

    gmask = jnp.zeros((N_GROUPS, t), F32)
    for _ in range(TOP_GROUPS):
        _, i = first_argmax(cur, sub, N_GROUPS)
        pick = sub == i
        gmask = jnp.where(pick, 1.0, gmask)
        cur = jnp.where(pick, ninf, cur)

    eid = lax.broadcasted_iota(I32, (N_EXPERTS, t), 0).astype(F32)
    emask = jnp.concatenate([jnp.broadcast_to(gmask[g:g + 1], (GROUP_SIZE, t)) for g in range(N_GROUPS)], axis=0)
    cur = jnp.where(emask > 0.5, biased, ninf)
    sels, idxs, ws = [], [], []
    onehot = jnp.zeros((N_EXPERTS, t), F32)
    for _ in range(TOP_K):
        _, i = first_argmax(cur, eid, N_EXPERTS)
        pick = eid == i
        sels.append(pick)
        idxs.append(i)
        ws.append(jnp.sum(jnp.where(pick, scores, 0.0), axis=0, keepdims=True))
        onehot = jnp.where(pick, 1.0, onehot)
        cur = jnp.where(pick, ninf, cur)
    wsum = ws[0]
    for k in range(1, TOP_K):
        wsum = wsum + ws[k]

    r = lax.broadcasted_iota(I32, (t, t), 0)
    c = lax.broadcasted_iota(I32, (t, t), 1)
    tri = (r < c).astype(BF16)
    before = jnp.dot(onehot.astype(BF16), tri, preferred_element_type=F32) + cnt_ref[:, 0:1]
    ranks = [jnp.sum(jnp.where(sels[k], before, 0.0), axis=0, keepdims=True) for k in range(TOP_K)]

    eidx_ref[...] = jnp.concatenate(idxs, axis=0).astype(I32)
    rank_ref[...] = jnp.concatenate(ranks, axis=0).astype(I32)
    wk = jnp.concatenate([w / wsum * ROUTE_SCALE for w in ws], axis=0)
    wfull = jnp.concatenate([wk, jnp.zeros((LANES - TOP_K, t), F32)], axis=0)
    w_ref[...] = wfull.T
    cnt_ref[...] = cnt_ref[...] + jnp.sum(onehot, axis=1, keepdims=True)


def _route(logits, router_bias):
    return pl.pallas_call(
        _route_kernel,
        grid=(N_TOK // T_ROUTE,),
        in_specs=[pl.BlockSpec((T_ROUTE, LANES), lambda i: (i, 0)),
                  pl.BlockSpec((N_EXPERTS, 1), lambda i: (0, 0))],
        out_specs=[pl.BlockSpec((TOP_K, T_ROUTE), lambda i: (0, i)),
                   pl.BlockSpec((T_ROUTE, LANES), lambda i: (i, 0)),
                   pl.BlockSpec((TOP_K, T_ROUTE), lambda i: (0, i)),
                   pl.BlockSpec((N_EXPERTS, LANES), lambda i: (0, 0))],
        out_shape=[jax.ShapeDtypeStruct((TOP_K, N_TOK), I32),
                   jax.ShapeDtypeStruct((N_TOK, LANES), F32),
                   jax.ShapeDtypeStruct((TOP_K, N_TOK), I32),
                   jax.ShapeDtypeStruct((N_EXPERTS, LANES), F32)],
        compiler_params=_cparams(("arbitrary",)),
        name="route",
    )(logits, router_bias.reshape(N_EXPERTS, 1))


def _slot_kernel(eidx_ref, rank_ref, off_ref, slot_ref):
    eidx = eidx_ref[...]
    off = off_ref[...]
    t = eidx.shape[1]
    eid = lax.broadcasted_iota(I32, (N_EXPERTS, t), 0)
    base = [jnp.sum(jnp.where(eid == eidx[k:k + 1], off, 0.0), axis=0, keepdims=True) for k in range(TOP_K)]
    slot_ref[...] = jnp.concatenate(base, axis=0).astype(I32) + rank_ref[...]


def _slots(eidx, rank, offsets):
    return pl.pallas_call(
        _slot_kernel,
        grid=(N_TOK // T_ROUTE,),
        in_specs=[pl.BlockSpec((TOP_K, T_ROUTE), lambda i: (0, i)),
                  pl.BlockSpec((TOP_K, T_ROUTE), lambda i: (0, i)),
                  pl.BlockSpec((N_EXPERTS, 1), lambda i: (0, 0))],
        out_specs=pl.BlockSpec((TOP_K, T_ROUTE), lambda i: (0, i)),
        out_shape=jax.ShapeDtypeStruct((TOP_K, N_TOK), I32),
        compiler_params=_cparams(("parallel",)),
        name="slots",
    )(eidx, rank, offsets.astype(F32).reshape(N_EXPERTS, 1))


def _dispatch_kernel(slot_ref, tail_ref, hp_ref, xs_ref, zero_ref, sem, zsem):
    step = pl.program_id(0)

    @pl.when(step == 0)
    def _():
        zero_ref[...] = jnp.zeros(zero_ref.shape, U32)

        def block_copy(b):
            start = pl.multiple_of(b * (TM_EXP * TOK_ROWS), TM_EXP * TOK_ROWS)
            return pltpu.make_async_copy(zero_ref, xs_ref.at[pl.ds(start, TM_EXP * TOK_ROWS)], zsem)

        def fill(b, _):
            @pl.when(tail_ref[b] > 0)
            def _():
                block_copy(b).start()
            return 0

        def fill_wait(b, _):
            @pl.when(tail_ref[b] > 0)
            def _():
                block_copy(b).wait()
            return 0

        lax.fori_loop(0, NB_EXP, fill, 0)
        lax.fori_loop(0, NB_EXP, fill_wait, 0)

    base = step * T_DISP

    def issue(t, _):
        for k in range(TOP_K):
            dst = pl.multiple_of(slot_ref[k * N_TOK + base + t] * TOK_ROWS, TOK_ROWS)
            src = pl.multiple_of(t * TOK_ROWS, TOK_ROWS)
            pltpu.make_async_copy(hp_ref.at[pl.ds(src, TOK_ROWS)], xs_ref.at[pl.ds(dst, TOK_ROWS)],
                                  sem).start(priority=k % 2)
        return 0

    lax.fori_loop(0, T_DISP, issue, 0, unroll=2)
    for k in range(TOP_K):
        pltpu.make_async_copy(hp_ref, xs_ref.at[pl.ds(0, T_DISP * TOK_ROWS)], sem).wait()


def _dispatch(slot_flat, tail_start, hp):
    return pl.pallas_call(
        _dispatch_kernel,
        grid_spec=pltpu.PrefetchScalarGridSpec(
            num_scalar_prefetch=2,
            grid=(N_TOK // T_DISP,),
            in_specs=[pl.BlockSpec((T_DISP * TOK_ROWS, LANES), lambda i, s, t: (i, 0))],
            out_specs=pl.BlockSpec(memory_space=pl.ANY),
            scratch_shapes=[pltpu.VMEM((TM_EXP * TOK_ROWS, LANES), U32),
                            pltpu.SemaphoreType.DMA,
                            pltpu.SemaphoreType.DMA]),
        out_shape=jax.ShapeDtypeStruct((P_ROWS * TOK_ROWS, LANES), U32),
        compiler_params=_cparams(("arbitrary",)),
        name="dispatch",
    )(slot_flat, tail_start, hp)


def _expert_kernel(be_ref, bv_ref, xs_ref, wg_ref, wu_ref, wd_ref, ys_ref, wgb_ref, wub_ref, wdb_ref):
    i = pl.program_id(0)
    changed = jnp.logical_or(i == 0, be_ref[i] != be_ref[jnp.maximum(i - 1, 0)])

    @pl.when(changed)
    def _():
        wgb_ref[...] = wg_ref[0, 0].astype(BF16)
        wub_ref[...] = wu_ref[0, 0].astype(BF16)
        wdb_ref[...] = wd_ref[0, 0].astype(BF16)

    @pl.when(bv_ref[i] > 0)
    def _():
        lo, hi = _unpack_bf16_pair(_load_token_tiles(xs_ref, TM_EXP))
        lo = lo.astype(BF16)
        hi = hi.astype(BF16)
        hg = (jnp.dot(lo, wgb_ref[0:PACK_W], preferred_element_type=F32)
              + jnp.dot(hi, wgb_ref[PACK_W:D_MODEL], preferred_element_type=F32))
        hu = (jnp.dot(lo, wub_ref[0:PACK_W], preferred_element_type=F32)
              + jnp.dot(hi, wub_ref[PACK_W:D_MODEL], preferred_element_type=F32))
        act = (_silu(hg) * hu).astype(BF16)
        y = jnp.dot(act, wdb_ref[...], preferred_element_type=F32)
        _store_token_tiles(ys_ref, _pack_bf16_pair(y[:, :PACK_W], y[:, PACK_W:]))

    @pl.when(bv_ref[i] == 0)
    def _():
        ys_ref[...] = jnp.zeros(ys_ref.shape, U32)


def _experts(layer, block_expert, block_valid, xs, w_gate, w_up, w_down):
    return pl.pallas_call(
        _expert_kernel,
        grid_spec=pltpu.PrefetchScalarGridSpec(
            num_scalar_prefetch=2,
            grid=(NB_EXP,),
            in_specs=[pl.BlockSpec((TM_EXP * TOK_ROWS, LANES), lambda i, be, bv: (i * bv[i], 0)),
                      pl.BlockSpec((1, 1, D_MODEL, EXPERT_FF), lambda i, be, bv: (layer, be[i], 0, 0)),
                      pl.BlockSpec((1, 1, D_MODEL, EXPERT_FF), lambda i, be, bv: (layer, be[i], 0, 0)),
                      pl.BlockSpec((1, 1, EXPERT_FF, D_MODEL), lambda i, be, bv: (layer, be[i], 0, 0))],
            out_specs=pl.BlockSpec((TM_EXP * TOK_ROWS, LANES), lambda i, be, bv: (i, 0)),
            scratch_shapes=[pltpu.VMEM((D_MODEL, EXPERT_FF), BF16),
                            pltpu.VMEM((D_MODEL, EXPERT_FF), BF16),
                            pltpu.VMEM((EXPERT_FF, D_MODEL), BF16)]),
        out_shape=jax.ShapeDtypeStruct((P_ROWS * TOK_ROWS, LANES), U32),
        compiler_params=_cparams(("arbitrary",)),
        name="experts",
    )(block_expert, block_valid, xs, w_gate, w_up, w_down)


def _combine_kernel(slot_ref, ys_ref, w_ref, hp_ref, x1_ref, gf_ref, wsg_ref, wsu_ref, wsd_ref, fn_ref, o_ref,
                    buf_ref, sem, *, final_norm):
    i = pl.program_id(0)
    nsteps = pl.num_programs(0)

    def issue(step, slot):
        base = step * T_DISP

        def body(t, _):
            for k in range(TOP_K):
                src = pl.multiple_of(slot_ref[k * N_TOK + base + t] * TOK_ROWS, TOK_ROWS)
                dst = pl.multiple_of(t * TOK_ROWS, TOK_ROWS)
                pltpu.make_async_copy(ys_ref.at[pl.ds(src, TOK_ROWS)], buf_ref.at[slot, k, pl.ds(dst, TOK_ROWS)],
                                      sem.at[slot]).start(priority=k % 2)
            return 0

        lax.fori_loop(0, T_DISP, body, 0, unroll=2)

    @pl.when(i == 0)
    def _():
        issue(0, 0)

    @pl.when(i + 1 < nsteps)
    def _():
        issue(i + 1, (i + 1) % 2)

    cur = i % 2

    for k in range(TOP_K):
        pltpu.make_async_copy(ys_ref.at[pl.ds(0, T_DISP * TOK_ROWS)], buf_ref.at[cur, k], sem.at[cur]).wait()

    w = w_ref[...]
    moe_lo = jnp.zeros((T_DISP, PACK_W), F32)
    moe_hi = jnp.zeros((T_DISP, PACK_W), F32)
    for k in range(TOP_K):
        lo, hi = _unpack_bf16_pair(_load_token_tiles(buf_ref.at[cur, k], T_DISP))
        wk = w[:, k:k + 1]
        moe_lo = moe_lo + wk * lo
        moe_hi = moe_hi + wk * hi

    hlo, hhi = _unpack_bf16_pair(_load_token_tiles(hp_ref, T_DISP))
    hlo = hlo.astype(BF16)
    hhi = hhi.astype(BF16)
    sg = (jnp.dot(hlo, wsg_ref[0:PACK_W], preferred_element_type=F32)
          + jnp.dot(hhi, wsg_ref[PACK_W:D_MODEL], preferred_element_type=F32))
    su = (jnp.dot(hlo, wsu_ref[0:PACK_W], preferred_element_type=F32)
          + jnp.dot(hhi, wsu_ref[PACK_W:D_MODEL], preferred_element_type=F32))
    shared = jnp.dot((_silu(sg) * su).astype(BF16), wsd_ref[...], preferred_element_type=F32)
    moe = jnp.concatenate([moe_lo, moe_hi], axis=1)
    out = x1_ref[...] + gf_ref[0] * (moe + shared)
    if final_norm:
        out = _rms(out, fn_ref[...])
    o_ref[...] = out


def _combine(slot_flat, ys, w_tok, hp, x1, g_f, wsg, wsu, wsd, final_gain, final_norm):
    per_b = SEQ // T_DISP
    return pl.pallas_call(
        functools.partial(_combine_kernel, final_norm=final_norm),
        grid_spec=pltpu.PrefetchScalarGridSpec(
            num_scalar_prefetch=1,
            grid=(N_TOK // T_DISP,),
            in_specs=[pl.BlockSpec(memory_space=pl.ANY),
                      pl.BlockSpec((T_DISP, LANES), lambda i, s: (i, 0)),
                      pl.BlockSpec((T_DISP * TOK_ROWS, LANES), lambda i, s: (i, 0)),
                      pl.BlockSpec((T_DISP, D_MODEL), lambda i, s: (i, 0)),
                      pl.BlockSpec((1, 1, D_MODEL), lambda i, s: (i // per_b, 0, 0)),
                      pl.BlockSpec((D_MODEL, SHARED_FF), lambda i, s: (0, 0)),
                      pl.BlockSpec((D_MODEL, SHARED_FF), lambda i, s: (0, 0)),
                      pl.BlockSpec((SHARED_FF, D_MODEL), lambda i, s: (0, 0)),
                      pl.BlockSpec((1, D_MODEL), lambda i, s: (0, 0))],
            out_specs=pl.BlockSpec((T_DISP, D_MODEL), lambda i, s: (i, 0)),
            scratch_shapes=[pltpu.VMEM((2, TOP_K, T_DISP * TOK_ROWS, LANES), U32),
                            pltpu.SemaphoreType.DMA((2,))]),
        out_shape=jax.ShapeDtypeStruct((N_TOK, D_MODEL), F32),
        compiler_params=_cparams(("arbitrary",)),
        name="combine",
    )(slot_flat, ys, w_tok, hp, x1, g_f, wsg, wsu, wsd, final_gain.reshape(1, D_MODEL))


def _moe_layer(layer, hp, logits, x1, g_f, router_bias, w_gate, w_up, w_down, ws_gate, ws_up, ws_down,
               final_gain, final_norm):
    eidx, w_tok, rank, counts = _route(logits, router_bias)
    cnt = counts[:, 0].astype(I32)
    padded = ((cnt + TM_EXP - 1) // TM_EXP) * TM_EXP
    ends = jnp.cumsum(padded)
    offsets = ends - padded
    slot = _slots(eidx, rank, offsets)
    slot_flat = slot.reshape(N_SLOTS)
    blk_start = jnp.arange(NB_EXP, dtype=I32) * TM_EXP
    block_expert = jnp.minimum(jnp.sum((blk_start[:, None] >= ends[None, :]).astype(I32), axis=1), N_EXPERTS - 1)
    block_valid = (blk_start < ends[-1]).astype(I32)
    last_blk = jnp.where(padded > 0, ends - TM_EXP, -1)
    needs_zero = jnp.any(blk_start[:, None] == last_blk[None, :], axis=1) | (block_valid == 0)
    xs = _dispatch(slot_flat, needs_zero.astype(I32), hp)
    ys = _experts(layer, block_expert, block_valid, xs, w_gate, w_up, w_down)
    return _combine(slot_flat, ys, w_tok, hp, x1, g_f, ws_gate.astype(BF16), ws_up.astype(BF16),
                    ws_down.astype(BF16), final_gain, final_norm)


def _rope_tables(dim):
    inv = ROPE_THETA ** (-jnp.arange(0, dim, 2, dtype=F32) / dim)
    ang = jnp.arange(SEQ, dtype=F32)[:, None] * inv[None, :]
    return jnp.cos(ang), jnp.sin(ang)


def _rot_half_cols(w):
    half = w.shape[-1] // 2
    return jnp.concatenate([-w[..., half:], w[..., :half]], axis=-1)


def _t5_bucket(rel):
    half = REL_BUCKETS // 2
    max_exact = half // 2
    ret = (rel > 0).astype(I32) * half
    n = jnp.abs(rel)
    nf = jnp.maximum(n, 1).astype(F32)
    large = max_exact + (jnp.log(nf / max_exact) / math.log(REL_MAX_DIST / max_exact)
                         * (half - max_exact)).astype(I32)
    large = jnp.minimum(large, half - 1)
    return ret + jnp.where(n < max_exact, n, large)


def _swa_bias_table(rel_bias):
    qi = jnp.arange(WINDOW)[:, None]
    kj = jnp.arange(3 * WINDOW)[None, :]
    rel = kj - WINDOW - qi
    onehot = (_t5_bucket(rel)[:, :, None] == jnp.arange(REL_BUCKETS)).astype(F32)
    bias = jnp.einsum('qjb,bh->hqj', onehot, rel_bias.astype(F32), precision=lax.Precision.HIGHEST)
    return jnp.where((jnp.abs(rel) <= WINDOW)[None], bias, NEG_BIG)


def kernel(x, c, w_mod, b_mod, norm_mix, norm_ffn, final_norm, w_in_ab, q_lat_norm, kv_lat_norm, w_uq, w_ukv, conv_w, conv_b, lru_w_a, lru_b_a, lru_w_x, lru_b_x, lru_lambda, w_out_ab, w_in_cd, ret_gn, swa_sinks, w_out_cd, rel_bias, w_router, router_bias, w_gate, w_up, w_down, ws_gate, ws_up, ws_down):
    xf = x.reshape(N_TOK, D_MODEL)
    mod = _modulation(c, w_mod, b_mod)
    cos_r, sin_r = _rope_tables(MLA_ROPE)
    cs_tab = jnp.concatenate([cos_r, cos_r, sin_r, sin_r], axis=1)
    cos_t, sin_t = _rope_tables(RET_DK)
    lg_ret = jnp.log1p(-(2.0 ** (-5.0 - jnp.arange(RET_HEADS, dtype=F32))))

    for layer in range(DEPTH):
        sh_m, sc_m, g_m, sh_f, sc_f, g_f = [m.reshape(BATCH, 1, D_MODEL) for m in jnp.split(mod[layer], 6, axis=-1)]
        i = layer // 2
        if layer % 2 == 0:
            w = w_in_ab[i]
            o1, o2, o3, o4 = np.cumsum((MLA_Q_RANK, MLA_KV_RANK, MLA_ROPE, LRU_WIDTH)).tolist()
            w_kr = w[:, o2:o3]
            w_in = jnp.concatenate([w[:, :o2], w[:, o3:], w_kr, _rot_half_cols(w_kr)], axis=1).astype(BF16)
            p0 = _in_projection(xf, norm_mix[layer], sc_m, sh_m, w_in, 640, "in_proj_ab")
            wq = w_uq[i].reshape(MLA_Q_RANK, MLA_HEADS, MLA_NOPE + MLA_ROPE)
            wq_r = wq[:, :, MLA_NOPE:]
            wq = jnp.concatenate([wq, _rot_half_cols(wq_r)], axis=-1).reshape(MLA_Q_RANK, MLA_HEADS * MLA_QK)
            q, k, v = _mla_up(p0, q_lat_norm[i], kv_lat_norm[i], wq.astype(BF16), w_ukv[i].astype(BF16), cs_tab)
            a_out = _mla_attention(q, k, v)
            w_gates = jnp.concatenate([lru_w_a[i, 0], lru_w_x[i, 0], lru_w_a[i, 1], lru_w_x[i, 1]], axis=-1).astype(BF16)
            b_gates = jnp.concatenate([b.reshape(LRU_BLOCKS, 1, LRU_BS) for b in
                                       (lru_b_a[i, 0], lru_b_x[i, 0], lru_b_a[i, 1], lru_b_x[i, 1])], axis=-1)
            b_out = _rglru(p0, conv_w[i], conv_b[i], w_gates, b_gates, lru_lambda[i])
            w_out = w_out_ab[i].astype(BF16)
        else:
            p1 = _in_projection(xf, norm_mix[layer], sc_m, sh_m, w_in_cd[i].astype(BF16), 512, "in_proj_cd")
            a_out = _retention(p1, lg_ret, cos_t, sin_t, ret_gn[i])
            b_out = _swa(p1, swa_sinks[i], _swa_bias_table(rel_bias))
            w_out = w_out_cd[i].astype(BF16)
        w_r = jnp.pad(w_router[layer], ((0, 0), (0, LANES - N_EXPERTS)))
        w_r_hi = w_r.astype(BF16)
        w_router_pad = jnp.concatenate([w_r_hi, (w_r - w_r_hi.astype(F32)).astype(BF16)], axis=1)
        x1, hp, logits = _out_projection(a_out, b_out, w_out, xf, g_m, norm_ffn[layer], sc_f, sh_f, w_router_pad)
        xf = _moe_layer(layer, hp, logits, x1, g_f, router_bias[layer], w_gate, w_up, w_down,
                        ws_gate[layer], ws_up[layer], ws_down[layer], final_norm, layer == DEPTH - 1)
    return xf.reshape(BATCH, SEQ, D_MODEL)
```

```python
import functools
import math

import numpy as np
import jax
import jax.numpy as jnp
from jax import lax
from jax.experimental import pallas as pl
from jax.experimental.pallas import tpu as pltpu

F32 = jnp.float32
BF16 = jnp.bfloat16
I32 = jnp.int32
U32 = jnp.uint32

D_MODEL = 2048
BATCH = 4
SEQ = 2048
DEPTH = 2
N_TOK = BATCH * SEQ
HALF = D_MODEL // 2
MLA_NOPE = 128
MLA_ROPE = 64
MLA_V = 128
MLA_HEADS = HALF // MLA_V
MLA_Q_RANK = D_MODEL // 4
MLA_KV_RANK = D_MODEL // 4
MLA_QK = 256
LRU_WIDTH = HALF
LRU_BLOCKS = 8
LRU_BS = LRU_WIDTH // LRU_BLOCKS
LRU_CONV = 4
LRU_C = 8.0
RET_DK = 256
RET_DV = 256
RET_HEADS = HALF // RET_DV
SWA_HD = 128
SWA_HEADS = HALF // SWA_HD
SWA_KV_HEADS = 2
SWA_G = SWA_HEADS // SWA_KV_HEADS
WINDOW = 128
REL_BUCKETS = 32
REL_MAX_DIST = 128
N_EXPERTS = 64
TOP_K = 8
N_GROUPS = 8
GROUP_SIZE = N_EXPERTS // N_GROUPS
TOP_GROUPS = 4
EXPERT_FF = D_MODEL // 4
SHARED_FF = D_MODEL // 4
ROUTE_SCALE = 2.5
ROPE_THETA = 10000.0
EPS = 1e-6
NEG_BIG = -1e30

LANES = 128
SUBLANES = 8
VMEM_LIMIT = 52 * 2**20

TM_PROJ = 256
TM_PROJ_SUB = 128
TM_UP = 512
TQ_ATT = 512
TQ_MLA = 2048
TQ_SUB = 256
TM_OUT = 512
TM_OUT_SUB = 256
T_ROUTE = 512
T_DISP = 256
TM_EXP = 512
TM_EXP_SUB = 256
N_SLOTS = N_TOK * TOP_K
NB_EXP = N_SLOTS // TM_EXP + N_EXPERTS
P_ROWS = NB_EXP * TM_EXP
PACK_W = D_MODEL // 2
TOK_ROWS = PACK_W // LANES
assert TOK_ROWS == SUBLANES

LRU_SEG = 260
LRU_ROWS = SUBLANES * LRU_SEG
assert LRU_ROWS >= SEQ and LRU_SEG % 8 == 4


def _cparams(sem, vmem=VMEM_LIMIT):
    return pltpu.CompilerParams(dimension_semantics=sem, vmem_limit_bytes=vmem)


def _sigmoid(x):
    return 0.5 * jnp.tanh(0.5 * x) + 0.5


def _silu(x):
    return x * _sigmoid(x)


def _rms(x, g):
    return x * lax.rsqrt(jnp.mean(x * x, axis=-1, keepdims=True) + EPS) * g


def _pack_bf16_pair(lo, hi):
    lo_b = lax.bitcast_convert_type(lo.astype(BF16).astype(F32), U32)
    hi_b = lax.bitcast_convert_type(hi.astype(BF16).astype(F32), U32)
    return (hi_b & jnp.uint32(0xFFFF0000)) | (lo_b >> 16)


def _unpack_bf16_pair(w):
    lo = lax.bitcast_convert_type(w << 16, F32)
    hi = lax.bitcast_convert_type(w & jnp.uint32(0xFFFF0000), F32)
    return lo, hi


def _store_token_tiles(ref, packed, tok0=0):
    t = packed.shape[0]
    for s in range(TOK_ROWS):
        ref[pl.ds(tok0 * TOK_ROWS + s, t, stride=TOK_ROWS), :] = packed[:, s * LANES:(s + 1) * LANES]


def _load_token_tiles(ref, t):
    return jnp.concatenate([ref[pl.ds(s, t, stride=TOK_ROWS), :] for s in range(TOK_ROWS)], axis=1)


def _mod_kernel(c_ref, w_ref, b_ref, o_ref):
    c = c_ref[...]
    ca = _silu(c).astype(BF16)
    o_ref[0] = jnp.dot(ca, w_ref[0].astype(BF16), preferred_element_type=F32) + b_ref[0]


def _modulation(c, w_mod, b_mod):
    tn = 1024
    cp = jnp.pad(c, ((0, SUBLANES - BATCH), (0, 0)))
    out = pl.pallas_call(
        _mod_kernel,
        grid=(DEPTH, 6 * D_MODEL // tn),
        in_specs=[pl.BlockSpec((SUBLANES, D_MODEL), lambda l, j: (0, 0)),
                  pl.BlockSpec((1, D_MODEL, tn), lambda l, j: (l, 0, j)),
                  pl.BlockSpec((1, 1, tn), lambda l, j: (l, 0, j))],
        out_specs=pl.BlockSpec((1, SUBLANES, tn), lambda l, j: (l, 0, j)),
        out_shape=jax.ShapeDtypeStruct((DEPTH, SUBLANES, 6 * D_MODEL), F32),
        compiler_params=_cparams(("parallel", "parallel")),
        name="adaln_mod",
    )(cp, w_mod, b_mod.reshape(DEPTH, 1, 6 * D_MODEL))
    return out[:, :BATCH]


def _inproj_kernel(x_ref, g_ref, sc_ref, sh_ref, w_ref, o_ref):
    for r in range(TM_PROJ // TM_PROJ_SUB):
        rows = pl.ds(r * TM_PROJ_SUB, TM_PROJ_SUB)
        y = _rms(x_ref[rows, :], g_ref[...])
        h = (y * (1.0 + sc_ref[0]) + sh_ref[0]).astype(BF16)
        o_ref[rows, :] = jnp.dot(h, w_ref[...], preferred_element_type=F32)


def _in_projection(x, gain, scale, shift, w_bf16, name):
    p = w_bf16.shape[1]
    per_b = SEQ // TM_PROJ
    return pl.pallas_call(
        _inproj_kernel,
        grid=(N_TOK // TM_PROJ,),
        in_specs=[pl.BlockSpec((TM_PROJ, D_MODEL), lambda i: (i, 0)),
                  pl.BlockSpec((1, D_MODEL), lambda i: (0, 0)),
                  pl.BlockSpec((1, 1, D_MODEL), lambda i: (i // per_b, 0, 0)),
                  pl.BlockSpec((1, 1, D_MODEL), lambda i: (i // per_b, 0, 0)),
                  pl.BlockSpec((D_MODEL, p), lambda i: (0, 0), pipeline_mode=pl.Buffered(1))],
        out_specs=pl.BlockSpec((TM_PROJ, p), lambda i: (i, 0)),
        out_shape=jax.ShapeDtypeStruct((N_TOK, p), F32),
        compiler_params=_cparams(("parallel",), 56 * 2**20),
        name=name,
    )(x, gain.reshape(1, D_MODEL), scale, shift, w_bf16)


def _mla_up_kernel(ql_ref, kvl_ref, kr_ref, qn_ref, kvn_ref, wq_ref, wkv_ref, cs_ref, q_ref, k_ref, v_ref):
    scale = (MLA_NOPE + MLA_ROPE) ** -0.5 * math.log2(math.e)
    hq = _rms(ql_ref[...], qn_ref[...]).astype(BF16)
    hkv = _rms(kvl_ref[...], kvn_ref[...]).astype(BF16)
    yq = jnp.dot(hq, wq_ref[...], preferred_element_type=F32) * scale
    ykv = jnp.dot(hkv, wkv_ref[...], preferred_element_type=F32)
    cs = cs_ref[...]
    lane = lax.broadcasted_iota(I32, cs.shape, 1)

    def rope_sum(blk):
        z = blk * cs
        return z + pltpu.roll(z, MLA_ROPE, 1)

    kr = jnp.where(lane < MLA_ROPE, rope_sum(kr_ref[...]), 0.0).astype(BF16)
    ones_col = jnp.where(lane == 0, 1.0, 0.0).astype(BF16)
    for h in range(MLA_HEADS):
        c0 = h * MLA_QK
        q_ref[0, h, :, 0:MLA_NOPE] = yq[:, c0:c0 + MLA_NOPE].astype(BF16)
        q_ref[0, h, :, MLA_NOPE:MLA_QK] = rope_sum(yq[:, c0 + MLA_NOPE:c0 + MLA_QK]).astype(BF16)
        k_ref[0, h, :, 0:MLA_NOPE] = ykv[:, c0:c0 + MLA_NOPE].astype(BF16)
        k_ref[0, h, :, MLA_NOPE:MLA_QK] = kr
        v_ref[0, h, :, 0:MLA_V] = ykv[:, c0 + MLA_NOPE:c0 + MLA_QK].astype(BF16)
        v_ref[0, h, :, MLA_V:2 * MLA_V] = ones_col


def _mla_up(p0, q_norm, kv_norm, wq, wkv, cs_tab):
    per_b = SEQ // TM_UP
    qk_shape = jax.ShapeDtypeStruct((BATCH, MLA_HEADS, SEQ, MLA_QK), BF16)
    return pl.pallas_call(
        _mla_up_kernel,
        grid=(N_TOK // TM_UP,),
        in_specs=[pl.BlockSpec((TM_UP, MLA_Q_RANK), lambda i: (i, 0)),
                  pl.BlockSpec((TM_UP, MLA_KV_RANK), lambda i: (i, 1)),
                  pl.BlockSpec((TM_UP, LANES), lambda i: (i, 24)),
                  pl.BlockSpec((1, MLA_Q_RANK), lambda i: (0, 0)),
                  pl.BlockSpec((1, MLA_KV_RANK), lambda i: (0, 0)),
                  pl.BlockSpec((MLA_Q_RANK, MLA_HEADS * MLA_QK), lambda i: (0, 0)),
                  pl.BlockSpec((MLA_KV_RANK, MLA_HEADS * MLA_QK), lambda i: (0, 0)),
                  pl.BlockSpec((TM_UP, LANES), lambda i: (i % per_b, 0))],
        out_specs=[pl.BlockSpec((1, MLA_HEADS, TM_UP, MLA_QK), lambda i: (i // per_b, 0, i % per_b, 0)),
                   pl.BlockSpec((1, MLA_HEADS, TM_UP, MLA_QK), lambda i: (i // per_b, 0, i % per_b, 0)),
                   pl.BlockSpec((1, MLA_HEADS, TM_UP, 2 * MLA_V), lambda i: (i // per_b, 0, i % per_b, 0))],
        out_shape=[qk_shape, qk_shape, jax.ShapeDtypeStruct((BATCH, MLA_HEADS, SEQ, 2 * MLA_V), BF16)],
        compiler_params=_cparams(("parallel",)),
        name="mla_up",
    )(p0, p0, p0, q_norm.reshape(1, -1), kv_norm.reshape(1, -1), wq, wkv, cs_tab)


def _mla_attn_kernel(q_ref, k_ref, v_ref, o_ref):
    k = k_ref[0, 0]
    v = v_ref[0, 0]
    for r in range(TQ_MLA // TQ_SUB):
        rows = pl.ds(r * TQ_SUB, TQ_SUB)
        s = lax.dot_general(q_ref[0, 0, rows, :], k, (((1,), (1,)), ((), ())), preferred_element_type=F32)
        p = jnp.exp2(s - jnp.max(s, axis=-1, keepdims=True))
        o = jnp.dot(p.astype(BF16), v, preferred_element_type=F32)
        o_ref[rows, :] = (o[:, 0:MLA_V] / o[:, MLA_V:MLA_V + 1]).astype(BF16)


def _mla_attention(q, k, v):
    nq = SEQ // TQ_MLA
    return pl.pallas_call(
        _mla_attn_kernel,
        grid=(BATCH, MLA_HEADS, nq),
        in_specs=[pl.BlockSpec((1, 1, TQ_MLA, MLA_QK), lambda b, h, i: (b, h, i, 0)),
                  pl.BlockSpec((1, 1, SEQ, MLA_QK), lambda b, h, i: (b, h, 0, 0)),
                  pl.BlockSpec((1, 1, SEQ, 2 * MLA_V), lambda b, h, i: (b, h, 0, 0))],
        out_specs=pl.BlockSpec((TQ_MLA, MLA_V), lambda b, h, i: (b * nq + i, h)),
        out_shape=jax.ShapeDtypeStruct((N_TOK, HALF), BF16),
        compiler_params=_cparams(("parallel", "parallel", "parallel")),
        name="mla_attn",
    )(q, k, v)


def _lru_kernel(x_ref, gate_ref, cw_ref, cb_ref, wg_ref, bg_ref, lam_ref, o_ref,
                af_ref, uf_ref, ab_ref, ub_ref, hf_ref, pf_ref, hb_ref, pb_ref, hs_ref):
    x = x_ref[...]
    row = lax.broadcasted_iota(I32, x.shape, 0)

    def shifted(d):
        r = pltpu.roll(x, (-d) % SEQ, 0)
        return jnp.where((row + d >= 0) & (row + d < SEQ), r, 0.0)

    cw = cw_ref[...]
    left = LRU_CONV // 2
    xc = cb_ref[...]
    for kk in range(LRU_CONV):
        d = kk - left
        xc = xc + cw[kk:kk + 1] * (x if d == 0 else shifted(d))

    gates = jnp.dot(xc.astype(BF16), wg_ref[0], preferred_element_type=F32) + bg_ref[0]
    lam = lam_ref[...]
    z = -lam
    sp = jnp.maximum(z, 0.0) + jnp.log1p(jnp.exp(-jnp.abs(z)))
    pad_rows = LRU_ROWS - SEQ
    for d, (a_ref, u_ref) in enumerate(((af_ref, uf_ref), (ab_ref, ub_ref))):
        r = _sigmoid(gates[:, d * 256:d * 256 + LRU_BS])
        i = _sigmoid(gates[:, d * 256 + LRU_BS:(d + 1) * 256])
        a = jnp.exp(r * (-LRU_C * sp[d:d + 1]))
        a_ref[0:SEQ] = a
        u_ref[0:SEQ] = jnp.sqrt(1.0 - a * a) * (i * xc)
        a_ref[SEQ:LRU_ROWS] = jnp.zeros((pad_rows, LANES), F32)
        u_ref[SEQ:LRU_ROWS] = jnp.zeros((pad_rows, LANES), F32)

    ones = jnp.ones((SUBLANES, LANES), F32)
    zeros = jnp.zeros((SUBLANES, LANES), F32)

    def seg(t):
        return pl.ds(t, SUBLANES, stride=LRU_SEG)

    def local_scan(s, carry):
        p_f, h_f, p_b, h_b = carry
        tf = s
        tb = LRU_SEG - 1 - s
        a = af_ref[seg(tf)]
        h_f = a * h_f + uf_ref[seg(tf)]
        p_f = a * p_f
        hf_ref[seg(tf)] = h_f
        pf_ref[seg(tf)] = p_f
        a = ab_ref[seg(tb)]
        h_b = a * h_b + ub_ref[seg(tb)]
        p_b = a * p_b
        hb_ref[seg(tb)] = h_b
        pb_ref[seg(tb)] = p_b
        return p_f, h_f, p_b, h_b

    p_f, h_f, p_b, h_b = lax.fori_loop(0, LRU_SEG, local_scan, (ones, zeros, ones, zeros), unroll=4)

    rows_f = []
    c = jnp.zeros((1, LANES), F32)
    for j in range(SUBLANES):
        rows_f.append(c)
        c = p_f[j:j + 1] * c + h_f[j:j + 1]
    rows_b = [None] * SUBLANES
    c = jnp.zeros((1, LANES), F32)
    for j in range(SUBLANES - 1, -1, -1):
        rows_b[j] = c
        c = p_b[j:j + 1] * c + h_b[j:j + 1]
    sub = lax.broadcasted_iota(I32, (SUBLANES, LANES), 0)
    c_f = zeros
    c_b = zeros
    for j in range(SUBLANES):
        c_f = jnp.where(sub == j, rows_f[j], c_f)
        c_b = jnp.where(sub == j, rows_b[j], c_b)

    def fixup(t, _):
        hs_ref[seg(t)] = (hf_ref[seg(t)] + pf_ref[seg(t)] * c_f) + (hb_ref[seg(t)] + pb_ref[seg(t)] * c_b)
        return 0

    lax.fori_loop(0, LRU_SEG, fixup, 0, unroll=4)

    g = gate_ref[...]
    gelu = 0.5 * g * (1.0 + jnp.tanh(math.sqrt(2.0 / math.pi) * (g + 0.044715 * (g * g * g))))
    o_ref[...] = (gelu * hs_ref[0:SEQ]).astype(BF16)


def _rglru(p0, conv_w, conv_b, w_gates, b_gates, lam):
    scan_buf = pltpu.VMEM((LRU_ROWS, LANES), F32)
    return pl.pallas_call(
        _lru_kernel,
        grid=(BATCH, LRU_BLOCKS),
        in_specs=[pl.BlockSpec((SEQ, LRU_BS), lambda b, g: (b, 8 + g)),
                  pl.BlockSpec((SEQ, LRU_BS), lambda b, g: (b, 16 + g)),
                  pl.BlockSpec((LRU_CONV, LRU_BS), lambda b, g: (0, g)),
                  pl.BlockSpec((1, LRU_BS), lambda b, g: (0, g)),
                  pl.BlockSpec((1, LRU_BS, 4 * LRU_BS), lambda b, g: (g, 0, 0)),
                  pl.BlockSpec((1, 1, 4 * LRU_BS), lambda b, g: (g, 0, 0)),
                  pl.BlockSpec((2, LRU_BS), lambda b, g: (0, g))],
        out_specs=pl.BlockSpec((SEQ, LRU_BS), lambda b, g: (b, g)),
        out_shape=jax.ShapeDtypeStruct((N_TOK, LRU_WIDTH), BF16),
        scratch_shapes=[scan_buf] * 9,
        compiler_params=_cparams(("parallel", "parallel")),
        name="rglru",
    )(p0, p0, conv_w, conv_b.reshape(1, -1), w_gates, b_gates, lam)


def _ret_kernel(lg_ref, q_ref, k_ref, v_ref, g_ref, cq_ref, sq_ref, ck_ref, sk_ref, gn_ref, o_ref, ks_ref, vs_ref):
    h = pl.program_id(1)
    qi = pl.program_id(2)
    half = RET_DK // 2

    def rope(t, c, s):
        t1, t2 = t[:, :half], t[:, half:]
        return jnp.concatenate([t1 * c - t2 * s, t2 * c + t1 * s], axis=1)

    @pl.when(qi == 0)
    def _():
        ks_ref[...] = (rope(k_ref[...], ck_ref[...], sk_ref[...]) * (RET_DK ** -0.5)).astype(BF16)
        vs_ref[...] = v_ref[...].astype(BF16)

    q = rope(q_ref[...], cq_ref[...], sq_ref[...]).astype(BF16)
    s = lax.dot_general(q, ks_ref[...], (((1,), (1,)), ((), ())), preferred_element_type=F32)
    n = qi * TQ_ATT + lax.broadcasted_iota(I32, s.shape, 0)
    m = lax.broadcasted_iota(I32, s.shape, 1)
    d = (n - m).astype(F32)
    lg_f = lg_ref[h]
    lg_b = lg_ref[RET_HEADS - 1 - h]
    dec = jnp.exp(jnp.where(d >= 0.0, lg_f * d, -lg_b * d))
    o = jnp.dot((s * dec).astype(BF16), vs_ref[...], preferred_element_type=F32)
    y = _rms(o, gn_ref[0])
    o_ref[...] = (_silu(g_ref[...]) * y).astype(BF16)


def _retention(p1, lg, cos_t, sin_t, ret_gn):
    nq = SEQ // TQ_ATT
    half = RET_DK // 2
    return pl.pallas_call(
        _ret_kernel,
        grid=(BATCH, RET_HEADS, nq),
        in_specs=[pl.BlockSpec(memory_space=pltpu.SMEM),
                  pl.BlockSpec((TQ_ATT, RET_DK), lambda b, h, i: (b * nq + i, h)),
                  pl.BlockSpec((SEQ, RET_DK), lambda b, h, i: (b, RET_HEADS + h)),
                  pl.BlockSpec((SEQ, RET_DV), lambda b, h, i: (b, 2 * RET_HEADS + h)),
                  pl.BlockSpec((TQ_ATT, RET_DV), lambda b, h, i: (b * nq + i, 3 * RET_HEADS + h)),
                  pl.BlockSpec((TQ_ATT, half), lambda b, h, i: (i, 0)),
                  pl.BlockSpec((TQ_ATT, half), lambda b, h, i: (i, 0)),
                  pl.BlockSpec((SEQ, half), lambda b, h, i: (0, 0)),
                  pl.BlockSpec((SEQ, half), lambda b, h, i: (0, 0)),
                  pl.BlockSpec((1, 1, RET_DV), lambda b, h, i: (h, 0, 0))],
        out_specs=pl.BlockSpec((TQ_ATT, RET_DV), lambda b, h, i: (b * nq + i, h)),
        out_shape=jax.ShapeDtypeStruct((N_TOK, HALF), BF16),
        scratch_shapes=[pltpu.VMEM((SEQ, RET_DK), BF16), pltpu.VMEM((SEQ, RET_DV), BF16)],
        compiler_params=_cparams(("parallel", "parallel", "arbitrary")),
        name="retention",
    )(lg, p1, p1, p1, p1, cos_t, sin_t, cos_t, sin_t, ret_gn.reshape(RET_HEADS, 1, RET_DV))


def _swa_kernel(sink_ref, q_ref, k_ref, v_ref, bias_ref, o_ref):
    kv = pl.program_id(1)
    n = pl.program_id(2)
    nb = SEQ // WINDOW
    w = WINDOW
    prev = jnp.maximum(n - 1, 0)
    nxt = jnp.minimum(n + 1, nb - 1)

    def rows(ref, blk):
        return ref[pl.ds(pl.multiple_of(blk * w, w), w), :].astype(BF16)

    kw = jnp.concatenate([rows(k_ref, prev), rows(k_ref, n), rows(k_ref, nxt)], axis=0)
    vw = jnp.concatenate([rows(v_ref, prev), rows(v_ref, n), rows(v_ref, nxt)], axis=0)
    qb = q_ref[...]
    q4 = jnp.concatenate([qb[:, g * SWA_HD:(g + 1) * SWA_HD] for g in range(SWA_G)], axis=0).astype(BF16)
    s = lax.dot_general(q4, kw, (((1,), (1,)), ((), ())), preferred_element_type=F32) * (SWA_HD ** -0.5)
    col = lax.broadcasted_iota(I32, (w, 3 * w), 1)
    outside = ((col < w) & (n == 0)) | ((col >= 2 * w) & (n == nb - 1))
    for g in range(SWA_G):
        sg = jnp.where(outside, NEG_BIG, s[g * w:(g + 1) * w] + bias_ref[g])
        sink = sink_ref[kv * SWA_G + g]
        m = jnp.maximum(jnp.max(sg, axis=-1, keepdims=True), sink)
        p = jnp.exp(sg - m)
        denom = jnp.sum(p, axis=-1, keepdims=True) + jnp.exp(sink - m)
        o = jnp.dot((p / denom).astype(BF16), vw, preferred_element_type=F32)
        o_ref[:, g * SWA_HD:(g + 1) * SWA_HD] = o.astype(BF16)


def _swa(p1, sinks, bias):
    nb = SEQ // WINDOW
    qcols = SWA_G * SWA_HD
    q_blk0 = (4 * RET_HEADS * RET_DK) // qcols
    k_blk0 = (4 * RET_HEADS * RET_DK + SWA_HEADS * SWA_HD) // SWA_HD
    v_blk0 = k_blk0 + SWA_KV_HEADS
    return pl.pallas_call(
        _swa_kernel,
        grid=(BATCH, SWA_KV_HEADS, nb),
        in_specs=[pl.BlockSpec(memory_space=pltpu.SMEM),
                  pl.BlockSpec((WINDOW, qcols), lambda b, kv, n: (b * nb + n, q_blk0 + kv)),
                  pl.BlockSpec((SEQ, SWA_HD), lambda b, kv, n: (b, k_blk0 + kv)),
                  pl.BlockSpec((SEQ, SWA_HD), lambda b, kv, n: (b, v_blk0 + kv)),
                  pl.BlockSpec((SWA_G, WINDOW, 3 * WINDOW), lambda b, kv, n: (kv, 0, 0))],
        out_specs=pl.BlockSpec((WINDOW, qcols), lambda b, kv, n: (b * nb + n, kv)),
        out_shape=jax.ShapeDtypeStruct((N_TOK, HALF), BF16),
        compiler_params=_cparams(("parallel", "parallel", "parallel")),
        name="swa",
    )(sinks, p1, p1, p1, bias)


def _outproj_kernel(a_ref, b_ref, wa_ref, wb_ref, x_ref, gm_ref, g_ref, sc_ref, sh_ref, wr_ref,
                    x1_ref, hp_ref, lg_ref):
    wr = wr_ref[...]
    for r in range(TM_OUT // TM_OUT_SUB):
        rows = pl.ds(r * TM_OUT_SUB, TM_OUT_SUB)
        mixed = (jnp.dot(a_ref[rows, :], wa_ref[...], preferred_element_type=F32)
                 + jnp.dot(b_ref[rows, :], wb_ref[...], preferred_element_type=F32))
        x1 = x_ref[rows, :] + gm_ref[0] * mixed
        x1_ref[rows, :] = x1
        hf = _rms(x1, g_ref[...]) * (1.0 + sc_ref[0]) + sh_ref[0]
        _store_token_tiles(hp_ref, _pack_bf16_pair(hf[:, :PACK_W], hf[:, PACK_W:]), r * TM_OUT_SUB)
        h_hi = hf.astype(BF16)
        h_lo = (hf - h_hi.astype(F32)).astype(BF16)
        t_hi = jnp.dot(h_hi, wr, preferred_element_type=F32)
        t_lo = jnp.dot(h_lo, wr, preferred_element_type=F32)
        lg_ref[rows, :] = (t_hi[:, :LANES] + t_hi[:, LANES:]) + (t_lo[:, :LANES] + t_lo[:, LANES:])


def _out_projection(a, b, w_out_bf16, x, g_m, gain, scale, shift, w_router_pad):
    per_b = SEQ // TM_OUT
    vec = pl.BlockSpec((1, 1, D_MODEL), lambda i: (i // per_b, 0, 0))
    return pl.pallas_call(
        _outproj_kernel,
        grid=(N_TOK // TM_OUT,),
        in_specs=[pl.BlockSpec((TM_OUT, HALF), lambda i: (i, 0)),
                  pl.BlockSpec((TM_OUT, HALF), lambda i: (i, 0)),
                  pl.BlockSpec((HALF, D_MODEL), lambda i: (0, 0)),
                  pl.BlockSpec((HALF, D_MODEL), lambda i: (1, 0)),
                  pl.BlockSpec((TM_OUT, D_MODEL), lambda i: (i, 0)),
                  vec,
                  pl.BlockSpec((1, D_MODEL), lambda i: (0, 0)),
                  vec, vec,
                  pl.BlockSpec((D_MODEL, 2 * LANES), lambda i: (0, 0))],
        out_specs=[pl.BlockSpec((TM_OUT, D_MODEL), lambda i: (i, 0)),
                   pl.BlockSpec((TM_OUT * TOK_ROWS, LANES), lambda i: (i, 0)),
                   pl.BlockSpec((TM_OUT, LANES), lambda i: (i, 0))],
        out_shape=[jax.ShapeDtypeStruct((N_TOK, D_MODEL), F32),
                   jax.ShapeDtypeStruct((N_TOK * TOK_ROWS, LANES), U32),
                   jax.ShapeDtypeStruct((N_TOK, LANES), F32)],
        compiler_params=_cparams(("parallel",)),
        name="out_proj",
    )(a, b, w_out_bf16, w_out_bf16, x, g_m, gain.reshape(1, D_MODEL), scale, shift, w_router_pad)


def _route_kernel(lg_ref, bias_ref, eidx_ref, w_ref, rank_ref, cnt_ref):
    step = pl.program_id(0)

    @pl.when(step == 0)
    def _():
        cnt_ref[...] = jnp.zeros(cnt_ref.shape, F32)

    t = T_ROUTE
    scores = jax.nn.sigmoid(lg_ref[...].T[:N_EXPERTS])
    biased = scores + bias_ref[...]
    sub = lax.broadcasted_iota(I32, (GROUP_SIZE, t), 0).astype(F32)
    ninf = -jnp.inf

    def first_argmax(v, idx, n):
        m = jnp.max(v, axis=0, keepdims=True)
        return m, jnp.min(jnp.where(v == m, idx, float(n)), axis=0, keepdims=True)

    gs = []
    for g in range(N_GROUPS):
        bg = biased[g * GROUP_SIZE:(g + 1) * GROUP_SIZE]
        m1, i1 = first_argmax(bg, sub, GROUP_SIZE)
        m2 = jnp.max(jnp.where(sub == i1, ninf, bg), axis=0, keepdims=True)
        gs.append(m1 + m2)
    cur = jnp.concatenate(gs, axis=0)

    gmask = jnp.zeros((N_GROUPS, t), F32)
    for _ in range(TOP_GROUPS):
        _, i = first_argmax(cur, sub, N_GROUPS)
        pick = sub == i
        gmask = jnp.where(pick, 1.0, gmask)
        cur = jnp.where(pick, ninf, cur)

    eid = lax.broadcasted_iota(I32, (N_EXPERTS, t), 0).astype(F32)
    emask = jnp.concatenate([jnp.broadcast_to(gmask[g:g + 1], (GROUP_SIZE, t)) for g in range(N_GROUPS)], axis=0)
    cur = jnp.where(emask > 0.5, biased, ninf)
    sels, idxs, ws = [], [], []
    onehot = jnp.zeros((N_EXPERTS, t), F32)
    for _ in range(TOP_K):
        _, i = first_argmax(cur, eid, N_EXPERTS)
        pick = eid == i
        sels.append(pick)
        idxs.append(i)
        ws.append(jnp.sum(jnp.where(pick, scores, 0.0), axis=0, keepdims=True))
        onehot = jnp.where(pick, 1.0, onehot)
        cur = jnp.where(pick, ninf, cur)
    wsum = ws[0]
    for k in range(1, TOP_K):
        wsum = wsum + ws[k]

    r = lax.broadcasted_iota(I32, (t, t), 0)
    c = lax.broadcasted_iota(I32, (t, t), 1)
    tri = (r < c).astype(BF16)
    before = jnp.dot(onehot.astype(BF16), tri, preferred_element_type=F32) + cnt_ref[:, 0:1]
    ranks = [jnp.sum(jnp.where(sels[k], before, 0.0), axis=0, keepdims=True) for k in range(TOP_K)]

    eidx_ref[...] = jnp.concatenate(idxs, axis=0).astype(I32)
    rank_ref[...] = jnp.concatenate(ranks, axis=0).astype(I32)
    wk = jnp.concatenate([w / wsum * ROUTE_SCALE for w in ws], axis=0)
    wfull = jnp.concatenate([wk, jnp.zeros((LANES - TOP_K, t), F32)], axis=0)
    w_ref[...] = wfull.T
    cnt_ref[...] = cnt_ref[...] + jnp.sum(onehot, axis=1, keepdims=True)


def _route(logits, router_bias):
    return pl.pallas_call(
        _route_kernel,
        grid=(N_TOK // T_ROUTE,),
        in_specs=[pl.BlockSpec((T_ROUTE, LANES), lambda i: (i, 0)),
                  pl.BlockSpec((N_EXPERTS, 1), lambda i: (0, 0))],
        out_specs=[pl.BlockSpec((TOP_K, T_ROUTE), lambda i: (0, i)),
                   pl.BlockSpec((T_ROUTE, LANES), lambda i: (i, 0)),
                   pl.BlockSpec((TOP_K, T_ROUTE), lambda i: (0, i)),
                   pl.BlockSpec((N_EXPERTS, LANES), lambda i: (0, 0))],
        out_shape=[jax.ShapeDtypeStruct((TOP_K, N_TOK), I32),
                   jax.ShapeDtypeStruct((N_TOK, LANES), F32),
                   jax.ShapeDtypeStruct((TOP_K, N_TOK), I32),
                   jax.ShapeDtypeStruct((N_EXPERTS, LANES), F32)],
        compiler_params=_cparams(("arbitrary",)),
        name="route",
    )(logits, router_bias.reshape(N_EXPERTS, 1))


def _slot_kernel(eidx_ref, rank_ref, off_ref, slot_ref):
    eidx = eidx_ref[...]
    off = off_ref[...]
    t = eidx.shape[1]
    eid = lax.broadcasted_iota(I32, (N_EXPERTS, t), 0)
    base = [jnp.sum(jnp.where(eid == eidx[k:k + 1], off, 0.0), axis=0, keepdims=True) for k in range(TOP_K)]
    slot_ref[...] = jnp.concatenate(base, axis=0).astype(I32) + rank_ref[...]


def _slots(eidx, rank, offsets):
    return pl.pallas_call(
        _slot_kernel,
        grid=(N_TOK // T_ROUTE,),
        in_specs=[pl.BlockSpec((TOP_K, T_ROUTE), lambda i: (0, i)),
                  pl.BlockSpec((TOP_K, T_ROUTE), lambda i: (0, i)),
                  pl.BlockSpec((N_EXPERTS, 1), lambda i: (0, 0))],
        out_specs=pl.BlockSpec((TOP_K, T_ROUTE), lambda i: (0, i)),
        out_shape=jax.ShapeDtypeStruct((TOP_K, N_TOK), I32),
        compiler_params=_cparams(("parallel",)),
        name="slots",
    )(eidx, rank, offsets.astype(F32).reshape(N_EXPERTS, 1))


def _dispatch_kernel(slot_ref, tail_ref, hp_ref, xs_ref, zero_ref, sem, zsem):
    step = pl.program_id(0)

    @pl.when(step == 0)
    def _():
        zero_ref[...] = jnp.zeros(zero_ref.shape, U32)

        def block_copy(b):
            start = pl.multiple_of(b * (TM_EXP * TOK_ROWS), TM_EXP * TOK_ROWS)
            return pltpu.make_async_copy(zero_ref, xs_ref.at[pl.ds(start, TM_EXP * TOK_ROWS)], zsem)

        def fill(b, _):
            @pl.when(tail_ref[b] > 0)
            def _():
                block_copy(b).start()
            return 0

        def fill_wait(b, _):
            @pl.when(tail_ref[b] > 0)
            def _():
                block_copy(b).wait()
            return 0

        lax.fori_loop(0, NB_EXP, fill, 0)
        lax.fori_loop(0, NB_EXP, fill_wait, 0)

    base = step * T_DISP

    def issue(t, _):
        for k in range(TOP_K):
            dst = pl.multiple_of(slot_ref[k * N_TOK + base + t] * TOK_ROWS, TOK_ROWS)
            src = pl.multiple_of(t * TOK_ROWS, TOK_ROWS)
            pltpu.make_async_copy(hp_ref.at[pl.ds(src, TOK_ROWS)], xs_ref.at[pl.ds(dst, TOK_ROWS)],
                                  sem).start(priority=k % 2)
        return 0

    lax.fori_loop(0, T_DISP, issue, 0, unroll=2)
    for k in range(TOP_K):
        pltpu.make_async_copy(hp_ref, xs_ref.at[pl.ds(0, T_DISP * TOK_ROWS)], sem).wait()


def _dispatch(slot_flat, tail_start, hp):
    return pl.pallas_call(
        _dispatch_kernel,
        grid_spec=pltpu.PrefetchScalarGridSpec(
            num_scalar_prefetch=2,
            grid=(N_TOK // T_DISP,),
            in_specs=[pl.BlockSpec((T_DISP * TOK_ROWS, LANES), lambda i, s, t: (i, 0))],
            out_specs=pl.BlockSpec(memory_space=pl.ANY),
            scratch_shapes=[pltpu.VMEM((TM_EXP * TOK_ROWS, LANES), U32),
                            pltpu.SemaphoreType.DMA,
                            pltpu.SemaphoreType.DMA]),
        out_shape=jax.ShapeDtypeStruct((P_ROWS * TOK_ROWS, LANES), U32),
        compiler_params=_cparams(("arbitrary",)),
        name="dispatch",
    )(slot_flat, tail_start, hp)


def _expert_kernel(be_ref, bv_ref, nx_ref, xs_ref, wg_hbm, wu_hbm, wd_hbm, ys_ref,
                   wgs_ref, wus_ref, wds_ref, wgb_ref, wub_ref, wdb_ref, sem, *, layer):
    i = pl.program_id(0)
    e = be_ref[i]
    changed = jnp.logical_or(i == 0, e != be_ref[jnp.maximum(i - 1, 0)])

    def stage(expert):
        return (pltpu.make_async_copy(wg_hbm.at[layer, expert], wgs_ref, sem.at[0]),
                pltpu.make_async_copy(wu_hbm.at[layer, expert], wus_ref, sem.at[1]),
                pltpu.make_async_copy(wd_hbm.at[layer, expert], wds_ref, sem.at[2]))

    @pl.when(changed)
    def _():
        @pl.when(i == 0)
        def _():
            for c in stage(e):
                c.start()

        for c in stage(e):
            c.wait()
        wgb_ref[...] = wgs_ref[...].astype(BF16)
        wub_ref[...] = wus_ref[...].astype(BF16)
        wdb_ref[...] = wds_ref[...].astype(BF16)

        @pl.when(nx_ref[i] >= 0)
        def _():
            for c in stage(nx_ref[i]):
                c.start()

    valid_rows = bv_ref[i]
    for r in range(TM_EXP // TM_EXP_SUB):
        @pl.when(valid_rows > r * TM_EXP_SUB)
        def _():
            xs_sub = xs_ref.at[pl.ds(r * TM_EXP_SUB * TOK_ROWS, TM_EXP_SUB * TOK_ROWS)]
            lo, hi = _unpack_bf16_pair(_load_token_tiles(xs_sub, TM_EXP_SUB))
            lo = lo.astype(BF16)
            hi = hi.astype(BF16)
            hg = (jnp.dot(lo, wgb_ref[0:PACK_W], preferred_element_type=F32)
                  + jnp.dot(hi, wgb_ref[PACK_W:D_MODEL], preferred_element_type=F32))
            hu = (jnp.dot(lo, wub_ref[0:PACK_W], preferred_element_type=F32)
                  + jnp.dot(hi, wub_ref[PACK_W:D_MODEL], preferred_element_type=F32))
            act = (_silu(hg) * hu).astype(BF16)
            y = jnp.dot(act, wdb_ref[...], preferred_element_type=F32)
            _store_token_tiles(ys_ref, _pack_bf16_pair(y[:, :PACK_W], y[:, PACK_W:]), r * TM_EXP_SUB)

        @pl.when(valid_rows <= r * TM_EXP_SUB)
        def _():
            ys_ref[pl.ds(r * TM_EXP_SUB * TOK_ROWS, TM_EXP_SUB * TOK_ROWS), :] = jnp.zeros(
                (TM_EXP_SUB * TOK_ROWS, LANES), U32)


def _experts(layer, block_expert, block_rows, next_expert, xs, w_gate, w_up, w_down):
    return pl.pallas_call(
        functools.partial(_expert_kernel, layer=layer),
        grid_spec=pltpu.PrefetchScalarGridSpec(
            num_scalar_prefetch=3,
            grid=(NB_EXP,),
            in_specs=[pl.BlockSpec((TM_EXP * TOK_ROWS, LANES), lambda i, be, bv, nx: (jnp.where(bv[i] > 0, i, 0), 0)),
                      pl.BlockSpec(memory_space=pl.ANY),
                      pl.BlockSpec(memory_space=pl.ANY),
                      pl.BlockSpec(memory_space=pl.ANY)],
            out_specs=pl.BlockSpec((TM_EXP * TOK_ROWS, LANES), lambda i, be, bv, nx: (i, 0)),
            scratch_shapes=[pltpu.VMEM((D_MODEL, EXPERT_FF), F32),
                            pltpu.VMEM((D_MODEL, EXPERT_FF), F32),
                            pltpu.VMEM((EXPERT_FF, D_MODEL), F32),
                            pltpu.VMEM((D_MODEL, EXPERT_FF), BF16),
                            pltpu.VMEM((D_MODEL, EXPERT_FF), BF16),
                            pltpu.VMEM((EXPERT_FF, D_MODEL), BF16),
                            pltpu.SemaphoreType.DMA((3,))]),
        out_shape=jax.ShapeDtypeStruct((P_ROWS * TOK_ROWS, LANES), U32),
        compiler_params=_cparams(("arbitrary",)),
        name="experts",
    )(block_expert, block_rows, next_expert, xs, w_gate, w_up, w_down)


def _combine_kernel(slot_ref, ys_ref, w_ref, hp_ref, x1_ref, gf_ref, wsg_ref, wsu_ref, wsd_ref, fn_ref, o_ref,
                    buf_ref, sem, *, final_norm):
    i = pl.program_id(0)
    nsteps = pl.num_programs(0)

    def issue(step, slot):
        base = step * T_DISP

        def body(t, _):
            for k in range(TOP_K):
                src = pl.multiple_of(slot_ref[k * N_TOK + base + t] * TOK_ROWS, TOK_ROWS)
                dst = pl.multiple_of(t * TOK_ROWS, TOK_ROWS)
                pltpu.make_async_copy(ys_ref.at[pl.ds(src, TOK_ROWS)], buf_ref.at[slot, k, pl.ds(dst, TOK_ROWS)],
                                      sem.at[slot]).start(priority=k % 2)
            return 0

        lax.fori_loop(0, T_DISP, body, 0, unroll=2)

    @pl.when(i == 0)
    def _():
        issue(0, 0)

    @pl.when(i + 1 < nsteps)
    def _():
        issue(i + 1, (i + 1) % 2)

    cur = i % 2

    for k in range(TOP_K):
        pltpu.make_async_copy(ys_ref.at[pl.ds(0, T_DISP * TOK_ROWS)], buf_ref.at[cur, k], sem.at[cur]).wait()

    w = w_ref[...]
    moe_lo = jnp.zeros((T_DISP, PACK_W), F32)
    moe_hi = jnp.zeros((T_DISP, PACK_W), F32)
    for k in range(TOP_K):
        lo, hi = _unpack_bf16_pair(_load_token_tiles(buf_ref.at[cur, k], T_DISP))
        wk = w[:, k:k + 1]
        moe_lo = moe_lo + wk * lo
        moe_hi = moe_hi + wk * hi

    hlo, hhi = _unpack_bf16_pair(_load_token_tiles(hp_ref, T_DISP))
    hlo = hlo.astype(BF16)
    hhi = hhi.astype(BF16)
    sg = (jnp.dot(hlo, wsg_ref[0:PACK_W], preferred_element_type=F32)
          + jnp.dot(hhi, wsg_ref[PACK_W:D_MODEL], preferred_element_type=F32))
    su = (jnp.dot(hlo, wsu_ref[0:PACK_W], preferred_element_type=F32)
          + jnp.dot(hhi, wsu_ref[PACK_W:D_MODEL], preferred_element_type=F32))
    shared = jnp.dot((_silu(sg) * su).astype(BF16), wsd_ref[...], preferred_element_type=F32)
    moe = jnp.concatenate([moe_lo, moe_hi], axis=1)
    out = x1_ref[...] + gf_ref[0] * (moe + shared)
    if final_norm:
        out = _rms(out, fn_ref[...])
    o_ref[...] = out


def _combine(slot_flat, ys, w_tok, hp, x1, g_f, wsg, wsu, wsd, final_gain, final_norm):
    per_b = SEQ // T_DISP
    return pl.pallas_call(
        functools.partial(_combine_kernel, final_norm=final_norm),
        grid_spec=pltpu.PrefetchScalarGridSpec(
            num_scalar_prefetch=1,
            grid=(N_TOK // T_DISP,),
            in_specs=[pl.BlockSpec(memory_space=pl.ANY),
                      pl.BlockSpec((T_DISP, LANES), lambda i, s: (i, 0)),
                      pl.BlockSpec((T_DISP * TOK_ROWS, LANES), lambda i, s: (i, 0)),
                      pl.BlockSpec((T_DISP, D_MODEL), lambda i, s: (i, 0)),
                      pl.BlockSpec((1, 1, D_MODEL), lambda i, s: (i // per_b, 0, 0)),
                      pl.BlockSpec((D_MODEL, SHARED_FF), lambda i, s: (0, 0)),
                      pl.BlockSpec((D_MODEL, SHARED_FF), lambda i, s: (0, 0)),
                      pl.BlockSpec((SHARED_FF, D_MODEL), lambda i, s: (0, 0)),
                      pl.BlockSpec((1, D_MODEL), lambda i, s: (0, 0))],
            out_specs=pl.BlockSpec((T_DISP, D_MODEL), lambda i, s: (i, 0)),
            scratch_shapes=[pltpu.VMEM((2, TOP_K, T_DISP * TOK_ROWS, LANES), U32),
                            pltpu.SemaphoreType.DMA((2,))]),
        out_shape=jax.ShapeDtypeStruct((N_TOK, D_MODEL), F32),
        compiler_params=_cparams(("arbitrary",)),
        name="combine",
    )(slot_flat, ys, w_tok, hp, x1, g_f, wsg, wsu, wsd, final_gain.reshape(1, D_MODEL))


def _moe_layer(layer, hp, logits, x1, g_f, router_bias, w_gate, w_up, w_down, ws_gate, ws_up, ws_down,
               final_gain, final_norm):
    eidx, w_tok, rank, counts = _route(logits, router_bias)
    cnt = counts[:, 0].astype(I32)
    padded = ((cnt + TM_EXP - 1) // TM_EXP) * TM_EXP
    ends = jnp.cumsum(padded)
    offsets = ends - padded
    slot = _slots(eidx, rank, offsets)
    slot_flat = slot.reshape(N_SLOTS)
    blk_start = jnp.arange(NB_EXP, dtype=I32) * TM_EXP
    expert_ids = jnp.arange(N_EXPERTS, dtype=I32)
    nonempty = cnt > 0
    last_nonempty = jnp.max(jnp.where(nonempty, expert_ids, 0))
    block_expert = jnp.minimum(jnp.sum((blk_start[:, None] >= ends[None, :]).astype(I32), axis=1), last_nonempty)
    block_valid = (blk_start < ends[-1]).astype(I32)
    block_rows = jnp.clip((offsets + cnt)[block_expert] - blk_start, 0, TM_EXP) * block_valid
    following = jnp.where(nonempty, expert_ids, N_EXPERTS)
    following = lax.cummin(following, reverse=True)
    following = jnp.concatenate([following[1:], jnp.full((1,), N_EXPERTS, I32)])
    next_expert = jnp.where(following < N_EXPERTS, following, -1)[block_expert]
    last_blk = jnp.where(padded > 0, ends - TM_EXP, -1)
    needs_zero = jnp.any(blk_start[:, None] == last_blk[None, :], axis=1) | (block_valid == 0)
    xs = _dispatch(slot_flat, needs_zero.astype(I32), hp)
    ys = _experts(layer, block_expert, block_rows, next_expert, xs, w_gate, w_up, w_down)
    return _combine(slot_flat, ys, w_tok, hp, x1, g_f, ws_gate.astype(BF16), ws_up.astype(BF16),
                    ws_down.astype(BF16), final_gain, final_norm)


def _rope_tables(dim):
    inv = ROPE_THETA ** (-jnp.arange(0, dim, 2, dtype=F32) / dim)
    ang = jnp.arange(SEQ, dtype=F32)[:, None] * inv[None, :]
    return jnp.cos(ang), jnp.sin(ang)


def _rot_half_cols(w):
    half = w.shape[-1] // 2
    return jnp.concatenate([-w[..., half:], w[..., :half]], axis=-1)


def _t5_bucket(rel):
    half = REL_BUCKETS // 2
    max_exact = half // 2
    ret = (rel > 0).astype(I32) * half
    n = jnp.abs(rel)
    nf = jnp.maximum(n, 1).astype(F32)
    large = max_exact + (jnp.log(nf / max_exact) / math.log(REL_MAX_DIST / max_exact)
                         * (half - max_exact)).astype(I32)
    large = jnp.minimum(large, half - 1)
    return ret + jnp.where(n < max_exact, n, large)


def _swa_bias_table(rel_bias):
    qi = jnp.arange(WINDOW)[:, None]
    kj = jnp.arange(3 * WINDOW)[None, :]
    rel = kj - WINDOW - qi
    onehot = (_t5_bucket(rel)[:, :, None] == jnp.arange(REL_BUCKETS)).astype(F32)
    bias = jnp.einsum('qjb,bh->hqj', onehot, rel_bias.astype(F32), precision=lax.Precision.HIGHEST)
    return jnp.where((jnp.abs(rel) <= WINDOW)[None], bias, NEG_BIG)


def kernel(x, c, w_mod, b_mod, norm_mix, norm_ffn, final_norm, w_in_ab, q_lat_norm, kv_lat_norm, w_uq, w_ukv, conv_w, conv_b, lru_w_a, lru_b_a, lru_w_x, lru_b_x, lru_lambda, w_out_ab, w_in_cd, ret_gn, swa_sinks, w_out_cd, rel_bias, w_router, router_bias, w_gate, w_up, w_down, ws_gate, ws_up, ws_down):
    xf = x.reshape(N_TOK, D_MODEL)
    mod = _modulation(c, w_mod, b_mod)
    cos_r, sin_r = _rope_tables(MLA_ROPE)
    cs_tab = jnp.concatenate([cos_r, cos_r, sin_r, sin_r], axis=1)
    cos_t, sin_t = _rope_tables(RET_DK)
    lg_ret = jnp.log1p(-(2.0 ** (-5.0 - jnp.arange(RET_HEADS, dtype=F32))))

    for layer in range(DEPTH):
        sh_m, sc_m, g_m, sh_f, sc_f, g_f = [m.reshape(BATCH, 1, D_MODEL) for m in jnp.split(mod[layer], 6, axis=-1)]
        i = layer // 2
        if layer % 2 == 0:
            w = w_in_ab[i]
            o1, o2, o3, o4 = np.cumsum((MLA_Q_RANK, MLA_KV_RANK, MLA_ROPE, LRU_WIDTH)).tolist()
            w_kr = w[:, o2:o3]
            w_in = jnp.concatenate([w[:, :o2], w[:, o3:], w_kr, _rot_half_cols(w_kr)], axis=1).astype(BF16)
            p0 = _in_projection(xf, norm_mix[layer], sc_m, sh_m, w_in, "in_proj_ab")
            wq = w_uq[i].reshape(MLA_Q_RANK, MLA_HEADS, MLA_NOPE + MLA_ROPE)
            wq_r = wq[:, :, MLA_NOPE:]
            wq = jnp.concatenate([wq, _rot_half_cols(wq_r)], axis=-1).reshape(MLA_Q_RANK, MLA_HEADS * MLA_QK)
            q, k, v = _mla_up(p0, q_lat_norm[i], kv_lat_norm[i], wq.astype(BF16), w_ukv[i].astype(BF16), cs_tab)
            a_out = _mla_attention(q, k, v)
            w_gates = jnp.concatenate([lru_w_a[i, 0], lru_w_x[i, 0], lru_w_a[i, 1], lru_w_x[i, 1]], axis=-1).astype(BF16)
            b_gates = jnp.concatenate([b.reshape(LRU_BLOCKS, 1, LRU_BS) for b in
                                       (lru_b_a[i, 0], lru_b_x[i, 0], lru_b_a[i, 1], lru_b_x[i, 1])], axis=-1)
            b_out = _rglru(p0, conv_w[i], conv_b[i], w_gates, b_gates, lru_lambda[i])
            w_out = w_out_ab[i].astype(BF16)
        else:
            p1 = _in_projection(xf, norm_mix[layer], sc_m, sh_m, w_in_cd[i].astype(BF16), "in_proj_cd")
            a_out = _retention(p1, lg_ret, cos_t, sin_t, ret_gn[i])
            b_out = _swa(p1, swa_sinks[i], _swa_bias_table(rel_bias))
            w_out = w_out_cd[i].astype(BF16)
        w_r = jnp.pad(w_router[layer], ((0, 0), (0, LANES - N_EXPERTS)))
        w_r_hi = w_r.astype(BF16)
        w_router_pad = jnp.concatenate([w_r_hi, (w_r - w_r_hi.astype(F32)).astype(BF16)], axis=1)
        x1, hp, logits = _out_projection(a_out, b_out, w_out, xf, g_m, norm_ffn[layer], sc_f, sh_f, w_router_pad)
        xf = _moe_layer(layer, hp, logits, x1, g_f, router_bias[layer], w_gate, w_up, w_down,
                        ws_gate[layer], ws_up[layer], ws_down[layer], final_norm, layer == DEPTH - 1)
    return xf.reshape(BATCH, SEQ, D_MODEL)
```

```python
import functools
import math

import numpy as np
import jax
import jax.numpy as jnp
from jax import lax
from jax.experimental import pallas as pl
from jax.experimental.pallas import tpu as pltpu

F32 = jnp.float32
BF16 = jnp.bfloat16
I32 = jnp.int32
U32 = jnp.uint32

D_MODEL = 2048
BATCH = 4
SEQ = 2048
DEPTH = 2
N_TOK = BATCH * SEQ
HALF = D_MODEL // 2
MLA_NOPE = 128
MLA_ROPE = 64
MLA_V = 128
MLA_HEADS = HALF // MLA_V
MLA_Q_RANK = D_MODEL // 4
MLA_KV_RANK = D_MODEL // 4
MLA_QK = 256
LRU_WIDTH = HALF
LRU_BLOCKS = 8
LRU_BS = LRU_WIDTH // LRU_BLOCKS
LRU_CONV = 4
LRU_C = 8.0
RET_DK = 256
RET_DV = 256
RET_HEADS = HALF // RET_DV
SWA_HD = 128
SWA_HEADS = HALF // SWA_HD
SWA_KV_HEADS = 2
SWA_G = SWA_HEADS // SWA_KV_HEADS
WINDOW = 128
REL_BUCKETS = 32
REL_MAX_DIST = 128
N_EXPERTS = 64
TOP_K = 8
N_GROUPS = 8
GROUP_SIZE = N_EXPERTS // N_GROUPS
TOP_GROUPS = 4
EXPERT_FF = D_MODEL // 4
SHARED_FF = D_MODEL // 4
ROUTE_SCALE = 2.5
ROPE_THETA = 10000.0
EPS = 1e-6
NEG_BIG = -1e30

LANES = 128
SUBLANES = 8
VMEM_LIMIT = 52 * 2**20

TM_PROJ = 256
TM_PROJ_SUB = 128
TM_UP = 512
TQ_ATT = 512
TQ_MLA = 2048
TQ_SUB = 256
TM_OUT = 512
TM_OUT_SUB = 256
T_DISP = 256
T_ROUTE = T_DISP
N_TILES = N_TOK // T_DISP
TILE_ROWS = T_DISP * TOP_K
TM_EXP = 512
TM_EXP_SUB = 256
N_SLOTS = N_TOK * TOP_K
NB_EXP = N_SLOTS // TM_EXP + N_EXPERTS
P_ROWS = NB_EXP * TM_EXP
PACK_W = D_MODEL // 2
TOK_ROWS = PACK_W // LANES
assert TOK_ROWS == SUBLANES

LRU_SEG = 260
LRU_ROWS = SUBLANES * LRU_SEG
assert LRU_ROWS >= SEQ and LRU_SEG % 8 == 4


def _cparams(sem, vmem=VMEM_LIMIT):
    return pltpu.CompilerParams(dimension_semantics=sem, vmem_limit_bytes=vmem)


def _sigmoid(x):
    return 0.5 * jnp.tanh(0.5 * x) + 0.5


def _silu(x):
    return x * _sigmoid(x)


def _rms(x, g):
    return x * lax.rsqrt(jnp.mean(x * x, axis=-1, keepdims=True) + EPS) * g


def _pack_bf16_pair(lo, hi):
    lo_b = lax.bitcast_convert_type(lo.astype(BF16).astype(F32), U32)
    hi_b = lax.bitcast_convert_type(hi.astype(BF16).astype(F32), U32)
    return (hi_b & jnp.uint32(0xFFFF0000)) | (lo_b >> 16)


def _unpack_bf16_pair(w):
    lo = lax.bitcast_convert_type(w << 16, F32)
    hi = lax.bitcast_convert_type(w & jnp.uint32(0xFFFF0000), F32)
    return lo, hi


def _store_token_tiles(ref, packed, tok0=0):
    t = packed.shape[0]
    for s in range(TOK_ROWS):
        ref[pl.ds(tok0 * TOK_ROWS + s, t, stride=TOK_ROWS), :] = packed[:, s * LANES:(s + 1) * LANES]


def _load_token_tiles(ref, t):
    return jnp.concatenate([ref[pl.ds(s, t, stride=TOK_ROWS), :] for s in range(TOK_ROWS)], axis=1)


def _mod_kernel(c_ref, w_ref, b_ref, o_ref):
    c = c_ref[...]
    ca = _silu(c).astype(BF16)
    o_ref[0] = jnp.dot(ca, w_ref[0].astype(BF16), preferred_element_type=F32) + b_ref[0]


def _modulation(c, w_mod, b_mod):
    tn = 1024
    cp = jnp.pad(c, ((0, SUBLANES - BATCH), (0, 0)))
    out = pl.pallas_call(
        _mod_kernel,
        grid=(DEPTH, 6 * D_MODEL // tn),
        in_specs=[pl.BlockSpec((SUBLANES, D_MODEL), lambda l, j: (0, 0)),
                  pl.BlockSpec((1, D_MODEL, tn), lambda l, j: (l, 0, j)),
                  pl.BlockSpec((1, 1, tn), lambda l, j: (l, 0, j))],
        out_specs=pl.BlockSpec((1, SUBLANES, tn), lambda l, j: (l, 0, j)),
        out_shape=jax.ShapeDtypeStruct((DEPTH, SUBLANES, 6 * D_MODEL), F32),
        compiler_params=_cparams(("parallel", "parallel")),
        name="adaln_mod",
    )(cp, w_mod, b_mod.reshape(DEPTH, 1, 6 * D_MODEL))
    return out[:, :BATCH]


def _inproj_kernel(x_ref, g_ref, sc_ref, sh_ref, w_ref, o_ref):
    for r in range(TM_PROJ // TM_PROJ_SUB):
        rows = pl.ds(r * TM_PROJ_SUB, TM_PROJ_SUB)
        y = _rms(x_ref[rows, :], g_ref[...])
        h = (y * (1.0 + sc_ref[0]) + sh_ref[0]).astype(BF16)
        o_ref[rows, :] = jnp.dot(h, w_ref[...], preferred_element_type=F32)


def _in_projection(x, gain, scale, shift, w_bf16, name):
    p = w_bf16.shape[1]
    per_b = SEQ // TM_PROJ
    return pl.pallas_call(
        _inproj_kernel,
        grid=(N_TOK // TM_PROJ,),
        in_specs=[pl.BlockSpec((TM_PROJ, D_MODEL), lambda i: (i, 0)),
                  pl.BlockSpec((1, D_MODEL), lambda i: (0, 0)),
                  pl.BlockSpec((1, 1, D_MODEL), lambda i: (i // per_b, 0, 0)),
                  pl.BlockSpec((1, 1, D_MODEL), lambda i: (i // per_b, 0, 0)),
                  pl.BlockSpec((D_MODEL, p), lambda i: (0, 0), pipeline_mode=pl.Buffered(1))],
        out_specs=pl.BlockSpec((TM_PROJ, p), lambda i: (i, 0)),
        out_shape=jax.ShapeDtypeStruct((N_TOK, p), F32),
        compiler_params=_cparams(("parallel",), 56 * 2**20),
        name=name,
    )(x, gain.reshape(1, D_MODEL), scale, shift, w_bf16)


def _mla_up_kernel(ql_ref, kvl_ref, kr_ref, qn_ref, kvn_ref, wq_ref, wkv_ref, cs_ref, q_ref, k_ref, v_ref):
    scale = (MLA_NOPE + MLA_ROPE) ** -0.5 * math.log2(math.e)
    hq = _rms(ql_ref[...], qn_ref[...]).astype(BF16)
    hkv = _rms(kvl_ref[...], kvn_ref[...]).astype(BF16)
    yq = jnp.dot(hq, wq_ref[...], preferred_element_type=F32) * scale
    ykv = jnp.dot(hkv, wkv_ref[...], preferred_element_type=F32)
    cs = cs_ref[...]
    lane = lax.broadcasted_iota(I32, cs.shape, 1)

    def rope_sum(blk):
        z = blk * cs
        return z + pltpu.roll(z, MLA_ROPE, 1)

    kr = jnp.where(lane < MLA_ROPE, rope_sum(kr_ref[...]), 0.0).astype(BF16)
    ones_col = jnp.where(lane == 0, 1.0, 0.0).astype(BF16)
    for h in range(MLA_HEADS):
        c0 = h * MLA_QK
        q_ref[0, h, :, 0:MLA_NOPE] = yq[:, c0:c0 + MLA_NOPE].astype(BF16)
        q_ref[0, h, :, MLA_NOPE:MLA_QK] = rope_sum(yq[:, c0 + MLA_NOPE:c0 + MLA_QK]).astype(BF16)
        k_ref[0, h, :, 0:MLA_NOPE] = ykv[:, c0:c0 + MLA_NOPE].astype(BF16)
        k_ref[0, h, :, MLA_NOPE:MLA_QK] = kr
        v_ref[0, h, :, 0:MLA_V] = ykv[:, c0 + MLA_NOPE:c0 + MLA_QK].astype(BF16)
        v_ref[0, h, :, MLA_V:2 * MLA_V] = ones_col


def _mla_up(p0, q_norm, kv_norm, wq, wkv, cs_tab):
    per_b = SEQ // TM_UP
    qk_shape = jax.ShapeDtypeStruct((BATCH, MLA_HEADS, SEQ, MLA_QK), BF16)
    return pl.pallas_call(
        _mla_up_kernel,
        grid=(N_TOK // TM_UP,),
        in_specs=[pl.BlockSpec((TM_UP, MLA_Q_RANK), lambda i: (i, 0)),
                  pl.BlockSpec((TM_UP, MLA_KV_RANK), lambda i: (i, 1)),
                  pl.BlockSpec((TM_UP, LANES), lambda i: (i, 24)),
                  pl.BlockSpec((1, MLA_Q_RANK), lambda i: (0, 0)),
                  pl.BlockSpec((1, MLA_KV_RANK), lambda i: (0, 0)),
                  pl.BlockSpec((MLA_Q_RANK, MLA_HEADS * MLA_QK), lambda i: (0, 0)),
                  pl.BlockSpec((MLA_KV_RANK, MLA_HEADS * MLA_QK), lambda i: (0, 0)),
                  pl.BlockSpec((TM_UP, LANES), lambda i: (i % per_b, 0))],
        out_specs=[pl.BlockSpec((1, MLA_HEADS, TM_UP, MLA_QK), lambda i: (i // per_b, 0, i % per_b, 0)),
                   pl.BlockSpec((1, MLA_HEADS, TM_UP, MLA_QK), lambda i: (i // per_b, 0, i % per_b, 0)),
                   pl.BlockSpec((1, MLA_HEADS, TM_UP, 2 * MLA_V), lambda i: (i // per_b, 0, i % per_b, 0))],
        out_shape=[qk_shape, qk_shape, jax.ShapeDtypeStruct((BATCH, MLA_HEADS, SEQ, 2 * MLA_V), BF16)],
        compiler_params=_cparams(("parallel",)),
        name="mla_up",
    )(p0, p0, p0, q_norm.reshape(1, -1), kv_norm.reshape(1, -1), wq, wkv, cs_tab)


def _mla_attn_kernel(q_ref, k_ref, v_ref, o_ref):
    k = k_ref[0, 0]
    v = v_ref[0, 0]
    for r in range(TQ_MLA // TQ_SUB):
        rows = pl.ds(r * TQ_SUB, TQ_SUB)
        s = lax.dot_general(q_ref[0, 0, rows, :], k, (((1,), (1,)), ((), ())), preferred_element_type=F32)
        p = jnp.exp2(s - jnp.max(s, axis=-1, keepdims=True))
        o = jnp.dot(p.astype(BF16), v, preferred_element_type=F32)
        o_ref[rows, :] = (o[:, 0:MLA_V] / o[:, MLA_V:MLA_V + 1]).astype(BF16)


def _mla_attention(q, k, v):
    nq = SEQ // TQ_MLA
    return pl.pallas_call(
        _mla_attn_kernel,
        grid=(BATCH, MLA_HEADS, nq),
        in_specs=[pl.BlockSpec((1, 1, TQ_MLA, MLA_QK), lambda b, h, i: (b, h, i, 0)),
                  pl.BlockSpec((1, 1, SEQ, MLA_QK), lambda b, h, i: (b, h, 0, 0)),
                  pl.BlockSpec((1, 1, SEQ, 2 * MLA_V), lambda b, h, i: (b, h, 0, 0))],
        out_specs=pl.BlockSpec((TQ_MLA, MLA_V), lambda b, h, i: (b * nq + i, h)),
        out_shape=jax.ShapeDtypeStruct((N_TOK, HALF), BF16),
        compiler_params=_cparams(("parallel", "parallel", "parallel")),
        name="mla_attn",
    )(q, k, v)


def _lru_kernel(x_ref, gate_ref, cw_ref, cb_ref, wg_ref, bg_ref, lam_ref, o_ref,
                af_ref, uf_ref, ab_ref, ub_ref, hf_ref, pf_ref, hb_ref, pb_ref, hs_ref):
    x = x_ref[...]
    row = lax.broadcasted_iota(I32, x.shape, 0)

    def shifted(d):
        r = pltpu.roll(x, (-d) % SEQ, 0)
        return jnp.where((row + d >= 0) & (row + d < SEQ), r, 0.0)

    cw = cw_ref[...]
    left = LRU_CONV // 2
    xc = cb_ref[...]
    for kk in range(LRU_CONV):
        d = kk - left
        xc = xc + cw[kk:kk + 1] * (x if d == 0 else shifted(d))

    gates = jnp.dot(xc.astype(BF16), wg_ref[0], preferred_element_type=F32) + bg_ref[0]
    lam = lam_ref[...]
    z = -lam
    sp = jnp.maximum(z, 0.0) + jnp.log1p(jnp.exp(-jnp.abs(z)))
    pad_rows = LRU_ROWS - SEQ
    for d, (a_ref, u_ref) in enumerate(((af_ref, uf_ref), (ab_ref, ub_ref))):
        r = _sigmoid(gates[:, d * 256:d * 256 + LRU_BS])
        i = _sigmoid(gates[:, d * 256 + LRU_BS:(d + 1) * 256])
        a = jnp.exp(r * (-LRU_C * sp[d:d + 1]))
        a_ref[0:SEQ] = a
        u_ref[0:SEQ] = jnp.sqrt(1.0 - a * a) * (i * xc)
        a_ref[SEQ:LRU_ROWS] = jnp.zeros((pad_rows, LANES), F32)
        u_ref[SEQ:LRU_ROWS] = jnp.zeros((pad_rows, LANES), F32)

    ones = jnp.ones((SUBLANES, LANES), F32)
    zeros = jnp.zeros((SUBLANES, LANES), F32)

    def seg(t):
        return pl.ds(t, SUBLANES, stride=LRU_SEG)

    def local_scan(s, carry):
        p_f, h_f, p_b, h_b = carry
        tf = s
        tb = LRU_SEG - 1 - s
        a = af_ref[seg(tf)]
        h_f = a * h_f + uf_ref[seg(tf)]
        p_f = a * p_f
        hf_ref[seg(tf)] = h_f
        pf_ref[seg(tf)] = p_f
        a = ab_ref[seg(tb)]
        h_b = a * h_b + ub_ref[seg(tb)]
        p_b = a * p_b
        hb_ref[seg(tb)] = h_b
        pb_ref[seg(tb)] = p_b
        return p_f, h_f, p_b, h_b

    p_f, h_f, p_b, h_b = lax.fori_loop(0, LRU_SEG, local_scan, (ones, zeros, ones, zeros), unroll=4)

    rows_f = []
    c = jnp.zeros((1, LANES), F32)
    for j in range(SUBLANES):
        rows_f.append(c)
        c = p_f[j:j + 1] * c + h_f[j:j + 1]
    rows_b = [None] * SUBLANES
    c = jnp.zeros((1, LANES), F32)
    for j in range(SUBLANES - 1, -1, -1):
        rows_b[j] = c
        c = p_b[j:j + 1] * c + h_b[j:j + 1]
    sub = lax.broadcasted_iota(I32, (SUBLANES, LANES), 0)
    c_f = zeros
    c_b = zeros
    for j in range(SUBLANES):
        c_f = jnp.where(sub == j, rows_f[j], c_f)
        c_b = jnp.where(sub == j, rows_b[j], c_b)

    def fixup(t, _):
        hs_ref[seg(t)] = (hf_ref[seg(t)] + pf_ref[seg(t)] * c_f) + (hb_ref[seg(t)] + pb_ref[seg(t)] * c_b)
        return 0

    lax.fori_loop(0, LRU_SEG, fixup, 0, unroll=4)

    g = gate_ref[...]
    gelu = 0.5 * g * (1.0 + jnp.tanh(math.sqrt(2.0 / math.pi) * (g + 0.044715 * (g * g * g))))
    o_ref[...] = (gelu * hs_ref[0:SEQ]).astype(BF16)


def _rglru(p0, conv_w, conv_b, w_gates, b_gates, lam):
    scan_buf = pltpu.VMEM((LRU_ROWS, LANES), F32)
    return pl.pallas_call(
        _lru_kernel,
        grid=(BATCH, LRU_BLOCKS),
        in_specs=[pl.BlockSpec((SEQ, LRU_BS), lambda b, g: (b, 8 + g)),
                  pl.BlockSpec((SEQ, LRU_BS), lambda b, g: (b, 16 + g)),
                  pl.BlockSpec((LRU_CONV, LRU_BS), lambda b, g: (0, g)),
                  pl.BlockSpec((1, LRU_BS), lambda b, g: (0, g)),
                  pl.BlockSpec((1, LRU_BS, 4 * LRU_BS), lambda b, g: (g, 0, 0)),
                  pl.BlockSpec((1, 1, 4 * LRU_BS), lambda b, g: (g, 0, 0)),
                  pl.BlockSpec((2, LRU_BS), lambda b, g: (0, g))],
        out_specs=pl.BlockSpec((SEQ, LRU_BS), lambda b, g: (b, g)),
        out_shape=jax.ShapeDtypeStruct((N_TOK, LRU_WIDTH), BF16),
        scratch_shapes=[scan_buf] * 9,
        compiler_params=_cparams(("parallel", "parallel")),
        name="rglru",
    )(p0, p0, conv_w, conv_b.reshape(1, -1), w_gates, b_gates, lam)


def _ret_kernel(lg_ref, q_ref, k_ref, v_ref, g_ref, cq_ref, sq_ref, ck_ref, sk_ref, gn_ref, o_ref, ks_ref, vs_ref):
    h = pl.program_id(1)
    qi = pl.program_id(2)
    half = RET_DK // 2

    def rope(t, c, s):
        t1, t2 = t[:, :half], t[:, half:]
        return jnp.concatenate([t1 * c - t2 * s, t2 * c + t1 * s], axis=1)

    @pl.when(qi == 0)
    def _():
        ks_ref[...] = (rope(k_ref[...], ck_ref[...], sk_ref[...]) * (RET_DK ** -0.5)).astype(BF16)
        vs_ref[...] = v_ref[...].astype(BF16)

    q = rope(q_ref[...], cq_ref[...], sq_ref[...]).astype(BF16)
    s = lax.dot_general(q, ks_ref[...], (((1,), (1,)), ((), ())), preferred_element_type=F32)
    n = qi * TQ_ATT + lax.broadcasted_iota(I32, s.shape, 0)
    m = lax.broadcasted_iota(I32, s.shape, 1)
    d = (n - m).astype(F32)
    lg_f = lg_ref[h]
    lg_b = lg_ref[RET_HEADS - 1 - h]
    dec = jnp.exp(jnp.where(d >= 0.0, lg_f * d, -lg_b * d))
    o = jnp.dot((s * dec).astype(BF16), vs_ref[...], preferred_element_type=F32)
    y = _rms(o, gn_ref[0])
    o_ref[...] = (_silu(g_ref[...]) * y).astype(BF16)


def _retention(p1, lg, cos_t, sin_t, ret_gn):
    nq = SEQ // TQ_ATT
    half = RET_DK // 2
    return pl.pallas_call(
        _ret_kernel,
        grid=(BATCH, RET_HEADS, nq),
        in_specs=[pl.BlockSpec(memory_space=pltpu.SMEM),
                  pl.BlockSpec((TQ_ATT, RET_DK), lambda b, h, i: (b * nq + i, h)),
                  pl.BlockSpec((SEQ, RET_DK), lambda b, h, i: (b, RET_HEADS + h)),
                  pl.BlockSpec((SEQ, RET_DV), lambda b, h, i: (b, 2 * RET_HEADS + h)),
                  pl.BlockSpec((TQ_ATT, RET_DV), lambda b, h, i: (b * nq + i, 3 * RET_HEADS + h)),
                  pl.BlockSpec((TQ_ATT, half), lambda b, h, i: (i, 0)),
                  pl.BlockSpec((TQ_ATT, half), lambda b, h, i: (i, 0)),
                  pl.BlockSpec((SEQ, half), lambda b, h, i: (0, 0)),
                  pl.BlockSpec((SEQ, half), lambda b, h, i: (0, 0)),
                  pl.BlockSpec((1, 1, RET_DV), lambda b, h, i: (h, 0, 0))],
        out_specs=pl.BlockSpec((TQ_ATT, RET_DV), lambda b, h, i: (b * nq + i, h)),
        out_shape=jax.ShapeDtypeStruct((N_TOK, HALF), BF16),
        scratch_shapes=[pltpu.VMEM((SEQ, RET_DK), BF16), pltpu.VMEM((SEQ, RET_DV), BF16)],
        compiler_params=_cparams(("parallel", "parallel", "arbitrary")),
        name="retention",
    )(lg, p1, p1, p1, p1, cos_t, sin_t, cos_t, sin_t, ret_gn.reshape(RET_HEADS, 1, RET_DV))


def _swa_kernel(sink_ref, q_ref, k_ref, v_ref, bias_ref, o_ref):
    kv = pl.program_id(1)
    n = pl.program_id(2)
    nb = SEQ // WINDOW
    w = WINDOW
    prev = jnp.maximum(n - 1, 0)
    nxt = jnp.minimum(n + 1, nb - 1)

    def rows(ref, blk):
        return ref[pl.ds(pl.multiple_of(blk * w, w), w), :].astype(BF16)

    kw = jnp.concatenate([rows(k_ref, prev), rows(k_ref, n), rows(k_ref, nxt)], axis=0)
    vw = jnp.concatenate([rows(v_ref, prev), rows(v_ref, n), rows(v_ref, nxt)], axis=0)
    qb = q_ref[...]
    q4 = jnp.concatenate([qb[:, g * SWA_HD:(g + 1) * SWA_HD] for g in range(SWA_G)], axis=0).astype(BF16)
    s = lax.dot_general(q4, kw, (((1,), (1,)), ((), ())), preferred_element_type=F32) * (SWA_HD ** -0.5)
    col = lax.broadcasted_iota(I32, (w, 3 * w), 1)
    outside = ((col < w) & (n == 0)) | ((col >= 2 * w) & (n == nb - 1))
    for g in range(SWA_G):
        sg = jnp.where(outside, NEG_BIG, s[g * w:(g + 1) * w] + bias_ref[g])
        sink = sink_ref[kv * SWA_G + g]
        m = jnp.maximum(jnp.max(sg, axis=-1, keepdims=True), sink)
        p = jnp.exp(sg - m)
        denom = jnp.sum(p, axis=-1, keepdims=True) + jnp.exp(sink - m)
        o = jnp.dot((p / denom).astype(BF16), vw, preferred_element_type=F32)
        o_ref[:, g * SWA_HD:(g + 1) * SWA_HD] = o.astype(BF16)


def _swa(p1, sinks, bias):
    nb = SEQ // WINDOW
    qcols = SWA_G * SWA_HD
    q_blk0 = (4 * RET_HEADS * RET_DK) // qcols
    k_blk0 = (4 * RET_HEADS * RET_DK + SWA_HEADS * SWA_HD) // SWA_HD
    v_blk0 = k_blk0 + SWA_KV_HEADS
    return pl.pallas_call(
        _swa_kernel,
        grid=(BATCH, SWA_KV_HEADS, nb),
        in_specs=[pl.BlockSpec(memory_space=pltpu.SMEM),
                  pl.BlockSpec((WINDOW, qcols), lambda b, kv, n: (b * nb + n, q_blk0 + kv)),
                  pl.BlockSpec((SEQ, SWA_HD), lambda b, kv, n: (b, k_blk0 + kv)),
                  pl.BlockSpec((SEQ, SWA_HD), lambda b, kv, n: (b, v_blk0 + kv)),
                  pl.BlockSpec((SWA_G, WINDOW, 3 * WINDOW), lambda b, kv, n: (kv, 0, 0))],
        out_specs=pl.BlockSpec((WINDOW, qcols), lambda b, kv, n: (b * nb + n, kv)),
        out_shape=jax.ShapeDtypeStruct((N_TOK, HALF), BF16),
        compiler_params=_cparams(("parallel", "parallel", "parallel")),
        name="swa",
    )(sinks, p1, p1, p1, bias)


def _outproj_kernel(a_ref, b_ref, wa_ref, wb_ref, x_ref, gm_ref, g_ref, sc_ref, sh_ref, wr_ref,
                    x1_ref, hp_ref, lg_ref):
    wr = wr_ref[...]
    for r in range(TM_OUT // TM_OUT_SUB):
        rows = pl.ds(r * TM_OUT_SUB, TM_OUT_SUB)
        mixed = (jnp.dot(a_ref[rows, :], wa_ref[...], preferred_element_type=F32)
                 + jnp.dot(b_ref[rows, :], wb_ref[...], preferred_element_type=F32))
        x1 = x_ref[rows, :] + gm_ref[0] * mixed
        x1_ref[rows, :] = x1
        hf = _rms(x1, g_ref[...]) * (1.0 + sc_ref[0]) + sh_ref[0]
        _store_token_tiles(hp_ref, _pack_bf16_pair(hf[:, :PACK_W], hf[:, PACK_W:]), r * TM_OUT_SUB)
        h_hi = hf.astype(BF16)
        h_lo = (hf - h_hi.astype(F32)).astype(BF16)
        t_hi = jnp.dot(h_hi, wr, preferred_element_type=F32)
        t_lo = jnp.dot(h_lo, wr, preferred_element_type=F32)
        lg_ref[rows, :] = (t_hi[:, :LANES] + t_hi[:, LANES:]) + (t_lo[:, :LANES] + t_lo[:, LANES:])


def _out_projection(a, b, w_out_bf16, x, g_m, gain, scale, shift, w_router_pad):
    per_b = SEQ // TM_OUT
    vec = pl.BlockSpec((1, 1, D_MODEL), lambda i: (i // per_b, 0, 0))
    return pl.pallas_call(
        _outproj_kernel,
        grid=(N_TOK // TM_OUT,),
        in_specs=[pl.BlockSpec((TM_OUT, HALF), lambda i: (i, 0)),
                  pl.BlockSpec((TM_OUT, HALF), lambda i: (i, 0)),
                  pl.BlockSpec((HALF, D_MODEL), lambda i: (0, 0)),
                  pl.BlockSpec((HALF, D_MODEL), lambda i: (1, 0)),
                  pl.BlockSpec((TM_OUT, D_MODEL), lambda i: (i, 0)),
                  vec,
                  pl.BlockSpec((1, D_MODEL), lambda i: (0, 0)),
                  vec, vec,
                  pl.BlockSpec((D_MODEL, 2 * LANES), lambda i: (0, 0))],
        out_specs=[pl.BlockSpec((TM_OUT, D_MODEL), lambda i: (i, 0)),
                   pl.BlockSpec((TM_OUT * TOK_ROWS, LANES), lambda i: (i, 0)),
                   pl.BlockSpec((TM_OUT, LANES), lambda i: (i, 0))],
        out_shape=[jax.ShapeDtypeStruct((N_TOK, D_MODEL), F32),
                   jax.ShapeDtypeStruct((N_TOK * TOK_ROWS, LANES), U32),
                   jax.ShapeDtypeStruct((N_TOK, LANES), F32)],
        compiler_params=_cparams(("parallel",)),
        name="out_proj",
    )(a, b, w_out_bf16, w_out_bf16, x, g_m, gain.reshape(1, D_MODEL), scale, shift, w_router_pad)


def _route_kernel(lg_ref, bias_ref, lslot_ref, w_ref, tab_ref, cnt_ref):
    step = pl.program_id(0)

    @pl.when(step == 0)
    def _():
        cnt_ref[...] = jnp.zeros(cnt_ref.shape, F32)

    t = T_ROUTE
    scores = jax.nn.sigmoid(lg_ref[...].T[:N_EXPERTS])
    biased = scores + bias_ref[...]
    sub = lax.broadcasted_iota(I32, (GROUP_SIZE, t), 0).astype(F32)
    ninf = -jnp.inf

    def first_argmax(v, idx, n):
        m = jnp.max(v, axis=0, keepdims=True)
        return m, jnp.min(jnp.where(v == m, idx, float(n)), axis=0, keepdims=True)

    gs = []
    for g in range(N_GROUPS):
        bg = biased[g * GROUP_SIZE:(g + 1) * GROUP_SIZE]
        m1, i1 = first_argmax(bg, sub, GROUP_SIZE)
        m2 = jnp.max(jnp.where(sub == i1, ninf, bg), axis=0, keepdims=True)
        gs.append(m1 + m2)
    cur = jnp.concatenate(gs, axis=0)

    gmask = jnp.zeros((N_GROUPS, t), F32)
    for _ in range(TOP_GROUPS):
        _, i = first_argmax(cur, sub, N_GROUPS)
        pick = sub == i
        gmask = jnp.where(pick, 1.0, gmask)
        cur = jnp.where(pick, ninf, cur)

    eid = lax.broadcasted_iota(I32, (N_EXPERTS, t), 0).astype(F32)
    emask = jnp.concatenate([jnp.broadcast_to(gmask[g:g + 1], (GROUP_SIZE, t)) for g in range(N_GROUPS)], axis=0)
    cur = jnp.where(emask > 0.5, biased, ninf)
    sels, ws = [], []
    onehot = jnp.zeros((N_EXPERTS, t), F32)
    for _ in range(TOP_K):
        _, i = first_argmax(cur, eid, N_EXPERTS)
        pick = eid == i
        sels.append(pick)
        ws.append(jnp.sum(jnp.where(pick, scores, 0.0), axis=0, keepdims=True))
        onehot = jnp.where(pick, 1.0, onehot)
        cur = jnp.where(pick, ninf, cur)
    wsum = ws[0]
    for k in range(1, TOP_K):
        wsum = wsum + ws[k]

    r = lax.broadcasted_iota(I32, (t, t), 0)
    c = lax.broadcasted_iota(I32, (t, t), 1)
    tri = (r < c).astype(BF16)
    earlier = jnp.dot(onehot.astype(BF16), tri, preferred_element_type=F32)
    tile_cnt = jnp.broadcast_to(jnp.sum(onehot, axis=1, keepdims=True), (N_EXPERTS, LANES))
    er = lax.broadcasted_iota(I32, (N_EXPERTS, N_EXPERTS), 0)
    ec = lax.broadcasted_iota(I32, (N_EXPERTS, N_EXPERTS), 1)
    run_start = jnp.dot((ec < er).astype(BF16), tile_cnt.astype(BF16), preferred_element_type=F32)
    pos = earlier + run_start[:, 0:1]
    lslots = [jnp.sum(jnp.where(sels[k], pos, 0.0), axis=0, keepdims=True) for k in range(TOP_K)]

    lslot_ref[...] = jnp.concatenate(lslots, axis=0).astype(I32)
    w_ref[...] = jnp.concatenate([w / wsum * ROUTE_SCALE for w in ws], axis=0)
    tab_ref[0, 0] = tile_cnt
    tab_ref[0, 1] = cnt_ref[...]
    tab_ref[0, 2] = run_start
    cnt_ref[...] = cnt_ref[...] + tile_cnt


def _route(logits, router_bias):
    ntiles = N_TOK // T_ROUTE
    return pl.pallas_call(
        _route_kernel,
        grid=(ntiles,),
        in_specs=[pl.BlockSpec((T_ROUTE, LANES), lambda i: (i, 0)),
                  pl.BlockSpec((N_EXPERTS, 1), lambda i: (0, 0))],
        out_specs=[pl.BlockSpec((TOP_K, T_ROUTE), lambda i: (0, i)),
                   pl.BlockSpec((TOP_K, T_ROUTE), lambda i: (0, i)),
                   pl.BlockSpec((1, 3, N_EXPERTS, LANES), lambda i: (i, 0, 0, 0)),
                   pl.BlockSpec((N_EXPERTS, LANES), lambda i: (0, 0))],
        out_shape=[jax.ShapeDtypeStruct((TOP_K, N_TOK), I32),
                   jax.ShapeDtypeStruct((TOP_K, N_TOK), F32),
                   jax.ShapeDtypeStruct((ntiles, 3, N_EXPERTS, LANES), F32),
                   jax.ShapeDtypeStruct((N_EXPERTS, LANES), F32)],
        compiler_params=_cparams(("arbitrary",)),
        name="route",
    )(logits, router_bias.reshape(N_EXPERTS, 1))


def _dispatch_kernel(lslot_ref, rcnt_ref, rloc_ref, rglb_ref, tail_ref, hp_ref, xs_ref, zero_ref, loc_ref, sem, zsem):
    step = pl.program_id(0)

    @pl.when(step == 0)
    def _():
        zero_ref[...] = jnp.zeros(zero_ref.shape, U32)

        def block_copy(b):
            start = pl.multiple_of(b * (TM_EXP * TOK_ROWS), TM_EXP * TOK_ROWS)
            return pltpu.make_async_copy(zero_ref, xs_ref.at[pl.ds(start, TM_EXP * TOK_ROWS)], zsem)

        def fill(b, _):
            @pl.when(tail_ref[b] > 0)
            def _():
                block_copy(b).start()
            return 0

        def fill_wait(b, _):
            @pl.when(tail_ref[b] > 0)
            def _():
                block_copy(b).wait()
            return 0

        lax.fori_loop(0, NB_EXP, fill, 0)
        lax.fori_loop(0, NB_EXP, fill_wait, 0)

    nsteps = pl.num_programs(0)
    cur = step % 2

    def tile_wait(slot):
        pltpu.make_async_copy(loc_ref.at[slot], xs_ref.at[pl.ds(0, TILE_ROWS * TOK_ROWS)], sem.at[slot]).wait()

    @pl.when(step >= 2)
    def _():
        tile_wait(cur)

    base = step * T_DISP

    def place(t, _):
        row = hp_ref[pl.ds(pl.multiple_of(t * TOK_ROWS, TOK_ROWS), TOK_ROWS), :]
        for k in range(TOP_K):
            dst = pl.multiple_of(lslot_ref[(base + t) * TOP_K + k], TOK_ROWS)
            loc_ref[cur, pl.ds(dst, TOK_ROWS), :] = row
        return 0

    lax.fori_loop(0, T_DISP, place, 0, unroll=2)

    def run(e, _):
        n = rcnt_ref[step * N_EXPERTS + e]

        @pl.when(n > 0)
        def _():
            src = pl.multiple_of(rloc_ref[step * N_EXPERTS + e] * TOK_ROWS, TOK_ROWS)
            dst = pl.multiple_of(rglb_ref[step * N_EXPERTS + e] * TOK_ROWS, TOK_ROWS)
            pltpu.make_async_copy(loc_ref.at[cur, pl.ds(src, n * TOK_ROWS)],
                                  xs_ref.at[pl.ds(dst, n * TOK_ROWS)], sem.at[cur]).start()
        return 0

    lax.fori_loop(0, N_EXPERTS, run, 0)

    @pl.when(step == nsteps - 1)
    def _():
        tile_wait(cur)

        @pl.when(nsteps > 1)
        def _():
            tile_wait(1 - cur)


def _dispatch(lslot_flat, run_cnt, run_loc, run_glb, needs_zero, hp):
    return pl.pallas_call(
        _dispatch_kernel,
        grid_spec=pltpu.PrefetchScalarGridSpec(
            num_scalar_prefetch=5,
            grid=(N_TILES,),
            in_specs=[pl.BlockSpec((T_DISP * TOK_ROWS, LANES), lambda i, *_: (i, 0))],
            out_specs=pl.BlockSpec(memory_space=pl.ANY),
            scratch_shapes=[pltpu.VMEM((TM_EXP * TOK_ROWS, LANES), U32),
                            pltpu.VMEM((2, TILE_ROWS * TOK_ROWS, LANES), U32),
                            pltpu.SemaphoreType.DMA((2,)),
                            pltpu.SemaphoreType.DMA]),
        out_shape=jax.ShapeDtypeStruct((P_ROWS * TOK_ROWS, LANES), U32),
        compiler_params=_cparams(("arbitrary",)),
        name="dispatch",
    )(lslot_flat, run_cnt, run_loc, run_glb, needs_zero, hp)


def _expert_kernel(be_ref, bv_ref, nx_ref, xs_ref, wg_hbm, wu_hbm, wd_hbm, ys_ref,
                   wgs_ref, wus_ref, wds_ref, wgb_ref, wub_ref, wdb_ref, sem, *, layer):
    i = pl.program_id(0)
    e = be_ref[i]
    changed = jnp.logical_or(i == 0, e != be_ref[jnp.maximum(i - 1, 0)])

    def stage(expert):
        return (pltpu.make_async_copy(wg_hbm.at[layer, expert], wgs_ref, sem.at[0]),
                pltpu.make_async_copy(wu_hbm.at[layer, expert], wus_ref, sem.at[1]),
                pltpu.make_async_copy(wd_hbm.at[layer, expert], wds_ref, sem.at[2]))

    @pl.when(changed)
    def _():
        @pl.when(i == 0)
        def _():
            for c in stage(e):
                c.start()

        for c in stage(e):
            c.wait()
        wgb_ref[...] = wgs_ref[...].astype(BF16)
        wub_ref[...] = wus_ref[...].astype(BF16)
        wdb_ref[...] = wds_ref[...].astype(BF16)

        @pl.when(nx_ref[i] >= 0)
        def _():
            for c in stage(nx_ref[i]):
                c.start()

    valid_rows = bv_ref[i]
    for r in range(TM_EXP // TM_EXP_SUB):
        @pl.when(valid_rows > r * TM_EXP_SUB)
        def _():
            xs_sub = xs_ref.at[pl.ds(r * TM_EXP_SUB * TOK_ROWS, TM_EXP_SUB * TOK_ROWS)]
            lo, hi = _unpack_bf16_pair(_load_token_tiles(xs_sub, TM_EXP_SUB))
            lo = lo.astype(BF16)
            hi = hi.astype(BF16)
            hg = (jnp.dot(lo, wgb_ref[0:PACK_W], preferred_element_type=F32)
                  + jnp.dot(hi, wgb_ref[PACK_W:D_MODEL], preferred_element_type=F32))
            hu = (jnp.dot(lo, wub_ref[0:PACK_W], preferred_element_type=F32)
                  + jnp.dot(hi, wub_ref[PACK_W:D_MODEL], preferred_element_type=F32))
            act = (_silu(hg) * hu).astype(BF16)
            y = jnp.dot(act, wdb_ref[...], preferred_element_type=F32)
            _store_token_tiles(ys_ref, _pack_bf16_pair(y[:, :PACK_W], y[:, PACK_W:]), r * TM_EXP_SUB)

        @pl.when(valid_rows <= r * TM_EXP_SUB)
        def _():
            ys_ref[pl.ds(r * TM_EXP_SUB * TOK_ROWS, TM_EXP_SUB * TOK_ROWS), :] = jnp.zeros(
                (TM_EXP_SUB * TOK_ROWS, LANES), U32)


def _experts(layer, block_expert, block_rows, next_expert, xs, w_gate, w_up, w_down):
    return pl.pallas_call(
        functools.partial(_expert_kernel, layer=layer),
        grid_spec=pltpu.PrefetchScalarGridSpec(
            num_scalar_prefetch=3,
            grid=(NB_EXP,),
            in_specs=[pl.BlockSpec((TM_EXP * TOK_ROWS, LANES), lambda i, be, bv, nx: (jnp.where(bv[i] > 0, i, 0), 0)),
                      pl.BlockSpec(memory_space=pl.ANY),
                      pl.BlockSpec(memory_space=pl.ANY),
                      pl.BlockSpec(memory_space=pl.ANY)],
            out_specs=pl.BlockSpec((TM_EXP * TOK_ROWS, LANES), lambda i, be, bv, nx: (i, 0)),
            scratch_shapes=[pltpu.VMEM((D_MODEL, EXPERT_FF), F32),
                            pltpu.VMEM((D_MODEL, EXPERT_FF), F32),
                            pltpu.VMEM((EXPERT_FF, D_MODEL), F32),
                            pltpu.VMEM((D_MODEL, EXPERT_FF), BF16),
                            pltpu.VMEM((D_MODEL, EXPERT_FF), BF16),
                            pltpu.VMEM((EXPERT_FF, D_MODEL), BF16),
                            pltpu.SemaphoreType.DMA((3,))]),
        out_shape=jax.ShapeDtypeStruct((P_ROWS * TOK_ROWS, LANES), U32),
        compiler_params=_cparams(("arbitrary",)),
        name="experts",
    )(block_expert, block_rows, next_expert, xs, w_gate, w_up, w_down)


def _combine_kernel(lslot_ref, rcnt_ref, rloc_ref, rglb_ref, w_ref, ys_ref, hp_ref, x1_ref, gf_ref,
                    wsg_ref, wsu_ref, wsd_ref, fn_ref, o_ref, buf_ref, mlo_ref, mhi_ref, sem, *, final_norm):
    i = pl.program_id(0)
    nsteps = pl.num_programs(0)

    def issue(step, slot):
        def run(e, _):
            n = rcnt_ref[step * N_EXPERTS + e]

            @pl.when(n > 0)
            def _():
                src = pl.multiple_of(rglb_ref[step * N_EXPERTS + e] * TOK_ROWS, TOK_ROWS)
                dst = pl.multiple_of(rloc_ref[step * N_EXPERTS + e] * TOK_ROWS, TOK_ROWS)
                pltpu.make_async_copy(ys_ref.at[pl.ds(src, n * TOK_ROWS)],
                                      buf_ref.at[slot, pl.ds(dst, n * TOK_ROWS)], sem.at[slot]).start()
            return 0

        lax.fori_loop(0, N_EXPERTS, run, 0)

    @pl.when(i == 0)
    def _():
        issue(0, 0)

    @pl.when(i + 1 < nsteps)
    def _():
        issue(i + 1, (i + 1) % 2)

    cur = i % 2
    pltpu.make_async_copy(ys_ref.at[pl.ds(0, TILE_ROWS * TOK_ROWS)], buf_ref.at[cur], sem.at[cur]).wait()

    base = i * T_DISP

    def token(t, _):
        acc_lo = jnp.zeros((TOK_ROWS, LANES), F32)
        acc_hi = jnp.zeros((TOK_ROWS, LANES), F32)
        for k in range(TOP_K):
            idx = (base + t) * TOP_K + k
            src = pl.multiple_of(lslot_ref[idx], TOK_ROWS)
            lo, hi = _unpack_bf16_pair(buf_ref[cur, pl.ds(src, TOK_ROWS), :])
            wk = w_ref[idx]
            acc_lo = acc_lo + wk * lo
            acc_hi = acc_hi + wk * hi
        dst = pl.ds(pl.multiple_of(t * TOK_ROWS, TOK_ROWS), TOK_ROWS)
        mlo_ref[dst, :] = acc_lo
        mhi_ref[dst, :] = acc_hi
        return 0

    lax.fori_loop(0, T_DISP, token, 0, unroll=2)
    moe_lo = _load_token_tiles(mlo_ref, T_DISP)
    moe_hi = _load_token_tiles(mhi_ref, T_DISP)

    hlo, hhi = _unpack_bf16_pair(_load_token_tiles(hp_ref, T_DISP))
    hlo = hlo.astype(BF16)
    hhi = hhi.astype(BF16)
    sg = (jnp.dot(hlo, wsg_ref[0:PACK_W], preferred_element_type=F32)
          + jnp.dot(hhi, wsg_ref[PACK_W:D_MODEL], preferred_element_type=F32))
    su = (jnp.dot(hlo, wsu_ref[0:PACK_W], preferred_element_type=F32)
          + jnp.dot(hhi, wsu_ref[PACK_W:D_MODEL], preferred_element_type=F32))
    shared = jnp.dot((_silu(sg) * su).astype(BF16), wsd_ref[...], preferred_element_type=F32)
    moe = jnp.concatenate([moe_lo, moe_hi], axis=1)
    out = x1_ref[...] + gf_ref[0] * (moe + shared)
    if final_norm:
        out = _rms(out, fn_ref[...])
    o_ref[...] = out


def _combine(lslot_flat, run_cnt, run_loc, run_glb, w_flat, ys, hp, x1, g_f, wsg, wsu, wsd, final_gain, final_norm):
    per_b = SEQ // T_DISP
    return pl.pallas_call(
        functools.partial(_combine_kernel, final_norm=final_norm),
        grid_spec=pltpu.PrefetchScalarGridSpec(
            num_scalar_prefetch=4,
            grid=(N_TILES,),
            in_specs=[pl.BlockSpec(memory_space=pltpu.SMEM),
                      pl.BlockSpec(memory_space=pl.ANY),
                      pl.BlockSpec((T_DISP * TOK_ROWS, LANES), lambda i, *_: (i, 0)),
                      pl.BlockSpec((T_DISP, D_MODEL), lambda i, *_: (i, 0)),
                      pl.BlockSpec((1, 1, D_MODEL), lambda i, *_: (i // per_b, 0, 0)),
                      pl.BlockSpec((D_MODEL, SHARED_FF), lambda i, *_: (0, 0)),
                      pl.BlockSpec((D_MODEL, SHARED_FF), lambda i, *_: (0, 0)),
                      pl.BlockSpec((SHARED_FF, D_MODEL), lambda i, *_: (0, 0)),
                      pl.BlockSpec((1, D_MODEL), lambda i, *_: (0, 0))],
            out_specs=pl.BlockSpec((T_DISP, D_MODEL), lambda i, *_: (i, 0)),
            scratch_shapes=[pltpu.VMEM((2, TILE_ROWS * TOK_ROWS, LANES), U32),
                            pltpu.VMEM((T_DISP * TOK_ROWS, LANES), F32),
                            pltpu.VMEM((T_DISP * TOK_ROWS, LANES), F32),
                            pltpu.SemaphoreType.DMA((2,))]),
        out_shape=jax.ShapeDtypeStruct((N_TOK, D_MODEL), F32),
        compiler_params=_cparams(("arbitrary",)),
        name="combine",
    )(lslot_flat, run_cnt, run_loc, run_glb, w_flat, ys, hp, x1, g_f, wsg, wsu, wsd, final_gain.reshape(1, D_MODEL))


def _moe_layer(layer, hp, logits, x1, g_f, router_bias, w_gate, w_up, w_down, ws_gate, ws_up, ws_down,
               final_gain, final_norm):
    lslot, w_k, tables, counts = _route(logits, router_bias)
    cnt = counts[:, 0].astype(I32)
    padded = ((cnt + TM_EXP - 1) // TM_EXP) * TM_EXP
    ends = jnp.cumsum(padded)
    offsets = ends - padded
    tables = tables[:, :, :, 0].astype(I32)
    run_cnt = tables[:, 0].reshape(-1)
    run_glb = (offsets[None, :] + tables[:, 1]).reshape(-1)
    run_loc = tables[:, 2].reshape(-1)
    lslot_flat = (lslot * TOK_ROWS).T.reshape(N_SLOTS)
    blk_start = jnp.arange(NB_EXP, dtype=I32) * TM_EXP
    expert_ids = jnp.arange(N_EXPERTS, dtype=I32)
    nonempty = cnt > 0
    last_nonempty = jnp.max(jnp.where(nonempty, expert_ids, 0))
    block_expert = jnp.minimum(jnp.sum((blk_start[:, None] >= ends[None, :]).astype(I32), axis=1), last_nonempty)
    block_valid = (blk_start < ends[-1]).astype(I32)
    block_rows = jnp.clip((offsets + cnt)[block_expert] - blk_start, 0, TM_EXP) * block_valid
    following = jnp.where(nonempty, expert_ids, N_EXPERTS)
    following = lax.cummin(following, reverse=True)
    following = jnp.concatenate([following[1:], jnp.full((1,), N_EXPERTS, I32)])
    next_expert = jnp.where(following < N_EXPERTS, following, -1)[block_expert]
    last_blk = jnp.where(padded > 0, ends - TM_EXP, -1)
    needs_zero = jnp.any(blk_start[:, None] == last_blk[None, :], axis=1) | (block_valid == 0)
    xs = _dispatch(lslot_flat, run_cnt, run_loc, run_glb, needs_zero.astype(I32), hp)
    ys = _experts(layer, block_expert, block_rows, next_expert, xs, w_gate, w_up, w_down)
    return _combine(lslot_flat, run_cnt, run_loc, run_glb, w_k.T.reshape(N_SLOTS), ys, hp, x1, g_f,
                    ws_gate.astype(BF16), ws_up.astype(BF16), ws_down.astype(BF16), final_gain, final_norm)


def _rope_tables(dim):
    inv = ROPE_THETA ** (-jnp.arange(0, dim, 2, dtype=F32) / dim)
    ang = jnp.arange(SEQ, dtype=F32)[:, None] * inv[None, :]
    return jnp.cos(ang), jnp.sin(ang)


def _rot_half_cols(w):
    half = w.shape[-1] // 2
    return jnp.concatenate([-w[..., half:], w[..., :half]], axis=-1)


def _t5_bucket(rel):
    half = REL_BUCKETS // 2
    max_exact = half // 2
    ret = (rel > 0).astype(I32) * half
    n = jnp.abs(rel)
    nf = jnp.maximum(n, 1).astype(F32)
    large = max_exact + (jnp.log(nf / max_exact) / math.log(REL_MAX_DIST / max_exact)
                         * (half - max_exact)).astype(I32)
    large = jnp.minimum(large, half - 1)
    return ret + jnp.where(n < max_exact, n, large)


def _swa_bias_table(rel_bias):
    qi = jnp.arange(WINDOW)[:, None]
    kj = jnp.arange(3 * WINDOW)[None, :]
    rel = kj - WINDOW - qi
    onehot = (_t5_bucket(rel)[:, :, None] == jnp.arange(REL_BUCKETS)).astype(F32)
    bias = jnp.einsum('qjb,bh->hqj', onehot, rel_bias.astype(F32), precision=lax.Precision.HIGHEST)
    return jnp.where((jnp.abs(rel) <= WINDOW)[None], bias, NEG_BIG)


def kernel(x, c, w_mod, b_mod, norm_mix, norm_ffn, final_norm, w_in_ab, q_lat_norm, kv_lat_norm, w_uq, w_ukv, conv_w, conv_b, lru_w_a, lru_b_a, lru_w_x, lru_b_x, lru_lambda, w_out_ab, w_in_cd, ret_gn, swa_sinks, w_out_cd, rel_bias, w_router, router_bias, w_gate, w_up, w_down, ws_gate, ws_up, ws_down):
    xf = x.reshape(N_TOK, D_MODEL)
    mod = _modulation(c, w_mod, b_mod)
    cos_r, sin_r = _rope_tables(MLA_ROPE)
    cs_tab = jnp.concatenate([cos_r, cos_r, sin_r, sin_r], axis=1)
    cos_t, sin_t = _rope_tables(RET_DK)
    lg_ret = jnp.log1p(-(2.0 ** (-5.0 - jnp.arange(RET_HEADS, dtype=F32))))

    for layer in range(DEPTH):
        sh_m, sc_m, g_m, sh_f, sc_f, g_f = [m.reshape(BATCH, 1, D_MODEL) for m in jnp.split(mod[layer], 6, axis=-1)]
        i = layer // 2
        if layer % 2 == 0:
            w = w_in_ab[i]
            o1, o2, o3, o4 = np.cumsum((MLA_Q_RANK, MLA_KV_RANK, MLA_ROPE, LRU_WIDTH)).tolist()
            w_kr = w[:, o2:o3]
            w_in = jnp.concatenate([w[:, :o2], w[:, o3:], w_kr, _rot_half_cols(w_kr)], axis=1).astype(BF16)
            p0 = _in_projection(xf, norm_mix[layer], sc_m, sh_m, w_in, "in_proj_ab")
            wq = w_uq[i].reshape(MLA_Q_RANK, MLA_HEADS, MLA_NOPE + MLA_ROPE)
            wq_r = wq[:, :, MLA_NOPE:]
            wq = jnp.concatenate([wq, _rot_half_cols(wq_r)], axis=-1).reshape(MLA_Q_RANK, MLA_HEADS * MLA_QK)
            q, k, v = _mla_up(p0, q_lat_norm[i], kv_lat_norm[i], wq.astype(BF16), w_ukv[i].astype(BF16), cs_tab)
            a_out = _mla_attention(q, k, v)
            w_gates = jnp.concatenate([lru_w_a[i, 0], lru_w_x[i, 0], lru_w_a[i, 1], lru_w_x[i, 1]], axis=-1).astype(BF16)
            b_gates = jnp.concatenate([b.reshape(LRU_BLOCKS, 1, LRU_BS) for b in
                                       (lru_b_a[i, 0], lru_b_x[i, 0], lru_b_a[i, 1], lru_b_x[i, 1])], axis=-1)
            b_out = _rglru(p0, conv_w[i], conv_b[i], w_gates, b_gates, lru_lambda[i])
            w_out = w_out_ab[i].astype(BF16)
        else:
            p1 = _in_projection(xf, norm_mix[layer], sc_m, sh_m, w_in_cd[i].astype(BF16), "in_proj_cd")
            a_out = _retention(p1, lg_ret, cos_t, sin_t, ret_gn[i])
            b_out = _swa(p1, swa_sinks[i], _swa_bias_table(rel_bias))
            w_out = w_out_cd[i].astype(BF16)
        w_r = jnp.pad(w_router[layer], ((0, 0), (0, LANES - N_EXPERTS)))
        w_r_hi = w_r.astype(BF16)
        w_router_pad = jnp.concatenate([w_r_hi, (w_r - w_r_hi.astype(F32)).astype(BF16)], axis=1)
        x1, hp, logits = _out_projection(a_out, b_out, w_out, xf, g_m, norm_ffn[layer], sc_f, sh_f, w_router_pad)
        xf = _moe_layer(layer, hp, logits, x1, g_f, router_bias[layer], w_gate, w_up, w_down,
                        ws_gate[layer], ws_up[layer], ws_down[layer], final_norm, layer == DEPTH - 1)
    return xf.reshape(BATCH, SEQ, D_MODEL)
```

```python
import functools
import math

import numpy as np
import jax
import jax.numpy as jnp
from jax import lax
from jax.experimental import pallas as pl
from jax.experimental.pallas import tpu as pltpu

F32 = jnp.float32
BF16 = jnp.bfloat16
I32 = jnp.int32
U32 = jnp.uint32

D_MODEL = 2048
BATCH = 4
SEQ = 2048
DEPTH = 2
N_TOK = BATCH * SEQ
HALF = D_MODEL // 2
MLA_NOPE = 128
MLA_ROPE = 64
MLA_V = 128
MLA_HEADS = HALF // MLA_V
MLA_Q_RANK = D_MODEL // 4
MLA_KV_RANK = D_MODEL // 4
MLA_QK = 256
LRU_WIDTH = HALF
LRU_BLOCKS = 8
LRU_BS = LRU_WIDTH // LRU_BLOCKS
LRU_CONV = 4
LRU_C = 8.0
RET_DK = 256
RET_DV = 256
RET_HEADS = HALF // RET_DV
SWA_HD = 128
SWA_HEADS = HALF // SWA_HD
SWA_KV_HEADS = 2
SWA_G = SWA_HEADS // SWA_KV_HEADS
WINDOW = 128
REL_BUCKETS = 32
REL_MAX_DIST = 128
N_EXPERTS = 64
TOP_K = 8
N_GROUPS = 8
GROUP_SIZE = N_EXPERTS // N_GROUPS
TOP_GROUPS = 4
EXPERT_FF = D_MODEL // 4
SHARED_FF = D_MODEL // 4
ROUTE_SCALE = 2.5
ROPE_THETA = 10000.0
EPS = 1e-6
NEG_BIG = -1e30

LANES = 128
SUBLANES = 8
VMEM_LIMIT = 52 * 2**20

TM_PROJ = 256
TM_PROJ_SUB = 128
TM_UP = 512
TQ_ATT = 512
TQ_MLA = 2048
TQ_SUB = 256
SWA_BLOCKS = 4
TM_OUT = 512
TM_OUT_SUB = 256
T_DISP = 256
T_ROUTE = T_DISP
N_TILES = N_TOK // T_DISP
TILE_ROWS = T_DISP * TOP_K
TM_EXP = 512
TM_EXP_SUB = 256
assert TM_EXP == 2 * TM_EXP_SUB
CAST_CHUNKS = 8
N_SLOTS = N_TOK * TOP_K
NB_EXP = N_SLOTS // TM_EXP + N_EXPERTS
P_ROWS = NB_EXP * TM_EXP
N_ZERO_RANGES = N_EXPERTS + NB_EXP
PACK_W = D_MODEL // 2
TOK_ROWS = PACK_W // LANES
assert TOK_ROWS == SUBLANES

LRU_SEG = 260
LRU_ROWS = SUBLANES * LRU_SEG
assert LRU_ROWS >= SEQ and LRU_SEG % 8 == 4


def _cparams(sem, vmem=VMEM_LIMIT):
    return pltpu.CompilerParams(dimension_semantics=sem, vmem_limit_bytes=vmem)


def _sigmoid(x):
    return 0.5 * jnp.tanh(0.5 * x) + 0.5


def _silu(x):
    return x * _sigmoid(x)


def _rms(x, g):
    return x * lax.rsqrt(jnp.mean(x * x, axis=-1, keepdims=True) + EPS) * g


def _pack_bf16_pair(lo, hi):
    lo_b = lax.bitcast_convert_type(lo.astype(BF16).astype(F32), U32)
    hi_b = lax.bitcast_convert_type(hi.astype(BF16).astype(F32), U32)
    return (hi_b & jnp.uint32(0xFFFF0000)) | (lo_b >> 16)


def _unpack_bf16_pair(w):
    lo = lax.bitcast_convert_type(w << 16, F32)
    hi = lax.bitcast_convert_type(w & jnp.uint32(0xFFFF0000), F32)
    return lo, hi


def _store_token_tiles(ref, packed, tok0=0):
    t = packed.shape[0]
    for s in range(TOK_ROWS):
        ref[pl.ds(tok0 * TOK_ROWS + s, t, stride=TOK_ROWS), :] = packed[:, s * LANES:(s + 1) * LANES]


def _load_token_tiles(ref, t):
    return jnp.concatenate([ref[pl.ds(s, t, stride=TOK_ROWS), :] for s in range(TOK_ROWS)], axis=1)


def _mod_kernel(c_ref, w_ref, b_ref, o_ref):
    c = c_ref[...]
    ca = _silu(c).astype(BF16)
    o_ref[0] = jnp.dot(ca, w_ref[0].astype(BF16), preferred_element_type=F32) + b_ref[0]


def _modulation(c, w_mod, b_mod):
    tn = 1024
    cp = jnp.pad(c, ((0, SUBLANES - BATCH), (0, 0)))
    out = pl.pallas_call(
        _mod_kernel,
        grid=(DEPTH, 6 * D_MODEL // tn),
        in_specs=[pl.BlockSpec((SUBLANES, D_MODEL), lambda l, j: (0, 0)),
                  pl.BlockSpec((1, D_MODEL, tn), lambda l, j: (l, 0, j)),
                  pl.BlockSpec((1, 1, tn), lambda l, j: (l, 0, j))],
        out_specs=pl.BlockSpec((1, SUBLANES, tn), lambda l, j: (l, 0, j)),
        out_shape=jax.ShapeDtypeStruct((DEPTH, SUBLANES, 6 * D_MODEL), F32),
        compiler_params=_cparams(("parallel", "parallel")),
        name="adaln_mod",
    )(cp, w_mod, b_mod.reshape(DEPTH, 1, 6 * D_MODEL))
    return out[:, :BATCH]


def _inproj_kernel(x_ref, g_ref, sc_ref, sh_ref, w_ref, o_ref):
    for r in range(TM_PROJ // TM_PROJ_SUB):
        rows = pl.ds(r * TM_PROJ_SUB, TM_PROJ_SUB)
        y = _rms(x_ref[rows, :], g_ref[...])
        h = (y * (1.0 + sc_ref[0]) + sh_ref[0]).astype(BF16)
        o_ref[rows, :] = jnp.dot(h, w_ref[...], preferred_element_type=F32)


def _in_projection(x, gain, scale, shift, w_bf16, name):
    p = w_bf16.shape[1]
    per_b = SEQ // TM_PROJ
    return pl.pallas_call(
        _inproj_kernel,
        grid=(N_TOK // TM_PROJ,),
        in_specs=[pl.BlockSpec((TM_PROJ, D_MODEL), lambda i: (i, 0)),
                  pl.BlockSpec((1, D_MODEL), lambda i: (0, 0)),
                  pl.BlockSpec((1, 1, D_MODEL), lambda i: (i // per_b, 0, 0)),
                  pl.BlockSpec((1, 1, D_MODEL), lambda i: (i // per_b, 0, 0)),
                  pl.BlockSpec((D_MODEL, p), lambda i: (0, 0), pipeline_mode=pl.Buffered(1))],
        out_specs=pl.BlockSpec((TM_PROJ, p), lambda i: (i, 0)),
        out_shape=jax.ShapeDtypeStruct((N_TOK, p), F32),
        compiler_params=_cparams(("parallel",), 56 * 2**20),
        name=name,
    )(x, gain.reshape(1, D_MODEL), scale, shift, w_bf16)


def _mla_up_kernel(ql_ref, kvl_ref, kr_ref, qn_ref, kvn_ref, wq_ref, wkv_ref, cs_ref, q_ref, k_ref, v_ref):
    scale = (MLA_NOPE + MLA_ROPE) ** -0.5 * math.log2(math.e)
    hq = _rms(ql_ref[...], qn_ref[...]).astype(BF16)
    hkv = _rms(kvl_ref[...], kvn_ref[...]).astype(BF16)
    yq = jnp.dot(hq, wq_ref[...], preferred_element_type=F32) * scale
    ykv = jnp.dot(hkv, wkv_ref[...], preferred_element_type=F32)
    cs = cs_ref[...]
    lane = lax.broadcasted_iota(I32, cs.shape, 1)

    def rope_sum(blk):
        z = blk * cs
        return z + pltpu.roll(z, MLA_ROPE, 1)

    kr = jnp.where(lane < MLA_ROPE, rope_sum(kr_ref[...]), 0.0).astype(BF16)
    ones_col = jnp.where(lane == 0, 1.0, 0.0).astype(BF16)
    for h in range(MLA_HEADS):
        c0 = h * MLA_QK
        q_ref[0, h, :, 0:MLA_NOPE] = yq[:, c0:c0 + MLA_NOPE].astype(BF16)
        q_ref[0, h, :, MLA_NOPE:MLA_QK] = rope_sum(yq[:, c0 + MLA_NOPE:c0 + MLA_QK]).astype(BF16)
        k_ref[0, h, :, 0:MLA_NOPE] = ykv[:, c0:c0 + MLA_NOPE].astype(BF16)
        k_ref[0, h, :, MLA_NOPE:MLA_QK] = kr
        v_ref[0, h, :, 0:MLA_V] = ykv[:, c0 + MLA_NOPE:c0 + MLA_QK].astype(BF16)
        v_ref[0, h, :, MLA_V:2 * MLA_V] = ones_col


def _mla_up(p0, q_norm, kv_norm, wq, wkv, cs_tab):
    per_b = SEQ // TM_UP
    qk_shape = jax.ShapeDtypeStruct((BATCH, MLA_HEADS, SEQ, MLA_QK), BF16)
    return pl.pallas_call(
        _mla_up_kernel,
        grid=(N_TOK // TM_UP,),
        in_specs=[pl.BlockSpec((TM_UP, MLA_Q_RANK), lambda i: (i, 0)),
                  pl.BlockSpec((TM_UP, MLA_KV_RANK), lambda i: (i, 1)),
                  pl.BlockSpec((TM_UP, LANES), lambda i: (i, 24)),
                  pl.BlockSpec((1, MLA_Q_RANK), lambda i: (0, 0)),
                  pl.BlockSpec((1, MLA_KV_RANK), lambda i: (0, 0)),
                  pl.BlockSpec((MLA_Q_RANK, MLA_HEADS * MLA_QK), lambda i: (0, 0)),
                  pl.BlockSpec((MLA_KV_RANK, MLA_HEADS * MLA_QK), lambda i: (0, 0)),
                  pl.BlockSpec((TM_UP, LANES), lambda i: (i % per_b, 0))],
        out_specs=[pl.BlockSpec((1, MLA_HEADS, TM_UP, MLA_QK), lambda i: (i // per_b, 0, i % per_b, 0)),
                   pl.BlockSpec((1, MLA_HEADS, TM_UP, MLA_QK), lambda i: (i // per_b, 0, i % per_b, 0)),
                   pl.BlockSpec((1, MLA_HEADS, TM_UP, 2 * MLA_V), lambda i: (i // per_b, 0, i % per_b, 0))],
        out_shape=[qk_shape, qk_shape, jax.ShapeDtypeStruct((BATCH, MLA_HEADS, SEQ, 2 * MLA_V), BF16)],
        compiler_params=_cparams(("parallel",)),
        name="mla_up",
    )(p0, p0, p0, q_norm.reshape(1, -1), kv_norm.reshape(1, -1), wq, wkv, cs_tab)


def _mla_attn_kernel(q_ref, k_ref, v_ref, o_ref):
    k = k_ref[0, 0]
    v = v_ref[0, 0]
    for r in range(TQ_MLA // TQ_SUB):
        rows = pl.ds(r * TQ_SUB, TQ_SUB)
        s = lax.dot_general(q_ref[0, 0, rows, :], k, (((1,), (1,)), ((), ())), preferred_element_type=F32)
        p = jnp.exp2(s - jnp.max(s, axis=-1, keepdims=True))
        o = jnp.dot(p.astype(BF16), v, preferred_element_type=F32)
        o_ref[rows, :] = (o[:, 0:MLA_V] / o[:, MLA_V:MLA_V + 1]).astype(BF16)


def _mla_attention(q, k, v):
    nq = SEQ // TQ_MLA
    return pl.pallas_call(
        _mla_attn_kernel,
        grid=(BATCH, MLA_HEADS, nq),
        in_specs=[pl.BlockSpec((1, 1, TQ_MLA, MLA_QK), lambda b, h, i: (b, h, i, 0)),
                  pl.BlockSpec((1, 1, SEQ, MLA_QK), lambda b, h, i: (b, h, 0, 0)),
                  pl.BlockSpec((1, 1, SEQ, 2 * MLA_V), lambda b, h, i: (b, h, 0, 0))],
        out_specs=pl.BlockSpec((TQ_MLA, MLA_V), lambda b, h, i: (b * nq + i, h)),
        out_shape=jax.ShapeDtypeStruct((N_TOK, HALF), BF16),
        compiler_params=_cparams(("parallel", "parallel", "parallel")),
        name="mla_attn",
    )(q, k, v)


def _lru_kernel(x_ref, gate_ref, cw_ref, cb_ref, wg_ref, bg_ref, lam_ref, o_ref,
                af_ref, uf_ref, ab_ref, ub_ref, hf_ref, pf_ref, hb_ref, pb_ref, hs_ref):
    x = x_ref[...]
    row = lax.broadcasted_iota(I32, x.shape, 0)

    def shifted(d):
        r = pltpu.roll(x, (-d) % SEQ, 0)
        return jnp.where((row + d >= 0) & (row + d < SEQ), r, 0.0)

    cw = cw_ref[...]
    left = LRU_CONV // 2
    xc = cb_ref[...]
    for kk in range(LRU_CONV):
        d = kk - left
        xc = xc + cw[kk:kk + 1] * (x if d == 0 else shifted(d))

    gates = jnp.dot(xc.astype(BF16), wg_ref[0], preferred_element_type=F32) + bg_ref[0]
    lam = lam_ref[...]
    z = -lam
    sp = jnp.maximum(z, 0.0) + jnp.log1p(jnp.exp(-jnp.abs(z)))
    pad_rows = LRU_ROWS - SEQ
    for d, (a_ref, u_ref) in enumerate(((af_ref, uf_ref), (ab_ref, ub_ref))):
        r = _sigmoid(gates[:, d * 256:d * 256 + LRU_BS])
        i = _sigmoid(gates[:, d * 256 + LRU_BS:(d + 1) * 256])
        a = jnp.exp(r * (-LRU_C * sp[d:d + 1]))
        a_ref[0:SEQ] = a
        u_ref[0:SEQ] = jnp.sqrt(1.0 - a * a) * (i * xc)
        a_ref[SEQ:LRU_ROWS] = jnp.zeros((pad_rows, LANES), F32)
        u_ref[SEQ:LRU_ROWS] = jnp.zeros((pad_rows, LANES), F32)

    ones = jnp.ones((SUBLANES, LANES), F32)
    zeros = jnp.zeros((SUBLANES, LANES), F32)

    def seg(t):
        return pl.ds(t, SUBLANES, stride=LRU_SEG)

    def local_scan(s, carry):
        p_f, h_f, p_b, h_b = carry
        tf = s
        tb = LRU_SEG - 1 - s
        a = af_ref[seg(tf)]
        h_f = a * h_f + uf_ref[seg(tf)]
        p_f = a * p_f
        hf_ref[seg(tf)] = h_f
        pf_ref[seg(tf)] = p_f
        a = ab_ref[seg(tb)]
        h_b = a * h_b + ub_ref[seg(tb)]
        p_b = a * p_b
        hb_ref[seg(tb)] = h_b
        pb_ref[seg(tb)] = p_b
        return p_f, h_f, p_b, h_b

    p_f, h_f, p_b, h_b = lax.fori_loop(0, LRU_SEG, local_scan, (ones, zeros, ones, zeros), unroll=4)

    rows_f = []
    c = jnp.zeros((1, LANES), F32)
    for j in range(SUBLANES):
        rows_f.append(c)
        c = p_f[j:j + 1] * c + h_f[j:j + 1]
    rows_b = [None] * SUBLANES
    c = jnp.zeros((1, LANES), F32)
    for j in range(SUBLANES - 1, -1, -1):
        rows_b[j] = c
        c = p_b[j:j + 1] * c + h_b[j:j + 1]
    sub = lax.broadcasted_iota(I32, (SUBLANES, LANES), 0)
    c_f = zeros
    c_b = zeros
    for j in range(SUBLANES):
        c_f = jnp.where(sub == j, rows_f[j], c_f)
        c_b = jnp.where(sub == j, rows_b[j], c_b)

    def fixup(t, _):
        hs_ref[seg(t)] = (hf_ref[seg(t)] + pf_ref[seg(t)] * c_f) + (hb_ref[seg(t)] + pb_ref[seg(t)] * c_b)
        return 0

    lax.fori_loop(0, LRU_SEG, fixup, 0, unroll=4)

    g = gate_ref[...]
    gelu = 0.5 * g * (1.0 + jnp.tanh(math.sqrt(2.0 / math.pi) * (g + 0.044715 * (g * g * g))))
    o_ref[...] = (gelu * hs_ref[0:SEQ]).astype(BF16)


def _rglru(p0, conv_w, conv_b, w_gates, b_gates, lam):
    scan_buf = pltpu.VMEM((LRU_ROWS, LANES), F32)
    return pl.pallas_call(
        _lru_kernel,
        grid=(BATCH, LRU_BLOCKS),
        in_specs=[pl.BlockSpec((SEQ, LRU_BS), lambda b, g: (b, 8 + g)),
                  pl.BlockSpec((SEQ, LRU_BS), lambda b, g: (b, 16 + g)),
                  pl.BlockSpec((LRU_CONV, LRU_BS), lambda b, g: (0, g)),
                  pl.BlockSpec((1, LRU_BS), lambda b, g: (0, g)),
                  pl.BlockSpec((1, LRU_BS, 4 * LRU_BS), lambda b, g: (g, 0, 0)),
                  pl.BlockSpec((1, 1, 4 * LRU_BS), lambda b, g: (g, 0, 0)),
                  pl.BlockSpec((2, LRU_BS), lambda b, g: (0, g))],
        out_specs=pl.BlockSpec((SEQ, LRU_BS), lambda b, g: (b, g)),
        out_shape=jax.ShapeDtypeStruct((N_TOK, LRU_WIDTH), BF16),
        scratch_shapes=[scan_buf] * 9,
        compiler_params=_cparams(("parallel", "parallel")),
        name="rglru",
    )(p0, p0, conv_w, conv_b.reshape(1, -1), w_gates, b_gates, lam)


def _ret_kernel(lg_ref, q_ref, k_ref, v_ref, g_ref, cq_ref, sq_ref, ck_ref, sk_ref, gn_ref, o_ref, ks_ref, vs_ref):
    h = pl.program_id(1)
    qi = pl.program_id(2)
    half = RET_DK // 2

    def rope(t, c, s):
        t1, t2 = t[:, :half], t[:, half:]
        return jnp.concatenate([t1 * c - t2 * s, t2 * c + t1 * s], axis=1)

    @pl.when(qi == 0)
    def _():
        ks_ref[...] = (rope(k_ref[...], ck_ref[...], sk_ref[...]) * (RET_DK ** -0.5)).astype(BF16)
        vs_ref[...] = v_ref[...].astype(BF16)

    q = rope(q_ref[...], cq_ref[...], sq_ref[...]).astype(BF16)
    s = lax.dot_general(q, ks_ref[...], (((1,), (1,)), ((), ())), preferred_element_type=F32)
    n = qi * TQ_ATT + lax.broadcasted_iota(I32, s.shape, 0)
    m = lax.broadcasted_iota(I32, s.shape, 1)
    d = (n - m).astype(F32)
    lg_f = lg_ref[h]
    lg_b = lg_ref[RET_HEADS - 1 - h]
    dec = jnp.exp(jnp.where(d >= 0.0, lg_f * d, -lg_b * d))
    o = jnp.dot((s * dec).astype(BF16), vs_ref[...], preferred_element_type=F32)
    y = _rms(o, gn_ref[0])
    o_ref[...] = (_silu(g_ref[...]) * y).astype(BF16)


def _retention(p1, lg, cos_t, sin_t, ret_gn):
    nq = SEQ // TQ_ATT
    half = RET_DK // 2
    return pl.pallas_call(
        _ret_kernel,
        grid=(BATCH, RET_HEADS, nq),
        in_specs=[pl.BlockSpec(memory_space=pltpu.SMEM),
                  pl.BlockSpec((TQ_ATT, RET_DK), lambda b, h, i: (b * nq + i, h)),
                  pl.BlockSpec((SEQ, RET_DK), lambda b, h, i: (b, RET_HEADS + h)),
                  pl.BlockSpec((SEQ, RET_DV), lambda b, h, i: (b, 2 * RET_HEADS + h)),
                  pl.BlockSpec((TQ_ATT, RET_DV), lambda b, h, i: (b * nq + i, 3 * RET_HEADS + h)),
                  pl.BlockSpec((TQ_ATT, half), lambda b, h, i: (i, 0)),
                  pl.BlockSpec((TQ_ATT, half), lambda b, h, i: (i, 0)),
                  pl.BlockSpec((SEQ, half), lambda b, h, i: (0, 0)),
                  pl.BlockSpec((SEQ, half), lambda b, h, i: (0, 0)),
                  pl.BlockSpec((1, 1, RET_DV), lambda b, h, i: (h, 0, 0))],
        out_specs=pl.BlockSpec((TQ_ATT, RET_DV), lambda b, h, i: (b * nq + i, h)),
        out_shape=jax.ShapeDtypeStruct((N_TOK, HALF), BF16),
        scratch_shapes=[pltpu.VMEM((SEQ, RET_DK), BF16), pltpu.VMEM((SEQ, RET_DV), BF16)],
        compiler_params=_cparams(("parallel", "parallel", "arbitrary")),
        name="retention",
    )(lg, p1, p1, p1, p1, cos_t, sin_t, cos_t, sin_t, ret_gn.reshape(RET_HEADS, 1, RET_DV))


def _swa_kernel(sink_ref, q_ref, k_ref, v_ref, bias_ref, o_ref):
    kv = pl.program_id(1)
    nb = SEQ // WINDOW
    w = WINDOW

    def rows(ref, blk):
        return ref[pl.ds(pl.multiple_of(blk * w, w), w), :].astype(BF16)

    col = lax.broadcasted_iota(I32, (w, 3 * w), 1)
    for j in range(SWA_BLOCKS):
        n = pl.program_id(2) * SWA_BLOCKS + j
        prev = jnp.maximum(n - 1, 0)
        nxt = jnp.minimum(n + 1, nb - 1)
        kw = jnp.concatenate([rows(k_ref, prev), rows(k_ref, n), rows(k_ref, nxt)], axis=0)
        vw = jnp.concatenate([rows(v_ref, prev), rows(v_ref, n), rows(v_ref, nxt)], axis=0)
        qb = q_ref[j * w:(j + 1) * w, :]
        q4 = jnp.concatenate([qb[:, g * SWA_HD:(g + 1) * SWA_HD] for g in range(SWA_G)], axis=0).astype(BF16)
        s = lax.dot_general(q4, kw, (((1,), (1,)), ((), ())), preferred_element_type=F32) * (SWA_HD ** -0.5)
        outside = ((col < w) & (n == 0)) | ((col >= 2 * w) & (n == nb - 1))
        for g in range(SWA_G):
            sg = jnp.where(outside, NEG_BIG, s[g * w:(g + 1) * w] + bias_ref[g])
            sink = sink_ref[kv * SWA_G + g]
            m = jnp.maximum(jnp.max(sg, axis=-1, keepdims=True), sink)
            p = jnp.exp(sg - m)
            denom = jnp.sum(p, axis=-1, keepdims=True) + jnp.exp(sink - m)
            o = jnp.dot((p / denom).astype(BF16), vw, preferred_element_type=F32)
            o_ref[j * w:(j + 1) * w, g * SWA_HD:(g + 1) * SWA_HD] = o.astype(BF16)


def _swa(p1, sinks, bias):
    nb = SEQ // WINDOW
    qcols = SWA_G * SWA_HD
    q_blk0 = (4 * RET_HEADS * RET_DK) // qcols
    k_blk0 = (4 * RET_HEADS * RET_DK + SWA_HEADS * SWA_HD) // SWA_HD
    v_blk0 = k_blk0 + SWA_KV_HEADS
    return pl.pallas_call(
        _swa_kernel,
        grid=(BATCH, SWA_KV_HEADS, nb // SWA_BLOCKS),
        in_specs=[pl.BlockSpec(memory_space=pltpu.SMEM),
                  pl.BlockSpec((SWA_BLOCKS * WINDOW, qcols), lambda b, kv, n: (b * (nb // SWA_BLOCKS) + n, q_blk0 + kv)),
                  pl.BlockSpec((SEQ, SWA_HD), lambda b, kv, n: (b, k_blk0 + kv)),
                  pl.BlockSpec((SEQ, SWA_HD), lambda b, kv, n: (b, v_blk0 + kv)),
                  pl.BlockSpec((SWA_G, WINDOW, 3 * WINDOW), lambda b, kv, n: (kv, 0, 0))],
        out_specs=pl.BlockSpec((SWA_BLOCKS * WINDOW, qcols), lambda b, kv, n: (b * (nb // SWA_BLOCKS) + n, kv)),
        out_shape=jax.ShapeDtypeStruct((N_TOK, HALF), BF16),
        compiler_params=_cparams(("parallel", "parallel", "parallel")),
        name="swa",
    )(sinks, p1, p1, p1, bias)


def _outproj_kernel(a_ref, b_ref, wa_ref, wb_ref, x_ref, gm_ref, g_ref, sc_ref, sh_ref, wr_ref,
                    x1_ref, hp_ref, lg_ref):
    wr = wr_ref[...]
    for r in range(TM_OUT // TM_OUT_SUB):
        rows = pl.ds(r * TM_OUT_SUB, TM_OUT_SUB)
        mixed = (jnp.dot(a_ref[rows, :], wa_ref[...], preferred_element_type=F32)
                 + jnp.dot(b_ref[rows, :], wb_ref[...], preferred_element_type=F32))
        x1 = x_ref[rows, :] + gm_ref[0] * mixed
        x1_ref[rows, :] = x1
        hf = _rms(x1, g_ref[...]) * (1.0 + sc_ref[0]) + sh_ref[0]
        _store_token_tiles(hp_ref, _pack_bf16_pair(hf[:, :PACK_W], hf[:, PACK_W:]), r * TM_OUT_SUB)
        h_hi = hf.astype(BF16)
        h_lo = (hf - h_hi.astype(F32)).astype(BF16)
        t_hi = jnp.dot(h_hi, wr, preferred_element_type=F32)
        t_lo = jnp.dot(h_lo, wr, preferred_element_type=F32)
        lg_ref[rows, :] = (t_hi[:, :LANES] + t_hi[:, LANES:]) + (t_lo[:, :LANES] + t_lo[:, LANES:])


def _out_projection(a, b, w_out_bf16, x, g_m, gain, scale, shift, w_router_pad):
    per_b = SEQ // TM_OUT
    vec = pl.BlockSpec((1, 1, D_MODEL), lambda i: (i // per_b, 0, 0))
    return pl.pallas_call(
        _outproj_kernel,
        grid=(N_TOK // TM_OUT,),
        in_specs=[pl.BlockSpec((TM_OUT, HALF), lambda i: (i, 0)),
                  pl.BlockSpec((TM_OUT, HALF), lambda i: (i, 0)),
                  pl.BlockSpec((HALF, D_MODEL), lambda i: (0, 0)),
                  pl.BlockSpec((HALF, D_MODEL), lambda i: (1, 0)),
                  pl.BlockSpec((TM_OUT, D_MODEL), lambda i: (i, 0)),
                  vec,
                  pl.BlockSpec((1, D_MODEL), lambda i: (0, 0)),
                  vec, vec,
                  pl.BlockSpec((D_MODEL, 2 * LANES), lambda i: (0, 0))],
        out_specs=[pl.BlockSpec((TM_OUT, D_MODEL), lambda i: (i, 0)),
                   pl.BlockSpec((TM_OUT * TOK_ROWS, LANES), lambda i: (i, 0)),
                   pl.BlockSpec((TM_OUT, LANES), lambda i: (i, 0))],
        out_shape=[jax.ShapeDtypeStruct((N_TOK, D_MODEL), F32),
                   jax.ShapeDtypeStruct((N_TOK * TOK_ROWS, LANES), U32),
                   jax.ShapeDtypeStruct((N_TOK, LANES), F32)],
        compiler_params=_cparams(("parallel",)),
        name="out_proj",
    )(a, b, w_out_bf16, w_out_bf16, x, g_m, gain.reshape(1, D_MODEL), scale, shift, w_router_pad)


def _route_kernel(lg_ref, bias_ref, lslot_ref, w_ref, tab_ref, cnt_ref):
    step = pl.program_id(0)

    @pl.when(step == 0)
    def _():
        cnt_ref[...] = jnp.zeros(cnt_ref.shape, F32)

    t = T_ROUTE
    scores = jax.nn.sigmoid(lg_ref[...].T[:N_EXPERTS])
    biased = scores + bias_ref[...]
    sub = lax.broadcasted_iota(I32, (GROUP_SIZE, t), 0).astype(F32)
    ninf = -jnp.inf

    def first_argmax(v, idx, n):
        m = jnp.max(v, axis=0, keepdims=True)
        return m, jnp.min(jnp.where(v == m, idx, float(n)), axis=0, keepdims=True)

    gs = []
    for g in range(N_GROUPS):
        bg = biased[g * GROUP_SIZE:(g + 1) * GROUP_SIZE]
        m1, i1 = first_argmax(bg, sub, GROUP_SIZE)
        m2 = jnp.max(jnp.where(sub == i1, ninf, bg), axis=0, keepdims=True)
        gs.append(m1 + m2)
    cur = jnp.concatenate(gs, axis=0)

    gmask = jnp.zeros((N_GROUPS, t), F32)
    for _ in range(TOP_GROUPS):
        _, i = first_argmax(cur, sub, N_GROUPS)
        pick = sub == i
        gmask = jnp.where(pick, 1.0, gmask)
        cur = jnp.where(pick, ninf, cur)

    eid = lax.broadcasted_iota(I32, (N_EXPERTS, t), 0).astype(F32)
    emask = jnp.concatenate([jnp.broadcast_to(gmask[g:g + 1], (GROUP_SIZE, t)) for g in range(N_GROUPS)], axis=0)
    cur = jnp.where(emask > 0.5, biased, ninf)
    sels, ws = [], []
    onehot = jnp.zeros((N_EXPERTS, t), F32)
    for _ in range(TOP_K):
        _, i = first_argmax(cur, eid, N_EXPERTS)
        pick = eid == i
        sels.append(pick)
        ws.append(jnp.sum(jnp.where(pick, scores, 0.0), axis=0, keepdims=True))
        onehot = jnp.where(pick, 1.0, onehot)
        cur = jnp.where(pick, ninf, cur)
    wsum = ws[0]
    for k in range(1, TOP_K):
        wsum = wsum + ws[k]

    r = lax.broadcasted_iota(I32, (t, t), 0)
    c = lax.broadcasted_iota(I32, (t, t), 1)
    tri = (r < c).astype(BF16)
    earlier = jnp.dot(onehot.astype(BF16), tri, preferred_element_type=F32)
    tile_cnt = jnp.broadcast_to(jnp.sum(onehot, axis=1, keepdims=True), (N_EXPERTS, LANES))
    er = lax.broadcasted_iota(I32, (N_EXPERTS, N_EXPERTS), 0)
    ec = lax.broadcasted_iota(I32, (N_EXPERTS, N_EXPERTS), 1)
    run_start = jnp.dot((ec < er).astype(BF16), tile_cnt.astype(BF16), preferred_element_type=F32)
    pos = earlier + run_start[:, 0:1]
    lslots = [jnp.sum(jnp.where(sels[k], pos, 0.0), axis=0, keepdims=True) for k in range(TOP_K)]

    lslot_ref[...] = jnp.concatenate(lslots, axis=0).astype(I32)
    w_ref[...] = jnp.concatenate([w / wsum * ROUTE_SCALE for w in ws], axis=0)
    tab_ref[0, 0] = tile_cnt
    tab_ref[0, 1] = cnt_ref[...]
    tab_ref[0, 2] = run_start
    cnt_ref[...] = cnt_ref[...] + tile_cnt


def _route(logits, router_bias):
    ntiles = N_TOK // T_ROUTE
    return pl.pallas_call(
        _route_kernel,
        grid=(ntiles,),
        in_specs=[pl.BlockSpec((T_ROUTE, LANES), lambda i: (i, 0)),
                  pl.BlockSpec((N_EXPERTS, 1), lambda i: (0, 0))],
        out_specs=[pl.BlockSpec((TOP_K, T_ROUTE), lambda i: (0, i)),
                   pl.BlockSpec((TOP_K, T_ROUTE), lambda i: (0, i)),
                   pl.BlockSpec((1, 3, N_EXPERTS, LANES), lambda i: (i, 0, 0, 0)),
                   pl.BlockSpec((N_EXPERTS, LANES), lambda i: (0, 0))],
        out_shape=[jax.ShapeDtypeStruct((TOP_K, N_TOK), I32),
                   jax.ShapeDtypeStruct((TOP_K, N_TOK), F32),
                   jax.ShapeDtypeStruct((ntiles, 3, N_EXPERTS, LANES), F32),
                   jax.ShapeDtypeStruct((N_EXPERTS, LANES), F32)],
        compiler_params=_cparams(("arbitrary",)),
        name="route",
    )(logits, router_bias.reshape(N_EXPERTS, 1))


def _dispatch_kernel(lslot_ref, rcnt_ref, rloc_ref, rglb_ref, zstart_ref, zlen_ref, hp_ref, xs_ref,
                     zero_ref, loc_ref, sem, zsem):
    step = pl.program_id(0)
    nsteps = pl.num_programs(0)

    def zero_copy(z):
        start = pl.multiple_of(zstart_ref[z] * TOK_ROWS, TOK_ROWS)
        n = zlen_ref[z] * TOK_ROWS
        return pltpu.make_async_copy(zero_ref.at[pl.ds(0, n)], xs_ref.at[pl.ds(start, n)], zsem)

    @pl.when(step == 0)
    def _():
        zero_ref[...] = jnp.zeros(zero_ref.shape, U32)

        def fill(z, _):
            @pl.when(zlen_ref[z] > 0)
            def _():
                zero_copy(z).start()
            return 0

        lax.fori_loop(0, N_ZERO_RANGES, fill, 0)

    @pl.when(step == nsteps - 1)
    def _():
        def fill_wait(z, _):
            @pl.when(zlen_ref[z] > 0)
            def _():
                zero_copy(z).wait()
            return 0

        lax.fori_loop(0, N_ZERO_RANGES, fill_wait, 0)

    cur = step % 2

    def tile_wait(slot):
        pltpu.make_async_copy(loc_ref.at[slot], xs_ref.at[pl.ds(0, TILE_ROWS * TOK_ROWS)], sem.at[slot]).wait()

    @pl.when(step >= 2)
    def _():
        tile_wait(cur)

    base = step * T_DISP

    def place(t, _):
        row = hp_ref[pl.ds(pl.multiple_of(t * TOK_ROWS, TOK_ROWS), TOK_ROWS), :]
        for k in range(TOP_K):
            dst = pl.multiple_of(lslot_ref[(base + t) * TOP_K + k], TOK_ROWS)
            loc_ref[cur, pl.ds(dst, TOK_ROWS), :] = row
        return 0

    lax.fori_loop(0, T_DISP, place, 0, unroll=2)

    def run(e, _):
        n = rcnt_ref[step * N_EXPERTS + e]

        @pl.when(n > 0)
        def _():
            src = pl.multiple_of(rloc_ref[step * N_EXPERTS + e] * TOK_ROWS, TOK_ROWS)
            dst = pl.multiple_of(rglb_ref[step * N_EXPERTS + e] * TOK_ROWS, TOK_ROWS)
            pltpu.make_async_copy(loc_ref.at[cur, pl.ds(src, n * TOK_ROWS)],
                                  xs_ref.at[pl.ds(dst, n * TOK_ROWS)], sem.at[cur]).start()
        return 0

    lax.fori_loop(0, N_EXPERTS, run, 0)

    @pl.when(step == nsteps - 1)
    def _():
        tile_wait(cur)

        @pl.when(nsteps > 1)
        def _():
            tile_wait(1 - cur)


def _dispatch(lslot_flat, run_cnt, run_loc, run_glb, zero_start, zero_len, hp):
    return pl.pallas_call(
        _dispatch_kernel,
        grid_spec=pltpu.PrefetchScalarGridSpec(
            num_scalar_prefetch=6,
            grid=(N_TILES,),
            in_specs=[pl.BlockSpec((T_DISP * TOK_ROWS, LANES), lambda i, *_: (i, 0))],
            out_specs=pl.BlockSpec(memory_space=pl.ANY),
            scratch_shapes=[pltpu.VMEM((TM_EXP * TOK_ROWS, LANES), U32),
                            pltpu.VMEM((2, TILE_ROWS * TOK_ROWS, LANES), U32),
                            pltpu.SemaphoreType.DMA((2,)),
                            pltpu.SemaphoreType.DMA]),
        out_shape=jax.ShapeDtypeStruct((P_ROWS * TOK_ROWS, LANES), U32),
        compiler_params=_cparams(("arbitrary",)),
        name="dispatch",
    )(lslot_flat, run_cnt, run_loc, run_glb, zero_start, zero_len, hp)


def _expert_kernel(be_ref, bv_ref, nx_ref, xs_ref, wg_hbm, wu_hbm, wd_hbm, ys_ref,
                   wgs_ref, wus_ref, wds_ref, wgb_ref, wub_ref, wdb_ref, sem, *, layer):
    i = pl.program_id(0)
    e = be_ref[i]
    changed = jnp.logical_or(i == 0, e != be_ref[jnp.maximum(i - 1, 0)])

    def stage(expert):
        return (pltpu.make_async_copy(wg_hbm.at[layer, expert], wgs_ref, sem.at[0]),
                pltpu.make_async_copy(wu_hbm.at[layer, expert], wus_ref, sem.at[1]),
                pltpu.make_async_copy(wd_hbm.at[layer, expert], wds_ref, sem.at[2]))

    @pl.when(changed)
    def _():
        @pl.when(i == 0)
        def _():
            for c in stage(e):
                c.start()

        for c in stage(e):
            c.wait()
        for src, dst in ((wgs_ref, wgb_ref), (wus_ref, wub_ref), (wds_ref, wdb_ref)):
            rows = src.shape[0] // CAST_CHUNKS
            for c in range(CAST_CHUNKS):
                dst[c * rows:(c + 1) * rows, :] = src[c * rows:(c + 1) * rows, :].astype(BF16)

        @pl.when(nx_ref[i] >= 0)
        def _():
            for c in stage(nx_ref[i]):
                c.start()

    def sub_block(r):
        xs_sub = xs_ref.at[pl.ds(r * TM_EXP_SUB * TOK_ROWS, TM_EXP_SUB * TOK_ROWS)]
        lo, hi = _unpack_bf16_pair(_load_token_tiles(xs_sub, TM_EXP_SUB))
        lo = lo.astype(BF16)
        hi = hi.astype(BF16)
        hg = (jnp.dot(lo, wgb_ref[0:PACK_W], preferred_element_type=F32)
              + jnp.dot(hi, wgb_ref[PACK_W:D_MODEL], preferred_element_type=F32))
        hu = (jnp.dot(lo, wub_ref[0:PACK_W], preferred_element_type=F32)
              + jnp.dot(hi, wub_ref[PACK_W:D_MODEL], preferred_element_type=F32))
        act = (_silu(hg) * hu).astype(BF16)
        y = jnp.dot(act, wdb_ref[...], preferred_element_type=F32)
        _store_token_tiles(ys_ref, _pack_bf16_pair(y[:, :PACK_W], y[:, PACK_W:]), r * TM_EXP_SUB)

    def zero_sub_block(r):
        ys_ref[pl.ds(r * TM_EXP_SUB * TOK_ROWS, TM_EXP_SUB * TOK_ROWS), :] = jnp.zeros(
            (TM_EXP_SUB * TOK_ROWS, LANES), U32)

    valid_rows = bv_ref[i]

    @pl.when(valid_rows > TM_EXP_SUB)
    def _():
        sub_block(0)
        sub_block(1)

    @pl.when(jnp.logical_and(valid_rows > 0, valid_rows <= TM_EXP_SUB))
    def _():
        sub_block(0)
        zero_sub_block(1)

    @pl.when(valid_rows <= 0)
    def _():
        zero_sub_block(0)
        zero_sub_block(1)


def _experts(layer, block_expert, block_rows, next_expert, xs, w_gate, w_up, w_down):
    return pl.pallas_call(
        functools.partial(_expert_kernel, layer=layer),
        grid_spec=pltpu.PrefetchScalarGridSpec(
            num_scalar_prefetch=3,
            grid=(NB_EXP,),
            in_specs=[pl.BlockSpec((TM_EXP * TOK_ROWS, LANES), lambda i, be, bv, nx: (jnp.where(bv[i] > 0, i, 0), 0)),
                      pl.BlockSpec(memory_space=pl.ANY),
                      pl.BlockSpec(memory_space=pl.ANY),
                      pl.BlockSpec(memory_space=pl.ANY)],
            out_specs=pl.BlockSpec((TM_EXP * TOK_ROWS, LANES), lambda i, be, bv, nx: (i, 0)),
            scratch_shapes=[pltpu.VMEM((D_MODEL, EXPERT_FF), F32),
                            pltpu.VMEM((D_MODEL, EXPERT_FF), F32),
                            pltpu.VMEM((EXPERT_FF, D_MODEL), F32),
                            pltpu.VMEM((D_MODEL, EXPERT_FF), BF16),
                            pltpu.VMEM((D_MODEL, EXPERT_FF), BF16),
                            pltpu.VMEM((EXPERT_FF, D_MODEL), BF16),
                            pltpu.SemaphoreType.DMA((3,))]),
        out_shape=jax.ShapeDtypeStruct((P_ROWS * TOK_ROWS, LANES), U32),
        compiler_params=_cparams(("arbitrary",)),
        name="experts",
    )(block_expert, block_rows, next_expert, xs, w_gate, w_up, w_down)


def _combine_kernel(lslot_ref, rcnt_ref, rloc_ref, rglb_ref, w_ref, ys_ref, hp_ref, x1_ref, gf_ref,
                    wsg_ref, wsu_ref, wsd_ref, fn_ref, o_ref, buf_ref, mlo_ref, mhi_ref, sem, *, final_norm):
    i = pl.program_id(0)
    nsteps = pl.num_programs(0)

    def issue(step, slot):
        def run(e, _):
            n = rcnt_ref[step * N_EXPERTS + e]

            @pl.when(n > 0)
            def _():
                src = pl.multiple_of(rglb_ref[step * N_EXPERTS + e] * TOK_ROWS, TOK_ROWS)
                dst = pl.multiple_of(rloc_ref[step * N_EXPERTS + e] * TOK_ROWS, TOK_ROWS)
                pltpu.make_async_copy(ys_ref.at[pl.ds(src, n * TOK_ROWS)],
                                      buf_ref.at[slot, pl.ds(dst, n * TOK_ROWS)], sem.at[slot]).start()
            return 0

        lax.fori_loop(0, N_EXPERTS, run, 0)

    @pl.when(i == 0)
    def _():
        issue(0, 0)

    @pl.when(i + 1 < nsteps)
    def _():
        issue(i + 1, (i + 1) % 2)

    cur = i % 2
    pltpu.make_async_copy(ys_ref.at[pl.ds(0, TILE_ROWS * TOK_ROWS)], buf_ref.at[cur], sem.at[cur]).wait()

    base = i * T_DISP

    def token(t, _):
        acc_lo = jnp.zeros((TOK_ROWS, LANES), F32)
        acc_hi = jnp.zeros((TOK_ROWS, LANES), F32)
        for k in range(TOP_K):
            idx = (base + t) * TOP_K + k
            src = pl.multiple_of(lslot_ref[idx], TOK_ROWS)
            lo, hi = _unpack_bf16_pair(buf_ref[cur, pl.ds(src, TOK_ROWS), :])
            wk = w_ref[idx]
            acc_lo = acc_lo + wk * lo
            acc_hi = acc_hi + wk * hi
        dst = pl.ds(pl.multiple_of(t * TOK_ROWS, TOK_ROWS), TOK_ROWS)
        mlo_ref[dst, :] = acc_lo
        mhi_ref[dst, :] = acc_hi
        return 0

    lax.fori_loop(0, T_DISP, token, 0, unroll=2)
    moe_lo = _load_token_tiles(mlo_ref, T_DISP)
    moe_hi = _load_token_tiles(mhi_ref, T_DISP)

    hlo, hhi = _unpack_bf16_pair(_load_token_tiles(hp_ref, T_DISP))
    hlo = hlo.astype(BF16)
    hhi = hhi.astype(BF16)
    sg = (jnp.dot(hlo, wsg_ref[0:PACK_W], preferred_element_type=F32)
          + jnp.dot(hhi, wsg_ref[PACK_W:D_MODEL], preferred_element_type=F32))
    su = (jnp.dot(hlo, wsu_ref[0:PACK_W], preferred_element_type=F32)
          + jnp.dot(hhi, wsu_ref[PACK_W:D_MODEL], preferred_element_type=F32))
    shared = jnp.dot((_silu(sg) * su).astype(BF16), wsd_ref[...], preferred_element_type=F32)
    moe = jnp.concatenate([moe_lo, moe_hi], axis=1)
    out = x1_ref[...] + gf_ref[0] * (moe + shared)
    if final_norm:
        out = _rms(out, fn_ref[...])
    o_ref[...] = out


def _combine(lslot_flat, run_cnt, run_loc, run_glb, w_flat, ys, hp, x1, g_f, wsg, wsu, wsd, final_gain, final_norm):
    per_b = SEQ // T_DISP
    return pl.pallas_call(
        functools.partial(_combine_kernel, final_norm=final_norm),
        grid_spec=pltpu.PrefetchScalarGridSpec(
            num_scalar_prefetch=4,
            grid=(N_TILES,),
            in_specs=[pl.BlockSpec(memory_space=pltpu.SMEM),
                      pl.BlockSpec(memory_space=pl.ANY),
                      pl.BlockSpec((T_DISP * TOK_ROWS, LANES), lambda i, *_: (i, 0)),
                      pl.BlockSpec((T_DISP, D_MODEL), lambda i, *_: (i, 0)),
                      pl.BlockSpec((1, 1, D_MODEL), lambda i, *_: (i // per_b, 0, 0)),
                      pl.BlockSpec((D_MODEL, SHARED_FF), lambda i, *_: (0, 0)),
                      pl.BlockSpec((D_MODEL, SHARED_FF), lambda i, *_: (0, 0)),
                      pl.BlockSpec((SHARED_FF, D_MODEL), lambda i, *_: (0, 0)),
                      pl.BlockSpec((1, D_MODEL), lambda i, *_: (0, 0))],
            out_specs=pl.BlockSpec((T_DISP, D_MODEL), lambda i, *_: (i, 0)),
            scratch_shapes=[pltpu.VMEM((2, TILE_ROWS * TOK_ROWS, LANES), U32),
                            pltpu.VMEM((T_DISP * TOK_ROWS, LANES), F32),
                            pltpu.VMEM((T_DISP * TOK_ROWS, LANES), F32),
                            pltpu.SemaphoreType.DMA((2,))]),
        out_shape=jax.ShapeDtypeStruct((N_TOK, D_MODEL), F32),
        compiler_params=_cparams(("arbitrary",)),
        name="combine",
    )(lslot_flat, run_cnt, run_loc, run_glb, w_flat, ys, hp, x1, g_f, wsg, wsu, wsd, final_gain.reshape(1, D_MODEL))


def _moe_layer(layer, hp, logits, x1, g_f, router_bias, w_gate, w_up, w_down, ws_gate, ws_up, ws_down,
               final_gain, final_norm):
    lslot, w_k, tables, counts = _route(logits, router_bias)
    cnt = counts[:, 0].astype(I32)
    padded = ((cnt + TM_EXP - 1) // TM_EXP) * TM_EXP
    ends = jnp.cumsum(padded)
    offsets = ends - padded
    tables = tables[:, :, :, 0].astype(I32)
    run_cnt = tables[:, 0].reshape(-1)
    run_glb = (offsets[None, :] + tables[:, 1]).reshape(-1)
    run_loc = tables[:, 2].reshape(-1)
    lslot_flat = (lslot * TOK_ROWS).T.reshape(N_SLOTS)
    blk_start = jnp.arange(NB_EXP, dtype=I32) * TM_EXP
    expert_ids = jnp.arange(N_EXPERTS, dtype=I32)
    nonempty = cnt > 0
    last_nonempty = jnp.max(jnp.where(nonempty, expert_ids, 0))
    block_expert = jnp.minimum(jnp.sum((blk_start[:, None] >= ends[None, :]).astype(I32), axis=1), last_nonempty)
    block_valid = (blk_start < ends[-1]).astype(I32)
    block_rows = jnp.clip((offsets + cnt)[block_expert] - blk_start, 0, TM_EXP) * block_valid
    following = jnp.where(nonempty, expert_ids, N_EXPERTS)
    following = lax.cummin(following, reverse=True)
    following = jnp.concatenate([following[1:], jnp.full((1,), N_EXPERTS, I32)])
    next_expert = jnp.where(following < N_EXPERTS, following, -1)[block_expert]
    zero_start = jnp.concatenate([offsets + cnt, blk_start])
    zero_len = jnp.concatenate([padded - cnt, (1 - block_valid) * TM_EXP])
    xs = _dispatch(lslot_flat, run_cnt, run_loc, run_glb, zero_start, zero_len, hp)
    ys = _experts(layer, block_expert, block_rows, next_expert, xs, w_gate, w_up, w_down)
    return _combine(lslot_flat, run_cnt, run_loc, run_glb, w_k.T.reshape(N_SLOTS), ys, hp, x1, g_f,
                    ws_gate.astype(BF16), ws_up.astype(BF16), ws_down.astype(BF16), final_gain, final_norm)


def _rope_tables(dim):
    inv = ROPE_THETA ** (-jnp.arange(0, dim, 2, dtype=F32) / dim)
    ang = jnp.arange(SEQ, dtype=F32)[:, None] * inv[None, :]
    return jnp.cos(ang), jnp.sin(ang)


def _rot_half_cols(w):
    half = w.shape[-1] // 2
    return jnp.concatenate([-w[..., half:], w[..., :half]], axis=-1)


def _t5_bucket(rel):
    half = REL_BUCKETS // 2
    max_exact = half // 2
    ret = (rel > 0).astype(I32) * half
    n = jnp.abs(rel)
    nf = jnp.maximum(n, 1).astype(F32)
    large = max_exact + (jnp.log(nf / max_exact) / math.log(REL_MAX_DIST / max_exact)
                         * (half - max_exact)).astype(I32)
    large = jnp.minimum(large, half - 1)
    return ret + jnp.where(n < max_exact, n, large)


def _swa_bias_table(rel_bias):
    qi = jnp.arange(WINDOW)[:, None]
    kj = jnp.arange(3 * WINDOW)[None, :]
    rel = kj - WINDOW - qi
    onehot = (_t5_bucket(rel)[:, :, None] == jnp.arange(REL_BUCKETS)).astype(F32)
    bias = jnp.einsum('qjb,bh->hqj', onehot, rel_bias.astype(F32), precision=lax.Precision.HIGHEST)
    return jnp.where((jnp.abs(rel) <= WINDOW)[None], bias, NEG_BIG)


def kernel(x, c, w_mod, b_mod, norm_mix, norm_ffn, final_norm, w_in_ab, q_lat_norm, kv_lat_norm, w_uq, w_ukv, conv_w, conv_b, lru_w_a, lru_b_a, lru_w_x, lru_b_x, lru_lambda, w_out_ab, w_in_cd, ret_gn, swa_sinks, w_out_cd, rel_bias, w_router, router_bias, w_gate, w_up, w_down, ws_gate, ws_up, ws_down):
    xf = x.reshape(N_TOK, D_MODEL)
    mod = _modulation(c, w_mod, b_mod)
    cos_r, sin_r = _rope_tables(MLA_ROPE)
    cs_tab = jnp.concatenate([cos_r, cos_r, sin_r, sin_r], axis=1)
    cos_t, sin_t = _rope_tables(RET_DK)
    lg_ret = jnp.log1p(-(2.0 ** (-5.0 - jnp.arange(RET_HEADS, dtype=F32))))

    for layer in range(DEPTH):
        sh_m, sc_m, g_m, sh_f, sc_f, g_f = [m.reshape(BATCH, 1, D_MODEL) for m in jnp.split(mod[layer], 6, axis=-1)]
        i = layer // 2
        if layer % 2 == 0:
            w = w_in_ab[i]
            o1, o2, o3, o4 = np.cumsum((MLA_Q_RANK, MLA_KV_RANK, MLA_ROPE, LRU_WIDTH)).tolist()
            w_kr = w[:, o2:o3]
            w_in = jnp.concatenate([w[:, :o2], w[:, o3:], w_kr, _rot_half_cols(w_kr)], axis=1).astype(BF16)
            p0 = _in_projection(xf, norm_mix[layer], sc_m, sh_m, w_in, "in_proj_ab")
            wq = w_uq[i].reshape(MLA_Q_RANK, MLA_HEADS, MLA_NOPE + MLA_ROPE)
            wq_r = wq[:, :, MLA_NOPE:]
            wq = jnp.concatenate([wq, _rot_half_cols(wq_r)], axis=-1).reshape(MLA_Q_RANK, MLA_HEADS * MLA_QK)
            q, k, v = _mla_up(p0, q_lat_norm[i], kv_lat_norm[i], wq.astype(BF16), w_ukv[i].astype(BF16), cs_tab)
            a_out = _mla_attention(q, k, v)
            w_gates = jnp.concatenate([lru_w_a[i, 0], lru_w_x[i, 0], lru_w_a[i, 1], lru_w_x[i, 1]], axis=-1).astype(BF16)
            b_gates = jnp.concatenate([b.reshape(LRU_BLOCKS, 1, LRU_BS) for b in
                                       (lru_b_a[i, 0], lru_b_x[i, 0], lru_b_a[i, 1], lru_b_x[i, 1])], axis=-1)
            b_out = _rglru(p0, conv_w[i], conv_b[i], w_gates, b_gates, lru_lambda[i])
            w_out = w_out_ab[i].astype(BF16)
        else:
            p1 = _in_projection(xf, norm_mix[layer], sc_m, sh_m, w_in_cd[i].astype(BF16), "in_proj_cd")
            a_out = _retention(p1, lg_ret, cos_t, sin_t, ret_gn[i])
            b_out = _swa(p1, swa_sinks[i], _swa_bias_table(rel_bias))
            w_out = w_out_cd[i].astype(BF16)
        w_r = jnp.pad(w_router[layer], ((0, 0), (0, LANES - N_EXPERTS)))
        w_r_hi = w_r.astype(BF16)
        w_router_pad = jnp.concatenate([w_r_hi, (w_r - w_r_hi.astype(F32)).astype(BF16)], axis=1)
        x1, hp, logits = _out_projection(a_out, b_out, w_out, xf, g_m, norm_ffn[layer], sc_f, sh_f, w_router_pad)
        xf = _moe_layer(layer, hp, logits, x1, g_f, router_bias[layer], w_gate, w_up, w_down,
                        ws_gate[layer], ws_up[layer], ws_down[layer], final_norm, layer == DEPTH - 1)
    return xf.reshape(BATCH, SEQ, D_MODEL)
```

```python
import functools
import math

import numpy as np
import jax
import jax.numpy as jnp
from jax import lax
from jax.experimental import pallas as pl
from jax.experimental.pallas import tpu as pltpu

F32 = jnp.float32
BF16 = jnp.bfloat16
I32 = jnp.int32
U32 = jnp.uint32

D_MODEL = 2048
BATCH = 4
SEQ = 2048
DEPTH = 2
N_TOK = BATCH * SEQ
HALF = D_MODEL // 2
MLA_NOPE = 128
MLA_ROPE = 64
MLA_V = 128
MLA_HEADS = HALF // MLA_V
MLA_Q_RANK = D_MODEL // 4
MLA_KV_RANK = D_MODEL // 4
MLA_QK = 256
LRU_WIDTH = HALF
LRU_BLOCKS = 8
LRU_BS = LRU_WIDTH // LRU_BLOCKS
LRU_CONV = 4
LRU_C = 8.0
RET_DK = 256
RET_DV = 256
RET_HEADS = HALF // RET_DV
SWA_HD = 128
SWA_HEADS = HALF // SWA_HD
SWA_KV_HEADS = 2
SWA_G = SWA_HEADS // SWA_KV_HEADS
WINDOW = 128
REL_BUCKETS = 32
REL_MAX_DIST = 128
N_EXPERTS = 64
TOP_K = 8
N_GROUPS = 8
GROUP_SIZE = N_EXPERTS // N_GROUPS
TOP_GROUPS = 4
EXPERT_FF = D_MODEL // 4
SHARED_FF = D_MODEL // 4
ROUTE_SCALE = 2.5
ROPE_THETA = 10000.0
EPS = 1e-6
NEG_BIG = -1e30

LANES = 128
SUBLANES = 8
VMEM_LIMIT = 52 * 2**20

TM_PROJ = 256
TM_PROJ_SUB = 128
TM_UP = 512
TQ_ATT = 512
TQ_MLA = 2048
TQ_SUB = 256
SWA_BLOCKS = 4
TM_OUT = 512
TM_OUT_SUB = 256
T_DISP = 256
T_ROUTE = T_DISP
N_TILES = N_TOK // T_DISP
TILE_ROWS = T_DISP * TOP_K
TM_EXP = 512
TM_EXP_SUB = 256
TM_EXP_TAIL = 128
assert TM_EXP == 2 * TM_EXP_SUB and TM_EXP % TM_EXP_TAIL == 0
CAST_VREGS = 32
N_SLOTS = N_TOK * TOP_K
NB_EXP = N_SLOTS // TM_EXP + N_EXPERTS
P_ROWS = NB_EXP * TM_EXP
N_ZERO_RANGES = N_EXPERTS + NB_EXP
PACK_W = D_MODEL // 2
TOK_ROWS = PACK_W // LANES
assert TOK_ROWS == SUBLANES
LOCAL_ROWS = TILE_ROWS * TOK_ROWS

LRU_SEG = 260
LRU_ROWS = SUBLANES * LRU_SEG
assert LRU_ROWS >= SEQ and LRU_SEG % 8 == 4


def _cparams(sem, vmem=VMEM_LIMIT):
    return pltpu.CompilerParams(dimension_semantics=sem, vmem_limit_bytes=vmem)


def _sigmoid(x):
    return 0.5 * jnp.tanh(0.5 * x) + 0.5


def _silu(x):
    return x * _sigmoid(x)


def _rms(x, g):
    return x * lax.rsqrt(jnp.mean(x * x, axis=-1, keepdims=True) + EPS) * g


def _pack_bf16_pair(lo, hi):
    lo_b = lax.bitcast_convert_type(lo.astype(BF16).astype(F32), U32)
    hi_b = lax.bitcast_convert_type(hi.astype(BF16).astype(F32), U32)
    return (hi_b & jnp.uint32(0xFFFF0000)) | (lo_b >> 16)


def _unpack_bf16_pair(w):
    lo = lax.bitcast_convert_type(w << 16, F32)
    hi = lax.bitcast_convert_type(w & jnp.uint32(0xFFFF0000), F32)
    return lo, hi


def _store_token_tiles(ref, packed, tok0=0):
    t = packed.shape[0]
    for s in range(TOK_ROWS):
        ref[pl.ds(tok0 * TOK_ROWS + s, t, stride=TOK_ROWS), :] = packed[:, s * LANES:(s + 1) * LANES]


def _load_token_tiles(ref, t):
    return jnp.concatenate([ref[pl.ds(s, t, stride=TOK_ROWS), :] for s in range(TOK_ROWS)], axis=1)


def _mod_kernel(c_ref, w_ref, b_ref, o_ref):
    c = c_ref[...]
    ca = _silu(c).astype(BF16)
    o_ref[0] = jnp.dot(ca, w_ref[0].astype(BF16), preferred_element_type=F32) + b_ref[0]


def _modulation(c, w_mod, b_mod):
    tn = 1024
    cp = jnp.pad(c, ((0, SUBLANES - BATCH), (0, 0)))
    out = pl.pallas_call(
        _mod_kernel,
        grid=(DEPTH, 6 * D_MODEL // tn),
        in_specs=[pl.BlockSpec((SUBLANES, D_MODEL), lambda l, j: (0, 0)),
                  pl.BlockSpec((1, D_MODEL, tn), lambda l, j: (l, 0, j)),
                  pl.BlockSpec((1, 1, tn), lambda l, j: (l, 0, j))],
        out_specs=pl.BlockSpec((1, SUBLANES, tn), lambda l, j: (l, 0, j)),
        out_shape=jax.ShapeDtypeStruct((DEPTH, SUBLANES, 6 * D_MODEL), F32),
        compiler_params=_cparams(("parallel", "parallel")),
        name="adaln_mod",
    )(cp, w_mod, b_mod.reshape(DEPTH, 1, 6 * D_MODEL))
    return out[:, :BATCH]


def _inproj_kernel(x_ref, g_ref, sc_ref, sh_ref, w_ref, o_ref):
    for r in range(TM_PROJ // TM_PROJ_SUB):
        rows = pl.ds(r * TM_PROJ_SUB, TM_PROJ_SUB)
        y = _rms(x_ref[rows, :], g_ref[...])
        h = (y * (1.0 + sc_ref[0]) + sh_ref[0]).astype(BF16)
        o_ref[rows, :] = jnp.dot(h, w_ref[...], preferred_element_type=F32)


def _in_projection(x, gain, scale, shift, w_bf16, name):
    p = w_bf16.shape[1]
    per_b = SEQ // TM_PROJ
    return pl.pallas_call(
        _inproj_kernel,
        grid=(N_TOK // TM_PROJ,),
        in_specs=[pl.BlockSpec((TM_PROJ, D_MODEL), lambda i: (i, 0)),
                  pl.BlockSpec((1, D_MODEL), lambda i: (0, 0)),
                  pl.BlockSpec((1, 1, D_MODEL), lambda i: (i // per_b, 0, 0)),
                  pl.BlockSpec((1, 1, D_MODEL), lambda i: (i // per_b, 0, 0)),
                  pl.BlockSpec((D_MODEL, p), lambda i: (0, 0), pipeline_mode=pl.Buffered(1))],
        out_specs=pl.BlockSpec((TM_PROJ, p), lambda i: (i, 0)),
        out_shape=jax.ShapeDtypeStruct((N_TOK, p), F32),
        compiler_params=_cparams(("parallel",), 56 * 2**20),
        name=name,
    )(x, gain.reshape(1, D_MODEL), scale, shift, w_bf16)


def _mla_up_kernel(ql_ref, kvl_ref, kr_ref, qn_ref, kvn_ref, wq_ref, wkv_ref, cs_ref, q_ref, k_ref, v_ref):
    scale = (MLA_NOPE + MLA_ROPE) ** -0.5 * math.log2(math.e)
    hq = _rms(ql_ref[...], qn_ref[...]).astype(BF16)
    hkv = _rms(kvl_ref[...], kvn_ref[...]).astype(BF16)
    yq = jnp.dot(hq, wq_ref[...], preferred_element_type=F32) * scale
    ykv = jnp.dot(hkv, wkv_ref[...], preferred_element_type=F32)
    cs = cs_ref[...]
    lane = lax.broadcasted_iota(I32, cs.shape, 1)

    def rope_sum(blk):
        z = blk * cs
        return z + pltpu.roll(z, MLA_ROPE, 1)

    kr = jnp.where(lane < MLA_ROPE, rope_sum(kr_ref[...]), 0.0).astype(BF16)
    ones_col = jnp.where(lane == 0, 1.0, 0.0).astype(BF16)
    for h in range(MLA_HEADS):
        c0 = h * MLA_QK
        q_ref[0, h, :, 0:MLA_NOPE] = yq[:, c0:c0 + MLA_NOPE].astype(BF16)
        q_ref[0, h, :, MLA_NOPE:MLA_QK] = rope_sum(yq[:, c0 + MLA_NOPE:c0 + MLA_QK]).astype(BF16)
        k_ref[0, h, :, 0:MLA_NOPE] = ykv[:, c0:c0 + MLA_NOPE].astype(BF16)
        k_ref[0, h, :, MLA_NOPE:MLA_QK] = kr
        v_ref[0, h, :, 0:MLA_V] = ykv[:, c0 + MLA_NOPE:c0 + MLA_QK].astype(BF16)
        v_ref[0, h, :, MLA_V:2 * MLA_V] = ones_col


def _mla_up(p0, q_norm, kv_norm, wq, wkv, cs_tab):
    per_b = SEQ // TM_UP
    qk_shape = jax.ShapeDtypeStruct((BATCH, MLA_HEADS, SEQ, MLA_QK), BF16)
    return pl.pallas_call(
        _mla_up_kernel,
        grid=(N_TOK // TM_UP,),
        in_specs=[pl.BlockSpec((TM_UP, MLA_Q_RANK), lambda i: (i, 0)),
                  pl.BlockSpec((TM_UP, MLA_KV_RANK), lambda i: (i, 1)),
                  pl.BlockSpec((TM_UP, LANES), lambda i: (i, 24)),
                  pl.BlockSpec((1, MLA_Q_RANK), lambda i: (0, 0)),
                  pl.BlockSpec((1, MLA_KV_RANK), lambda i: (0, 0)),
                  pl.BlockSpec((MLA_Q_RANK, MLA_HEADS * MLA_QK), lambda i: (0, 0)),
                  pl.BlockSpec((MLA_KV_RANK, MLA_HEADS * MLA_QK), lambda i: (0, 0)),
                  pl.BlockSpec((TM_UP, LANES), lambda i: (i % per_b, 0))],
        out_specs=[pl.BlockSpec((1, MLA_HEADS, TM_UP, MLA_QK), lambda i: (i // per_b, 0, i % per_b, 0)),
                   pl.BlockSpec((1, MLA_HEADS, TM_UP, MLA_QK), lambda i: (i // per_b, 0, i % per_b, 0)),
                   pl.BlockSpec((1, MLA_HEADS, TM_UP, 2 * MLA_V), lambda i: (i // per_b, 0, i % per_b, 0))],
        out_shape=[qk_shape, qk_shape, jax.ShapeDtypeStruct((BATCH, MLA_HEADS, SEQ, 2 * MLA_V), BF16)],
        compiler_params=_cparams(("parallel",)),
        name="mla_up",
    )(p0, p0, p0, q_norm.reshape(1, -1), kv_norm.reshape(1, -1), wq, wkv, cs_tab)


def _mla_attn_kernel(q_ref, k_ref, v_ref, o_ref):
    k = k_ref[0, 0]
    v = v_ref[0, 0]
    for r in range(TQ_MLA // TQ_SUB):
        rows = pl.ds(r * TQ_SUB, TQ_SUB)
        s = lax.dot_general(q_ref[0, 0, rows, :], k, (((1,), (1,)), ((), ())), preferred_element_type=F32)
        p = jnp.exp2(s - jnp.max(s, axis=-1, keepdims=True))
        o = jnp.dot(p.astype(BF16), v, preferred_element_type=F32)
        o_ref[rows, :] = (o[:, 0:MLA_V] / o[:, MLA_V:MLA_V + 1]).astype(BF16)


def _mla_attention(q, k, v):
    nq = SEQ // TQ_MLA
    return pl.pallas_call(
        _mla_attn_kernel,
        grid=(BATCH, MLA_HEADS, nq),
        in_specs=[pl.BlockSpec((1, 1, TQ_MLA, MLA_QK), lambda b, h, i: (b, h, i, 0)),
                  pl.BlockSpec((1, 1, SEQ, MLA_QK), lambda b, h, i: (b, h, 0, 0)),
                  pl.BlockSpec((1, 1, SEQ, 2 * MLA_V), lambda b, h, i: (b, h, 0, 0))],
        out_specs=pl.BlockSpec((TQ_MLA, MLA_V), lambda b, h, i: (b * nq + i, h)),
        out_shape=jax.ShapeDtypeStruct((N_TOK, HALF), BF16),
        compiler_params=_cparams(("parallel", "parallel", "parallel")),
        name="mla_attn",
    )(q, k, v)


def _lru_kernel(x_ref, gate_ref, cw_ref, cb_ref, wg_ref, bg_ref, lam_ref, o_ref,
                af_ref, uf_ref, ab_ref, ub_ref, hf_ref, pf_ref, hb_ref, pb_ref, hs_ref):
    x = x_ref[...]
    row = lax.broadcasted_iota(I32, x.shape, 0)

    def shifted(d):
        r = pltpu.roll(x, (-d) % SEQ, 0)
        return jnp.where((row + d >= 0) & (row + d < SEQ), r, 0.0)

    cw = cw_ref[...]
    left = LRU_CONV // 2
    xc = cb_ref[...]
    for kk in range(LRU_CONV):
        d = kk - left
        xc = xc + cw[kk:kk + 1] * (x if d == 0 else shifted(d))

    gates = jnp.dot(xc.astype(BF16), wg_ref[0], preferred_element_type=F32) + bg_ref[0]
    lam = lam_ref[...]
    z = -lam
    sp = jnp.maximum(z, 0.0) + jnp.log1p(jnp.exp(-jnp.abs(z)))
    pad_rows = LRU_ROWS - SEQ
    for d, (a_ref, u_ref) in enumerate(((af_ref, uf_ref), (ab_ref, ub_ref))):
        r = _sigmoid(gates[:, d * 256:d * 256 + LRU_BS])
        i = _sigmoid(gates[:, d * 256 + LRU_BS:(d + 1) * 256])
        a = jnp.exp(r * (-LRU_C * sp[d:d + 1]))
        a_ref[0:SEQ] = a
        u_ref[0:SEQ] = jnp.sqrt(1.0 - a * a) * (i * xc)
        a_ref[SEQ:LRU_ROWS] = jnp.zeros((pad_rows, LANES), F32)
        u_ref[SEQ:LRU_ROWS] = jnp.zeros((pad_rows, LANES), F32)

    ones = jnp.ones((SUBLANES, LANES), F32)
    zeros = jnp.zeros((SUBLANES, LANES), F32)

    def seg(t):
        return pl.ds(t, SUBLANES, stride=LRU_SEG)

    def local_scan(s, carry):
        p_f, h_f, p_b, h_b = carry
        tf = s
        tb = LRU_SEG - 1 - s
        a = af_ref[seg(tf)]
        h_f = a * h_f + uf_ref[seg(tf)]
        p_f = a * p_f
        hf_ref[seg(tf)] = h_f
        pf_ref[seg(tf)] = p_f
        a = ab_ref[seg(tb)]
        h_b = a * h_b + ub_ref[seg(tb)]
        p_b = a * p_b
        hb_ref[seg(tb)] = h_b
        pb_ref[seg(tb)] = p_b
        return p_f, h_f, p_b, h_b

    p_f, h_f, p_b, h_b = lax.fori_loop(0, LRU_SEG, local_scan, (ones, zeros, ones, zeros), unroll=4)

    rows_f = []
    c = jnp.zeros((1, LANES), F32)
    for j in range(SUBLANES):
        rows_f.append(c)
        c = p_f[j:j + 1] * c + h_f[j:j + 1]
    rows_b = [None] * SUBLANES
    c = jnp.zeros((1, LANES), F32)
    for j in range(SUBLANES - 1, -1, -1):
        rows_b[j] = c
        c = p_b[j:j + 1] * c + h_b[j:j + 1]
    sub = lax.broadcasted_iota(I32, (SUBLANES, LANES), 0)
    c_f = zeros
    c_b = zeros
    for j in range(SUBLANES):
        c_f = jnp.where(sub == j, rows_f[j], c_f)
        c_b = jnp.where(sub == j, rows_b[j], c_b)

    def fixup(t, _):
        hs_ref[seg(t)] = (hf_ref[seg(t)] + pf_ref[seg(t)] * c_f) + (hb_ref[seg(t)] + pb_ref[seg(t)] * c_b)
        return 0

    lax.fori_loop(0, LRU_SEG, fixup, 0, unroll=4)

    g = gate_ref[...]
    gelu = 0.5 * g * (1.0 + jnp.tanh(math.sqrt(2.0 / math.pi) * (g + 0.044715 * (g * g * g))))
    o_ref[...] = (gelu * hs_ref[0:SEQ]).astype(BF16)


def _rglru(p0, conv_w, conv_b, w_gates, b_gates, lam):
    scan_buf = pltpu.VMEM((LRU_ROWS, LANES), F32)
    return pl.pallas_call(
        _lru_kernel,
        grid=(BATCH, LRU_BLOCKS),
        in_specs=[pl.BlockSpec((SEQ, LRU_BS), lambda b, g: (b, 8 + g)),
                  pl.BlockSpec((SEQ, LRU_BS), lambda b, g: (b, 16 + g)),
                  pl.BlockSpec((LRU_CONV, LRU_BS), lambda b, g: (0, g)),
                  pl.BlockSpec((1, LRU_BS), lambda b, g: (0, g)),
                  pl.BlockSpec((1, LRU_BS, 4 * LRU_BS), lambda b, g: (g, 0, 0)),
                  pl.BlockSpec((1, 1, 4 * LRU_BS), lambda b, g: (g, 0, 0)),
                  pl.BlockSpec((2, LRU_BS), lambda b, g: (0, g))],
        out_specs=pl.BlockSpec((SEQ, LRU_BS), lambda b, g: (b, g)),
        out_shape=jax.ShapeDtypeStruct((N_TOK, LRU_WIDTH), BF16),
        scratch_shapes=[scan_buf] * 9,
        compiler_params=_cparams(("parallel", "parallel")),
        name="rglru",
    )(p0, p0, conv_w, conv_b.reshape(1, -1), w_gates, b_gates, lam)


def _ret_kernel(lg_ref, q_ref, k_ref, v_ref, g_ref, cq_ref, sq_ref, ck_ref, sk_ref, gn_ref, o_ref, ks_ref, vs_ref):
    h = pl.program_id(1)
    qi = pl.program_id(2)
    half = RET_DK // 2

    def rope(t, c, s):
        t1, t2 = t[:, :half], t[:, half:]
        return jnp.concatenate([t1 * c - t2 * s, t2 * c + t1 * s], axis=1)

    @pl.when(qi == 0)
    def _():
        ks_ref[...] = (rope(k_ref[...], ck_ref[...], sk_ref[...]) * (RET_DK ** -0.5)).astype(BF16)
        vs_ref[...] = v_ref[...].astype(BF16)

    q = rope(q_ref[...], cq_ref[...], sq_ref[...]).astype(BF16)
    s = lax.dot_general(q, ks_ref[...], (((1,), (1,)), ((), ())), preferred_element_type=F32)
    n = qi * TQ_ATT + lax.broadcasted_iota(I32, s.shape, 0)
    m = lax.broadcasted_iota(I32, s.shape, 1)
    d = (n - m).astype(F32)
    lg_f = lg_ref[h]
    lg_b = lg_ref[RET_HEADS - 1 - h]
    dec = jnp.exp(jnp.where(d >= 0.0, lg_f * d, -lg_b * d))
    o = jnp.dot((s * dec).astype(BF16), vs_ref[...], preferred_element_type=F32)
    y = _rms(o, gn_ref[0])
    o_ref[...] = (_silu(g_ref[...]) * y).astype(BF16)


def _retention(p1, lg, cos_t, sin_t, ret_gn):
    nq = SEQ // TQ_ATT
    half = RET_DK // 2
    return pl.pallas_call(
        _ret_kernel,
        grid=(BATCH, RET_HEADS, nq),
        in_specs=[pl.BlockSpec(memory_space=pltpu.SMEM),
                  pl.BlockSpec((TQ_ATT, RET_DK), lambda b, h, i: (b * nq + i, h)),
                  pl.BlockSpec((SEQ, RET_DK), lambda b, h, i: (b, RET_HEADS + h)),
                  pl.BlockSpec((SEQ, RET_DV), lambda b, h, i: (b, 2 * RET_HEADS + h)),
                  pl.BlockSpec((TQ_ATT, RET_DV), lambda b, h, i: (b * nq + i, 3 * RET_HEADS + h)),
                  pl.BlockSpec((TQ_ATT, half), lambda b, h, i: (i, 0)),
                  pl.BlockSpec((TQ_ATT, half), lambda b, h, i: (i, 0)),
                  pl.BlockSpec((SEQ, half), lambda b, h, i: (0, 0)),
                  pl.BlockSpec((SEQ, half), lambda b, h, i: (0, 0)),
                  pl.BlockSpec((1, 1, RET_DV), lambda b, h, i: (h, 0, 0))],
        out_specs=pl.BlockSpec((TQ_ATT, RET_DV), lambda b, h, i: (b * nq + i, h)),
        out_shape=jax.ShapeDtypeStruct((N_TOK, HALF), BF16),
        scratch_shapes=[pltpu.VMEM((SEQ, RET_DK), BF16), pltpu.VMEM((SEQ, RET_DV), BF16)],
        compiler_params=_cparams(("parallel", "parallel", "arbitrary")),
        name="retention",
    )(lg, p1, p1, p1, p1, cos_t, sin_t, cos_t, sin_t, ret_gn.reshape(RET_HEADS, 1, RET_DV))


def _swa_kernel(sink_ref, q_ref, k_ref, v_ref, bias_ref, o_ref):
    kv = pl.program_id(1)
    nb = SEQ // WINDOW
    w = WINDOW

    def rows(ref, blk):
        return ref[pl.ds(pl.multiple_of(blk * w, w), w), :].astype(BF16)

    col = lax.broadcasted_iota(I32, (w, 3 * w), 1)
    for j in range(SWA_BLOCKS):
        n = pl.program_id(2) * SWA_BLOCKS + j
        prev = jnp.maximum(n - 1, 0)
        nxt = jnp.minimum(n + 1, nb - 1)
        kw = jnp.concatenate([rows(k_ref, prev), rows(k_ref, n), rows(k_ref, nxt)], axis=0)
        vw = jnp.concatenate([rows(v_ref, prev), rows(v_ref, n), rows(v_ref, nxt)], axis=0)
        qb = q_ref[j * w:(j + 1) * w, :]
        q4 = jnp.concatenate([qb[:, g * SWA_HD:(g + 1) * SWA_HD] for g in range(SWA_G)], axis=0).astype(BF16)
        s = lax.dot_general(q4, kw, (((1,), (1,)), ((), ())), preferred_element_type=F32) * (SWA_HD ** -0.5)
        outside = ((col < w) & (n == 0)) | ((col >= 2 * w) & (n == nb - 1))
        for g in range(SWA_G):
            sg = jnp.where(outside, NEG_BIG, s[g * w:(g + 1) * w] + bias_ref[g])
            sink = sink_ref[kv * SWA_G + g]
            m = jnp.maximum(jnp.max(sg, axis=-1, keepdims=True), sink)
            p = jnp.exp(sg - m)
            denom = jnp.sum(p, axis=-1, keepdims=True) + jnp.exp(sink - m)
            o = jnp.dot((p / denom).astype(BF16), vw, preferred_element_type=F32)
            o_ref[j * w:(j + 1) * w, g * SWA_HD:(g + 1) * SWA_HD] = o.astype(BF16)


def _swa(p1, sinks, bias):
    nb = SEQ // WINDOW
    qcols = SWA_G * SWA_HD
    q_blk0 = (4 * RET_HEADS * RET_DK) // qcols
    k_blk0 = (4 * RET_HEADS * RET_DK + SWA_HEADS * SWA_HD) // SWA_HD
    v_blk0 = k_blk0 + SWA_KV_HEADS
    return pl.pallas_call(
        _swa_kernel,
        grid=(BATCH, SWA_KV_HEADS, nb // SWA_BLOCKS),
        in_specs=[pl.BlockSpec(memory_space=pltpu.SMEM),
                  pl.BlockSpec((SWA_BLOCKS * WINDOW, qcols), lambda b, kv, n: (b * (nb // SWA_BLOCKS) + n, q_blk0 + kv)),
                  pl.BlockSpec((SEQ, SWA_HD), lambda b, kv, n: (b, k_blk0 + kv)),
                  pl.BlockSpec((SEQ, SWA_HD), lambda b, kv, n: (b, v_blk0 + kv)),
                  pl.BlockSpec((SWA_G, WINDOW, 3 * WINDOW), lambda b, kv, n: (kv, 0, 0))],
        out_specs=pl.BlockSpec((SWA_BLOCKS * WINDOW, qcols), lambda b, kv, n: (b * (nb // SWA_BLOCKS) + n, kv)),
        out_shape=jax.ShapeDtypeStruct((N_TOK, HALF), BF16),
        compiler_params=_cparams(("parallel", "parallel", "parallel")),
        name="swa",
    )(sinks, p1, p1, p1, bias)


def _outproj_kernel(a_ref, b_ref, wa_ref, wb_ref, x_ref, gm_ref, g_ref, sc_ref, sh_ref, wr_ref,
                    x1_ref, hp_ref, lg_ref):
    wr = wr_ref[...]
    for r in range(TM_OUT // TM_OUT_SUB):
        rows = pl.ds(r * TM_OUT_SUB, TM_OUT_SUB)
        mixed = (jnp.dot(a_ref[rows, :], wa_ref[...], preferred_element_type=F32)
                 + jnp.dot(b_ref[rows, :], wb_ref[...], preferred_element_type=F32))
        x1 = x_ref[rows, :] + gm_ref[0] * mixed
        x1_ref[rows, :] = x1
        hf = _rms(x1, g_ref[...]) * (1.0 + sc_ref[0]) + sh_ref[0]
        _store_token_tiles(hp_ref, _pack_bf16_pair(hf[:, :PACK_W], hf[:, PACK_W:]), r * TM_OUT_SUB)
        h_hi = hf.astype(BF16)
        h_lo = (hf - h_hi.astype(F32)).astype(BF16)
        t_hi = jnp.dot(h_hi, wr, preferred_element_type=F32)
        t_lo = jnp.dot(h_lo, wr, preferred_element_type=F32)
        lg_ref[rows, :] = (t_hi[:, :LANES] + t_hi[:, LANES:]) + (t_lo[:, :LANES] + t_lo[:, LANES:])


def _out_projection(a, b, w_out_bf16, x, g_m, gain, scale, shift, w_router_pad):
    per_b = SEQ // TM_OUT
    vec = pl.BlockSpec((1, 1, D_MODEL), lambda i: (i // per_b, 0, 0))
    return pl.pallas_call(
        _outproj_kernel,
        grid=(N_TOK // TM_OUT,),
        in_specs=[pl.BlockSpec((TM_OUT, HALF), lambda i: (i, 0)),
                  pl.BlockSpec((TM_OUT, HALF), lambda i: (i, 0)),
                  pl.BlockSpec((HALF, D_MODEL), lambda i: (0, 0)),
                  pl.BlockSpec((HALF, D_MODEL), lambda i: (1, 0)),
                  pl.BlockSpec((TM_OUT, D_MODEL), lambda i: (i, 0)),
                  vec,
                  pl.BlockSpec((1, D_MODEL), lambda i: (0, 0)),
                  vec, vec,
                  pl.BlockSpec((D_MODEL, 2 * LANES), lambda i: (0, 0))],
        out_specs=[pl.BlockSpec((TM_OUT, D_MODEL), lambda i: (i, 0)),
                   pl.BlockSpec((TM_OUT * TOK_ROWS, LANES), lambda i: (i, 0)),
                   pl.BlockSpec((TM_OUT, LANES), lambda i: (i, 0))],
        out_shape=[jax.ShapeDtypeStruct((N_TOK, D_MODEL), F32),
                   jax.ShapeDtypeStruct((N_TOK * TOK_ROWS, LANES), U32),
                   jax.ShapeDtypeStruct((N_TOK, LANES), F32)],
        compiler_params=_cparams(("parallel",)),
        name="out_proj",
    )(a, b, w_out_bf16, w_out_bf16, x, g_m, gain.reshape(1, D_MODEL), scale, shift, w_router_pad)


def _route_kernel(lg_ref, bias_ref, lslot_ref, w_ref, tab_ref, cnt_ref):
    step = pl.program_id(0)

    @pl.when(step == 0)
    def _():
        cnt_ref[...] = jnp.zeros(cnt_ref.shape, F32)

    t = T_ROUTE
    scores = jax.nn.sigmoid(lg_ref[...].T[:N_EXPERTS])
    biased = scores + bias_ref[...]
    sub = lax.broadcasted_iota(I32, (GROUP_SIZE, t), 0).astype(F32)
    ninf = -jnp.inf

    def first_argmax(v, idx, n):
        m = jnp.max(v, axis=0, keepdims=True)
        return m, jnp.min(jnp.where(v == m, idx, float(n)), axis=0, keepdims=True)

    gs = []
    for g in range(N_GROUPS):
        bg = biased[g * GROUP_SIZE:(g + 1) * GROUP_SIZE]
        m1, i1 = first_argmax(bg, sub, GROUP_SIZE)
        m2 = jnp.max(jnp.where(sub == i1, ninf, bg), axis=0, keepdims=True)
        gs.append(m1 + m2)
    cur = jnp.concatenate(gs, axis=0)

    gmask = jnp.zeros((N_GROUPS, t), F32)
    for _ in range(TOP_GROUPS):
        _, i = first_argmax(cur, sub, N_GROUPS)
        pick = sub == i
        gmask = jnp.where(pick, 1.0, gmask)
        cur = jnp.where(pick, ninf, cur)

    eid = lax.broadcasted_iota(I32, (N_EXPERTS, t), 0).astype(F32)
    emask = jnp.concatenate([jnp.broadcast_to(gmask[g:g + 1], (GROUP_SIZE, t)) for g in range(N_GROUPS)], axis=0)
    cur = jnp.where(emask > 0.5, biased, ninf)
    sels, ws = [], []
    onehot = jnp.zeros((N_EXPERTS, t), F32)
    for _ in range(TOP_K):
        _, i = first_argmax(cur, eid, N_EXPERTS)
        pick = eid == i
        sels.append(pick)
        ws.append(jnp.sum(jnp.where(pick, scores, 0.0), axis=0, keepdims=True))
        onehot = jnp.where(pick, 1.0, onehot)
        cur = jnp.where(pick, ninf, cur)
    wsum = ws[0]
    for k in range(1, TOP_K):
        wsum = wsum + ws[k]

    r = lax.broadcasted_iota(I32, (t, t), 0)
    c = lax.broadcasted_iota(I32, (t, t), 1)
    tri = (r < c).astype(BF16)
    earlier = jnp.dot(onehot.astype(BF16), tri, preferred_element_type=F32)
    tile_cnt = jnp.broadcast_to(jnp.sum(onehot, axis=1, keepdims=True), (N_EXPERTS, LANES))
    er = lax.broadcasted_iota(I32, (N_EXPERTS, N_EXPERTS), 0)
    ec = lax.broadcasted_iota(I32, (N_EXPERTS, N_EXPERTS), 1)
    run_start = jnp.dot((ec < er).astype(BF16), tile_cnt.astype(BF16), preferred_element_type=F32)
    pos = earlier + run_start[:, 0:1]
    lslots = [jnp.sum(jnp.where(sels[k], pos, 0.0), axis=0, keepdims=True) for k in range(TOP_K)]

    lslot_ref[...] = jnp.concatenate(lslots, axis=0).astype(I32)
    w_ref[...] = jnp.concatenate([w / wsum * ROUTE_SCALE for w in ws], axis=0)
    tab_ref[0, 0] = tile_cnt
    tab_ref[0, 1] = cnt_ref[...]
    tab_ref[0, 2] = run_start
    cnt_ref[...] = cnt_ref[...] + tile_cnt


def _route(logits, router_bias):
    ntiles = N_TOK // T_ROUTE
    return pl.pallas_call(
        _route_kernel,
        grid=(ntiles,),
        in_specs=[pl.BlockSpec((T_ROUTE, LANES), lambda i: (i, 0)),
                  pl.BlockSpec((N_EXPERTS, 1), lambda i: (0, 0))],
        out_specs=[pl.BlockSpec((TOP_K, T_ROUTE), lambda i: (0, i)),
                   pl.BlockSpec((TOP_K, T_ROUTE), lambda i: (0, i)),
                   pl.BlockSpec((1, 3, N_EXPERTS, LANES), lambda i: (i, 0, 0, 0)),
                   pl.BlockSpec((N_EXPERTS, LANES), lambda i: (0, 0))],
        out_shape=[jax.ShapeDtypeStruct((TOP_K, N_TOK), I32),
                   jax.ShapeDtypeStruct((TOP_K, N_TOK), F32),
                   jax.ShapeDtypeStruct((ntiles, 3, N_EXPERTS, LANES), F32),
                   jax.ShapeDtypeStruct((N_EXPERTS, LANES), F32)],
        compiler_params=_cparams(("arbitrary",)),
        name="route",
    )(logits, router_bias.reshape(N_EXPERTS, 1))


def _dispatch_kernel(lslot_ref, rcnt_ref, rloc_ref, rglb_ref, zstart_ref, zlen_ref, hp_ref, xs_ref,
                     zero_ref, loc_ref, sem, zsem):
    step = pl.program_id(0)
    nsteps = pl.num_programs(0)

    def zero_copy(z):
        start = pl.multiple_of(zstart_ref[z] * TOK_ROWS, TOK_ROWS)
        n = zlen_ref[z] * TOK_ROWS
        return pltpu.make_async_copy(zero_ref.at[pl.ds(0, n)], xs_ref.at[pl.ds(start, n)], zsem)

    @pl.when(step == 0)
    def _():
        zero_ref[...] = jnp.zeros(zero_ref.shape, U32)

        def fill(z, _):
            @pl.when(zlen_ref[z] > 0)
            def _():
                zero_copy(z).start()
            return 0

        lax.fori_loop(0, N_ZERO_RANGES, fill, 0)

    @pl.when(step == nsteps - 1)
    def _():
        def fill_wait(z, _):
            @pl.when(zlen_ref[z] > 0)
            def _():
                zero_copy(z).wait()
            return 0

        lax.fori_loop(0, N_ZERO_RANGES, fill_wait, 0)

    cur = step % 2

    def tile_wait(slot):
        half = pl.ds(pl.multiple_of(slot * LOCAL_ROWS, LOCAL_ROWS), LOCAL_ROWS)
        pltpu.make_async_copy(loc_ref.at[half], xs_ref.at[pl.ds(0, LOCAL_ROWS)], sem.at[slot]).wait()

    @pl.when(step >= 2)
    def _():
        tile_wait(cur)

    base = step * T_DISP

    def place(t, _):
        row = hp_ref[pl.ds(pl.multiple_of(t * TOK_ROWS, TOK_ROWS), TOK_ROWS), :]
        for k in range(TOP_K):
            dst = pl.multiple_of(lslot_ref[(base + t) * TOP_K + k], TOK_ROWS)
            loc_ref[pl.ds(dst, TOK_ROWS), :] = row
        return 0

    lax.fori_loop(0, T_DISP, place, 0, unroll=2)

    def run(e, _):
        n = rcnt_ref[step * N_EXPERTS + e]

        @pl.when(n > 0)
        def _():
            src = pl.multiple_of(rloc_ref[step * N_EXPERTS + e], TOK_ROWS)
            dst = pl.multiple_of(rglb_ref[step * N_EXPERTS + e] * TOK_ROWS, TOK_ROWS)
            pltpu.make_async_copy(loc_ref.at[pl.ds(src, n * TOK_ROWS)],
                                  xs_ref.at[pl.ds(dst, n * TOK_ROWS)], sem.at[cur]).start()
        return 0

    lax.fori_loop(0, N_EXPERTS, run, 0)

    @pl.when(step == nsteps - 1)
    def _():
        tile_wait(cur)

        @pl.when(nsteps > 1)
        def _():
            tile_wait(1 - cur)


def _dispatch(lslot_flat, run_cnt, run_loc, run_glb, zero_start, zero_len, hp):
    return pl.pallas_call(
        _dispatch_kernel,
        grid_spec=pltpu.PrefetchScalarGridSpec(
            num_scalar_prefetch=6,
            grid=(N_TILES,),
            in_specs=[pl.BlockSpec((T_DISP * TOK_ROWS, LANES), lambda i, *_: (i, 0))],
            out_specs=pl.BlockSpec(memory_space=pl.ANY),
            scratch_shapes=[pltpu.VMEM((TM_EXP * TOK_ROWS, LANES), U32),
                            pltpu.VMEM((2 * LOCAL_ROWS, LANES), U32),
                            pltpu.SemaphoreType.DMA((2,)),
                            pltpu.SemaphoreType.DMA]),
        out_shape=jax.ShapeDtypeStruct((P_ROWS * TOK_ROWS, LANES), U32),
        compiler_params=_cparams(("arbitrary",)),
        name="dispatch",
    )(lslot_flat, run_cnt, run_loc, run_glb, zero_start, zero_len, hp)


def _expert_kernel(be_ref, bv_ref, nx_ref, xs_ref, wg_hbm, wu_hbm, wd_hbm, ys_ref,
                   wgs_ref, wus_ref, wds_ref, wgb_ref, wub_ref, wdb_ref, sem, *, layer):
    i = pl.program_id(0)
    e = be_ref[i]
    changed = jnp.logical_or(i == 0, e != be_ref[jnp.maximum(i - 1, 0)])

    def stage(expert):
        return (pltpu.make_async_copy(wg_hbm.at[layer, expert], wgs_ref, sem.at[0]),
                pltpu.make_async_copy(wu_hbm.at[layer, expert], wus_ref, sem.at[1]),
                pltpu.make_async_copy(wd_hbm.at[layer, expert], wds_ref, sem.at[2]))

    @pl.when(changed)
    def _():
        @pl.when(i == 0)
        def _():
            for c in stage(e):
                c.start()

        for c in stage(e):
            c.wait()
        for src, dst in ((wgs_ref, wgb_ref), (wus_ref, wub_ref), (wds_ref, wdb_ref)):
            rows = CAST_VREGS * SUBLANES * LANES // src.shape[1]

            def cast_piece(c, _, src=src, dst=dst, rows=rows):
                sl = pl.ds(pl.multiple_of(c * rows, rows), rows)
                dst[sl, :] = src[sl, :].astype(BF16)
                return 0

            lax.fori_loop(0, src.shape[0] // rows, cast_piece, 0, unroll=2)

        @pl.when(nx_ref[i] >= 0)
        def _():
            for c in stage(nx_ref[i]):
                c.start()

    def sub_block(row0, nrows):
        xs_sub = xs_ref.at[pl.ds(row0 * TOK_ROWS, nrows * TOK_ROWS)]
        lo, hi = _unpack_bf16_pair(_load_token_tiles(xs_sub, nrows))
        lo = lo.astype(BF16)
        hi = hi.astype(BF16)
        hg = (jnp.dot(lo, wgb_ref[0:PACK_W], preferred_element_type=F32)
              + jnp.dot(hi, wgb_ref[PACK_W:D_MODEL], preferred_element_type=F32))
        hu = (jnp.dot(lo, wub_ref[0:PACK_W], preferred_element_type=F32)
              + jnp.dot(hi, wub_ref[PACK_W:D_MODEL], preferred_element_type=F32))
        act = (_silu(hg) * hu).astype(BF16)
        y = jnp.dot(act, wdb_ref[...], preferred_element_type=F32)
        _store_token_tiles(ys_ref, _pack_bf16_pair(y[:, :PACK_W], y[:, PACK_W:]), row0)

    valid_rows = bv_ref[i]
    full_from = TM_EXP - TM_EXP_TAIL

    @pl.when(valid_rows > full_from)
    def _():
        sub_block(0, TM_EXP_SUB)
        sub_block(TM_EXP_SUB, TM_EXP_SUB)

    @pl.when(valid_rows <= full_from)
    def _():
        for r in range(TM_EXP // TM_EXP_TAIL):
            @pl.when(valid_rows > r * TM_EXP_TAIL)
            def _():
                sub_block(r * TM_EXP_TAIL, TM_EXP_TAIL)

            @pl.when(valid_rows <= r * TM_EXP_TAIL)
            def _():
                ys_ref[pl.ds(r * TM_EXP_TAIL * TOK_ROWS, TM_EXP_TAIL * TOK_ROWS), :] = jnp.zeros(
                    (TM_EXP_TAIL * TOK_ROWS, LANES), U32)


def _experts(layer, block_expert, block_rows, next_expert, xs, w_gate, w_up, w_down):
    return pl.pallas_call(
        functools.partial(_expert_kernel, layer=layer),
        grid_spec=pltpu.PrefetchScalarGridSpec(
            num_scalar_prefetch=3,
            grid=(NB_EXP,),
            in_specs=[pl.BlockSpec((TM_EXP * TOK_ROWS, LANES), lambda i, be, bv, nx: (jnp.where(bv[i] > 0, i, 0), 0)),
                      pl.BlockSpec(memory_space=pl.ANY),
                      pl.BlockSpec(memory_space=pl.ANY),
                      pl.BlockSpec(memory_space=pl.ANY)],
            out_specs=pl.BlockSpec((TM_EXP * TOK_ROWS, LANES), lambda i, be, bv, nx: (i, 0)),
            scratch_shapes=[pltpu.VMEM((D_MODEL, EXPERT_FF), F32),
                            pltpu.VMEM((D_MODEL, EXPERT_FF), F32),
                            pltpu.VMEM((EXPERT_FF, D_MODEL), F32),
                            pltpu.VMEM((D_MODEL, EXPERT_FF), BF16),
                            pltpu.VMEM((D_MODEL, EXPERT_FF), BF16),
                            pltpu.VMEM((EXPERT_FF, D_MODEL), BF16),
                            pltpu.SemaphoreType.DMA((3,))]),
        out_shape=jax.ShapeDtypeStruct((P_ROWS * TOK_ROWS, LANES), U32),
        compiler_params=_cparams(("arbitrary",)),
        name="experts",
    )(block_expert, block_rows, next_expert, xs, w_gate, w_up, w_down)


def _combine_kernel(lslot_ref, rcnt_ref, rloc_ref, rglb_ref, w_ref, ys_ref, hp_ref, x1_ref, gf_ref,
                    wsg_ref, wsu_ref, wsd_ref, fn_ref, o_ref, buf_ref, mlo_ref, mhi_ref, sem, *, final_norm):
    i = pl.program_id(0)
    nsteps = pl.num_programs(0)

    def issue(step, slot):
        def run(e, _):
            n = rcnt_ref[step * N_EXPERTS + e]

            @pl.when(n > 0)
            def _():
                src = pl.multiple_of(rglb_ref[step * N_EXPERTS + e] * TOK_ROWS, TOK_ROWS)
                dst = pl.multiple_of(rloc_ref[step * N_EXPERTS + e], TOK_ROWS)
                pltpu.make_async_copy(ys_ref.at[pl.ds(src, n * TOK_ROWS)],
                                      buf_ref.at[pl.ds(dst, n * TOK_ROWS)], sem.at[slot]).start()
            return 0

        lax.fori_loop(0, N_EXPERTS, run, 0)

    @pl.when(i == 0)
    def _():
        issue(0, 0)

    @pl.when(i + 1 < nsteps)
    def _():
        issue(i + 1, (i + 1) % 2)

    cur = i % 2
    cur_half = pl.ds(pl.multiple_of(cur * LOCAL_ROWS, LOCAL_ROWS), LOCAL_ROWS)
    pltpu.make_async_copy(ys_ref.at[pl.ds(0, LOCAL_ROWS)], buf_ref.at[cur_half], sem.at[cur]).wait()

    base = i * T_DISP

    def token(t, _):
        acc_lo = jnp.zeros((TOK_ROWS, LANES), F32)
        acc_hi = jnp.zeros((TOK_ROWS, LANES), F32)
        for k in range(TOP_K):
            idx = (base + t) * TOP_K + k
            src = pl.multiple_of(lslot_ref[idx], TOK_ROWS)
            lo, hi = _unpack_bf16_pair(buf_ref[pl.ds(src, TOK_ROWS), :])
            wk = w_ref[idx]
            acc_lo = acc_lo + wk * lo
            acc_hi = acc_hi + wk * hi
        dst = pl.ds(pl.multiple_of(t * TOK_ROWS, TOK_ROWS), TOK_ROWS)
        mlo_ref[dst, :] = acc_lo
        mhi_ref[dst, :] = acc_hi
        return 0

    lax.fori_loop(0, T_DISP, token, 0, unroll=2)
    moe_lo = _load_token_tiles(mlo_ref, T_DISP)
    moe_hi = _load_token_tiles(mhi_ref, T_DISP)

    hlo, hhi = _unpack_bf16_pair(_load_token_tiles(hp_ref, T_DISP))
    hlo = hlo.astype(BF16)
    hhi = hhi.astype(BF16)
    sg = (jnp.dot(hlo, wsg_ref[0:PACK_W], preferred_element_type=F32)
          + jnp.dot(hhi, wsg_ref[PACK_W:D_MODEL], preferred_element_type=F32))
    su = (jnp.dot(hlo, wsu_ref[0:PACK_W], preferred_element_type=F32)
          + jnp.dot(hhi, wsu_ref[PACK_W:D_MODEL], preferred_element_type=F32))
    shared = jnp.dot((_silu(sg) * su).astype(BF16), wsd_ref[...], preferred_element_type=F32)
    moe = jnp.concatenate([moe_lo, moe_hi], axis=1)
    out = x1_ref[...] + gf_ref[0] * (moe + shared)
    if final_norm:
        out = _rms(out, fn_ref[...])
    o_ref[...] = out


def _combine(lslot_flat, run_cnt, run_loc, run_glb, w_flat, ys, hp, x1, g_f, wsg, wsu, wsd, final_gain, final_norm):
    per_b = SEQ // T_DISP
    return pl.pallas_call(
        functools.partial(_combine_kernel, final_norm=final_norm),
        grid_spec=pltpu.PrefetchScalarGridSpec(
            num_scalar_prefetch=4,
            grid=(N_TILES,),
            in_specs=[pl.BlockSpec(memory_space=pltpu.SMEM),
                      pl.BlockSpec(memory_space=pl.ANY),
                      pl.BlockSpec((T_DISP * TOK_ROWS, LANES), lambda i, *_: (i, 0)),
                      pl.BlockSpec((T_DISP, D_MODEL), lambda i, *_: (i, 0)),
                      pl.BlockSpec((1, 1, D_MODEL), lambda i, *_: (i // per_b, 0, 0)),
                      pl.BlockSpec((D_MODEL, SHARED_FF), lambda i, *_: (0, 0)),
                      pl.BlockSpec((D_MODEL, SHARED_FF), lambda i, *_: (0, 0)),
                      pl.BlockSpec((SHARED_FF, D_MODEL), lambda i, *_: (0, 0)),
                      pl.BlockSpec((1, D_MODEL), lambda i, *_: (0, 0))],
            out_specs=pl.BlockSpec((T_DISP, D_MODEL), lambda i, *_: (i, 0)),
            scratch_shapes=[pltpu.VMEM((2 * LOCAL_ROWS, LANES), U32),
                            pltpu.VMEM((T_DISP * TOK_ROWS, LANES), F32),
                            pltpu.VMEM((T_DISP * TOK_ROWS, LANES), F32),
                            pltpu.SemaphoreType.DMA((2,))]),
        out_shape=jax.ShapeDtypeStruct((N_TOK, D_MODEL), F32),
        compiler_params=_cparams(("arbitrary",)),
        name="combine",
    )(lslot_flat, run_cnt, run_loc, run_glb, w_flat, ys, hp, x1, g_f, wsg, wsu, wsd, final_gain.reshape(1, D_MODEL))


def _moe_layer(layer, hp, logits, x1, g_f, router_bias, w_gate, w_up, w_down, ws_gate, ws_up, ws_down,
               final_gain, final_norm):
    lslot, w_k, tables, counts = _route(logits, router_bias)
    cnt = counts[:, 0].astype(I32)
    padded = ((cnt + TM_EXP - 1) // TM_EXP) * TM_EXP
    ends = jnp.cumsum(padded)
    offsets = ends - padded
    tables = tables[:, :, :, 0].astype(I32)
    run_cnt = tables[:, 0].reshape(-1)
    run_glb = (offsets[None, :] + tables[:, 1]).reshape(-1)
    half_row0 = (jnp.arange(N_TILES, dtype=I32) % 2) * LOCAL_ROWS
    run_loc = (tables[:, 2] * TOK_ROWS + half_row0[:, None]).reshape(-1)
    lslot_flat = (lslot * TOK_ROWS + jnp.repeat(half_row0, T_DISP)[None, :]).T.reshape(N_SLOTS)
    blk_start = jnp.arange(NB_EXP, dtype=I32) * TM_EXP
    expert_ids = jnp.arange(N_EXPERTS, dtype=I32)
    nonempty = cnt > 0
    last_nonempty = jnp.max(jnp.where(nonempty, expert_ids, 0))
    block_expert = jnp.minimum(jnp.sum((blk_start[:, None] >= ends[None, :]).astype(I32), axis=1), last_nonempty)
    block_valid = (blk_start < ends[-1]).astype(I32)
    block_rows = jnp.clip((offsets + cnt)[block_expert] - blk_start, 0, TM_EXP) * block_valid
    following = jnp.where(nonempty, expert_ids, N_EXPERTS)
    following = lax.cummin(following, reverse=True)
    following = jnp.concatenate([following[1:], jnp.full((1,), N_EXPERTS, I32)])
    next_expert = jnp.where(following < N_EXPERTS, following, -1)[block_expert]
    zero_start = jnp.concatenate([offsets + cnt, blk_start])
    zero_len = jnp.concatenate([padded - cnt, (1 - block_valid) * TM_EXP])
    xs = _dispatch(lslot_flat, run_cnt, run_loc, run_glb, zero_start, zero_len, hp)
    ys = _experts(layer, block_expert, block_rows, next_expert, xs, w_gate, w_up, w_down)
    return _combine(lslot_flat, run_cnt, run_loc, run_glb, w_k.T.reshape(N_SLOTS), ys, hp, x1, g_f,
                    ws_gate.astype(BF16), ws_up.astype(BF16), ws_down.astype(BF16), final_gain, final_norm)


def _rope_tables(dim):
    inv = ROPE_THETA ** (-jnp.arange(0, dim, 2, dtype=F32) / dim)
    ang = jnp.arange(SEQ, dtype=F32)[:, None] * inv[None, :]
    return jnp.cos(ang), jnp.sin(ang)


def _rot_half_cols(w):
    half = w.shape[-1] // 2
    return jnp.concatenate([-w[..., half:], w[..., :half]], axis=-1)


def _t5_bucket(rel):
    half = REL_BUCKETS // 2
    max_exact = half // 2
    ret = (rel > 0).astype(I32) * half
    n = jnp.abs(rel)
    nf = jnp.maximum(n, 1).astype(F32)
    large = max_exact + (jnp.log(nf / max_exact) / math.log(REL_MAX_DIST / max_exact)
                         * (half - max_exact)).astype(I32)
    large = jnp.minimum(large, half - 1)
    return ret + jnp.where(n < max_exact, n, large)


def _swa_bias_table(rel_bias):
    qi = jnp.arange(WINDOW)[:, None]
    kj = jnp.arange(3 * WINDOW)[None, :]
    rel = kj - WINDOW - qi
    onehot = (_t5_bucket(rel)[:, :, None] == jnp.arange(REL_BUCKETS)).astype(F32)
    bias = jnp.einsum('qjb,bh->hqj', onehot, rel_bias.astype(F32), precision=lax.Precision.HIGHEST)
    return jnp.where((jnp.abs(rel) <= WINDOW)[None], bias, NEG_BIG)


def kernel(x, c, w_mod, b_mod, norm_mix, norm_ffn, final_norm, w_in_ab, q_lat_norm, kv_lat_norm, w_uq, w_ukv, conv_w, conv_b, lru_w_a, lru_b_a, lru_w_x, lru_b_x, lru_lambda, w_out_ab, w_in_cd, ret_gn, swa_sinks, w_out_cd, rel_bias, w_router, router_bias, w_gate, w_up, w_down, ws_gate, ws_up, ws_down):
    xf = x.reshape(N_TOK, D_MODEL)
    mod = _modulation(c, w_mod, b_mod)
    cos_r, sin_r = _rope_tables(MLA_ROPE)
    cs_tab = jnp.concatenate([cos_r, cos_r, sin_r, sin_r], axis=1)
    cos_t, sin_t = _rope_tables(RET_DK)
    lg_ret = jnp.log1p(-(2.0 ** (-5.0 - jnp.arange(RET_HEADS, dtype=F32))))

    for layer in range(DEPTH):
        sh_m, sc_m, g_m, sh_f, sc_f, g_f = [m.reshape(BATCH, 1, D_MODEL) for m in jnp.split(mod[layer], 6, axis=-1)]
        i = layer // 2
        if layer % 2 == 0:
            w = w_in_ab[i]
            o1, o2, o3, o4 = np.cumsum((MLA_Q_RANK, MLA_KV_RANK, MLA_ROPE, LRU_WIDTH)).tolist()
            w_kr = w[:, o2:o3]
            w_in = jnp.concatenate([w[:, :o2], w[:, o3:], w_kr, _rot_half_cols(w_kr)], axis=1).astype(BF16)
            p0 = _in_projection(xf, norm_mix[layer], sc_m, sh_m, w_in, "in_proj_ab")
            wq = w_uq[i].reshape(MLA_Q_RANK, MLA_HEADS, MLA_NOPE + MLA_ROPE)
            wq_r = wq[:, :, MLA_NOPE:]
            wq = jnp.concatenate([wq, _rot_half_cols(wq_r)], axis=-1).reshape(MLA_Q_RANK, MLA_HEADS * MLA_QK)
            q, k, v = _mla_up(p0, q_lat_norm[i], kv_lat_norm[i], wq.astype(BF16), w_ukv[i].astype(BF16), cs_tab)
            a_out = _mla_attention(q, k, v)
            w_gates = jnp.concatenate([lru_w_a[i, 0], lru_w_x[i, 0], lru_w_a[i, 1], lru_w_x[i, 1]], axis=-1).astype(BF16)
            b_gates = jnp.concatenate([b.reshape(LRU_BLOCKS, 1, LRU_BS) for b in
                                       (lru_b_a[i, 0], lru_b_x[i, 0], lru_b_a[i, 1], lru_b_x[i, 1])], axis=-1)
            b_out = _rglru(p0, conv_w[i], conv_b[i], w_gates, b_gates, lru_lambda[i])
            w_out = w_out_ab[i].astype(BF16)
        else:
            p1 = _in_projection(xf, norm_mix[layer], sc_m, sh_m, w_in_cd[i].astype(BF16), "in_proj_cd")
            a_out = _retention(p1, lg_ret, cos_t, sin_t, ret_gn[i])
            b_out = _swa(p1, swa_sinks[i], _swa_bias_table(rel_bias))
            w_out = w_out_cd[i].astype(BF16)
        w_r = jnp.pad(w_router[layer], ((0, 0), (0, LANES - N_EXPERTS)))
        w_r_hi = w_r.astype(BF16)
        w_router_pad = jnp.concatenate([w_r_hi, (w_r - w_r_hi.astype(F32)).astype(BF16)], axis=1)
        x1, hp, logits = _out_projection(a_out, b_out, w_out, xf, g_m, norm_ffn[layer], sc_f, sh_f, w_router_pad)
        xf = _moe_layer(layer, hp, logits, x1, g_f, router_bias[layer], w_gate, w_up, w_down,
                        ws_gate[layer], ws_up[layer], ws_down[layer], final_norm, layer == DEPTH - 1)
    return xf.reshape(BATCH, SEQ, D_MODEL)
```

```python
import functools
import math

import numpy as np
import jax
import jax.numpy as jnp
from jax import lax
from jax.experimental import pallas as pl
from jax.experimental.pallas import tpu as pltpu

F32 = jnp.float32
BF16 = jnp.bfloat16
I32 = jnp.int32
U32 = jnp.uint32

D_MODEL = 2048
BATCH = 4
SEQ = 2048
DEPTH = 2
N_TOK = BATCH * SEQ
HALF = D_MODEL // 2
MLA_NOPE = 128
MLA_ROPE = 64
MLA_V = 128
MLA_HEADS = HALF // MLA_V
MLA_Q_RANK = D_MODEL // 4
MLA_KV_RANK = D_MODEL // 4
MLA_QK = 256
LRU_WIDTH = HALF
LRU_BLOCKS = 8
LRU_BS = LRU_WIDTH // LRU_BLOCKS
LRU_CONV = 4
LRU_C = 8.0
RET_DK = 256
RET_DV = 256
RET_HEADS = HALF // RET_DV
SWA_HD = 128
SWA_HEADS = HALF // SWA_HD
SWA_KV_HEADS = 2
SWA_G = SWA_HEADS // SWA_KV_HEADS
WINDOW = 128
REL_BUCKETS = 32
REL_MAX_DIST = 128
N_EXPERTS = 64
TOP_K = 8
N_GROUPS = 8
GROUP_SIZE = N_EXPERTS // N_GROUPS
TOP_GROUPS = 4
EXPERT_FF = D_MODEL // 4
SHARED_FF = D_MODEL // 4
ROUTE_SCALE = 2.5
ROPE_THETA = 10000.0
EPS = 1e-6
NEG_BIG = -1e30

LANES = 128
SUBLANES = 8
VMEM_LIMIT = 52 * 2**20

TM_PROJ = 256
TM_PROJ_SUB = 128
TM_UP = 512
TQ_ATT = 512
TQ_MLA = 2048
TQ_SUB = 256
SWA_BLOCKS = 4
TM_OUT = 512
TM_OUT_SUB = 256
T_DISP = 256
T_ROUTE = T_DISP
N_TILES = N_TOK // T_DISP
TILE_ROWS = T_DISP * TOP_K
TM_EXP = 512
TM_EXP_SUB = 256
TM_EXP_TAIL = 128
assert TM_EXP == 2 * TM_EXP_SUB and TM_EXP % TM_EXP_TAIL == 0
CAST_VREGS = 32
CAST_PIECES = 16
N_SLOTS = N_TOK * TOP_K
NB_EXP = N_SLOTS // TM_EXP + N_EXPERTS
P_ROWS = NB_EXP * TM_EXP
N_ZERO_RANGES = N_EXPERTS + NB_EXP
PACK_W = D_MODEL // 2
TOK_ROWS = PACK_W // LANES
assert TOK_ROWS == SUBLANES
LOCAL_ROWS = TILE_ROWS * TOK_ROWS

LRU_SEG = 260
LRU_ROWS = SUBLANES * LRU_SEG
assert LRU_ROWS >= SEQ and LRU_SEG % 8 == 4


def _cparams(sem, vmem=VMEM_LIMIT):
    return pltpu.CompilerParams(dimension_semantics=sem, vmem_limit_bytes=vmem)


def _sigmoid(x):
    return 0.5 * jnp.tanh(0.5 * x) + 0.5


def _silu(x):
    return x * _sigmoid(x)


def _rms(x, g):
    return x * lax.rsqrt(jnp.mean(x * x, axis=-1, keepdims=True) + EPS) * g


def _pack_bf16_pair(lo, hi):
    lo_b = lax.bitcast_convert_type(lo.astype(BF16).astype(F32), U32)
    hi_b = lax.bitcast_convert_type(hi.astype(BF16).astype(F32), U32)
    return (hi_b & jnp.uint32(0xFFFF0000)) | (lo_b >> 16)


def _unpack_bf16_pair(w):
    lo = lax.bitcast_convert_type(w << 16, F32)
    hi = lax.bitcast_convert_type(w & jnp.uint32(0xFFFF0000), F32)
    return lo, hi


def _store_token_tiles(ref, packed, tok0=0):
    t = packed.shape[0]
    for s in range(TOK_ROWS):
        ref[pl.ds(tok0 * TOK_ROWS + s, t, stride=TOK_ROWS), :] = packed[:, s * LANES:(s + 1) * LANES]


def _load_token_tiles(ref, t):
    return jnp.concatenate([ref[pl.ds(s, t, stride=TOK_ROWS), :] for s in range(TOK_ROWS)], axis=1)


def _mod_kernel(c_ref, w_ref, b_ref, o_ref):
    c = c_ref[...]
    ca = _silu(c).astype(BF16)
    o_ref[0] = jnp.dot(ca, w_ref[0].astype(BF16), preferred_element_type=F32) + b_ref[0]


def _modulation(c, w_mod, b_mod):
    tn = 1024
    cp = jnp.pad(c, ((0, SUBLANES - BATCH), (0, 0)))
    out = pl.pallas_call(
        _mod_kernel,
        grid=(DEPTH, 6 * D_MODEL // tn),
        in_specs=[pl.BlockSpec((SUBLANES, D_MODEL), lambda l, j: (0, 0)),
                  pl.BlockSpec((1, D_MODEL, tn), lambda l, j: (l, 0, j)),
                  pl.BlockSpec((1, 1, tn), lambda l, j: (l, 0, j))],
        out_specs=pl.BlockSpec((1, SUBLANES, tn), lambda l, j: (l, 0, j)),
        out_shape=jax.ShapeDtypeStruct((DEPTH, SUBLANES, 6 * D_MODEL), F32),
        compiler_params=_cparams(("parallel", "parallel")),
        name="adaln_mod",
    )(cp, w_mod, b_mod.reshape(DEPTH, 1, 6 * D_MODEL))
    return out[:, :BATCH]


def _inproj_kernel(x_ref, g_ref, sc_ref, sh_ref, w_ref, o_ref):
    for r in range(TM_PROJ // TM_PROJ_SUB):
        rows = pl.ds(r * TM_PROJ_SUB, TM_PROJ_SUB)
        y = _rms(x_ref[rows, :], g_ref[...])
        h = (y * (1.0 + sc_ref[0]) + sh_ref[0]).astype(BF16)
        o_ref[rows, :] = jnp.dot(h, w_ref[...], preferred_element_type=F32)


def _in_projection(x, gain, scale, shift, w_bf16, name):
    p = w_bf16.shape[1]
    per_b = SEQ // TM_PROJ
    return pl.pallas_call(
        _inproj_kernel,
        grid=(N_TOK // TM_PROJ,),
        in_specs=[pl.BlockSpec((TM_PROJ, D_MODEL), lambda i: (i, 0)),
                  pl.BlockSpec((1, D_MODEL), lambda i: (0, 0)),
                  pl.BlockSpec((1, 1, D_MODEL), lambda i: (i // per_b, 0, 0)),
                  pl.BlockSpec((1, 1, D_MODEL), lambda i: (i // per_b, 0, 0)),
                  pl.BlockSpec((D_MODEL, p), lambda i: (0, 0), pipeline_mode=pl.Buffered(1))],
        out_specs=pl.BlockSpec((TM_PROJ, p), lambda i: (i, 0)),
        out_shape=jax.ShapeDtypeStruct((N_TOK, p), F32),
        compiler_params=_cparams(("parallel",), 56 * 2**20),
        name=name,
    )(x, gain.reshape(1, D_MODEL), scale, shift, w_bf16)


def _mla_up_kernel(ql_ref, kvl_ref, kr_ref, qn_ref, kvn_ref, wq_ref, wkv_ref, cs_ref, q_ref, k_ref, v_ref):
    scale = (MLA_NOPE + MLA_ROPE) ** -0.5 * math.log2(math.e)
    hq = _rms(ql_ref[...], qn_ref[...]).astype(BF16)
    hkv = _rms(kvl_ref[...], kvn_ref[...]).astype(BF16)
    yq = jnp.dot(hq, wq_ref[...], preferred_element_type=F32) * scale
    ykv = jnp.dot(hkv, wkv_ref[...], preferred_element_type=F32)
    cs = cs_ref[...]
    lane = lax.broadcasted_iota(I32, cs.shape, 1)

    def rope_sum(blk):
        z = blk * cs
        return z + pltpu.roll(z, MLA_ROPE, 1)

    kr = jnp.where(lane < MLA_ROPE, rope_sum(kr_ref[...]), 0.0).astype(BF16)
    ones_col = jnp.where(lane == 0, 1.0, 0.0).astype(BF16)
    for h in range(MLA_HEADS):
        c0 = h * MLA_QK
        q_ref[0, h, :, 0:MLA_NOPE] = yq[:, c0:c0 + MLA_NOPE].astype(BF16)
        q_ref[0, h, :, MLA_NOPE:MLA_QK] = rope_sum(yq[:, c0 + MLA_NOPE:c0 + MLA_QK]).astype(BF16)
        k_ref[0, h, :, 0:MLA_NOPE] = ykv[:, c0:c0 + MLA_NOPE].astype(BF16)
        k_ref[0, h, :, MLA_NOPE:MLA_QK] = kr
        v_ref[0, h, :, 0:MLA_V] = ykv[:, c0 + MLA_NOPE:c0 + MLA_QK].astype(BF16)
        v_ref[0, h, :, MLA_V:2 * MLA_V] = ones_col


def _mla_up(p0, q_norm, kv_norm, wq, wkv, cs_tab):
    per_b = SEQ // TM_UP
    qk_shape = jax.ShapeDtypeStruct((BATCH, MLA_HEADS, SEQ, MLA_QK), BF16)
    return pl.pallas_call(
        _mla_up_kernel,
        grid=(N_TOK // TM_UP,),
        in_specs=[pl.BlockSpec((TM_UP, MLA_Q_RANK), lambda i: (i, 0)),
                  pl.BlockSpec((TM_UP, MLA_KV_RANK), lambda i: (i, 1)),
                  pl.BlockSpec((TM_UP, LANES), lambda i: (i, 24)),
                  pl.BlockSpec((1, MLA_Q_RANK), lambda i: (0, 0)),
                  pl.BlockSpec((1, MLA_KV_RANK), lambda i: (0, 0)),
                  pl.BlockSpec((MLA_Q_RANK, MLA_HEADS * MLA_QK), lambda i: (0, 0)),
                  pl.BlockSpec((MLA_KV_RANK, MLA_HEADS * MLA_QK), lambda i: (0, 0)),
                  pl.BlockSpec((TM_UP, LANES), lambda i: (i % per_b, 0))],
        out_specs=[pl.BlockSpec((1, MLA_HEADS, TM_UP, MLA_QK), lambda i: (i // per_b, 0, i % per_b, 0)),
                   pl.BlockSpec((1, MLA_HEADS, TM_UP, MLA_QK), lambda i: (i // per_b, 0, i % per_b, 0)),
                   pl.BlockSpec((1, MLA_HEADS, TM_UP, 2 * MLA_V), lambda i: (i // per_b, 0, i % per_b, 0))],
        out_shape=[qk_shape, qk_shape, jax.ShapeDtypeStruct((BATCH, MLA_HEADS, SEQ, 2 * MLA_V), BF16)],
        compiler_params=_cparams(("parallel",)),
        name="mla_up",
    )(p0, p0, p0, q_norm.reshape(1, -1), kv_norm.reshape(1, -1), wq, wkv, cs_tab)


def _mla_attn_kernel(q_ref, k_ref, v_ref, o_ref):
    k = k_ref[0, 0]
    v = v_ref[0, 0]
    for r in range(TQ_MLA // TQ_SUB):
        rows = pl.ds(r * TQ_SUB, TQ_SUB)
        s = lax.dot_general(q_ref[0, 0, rows, :], k, (((1,), (1,)), ((), ())), preferred_element_type=F32)
        p = jnp.exp2(s - jnp.max(s, axis=-1, keepdims=True))
        o = jnp.dot(p.astype(BF16), v, preferred_element_type=F32)
        o_ref[rows, :] = (o[:, 0:MLA_V] / o[:, MLA_V:MLA_V + 1]).astype(BF16)


def _mla_attention(q, k, v):
    nq = SEQ // TQ_MLA
    return pl.pallas_call(
        _mla_attn_kernel,
        grid=(BATCH, MLA_HEADS, nq),
        in_specs=[pl.BlockSpec((1, 1, TQ_MLA, MLA_QK), lambda b, h, i: (b, h, i, 0)),
                  pl.BlockSpec((1, 1, SEQ, MLA_QK), lambda b, h, i: (b, h, 0, 0)),
                  pl.BlockSpec((1, 1, SEQ, 2 * MLA_V), lambda b, h, i: (b, h, 0, 0))],
        out_specs=pl.BlockSpec((TQ_MLA, MLA_V), lambda b, h, i: (b * nq + i, h)),
        out_shape=jax.ShapeDtypeStruct((N_TOK, HALF), BF16),
        compiler_params=_cparams(("parallel", "parallel", "parallel")),
        name="mla_attn",
    )(q, k, v)


def _lru_kernel(x_ref, gate_ref, cw_ref, cb_ref, wg_ref, bg_ref, lam_ref, o_ref,
                af_ref, uf_ref, ab_ref, ub_ref, hf_ref, pf_ref, hb_ref, pb_ref, hs_ref):
    x = x_ref[...]
    row = lax.broadcasted_iota(I32, x.shape, 0)

    def shifted(d):
        r = pltpu.roll(x, (-d) % SEQ, 0)
        return jnp.where((row + d >= 0) & (row + d < SEQ), r, 0.0)

    cw = cw_ref[...]
    left = LRU_CONV // 2
    xc = cb_ref[...]
    for kk in range(LRU_CONV):
        d = kk - left
        xc = xc + cw[kk:kk + 1] * (x if d == 0 else shifted(d))

    gates = jnp.dot(xc.astype(BF16), wg_ref[0], preferred_element_type=F32) + bg_ref[0]
    lam = lam_ref[...]
    z = -lam
    sp = jnp.maximum(z, 0.0) + jnp.log1p(jnp.exp(-jnp.abs(z)))
    pad_rows = LRU_ROWS - SEQ
    for d, (a_ref, u_ref) in enumerate(((af_ref, uf_ref), (ab_ref, ub_ref))):
        r = _sigmoid(gates[:, d * 256:d * 256 + LRU_BS])
        i = _sigmoid(gates[:, d * 256 + LRU_BS:(d + 1) * 256])
        a = jnp.exp(r * (-LRU_C * sp[d:d + 1]))
        a_ref[0:SEQ] = a
        u_ref[0:SEQ] = jnp.sqrt(1.0 - a * a) * (i * xc)
        a_ref[SEQ:LRU_ROWS] = jnp.zeros((pad_rows, LANES), F32)
        u_ref[SEQ:LRU_ROWS] = jnp.zeros((pad_rows, LANES), F32)

    ones = jnp.ones((SUBLANES, LANES), F32)
    zeros = jnp.zeros((SUBLANES, LANES), F32)

    def seg(t):
        return pl.ds(t, SUBLANES, stride=LRU_SEG)

    def local_scan(s, carry):
        p_f, h_f, p_b, h_b = carry
        tf = s
        tb = LRU_SEG - 1 - s
        a = af_ref[seg(tf)]
        h_f = a * h_f + uf_ref[seg(tf)]
        p_f = a * p_f
        hf_ref[seg(tf)] = h_f
        pf_ref[seg(tf)] = p_f
        a = ab_ref[seg(tb)]
        h_b = a * h_b + ub_ref[seg(tb)]
        p_b = a * p_b
        hb_ref[seg(tb)] = h_b
        pb_ref[seg(tb)] = p_b
        return p_f, h_f, p_b, h_b

    p_f, h_f, p_b, h_b = lax.fori_loop(0, LRU_SEG, local_scan, (ones, zeros, ones, zeros), unroll=4)

    rows_f = []
    c = jnp.zeros((1, LANES), F32)
    for j in range(SUBLANES):
        rows_f.append(c)
        c = p_f[j:j + 1] * c + h_f[j:j + 1]
    rows_b = [None] * SUBLANES
    c = jnp.zeros((1, LANES), F32)
    for j in range(SUBLANES - 1, -1, -1):
        rows_b[j] = c
        c = p_b[j:j + 1] * c + h_b[j:j + 1]
    sub = lax.broadcasted_iota(I32, (SUBLANES, LANES), 0)
    c_f = zeros
    c_b = zeros
    for j in range(SUBLANES):
        c_f = jnp.where(sub == j, rows_f[j], c_f)
        c_b = jnp.where(sub == j, rows_b[j], c_b)

    def fixup(t, _):
        hs_ref[seg(t)] = (hf_ref[seg(t)] + pf_ref[seg(t)] * c_f) + (hb_ref[seg(t)] + pb_ref[seg(t)] * c_b)
        return 0

    lax.fori_loop(0, LRU_SEG, fixup, 0, unroll=4)

    g = gate_ref[...]
    gelu = 0.5 * g * (1.0 + jnp.tanh(math.sqrt(2.0 / math.pi) * (g + 0.044715 * (g * g * g))))
    o_ref[...] = (gelu * hs_ref[0:SEQ]).astype(BF16)


def _rglru(p0, conv_w, conv_b, w_gates, b_gates, lam):
    scan_buf = pltpu.VMEM((LRU_ROWS, LANES), F32)
    return pl.pallas_call(
        _lru_kernel,
        grid=(BATCH, LRU_BLOCKS),
        in_specs=[pl.BlockSpec((SEQ, LRU_BS), lambda b, g: (b, 8 + g)),
                  pl.BlockSpec((SEQ, LRU_BS), lambda b, g: (b, 16 + g)),
                  pl.BlockSpec((LRU_CONV, LRU_BS), lambda b, g: (0, g)),
                  pl.BlockSpec((1, LRU_BS), lambda b, g: (0, g)),
                  pl.BlockSpec((1, LRU_BS, 4 * LRU_BS), lambda b, g: (g, 0, 0)),
                  pl.BlockSpec((1, 1, 4 * LRU_BS), lambda b, g: (g, 0, 0)),
                  pl.BlockSpec((2, LRU_BS), lambda b, g: (0, g))],
        out_specs=pl.BlockSpec((SEQ, LRU_BS), lambda b, g: (b, g)),
        out_shape=jax.ShapeDtypeStruct((N_TOK, LRU_WIDTH), BF16),
        scratch_shapes=[scan_buf] * 9,
        compiler_params=_cparams(("parallel", "parallel")),
        name="rglru",
    )(p0, p0, conv_w, conv_b.reshape(1, -1), w_gates, b_gates, lam)


def _ret_kernel(lg_ref, q_ref, k_ref, v_ref, g_ref, cq_ref, sq_ref, ck_ref, sk_ref, gn_ref, o_ref, ks_ref, vs_ref):
    h = pl.program_id(1)
    qi = pl.program_id(2)
    half = RET_DK // 2

    def rope(t, c, s):
        t1, t2 = t[:, :half], t[:, half:]
        return jnp.concatenate([t1 * c - t2 * s, t2 * c + t1 * s], axis=1)

    @pl.when(qi == 0)
    def _():
        ks_ref[...] = (rope(k_ref[...], ck_ref[...], sk_ref[...]) * (RET_DK ** -0.5)).astype(BF16)
        vs_ref[...] = v_ref[...].astype(BF16)

    q = rope(q_ref[...], cq_ref[...], sq_ref[...]).astype(BF16)
    s = lax.dot_general(q, ks_ref[...], (((1,), (1,)), ((), ())), preferred_element_type=F32)
    n = qi * TQ_ATT + lax.broadcasted_iota(I32, s.shape, 0)
    m = lax.broadcasted_iota(I32, s.shape, 1)
    d = (n - m).astype(F32)
    lg_f = lg_ref[h]
    lg_b = lg_ref[RET_HEADS - 1 - h]
    dec = jnp.exp(jnp.where(d >= 0.0, lg_f * d, -lg_b * d))
    o = jnp.dot((s * dec).astype(BF16), vs_ref[...], preferred_element_type=F32)
    y = _rms(o, gn_ref[0])
    o_ref[...] = (_silu(g_ref[...]) * y).astype(BF16)


def _retention(p1, lg, cos_t, sin_t, ret_gn):
    nq = SEQ // TQ_ATT
    half = RET_DK // 2
    return pl.pallas_call(
        _ret_kernel,
        grid=(BATCH, RET_HEADS, nq),
        in_specs=[pl.BlockSpec(memory_space=pltpu.SMEM),
                  pl.BlockSpec((TQ_ATT, RET_DK), lambda b, h, i: (b * nq + i, h)),
                  pl.BlockSpec((SEQ, RET_DK), lambda b, h, i: (b, RET_HEADS + h)),
                  pl.BlockSpec((SEQ, RET_DV), lambda b, h, i: (b, 2 * RET_HEADS + h)),
                  pl.BlockSpec((TQ_ATT, RET_DV), lambda b, h, i: (b * nq + i, 3 * RET_HEADS + h)),
                  pl.BlockSpec((TQ_ATT, half), lambda b, h, i: (i, 0)),
                  pl.BlockSpec((TQ_ATT, half), lambda b, h, i: (i, 0)),
                  pl.BlockSpec((SEQ, half), lambda b, h, i: (0, 0)),
                  pl.BlockSpec((SEQ, half), lambda b, h, i: (0, 0)),
                  pl.BlockSpec((1, 1, RET_DV), lambda b, h, i: (h, 0, 0))],
        out_specs=pl.BlockSpec((TQ_ATT, RET_DV), lambda b, h, i: (b * nq + i, h)),
        out_shape=jax.ShapeDtypeStruct((N_TOK, HALF), BF16),
        scratch_shapes=[pltpu.VMEM((SEQ, RET_DK), BF16), pltpu.VMEM((SEQ, RET_DV), BF16)],
        compiler_params=_cparams(("parallel", "parallel", "arbitrary")),
        name="retention",
    )(lg, p1, p1, p1, p1, cos_t, sin_t, cos_t, sin_t, ret_gn.reshape(RET_HEADS, 1, RET_DV))


def _swa_kernel(sink_ref, q_ref, k_ref, v_ref, bias_ref, o_ref):
    kv = pl.program_id(1)
    nb = SEQ // WINDOW
    w = WINDOW

    def rows(ref, blk):
        return ref[pl.ds(pl.multiple_of(blk * w, w), w), :].astype(BF16)

    col = lax.broadcasted_iota(I32, (w, 3 * w), 1)
    for j in range(SWA_BLOCKS):
        n = pl.program_id(2) * SWA_BLOCKS + j
        prev = jnp.maximum(n - 1, 0)
        nxt = jnp.minimum(n + 1, nb - 1)
        kw = jnp.concatenate([rows(k_ref, prev), rows(k_ref, n), rows(k_ref, nxt)], axis=0)
        vw = jnp.concatenate([rows(v_ref, prev), rows(v_ref, n), rows(v_ref, nxt)], axis=0)
        qb = q_ref[j * w:(j + 1) * w, :]
        q4 = jnp.concatenate([qb[:, g * SWA_HD:(g + 1) * SWA_HD] for g in range(SWA_G)], axis=0).astype(BF16)
        s = lax.dot_general(q4, kw, (((1,), (1,)), ((), ())), preferred_element_type=F32) * (SWA_HD ** -0.5)
        outside = ((col < w) & (n == 0)) | ((col >= 2 * w) & (n == nb - 1))
        for g in range(SWA_G):
            sg = jnp.where(outside, NEG_BIG, s[g * w:(g + 1) * w] + bias_ref[g])
            sink = sink_ref[kv * SWA_G + g]
            m = jnp.maximum(jnp.max(sg, axis=-1, keepdims=True), sink)
            p = jnp.exp(sg - m)
            denom = jnp.sum(p, axis=-1, keepdims=True) + jnp.exp(sink - m)
            o = jnp.dot((p / denom).astype(BF16), vw, preferred_element_type=F32)
            o_ref[j * w:(j + 1) * w, g * SWA_HD:(g + 1) * SWA_HD] = o.astype(BF16)


def _swa(p1, sinks, bias):
    nb = SEQ // WINDOW
    qcols = SWA_G * SWA_HD
    q_blk0 = (4 * RET_HEADS * RET_DK) // qcols
    k_blk0 = (4 * RET_HEADS * RET_DK + SWA_HEADS * SWA_HD) // SWA_HD
    v_blk0 = k_blk0 + SWA_KV_HEADS
    return pl.pallas_call(
        _swa_kernel,
        grid=(BATCH, SWA_KV_HEADS, nb // SWA_BLOCKS),
        in_specs=[pl.BlockSpec(memory_space=pltpu.SMEM),
                  pl.BlockSpec((SWA_BLOCKS * WINDOW, qcols), lambda b, kv, n: (b * (nb // SWA_BLOCKS) + n, q_blk0 + kv)),
                  pl.BlockSpec((SEQ, SWA_HD), lambda b, kv, n: (b, k_blk0 + kv)),
                  pl.BlockSpec((SEQ, SWA_HD), lambda b, kv, n: (b, v_blk0 + kv)),
                  pl.BlockSpec((SWA_G, WINDOW, 3 * WINDOW), lambda b, kv, n: (kv, 0, 0))],
        out_specs=pl.BlockSpec((SWA_BLOCKS * WINDOW, qcols), lambda b, kv, n: (b * (nb // SWA_BLOCKS) + n, kv)),
        out_shape=jax.ShapeDtypeStruct((N_TOK, HALF), BF16),
        compiler_params=_cparams(("parallel", "parallel", "parallel")),
        name="swa",
    )(sinks, p1, p1, p1, bias)


def _outproj_kernel(a_ref, b_ref, wa_ref, wb_ref, x_ref, gm_ref, g_ref, sc_ref, sh_ref, wr_ref,
                    x1_ref, hp_ref, lg_ref):
    wr = wr_ref[...]
    for r in range(TM_OUT // TM_OUT_SUB):
        rows = pl.ds(r * TM_OUT_SUB, TM_OUT_SUB)
        mixed = (jnp.dot(a_ref[rows, :], wa_ref[...], preferred_element_type=F32)
                 + jnp.dot(b_ref[rows, :], wb_ref[...], preferred_element_type=F32))
        x1 = x_ref[rows, :] + gm_ref[0] * mixed
        x1_ref[rows, :] = x1
        hf = _rms(x1, g_ref[...]) * (1.0 + sc_ref[0]) + sh_ref[0]
        _store_token_tiles(hp_ref, _pack_bf16_pair(hf[:, :PACK_W], hf[:, PACK_W:]), r * TM_OUT_SUB)
        h_hi = hf.astype(BF16)
        h_lo = (hf - h_hi.astype(F32)).astype(BF16)
        t_hi = jnp.dot(h_hi, wr, preferred_element_type=F32)
        t_lo = jnp.dot(h_lo, wr, preferred_element_type=F32)
        lg_ref[rows, :] = (t_hi[:, :LANES] + t_hi[:, LANES:]) + (t_lo[:, :LANES] + t_lo[:, LANES:])


def _out_projection(a, b, w_out_bf16, x, g_m, gain, scale, shift, w_router_pad):
    per_b = SEQ // TM_OUT
    vec = pl.BlockSpec((1, 1, D_MODEL), lambda i: (i // per_b, 0, 0))
    return pl.pallas_call(
        _outproj_kernel,
        grid=(N_TOK // TM_OUT,),
        in_specs=[pl.BlockSpec((TM_OUT, HALF), lambda i: (i, 0)),
                  pl.BlockSpec((TM_OUT, HALF), lambda i: (i, 0)),
                  pl.BlockSpec((HALF, D_MODEL), lambda i: (0, 0)),
                  pl.BlockSpec((HALF, D_MODEL), lambda i: (1, 0)),
                  pl.BlockSpec((TM_OUT, D_MODEL), lambda i: (i, 0)),
                  vec,
                  pl.BlockSpec((1, D_MODEL), lambda i: (0, 0)),
                  vec, vec,
                  pl.BlockSpec((D_MODEL, 2 * LANES), lambda i: (0, 0))],
        out_specs=[pl.BlockSpec((TM_OUT, D_MODEL), lambda i: (i, 0)),
                   pl.BlockSpec((TM_OUT * TOK_ROWS, LANES), lambda i: (i, 0)),
                   pl.BlockSpec((TM_OUT, LANES), lambda i: (i, 0))],
        out_shape=[jax.ShapeDtypeStruct((N_TOK, D_MODEL), F32),
                   jax.ShapeDtypeStruct((N_TOK * TOK_ROWS, LANES), U32),
                   jax.ShapeDtypeStruct((N_TOK, LANES), F32)],
        compiler_params=_cparams(("parallel",)),
        name="out_proj",
    )(a, b, w_out_bf16, w_out_bf16, x, g_m, gain.reshape(1, D_MODEL), scale, shift, w_router_pad)


def _route_kernel(lg_ref, bias_ref, lslot_ref, w_ref, tab_ref, cnt_ref):
    step = pl.program_id(0)

    @pl.when(step == 0)
    def _():
        cnt_ref[...] = jnp.zeros(cnt_ref.shape, F32)

    t = T_ROUTE
    scores = jax.nn.sigmoid(lg_ref[...].T[:N_EXPERTS])
    biased = scores + bias_ref[...]
    sub = lax.broadcasted_iota(I32, (GROUP_SIZE, t), 0).astype(F32)
    ninf = -jnp.inf

    def first_argmax(v, idx, n):
        m = jnp.max(v, axis=0, keepdims=True)
        return m, jnp.min(jnp.where(v == m, idx, float(n)), axis=0, keepdims=True)

    gs = []
    for g in range(N_GROUPS):
        bg = biased[g * GROUP_SIZE:(g + 1) * GROUP_SIZE]
        m1, i1 = first_argmax(bg, sub, GROUP_SIZE)
        m2 = jnp.max(jnp.where(sub == i1, ninf, bg), axis=0, keepdims=True)
        gs.append(m1 + m2)
    cur = jnp.concatenate(gs, axis=0)

    gmask = jnp.zeros((N_GROUPS, t), F32)
    for _ in range(TOP_GROUPS):
        _, i = first_argmax(cur, sub, N_GROUPS)
        pick = sub == i
        gmask = jnp.where(pick, 1.0, gmask)
        cur = jnp.where(pick, ninf, cur)

    eid = lax.broadcasted_iota(I32, (N_EXPERTS, t), 0).astype(F32)
    emask = jnp.concatenate([jnp.broadcast_to(gmask[g:g + 1], (GROUP_SIZE, t)) for g in range(N_GROUPS)], axis=0)
    cur = jnp.where(emask > 0.5, biased, ninf)
    sels, ws = [], []
    onehot = jnp.zeros((N_EXPERTS, t), F32)
    for _ in range(TOP_K):
        _, i = first_argmax(cur, eid, N_EXPERTS)
        pick = eid == i
        sels.append(pick)
        ws.append(jnp.sum(jnp.where(pick, scores, 0.0), axis=0, keepdims=True))
        onehot = jnp.where(pick, 1.0, onehot)
        cur = jnp.where(pick, ninf, cur)
    wsum = ws[0]
    for k in range(1, TOP_K):
        wsum = wsum + ws[k]

    r = lax.broadcasted_iota(I32, (t, t), 0)
    c = lax.broadcasted_iota(I32, (t, t), 1)
    tri = (r < c).astype(BF16)
    earlier = jnp.dot(onehot.astype(BF16), tri, preferred_element_type=F32)
    tile_cnt = jnp.broadcast_to(jnp.sum(onehot, axis=1, keepdims=True), (N_EXPERTS, LANES))
    er = lax.broadcasted_iota(I32, (N_EXPERTS, N_EXPERTS), 0)
    ec = lax.broadcasted_iota(I32, (N_EXPERTS, N_EXPERTS), 1)
    run_start = jnp.dot((ec < er).astype(BF16), tile_cnt.astype(BF16), preferred_element_type=F32)
    pos = earlier + run_start[:, 0:1]
    lslots = [jnp.sum(jnp.where(sels[k], pos, 0.0), axis=0, keepdims=True) for k in range(TOP_K)]

    lslot_ref[...] = jnp.concatenate(lslots, axis=0).astype(I32)
    w_ref[...] = jnp.concatenate([w / wsum * ROUTE_SCALE for w in ws], axis=0)
    tab_ref[0, 0] = tile_cnt
    tab_ref[0, 1] = cnt_ref[...]
    tab_ref[0, 2] = run_start
    cnt_ref[...] = cnt_ref[...] + tile_cnt


def _route(logits, router_bias):
    ntiles = N_TOK // T_ROUTE
    return pl.pallas_call(
        _route_kernel,
        grid=(ntiles,),
        in_specs=[pl.BlockSpec((T_ROUTE, LANES), lambda i: (i, 0)),
                  pl.BlockSpec((N_EXPERTS, 1), lambda i: (0, 0))],
        out_specs=[pl.BlockSpec((TOP_K, T_ROUTE), lambda i: (0, i)),
                   pl.BlockSpec((TOP_K, T_ROUTE), lambda i: (0, i)),
                   pl.BlockSpec((1, 3, N_EXPERTS, LANES), lambda i: (i, 0, 0, 0)),
                   pl.BlockSpec((N_EXPERTS, LANES), lambda i: (0, 0))],
        out_shape=[jax.ShapeDtypeStruct((TOP_K, N_TOK), I32),
                   jax.ShapeDtypeStruct((TOP_K, N_TOK), F32),
                   jax.ShapeDtypeStruct((ntiles, 3, N_EXPERTS, LANES), F32),
                   jax.ShapeDtypeStruct((N_EXPERTS, LANES), F32)],
        compiler_params=_cparams(("arbitrary",)),
        name="route",
    )(logits, router_bias.reshape(N_EXPERTS, 1))


def _dispatch_kernel(lslot_ref, rcnt_ref, rloc_ref, rglb_ref, zstart_ref, zlen_ref, hp_ref, xs_ref,
                     zero_ref, loc_ref, sem, zsem):
    step = pl.program_id(0)
    nsteps = pl.num_programs(0)

    def zero_copy(z):
        start = pl.multiple_of(zstart_ref[z] * TOK_ROWS, TOK_ROWS)
        n = zlen_ref[z] * TOK_ROWS
        return pltpu.make_async_copy(zero_ref.at[pl.ds(0, n)], xs_ref.at[pl.ds(start, n)], zsem)

    @pl.when(step == 0)
    def _():
        zero_ref[...] = jnp.zeros(zero_ref.shape, U32)

        def fill(z, _):
            @pl.when(zlen_ref[z] > 0)
            def _():
                zero_copy(z).start()
            return 0

        lax.fori_loop(0, N_ZERO_RANGES, fill, 0)

    @pl.when(step == nsteps - 1)
    def _():
        def fill_wait(z, _):
            @pl.when(zlen_ref[z] > 0)
            def _():
                zero_copy(z).wait()
            return 0

        lax.fori_loop(0, N_ZERO_RANGES, fill_wait, 0)

    cur = step % 2

    def tile_wait(slot):
        half = pl.ds(pl.multiple_of(slot * LOCAL_ROWS, LOCAL_ROWS), LOCAL_ROWS)
        pltpu.make_async_copy(loc_ref.at[half], xs_ref.at[pl.ds(0, LOCAL_ROWS)], sem.at[slot]).wait()

    @pl.when(step >= 2)
    def _():
        tile_wait(cur)

    base = step * T_DISP

    def place(t, _):
        row = hp_ref[pl.ds(pl.multiple_of(t * TOK_ROWS, TOK_ROWS), TOK_ROWS), :]
        for k in range(TOP_K):
            dst = pl.multiple_of(lslot_ref[(base + t) * TOP_K + k], TOK_ROWS)
            loc_ref[pl.ds(dst, TOK_ROWS), :] = row
        return 0

    lax.fori_loop(0, T_DISP, place, 0, unroll=2)

    def run(e, _):
        n = rcnt_ref[step * N_EXPERTS + e]

        @pl.when(n > 0)
        def _():
            src = pl.multiple_of(rloc_ref[step * N_EXPERTS + e], TOK_ROWS)
            dst = pl.multiple_of(rglb_ref[step * N_EXPERTS + e] * TOK_ROWS, TOK_ROWS)
            pltpu.make_async_copy(loc_ref.at[pl.ds(src, n * TOK_ROWS)],
                                  xs_ref.at[pl.ds(dst, n * TOK_ROWS)], sem.at[cur]).start()
        return 0

    lax.fori_loop(0, N_EXPERTS, run, 0)

    @pl.when(step == nsteps - 1)
    def _():
        tile_wait(cur)

        @pl.when(nsteps > 1)
        def _():
            tile_wait(1 - cur)


def _dispatch(lslot_flat, run_cnt, run_loc, run_glb, zero_start, zero_len, hp):
    return pl.pallas_call(
        _dispatch_kernel,
        grid_spec=pltpu.PrefetchScalarGridSpec(
            num_scalar_prefetch=6,
            grid=(N_TILES,),
            in_specs=[pl.BlockSpec((T_DISP * TOK_ROWS, LANES), lambda i, *_: (i, 0))],
            out_specs=pl.BlockSpec(memory_space=pl.ANY),
            scratch_shapes=[pltpu.VMEM((TM_EXP * TOK_ROWS, LANES), U32),
                            pltpu.VMEM((2 * LOCAL_ROWS, LANES), U32),
                            pltpu.SemaphoreType.DMA((2,)),
                            pltpu.SemaphoreType.DMA]),
        out_shape=jax.ShapeDtypeStruct((P_ROWS * TOK_ROWS, LANES), U32),
        compiler_params=_cparams(("arbitrary",)),
        name="dispatch",
    )(lslot_flat, run_cnt, run_loc, run_glb, zero_start, zero_len, hp)


def _expert_kernel(be_ref, bv_ref, nx_ref, xs_ref, wg_hbm, wu_hbm, wd_hbm, ys_ref,
                   wgs_ref, wus_ref, wds_ref, wgb_ref, wub_ref, wdb_ref, sem, *, layer):
    i = pl.program_id(0)
    e = be_ref[i]
    changed = jnp.logical_or(i == 0, e != be_ref[jnp.maximum(i - 1, 0)])

    def stage(expert):
        return (pltpu.make_async_copy(wg_hbm.at[layer, expert], wgs_ref, sem.at[0]),
                pltpu.make_async_copy(wu_hbm.at[layer, expert], wus_ref, sem.at[1]),
                pltpu.make_async_copy(wd_hbm.at[layer, expert], wds_ref, sem.at[2]))

    @pl.when(changed)
    def _():
        @pl.when(i == 0)
        def _():
            for c in stage(e):
                c.start()

        for c in stage(e):
            c.wait()
        rows = CAST_VREGS * SUBLANES * LANES // EXPERT_FF

        def cast_piece(c, _):
            sl = pl.ds(pl.multiple_of(c * rows, rows), rows)
            wgb_ref[sl, :] = wgs_ref[sl, :].astype(BF16)
            return 0

        lax.fori_loop(0, D_MODEL // rows, cast_piece, 0, unroll=2)

    def cast_up_down():
        for src, dst in ((wus_ref, wub_ref), (wds_ref, wdb_ref)):
            rows = src.shape[0] // CAST_PIECES
            for c in range(CAST_PIECES):
                dst[c * rows:(c + 1) * rows, :] = src[c * rows:(c + 1) * rows, :].astype(BF16)

    def sub_block(row0, nrows):
        xs_sub = xs_ref.at[pl.ds(row0 * TOK_ROWS, nrows * TOK_ROWS)]
        lo, hi = _unpack_bf16_pair(_load_token_tiles(xs_sub, nrows))
        lo = lo.astype(BF16)
        hi = hi.astype(BF16)
        hg = (jnp.dot(lo, wgb_ref[0:PACK_W], preferred_element_type=F32)
              + jnp.dot(hi, wgb_ref[PACK_W:D_MODEL], preferred_element_type=F32))
        hu = (jnp.dot(lo, wub_ref[0:PACK_W], preferred_element_type=F32)
              + jnp.dot(hi, wub_ref[PACK_W:D_MODEL], preferred_element_type=F32))
        act = (_silu(hg) * hu).astype(BF16)
        y = jnp.dot(act, wdb_ref[...], preferred_element_type=F32)
        _store_token_tiles(ys_ref, _pack_bf16_pair(y[:, :PACK_W], y[:, PACK_W:]), row0)

    valid_rows = bv_ref[i]
    full = valid_rows > TM_EXP - TM_EXP_TAIL
    unchanged = jnp.logical_not(changed)

    @pl.when(jnp.logical_and(full, changed))
    def _():
        cast_up_down()
        sub_block(0, TM_EXP_SUB)
        sub_block(TM_EXP_SUB, TM_EXP_SUB)

    @pl.when(jnp.logical_and(full, unchanged))
    def _():
        sub_block(0, TM_EXP_SUB)
        sub_block(TM_EXP_SUB, TM_EXP_SUB)

    @pl.when(jnp.logical_not(full))
    def _():
        @pl.when(changed)
        def _():
            cast_up_down()
            sub_block(0, TM_EXP_TAIL)

        for r in range(TM_EXP // TM_EXP_TAIL):
            if r == 0:
                compute = jnp.logical_and(unchanged, valid_rows > 0)
                skip = jnp.logical_and(unchanged, valid_rows <= 0)
            else:
                compute = valid_rows > r * TM_EXP_TAIL
                skip = valid_rows <= r * TM_EXP_TAIL

            @pl.when(compute)
            def _():
                sub_block(r * TM_EXP_TAIL, TM_EXP_TAIL)

            @pl.when(skip)
            def _():
                ys_ref[pl.ds(r * TM_EXP_TAIL * TOK_ROWS, TM_EXP_TAIL * TOK_ROWS), :] = jnp.zeros(
                    (TM_EXP_TAIL * TOK_ROWS, LANES), U32)

    @pl.when(jnp.logical_and(changed, nx_ref[i] >= 0))
    def _():
        for c in stage(nx_ref[i]):
            c.start()


def _experts(layer, block_expert, block_rows, next_expert, xs, w_gate, w_up, w_down):
    return pl.pallas_call(
        functools.partial(_expert_kernel, layer=layer),
        grid_spec=pltpu.PrefetchScalarGridSpec(
            num_scalar_prefetch=3,
            grid=(NB_EXP,),
            in_specs=[pl.BlockSpec((TM_EXP * TOK_ROWS, LANES), lambda i, be, bv, nx: (jnp.where(bv[i] > 0, i, 0), 0)),
                      pl.BlockSpec(memory_space=pl.ANY),
                      pl.BlockSpec(memory_space=pl.ANY),
                      pl.BlockSpec(memory_space=pl.ANY)],
            out_specs=pl.BlockSpec((TM_EXP * TOK_ROWS, LANES), lambda i, be, bv, nx: (i, 0)),
            scratch_shapes=[pltpu.VMEM((D_MODEL, EXPERT_FF), F32),
                            pltpu.VMEM((D_MODEL, EXPERT_FF), F32),
                            pltpu.VMEM((EXPERT_FF, D_MODEL), F32),
                            pltpu.VMEM((D_MODEL, EXPERT_FF), BF16),
                            pltpu.VMEM((D_MODEL, EXPERT_FF), BF16),
                            pltpu.VMEM((EXPERT_FF, D_MODEL), BF16),
                            pltpu.SemaphoreType.DMA((3,))]),
        out_shape=jax.ShapeDtypeStruct((P_ROWS * TOK_ROWS, LANES), U32),
        compiler_params=_cparams(("arbitrary",)),
        name="experts",
    )(block_expert, block_rows, next_expert, xs, w_gate, w_up, w_down)


def _combine_kernel(lslot_ref, rcnt_ref, rloc_ref, rglb_ref, w_ref, ys_ref, hp_ref, x1_ref, gf_ref,
                    wsg_ref, wsu_ref, wsd_ref, fn_ref, o_ref, buf_ref, mlo_ref, mhi_ref, sem, *, final_norm):
    i = pl.program_id(0)
    nsteps = pl.num_programs(0)

    def issue(step, slot):
        def run(e, _):
            n = rcnt_ref[step * N_EXPERTS + e]

            @pl.when(n > 0)
            def _():
                src = pl.multiple_of(rglb_ref[step * N_EXPERTS + e] * TOK_ROWS, TOK_ROWS)
                dst = pl.multiple_of(rloc_ref[step * N_EXPERTS + e], TOK_ROWS)
                pltpu.make_async_copy(ys_ref.at[pl.ds(src, n * TOK_ROWS)],
                                      buf_ref.at[pl.ds(dst, n * TOK_ROWS)], sem.at[slot]).start()
            return 0

        lax.fori_loop(0, N_EXPERTS, run, 0)

    @pl.when(i == 0)
    def _():
        issue(0, 0)

    @pl.when(i + 1 < nsteps)
    def _():
        issue(i + 1, (i + 1) % 2)

    cur = i % 2
    cur_half = pl.ds(pl.multiple_of(cur * LOCAL_ROWS, LOCAL_ROWS), LOCAL_ROWS)
    pltpu.make_async_copy(ys_ref.at[pl.ds(0, LOCAL_ROWS)], buf_ref.at[cur_half], sem.at[cur]).wait()

    base = i * T_DISP

    def token(t, _):
        acc_lo = jnp.zeros((TOK_ROWS, LANES), F32)
        acc_hi = jnp.zeros((TOK_ROWS, LANES), F32)
        for k in range(TOP_K):
            idx = (base + t) * TOP_K + k
            src = pl.multiple_of(lslot_ref[idx], TOK_ROWS)
            lo, hi = _unpack_bf16_pair(buf_ref[pl.ds(src, TOK_ROWS), :])
            wk = w_ref[idx]
            acc_lo = acc_lo + wk * lo
            acc_hi = acc_hi + wk * hi
        dst = pl.ds(pl.multiple_of(t * TOK_ROWS, TOK_ROWS), TOK_ROWS)
        mlo_ref[dst, :] = acc_lo
        mhi_ref[dst, :] = acc_hi
        return 0

    lax.fori_loop(0, T_DISP, token, 0, unroll=2)
    moe_lo = _load_token_tiles(mlo_ref, T_DISP)
    moe_hi = _load_token_tiles(mhi_ref, T_DISP)

    hlo, hhi = _unpack_bf16_pair(_load_token_tiles(hp_ref, T_DISP))
    hlo = hlo.astype(BF16)
    hhi = hhi.astype(BF16)
    sg = (jnp.dot(hlo, wsg_ref[0:PACK_W], preferred_element_type=F32)
          + jnp.dot(hhi, wsg_ref[PACK_W:D_MODEL], preferred_element_type=F32))
    su = (jnp.dot(hlo, wsu_ref[0:PACK_W], preferred_element_type=F32)
          + jnp.dot(hhi, wsu_ref[PACK_W:D_MODEL], preferred_element_type=F32))
    shared = jnp.dot((_silu(sg) * su).astype(BF16), wsd_ref[...], preferred_element_type=F32)
    moe = jnp.concatenate([moe_lo, moe_hi], axis=1)
    out = x1_ref[...] + gf_ref[0] * (moe + shared)
    if final_norm:
        out = _rms(out, fn_ref[...])
    o_ref[...] = out


def _combine(lslot_flat, run_cnt, run_loc, run_glb, w_flat, ys, hp, x1, g_f, wsg, wsu, wsd, final_gain, final_norm):
    per_b = SEQ // T_DISP
    return pl.pallas_call(
        functools.partial(_combine_kernel, final_norm=final_norm),
        grid_spec=pltpu.PrefetchScalarGridSpec(
            num_scalar_prefetch=4,
            grid=(N_TILES,),
            in_specs=[pl.BlockSpec(memory_space=pltpu.SMEM),
                      pl.BlockSpec(memory_space=pl.ANY),
                      pl.BlockSpec((T_DISP * TOK_ROWS, LANES), lambda i, *_: (i, 0)),
                      pl.BlockSpec((T_DISP, D_MODEL), lambda i, *_: (i, 0)),
                      pl.BlockSpec((1, 1, D_MODEL), lambda i, *_: (i // per_b, 0, 0)),
                      pl.BlockSpec((D_MODEL, SHARED_FF), lambda i, *_: (0, 0)),
                      pl.BlockSpec((D_MODEL, SHARED_FF), lambda i, *_: (0, 0)),
                      pl.BlockSpec((SHARED_FF, D_MODEL), lambda i, *_: (0, 0)),
                      pl.BlockSpec((1, D_MODEL), lambda i, *_: (0, 0))],
            out_specs=pl.BlockSpec((T_DISP, D_MODEL), lambda i, *_: (i, 0)),
            scratch_shapes=[pltpu.VMEM((2 * LOCAL_ROWS, LANES), U32),
                            pltpu.VMEM((T_DISP * TOK_ROWS, LANES), F32),
                            pltpu.VMEM((T_DISP * TOK_ROWS, LANES), F32),
                            pltpu.SemaphoreType.DMA((2,))]),
        out_shape=jax.ShapeDtypeStruct((N_TOK, D_MODEL), F32),
        compiler_params=_cparams(("arbitrary",)),
        name="combine",
    )(lslot_flat, run_cnt, run_loc, run_glb, w_flat, ys, hp, x1, g_f, wsg, wsu, wsd, final_gain.reshape(1, D_MODEL))


def _moe_layer(layer, hp, logits, x1, g_f, router_bias, w_gate, w_up, w_down, ws_gate, ws_up, ws_down,
               final_gain, final_norm):
    lslot, w_k, tables, counts = _route(logits, router_bias)
    cnt = counts[:, 0].astype(I32)
    padded = ((cnt + TM_EXP - 1) // TM_EXP) * TM_EXP
    ends = jnp.cumsum(padded)
    offsets = ends - padded
    tables = tables[:, :, :, 0].astype(I32)
    run_cnt = tables[:, 0].reshape(-1)
    run_glb = (offsets[None, :] + tables[:, 1]).reshape(-1)
    half_row0 = (jnp.arange(N_TILES, dtype=I32) % 2) * LOCAL_ROWS
    run_loc = (tables[:, 2] * TOK_ROWS + half_row0[:, None]).reshape(-1)
    lslot_flat = (lslot * TOK_ROWS + jnp.repeat(half_row0, T_DISP)[None, :]).T.reshape(N_SLOTS)
    blk_start = jnp.arange(NB_EXP, dtype=I32) * TM_EXP
    expert_ids = jnp.arange(N_EXPERTS, dtype=I32)
    nonempty = cnt > 0
    last_nonempty = jnp.max(jnp.where(nonempty, expert_ids, 0))
    block_expert = jnp.minimum(jnp.sum((blk_start[:, None] >= ends[None, :]).astype(I32), axis=1), last_nonempty)
    block_valid = (blk_start < ends[-1]).astype(I32)
    block_rows = jnp.clip((offsets + cnt)[block_expert] - blk_start, 0, TM_EXP) * block_valid
    following = jnp.where(nonempty, expert_ids, N_EXPERTS)
    following = lax.cummin(following, reverse=True)
    following = jnp.concatenate([following[1:], jnp.full((1,), N_EXPERTS, I32)])
    next_expert = jnp.where(following < N_EXPERTS, following, -1)[block_expert]
    zero_start = jnp.concatenate([offsets + cnt, blk_start])
    zero_len = jnp.concatenate([padded - cnt, (1 - block_valid) * TM_EXP])
    xs = _dispatch(lslot_flat, run_cnt, run_loc, run_glb, zero_start, zero_len, hp)
    ys = _experts(layer, block_expert, block_rows, next_expert, xs, w_gate, w_up, w_down)
    return _combine(lslot_flat, run_cnt, run_loc, run_glb, w_k.T.reshape(N_SLOTS), ys, hp, x1, g_f,
                    ws_gate.astype(BF16), ws_up.astype(BF16), ws_down.astype(BF16), final_gain, final_norm)


def _rope_tables(dim):
    inv = ROPE_THETA ** (-jnp.arange(0, dim, 2, dtype=F32) / dim)
    ang = jnp.arange(SEQ, dtype=F32)[:, None] * inv[None, :]
    return jnp.cos(ang), jnp.sin(ang)


def _rot_half_cols(w):
    half = w.shape[-1] // 2
    return jnp.concatenate([-w[..., half:], w[..., :half]], axis=-1)


def _t5_bucket(rel):
    half = REL_BUCKETS // 2
    max_exact = half // 2
    ret = (rel > 0).astype(I32) * half
    n = jnp.abs(rel)
    nf = jnp.maximum(n, 1).astype(F32)
    large = max_exact + (jnp.log(nf / max_exact) / math.log(REL_MAX_DIST / max_exact)
                         * (half - max_exact)).astype(I32)
    large = jnp.minimum(large, half - 1)
    return ret + jnp.where(n < max_exact, n, large)


def _swa_bias_table(rel_bias):
    qi = jnp.arange(WINDOW)[:, None]
    kj = jnp.arange(3 * WINDOW)[None, :]
    rel = kj - WINDOW - qi
    onehot = (_t5_bucket(rel)[:, :, None] == jnp.arange(REL_BUCKETS)).astype(F32)
    bias = jnp.einsum('qjb,bh->hqj', onehot, rel_bias.astype(F32), precision=lax.Precision.HIGHEST)
    return jnp.where((jnp.abs(rel) <= WINDOW)[None], bias, NEG_BIG)


def kernel(x, c, w_mod, b_mod, norm_mix, norm_ffn, final_norm, w_in_ab, q_lat_norm, kv_lat_norm, w_uq, w_ukv, conv_w, conv_b, lru_w_a, lru_b_a, lru_w_x, lru_b_x, lru_lambda, w_out_ab, w_in_cd, ret_gn, swa_sinks, w_out_cd, rel_bias, w_router, router_bias, w_gate, w_up, w_down, ws_gate, ws_up, ws_down):
    xf = x.reshape(N_TOK, D_MODEL)
    mod = _modulation(c, w_mod, b_mod)
    cos_r, sin_r = _rope_tables(MLA_ROPE)
    cs_tab = jnp.concatenate([cos_r, cos_r, sin_r, sin_r], axis=1)
    cos_t, sin_t = _rope_tables(RET_DK)
    lg_ret = jnp.log1p(-(2.0 ** (-5.0 - jnp.arange(RET_HEADS, dtype=F32))))

    for layer in range(DEPTH):
        sh_m, sc_m, g_m, sh_f, sc_f, g_f = [m.reshape(BATCH, 1, D_MODEL) for m in jnp.split(mod[layer], 6, axis=-1)]
        i = layer // 2
        if layer % 2 == 0:
            w = w_in_ab[i]
            o1, o2, o3, o4 = np.cumsum((MLA_Q_RANK, MLA_KV_RANK, MLA_ROPE, LRU_WIDTH)).tolist()
            w_kr = w[:, o2:o3]
            w_in = jnp.concatenate([w[:, :o2], w[:, o3:], w_kr, _rot_half_cols(w_kr)], axis=1).astype(BF16)
            p0 = _in_projection(xf, norm_mix[layer], sc_m, sh_m, w_in, "in_proj_ab")
            wq = w_uq[i].reshape(MLA_Q_RANK, MLA_HEADS, MLA_NOPE + MLA_ROPE)
            wq_r = wq[:, :, MLA_NOPE:]
            wq = jnp.concatenate([wq, _rot_half_cols(wq_r)], axis=-1).reshape(MLA_Q_RANK, MLA_HEADS * MLA_QK)
            q, k, v = _mla_up(p0, q_lat_norm[i], kv_lat_norm[i], wq.astype(BF16), w_ukv[i].astype(BF16), cs_tab)
            a_out = _mla_attention(q, k, v)
            w_gates = jnp.concatenate([lru_w_a[i, 0], lru_w_x[i, 0], lru_w_a[i, 1], lru_w_x[i, 1]], axis=-1).astype(BF16)
            b_gates = jnp.concatenate([b.reshape(LRU_BLOCKS, 1, LRU_BS) for b in
                                       (lru_b_a[i, 0], lru_b_x[i, 0], lru_b_a[i, 1], lru_b_x[i, 1])], axis=-1)
            b_out = _rglru(p0, conv_w[i], conv_b[i], w_gates, b_gates, lru_lambda[i])
            w_out = w_out_ab[i].astype(BF16)
        else:
            p1 = _in_projection(xf, norm_mix[layer], sc_m, sh_m, w_in_cd[i].astype(BF16), "in_proj_cd")
            a_out = _retention(p1, lg_ret, cos_t, sin_t, ret_gn[i])
            b_out = _swa(p1, swa_sinks[i], _swa_bias_table(rel_bias))
            w_out = w_out_cd[i].astype(BF16)
        w_r = jnp.pad(w_router[layer], ((0, 0), (0, LANES - N_EXPERTS)))
        w_r_hi = w_r.astype(BF16)
        w_router_pad = jnp.concatenate([w_r_hi, (w_r - w_r_hi.astype(F32)).astype(BF16)], axis=1)
        x1, hp, logits = _out_projection(a_out, b_out, w_out, xf, g_m, norm_ffn[layer], sc_f, sh_f, w_router_pad)
        xf = _moe_layer(layer, hp, logits, x1, g_f, router_bias[layer], w_gate, w_up, w_down,
                        ws_gate[layer], ws_up[layer], ws_down[layer], final_norm, layer == DEPTH - 1)
    return xf.reshape(BATCH, SEQ, D_MODEL)
```

```python
import functools
import math

import numpy as np
import jax
import jax.numpy as jnp
from jax import lax
from jax.experimental import pallas as pl
from jax.experimental.pallas import tpu as pltpu

F32 = jnp.float32
BF16 = jnp.bfloat16
I32 = jnp.int32
U32 = jnp.uint32

D_MODEL = 2048
BATCH = 4
SEQ = 2048
DEPTH = 2
N_TOK = BATCH * SEQ
HALF = D_MODEL // 2
MLA_NOPE = 128
MLA_ROPE = 64
MLA_V = 128
MLA_HEADS = HALF // MLA_V
MLA_Q_RANK = D_MODEL // 4
MLA_KV_RANK = D_MODEL // 4
MLA_QK = 256
LRU_WIDTH = HALF
LRU_BLOCKS = 8
LRU_BS = LRU_WIDTH // LRU_BLOCKS
LRU_CONV = 4
LRU_C = 8.0
RET_DK = 256
RET_DV = 256
RET_HEADS = HALF // RET_DV
SWA_HD = 128
SWA_HEADS = HALF // SWA_HD
SWA_KV_HEADS = 2
SWA_G = SWA_HEADS // SWA_KV_HEADS
WINDOW = 128
REL_BUCKETS = 32
REL_MAX_DIST = 128
N_EXPERTS = 64
TOP_K = 8
N_GROUPS = 8
GROUP_SIZE = N_EXPERTS // N_GROUPS
TOP_GROUPS = 4
EXPERT_FF = D_MODEL // 4
SHARED_FF = D_MODEL // 4
ROUTE_SCALE = 2.5
ROPE_THETA = 10000.0
EPS = 1e-6
NEG_BIG = -1e30

LANES = 128
SUBLANES = 8
VMEM_LIMIT = 52 * 2**20

TM_PROJ = 256
TM_PROJ_SUB = 128
TM_UP = 512
TQ_ATT = 512
TQ_MLA = 2048
TQ_SUB = 256
SWA_BLOCKS = 4
TM_OUT = 512
TM_OUT_SUB = 256
T_DISP = 256
T_ROUTE = T_DISP
N_TILES = N_TOK // T_DISP
TILE_ROWS = T_DISP * TOP_K
TM_EXP = 512
TM_EXP_SUB = 256
TM_EXP_TAIL = 128
assert TM_EXP == 2 * TM_EXP_SUB and TM_EXP % TM_EXP_TAIL == 0
CAST_VREGS = 32
CAST_PIECES = 16
N_SLOTS = N_TOK * TOP_K
NB_EXP = N_SLOTS // TM_EXP + N_EXPERTS
P_ROWS = NB_EXP * TM_EXP
N_ZERO_RANGES = N_EXPERTS + NB_EXP
PACK_W = D_MODEL // 2
TOK_ROWS = PACK_W // LANES
assert TOK_ROWS == SUBLANES
LOCAL_ROWS = TILE_ROWS * TOK_ROWS

LRU_SEG = 260
LRU_ROWS = SUBLANES * LRU_SEG
assert LRU_ROWS >= SEQ and LRU_SEG % 8 == 4


def _cparams(sem, vmem=VMEM_LIMIT):
    return pltpu.CompilerParams(dimension_semantics=sem, vmem_limit_bytes=vmem)


def _sigmoid(x):
    return 0.5 * jnp.tanh(0.5 * x) + 0.5


def _silu(x):
    return x * _sigmoid(x)


def _rms(x, g):
    return x * lax.rsqrt(jnp.mean(x * x, axis=-1, keepdims=True) + EPS) * g


def _pack_bf16_pair(lo, hi):
    lo_b = lax.bitcast_convert_type(lo.astype(BF16).astype(F32), U32)
    hi_b = lax.bitcast_convert_type(hi.astype(BF16).astype(F32), U32)
    return (hi_b & jnp.uint32(0xFFFF0000)) | (lo_b >> 16)


def _unpack_bf16_pair(w):
    lo = lax.bitcast_convert_type(w << 16, F32)
    hi = lax.bitcast_convert_type(w & jnp.uint32(0xFFFF0000), F32)
    return lo, hi


def _store_token_tiles(ref, packed, tok0=0):
    t = packed.shape[0]
    for s in range(TOK_ROWS):
        ref[pl.ds(tok0 * TOK_ROWS + s, t, stride=TOK_ROWS), :] = packed[:, s * LANES:(s + 1) * LANES]


def _load_token_tiles(ref, t):
    return jnp.concatenate([ref[pl.ds(s, t, stride=TOK_ROWS), :] for s in range(TOK_ROWS)], axis=1)


def _mod_kernel(c_ref, w_ref, b_ref, o_ref):
    c = c_ref[...]
    ca = _silu(c).astype(BF16)
    o_ref[0] = jnp.dot(ca, w_ref[0].astype(BF16), preferred_element_type=F32) + b_ref[0]


def _modulation(c, w_mod, b_mod):
    tn = 1024
    cp = jnp.pad(c, ((0, SUBLANES - BATCH), (0, 0)))
    out = pl.pallas_call(
        _mod_kernel,
        grid=(DEPTH, 6 * D_MODEL // tn),
        in_specs=[pl.BlockSpec((SUBLANES, D_MODEL), lambda l, j: (0, 0)),
                  pl.BlockSpec((1, D_MODEL, tn), lambda l, j: (l, 0, j)),
                  pl.BlockSpec((1, 1, tn), lambda l, j: (l, 0, j))],
        out_specs=pl.BlockSpec((1, SUBLANES, tn), lambda l, j: (l, 0, j)),
        out_shape=jax.ShapeDtypeStruct((DEPTH, SUBLANES, 6 * D_MODEL), F32),
        compiler_params=_cparams(("parallel", "parallel")),
        name="adaln_mod",
    )(cp, w_mod, b_mod.reshape(DEPTH, 1, 6 * D_MODEL))
    return out[:, :BATCH]


def _inproj_kernel(x_ref, g_ref, sc_ref, sh_ref, w_ref, o_ref):
    for r in range(TM_PROJ // TM_PROJ_SUB):
        rows = pl.ds(r * TM_PROJ_SUB, TM_PROJ_SUB)
        y = _rms(x_ref[rows, :], g_ref[...])
        h = (y * (1.0 + sc_ref[0]) + sh_ref[0]).astype(BF16)
        o_ref[rows, :] = jnp.dot(h, w_ref[...], preferred_element_type=F32)


def _in_projection(x, gain, scale, shift, w_bf16, name):
    p = w_bf16.shape[1]
    per_b = SEQ // TM_PROJ
    return pl.pallas_call(
        _inproj_kernel,
        grid=(N_TOK // TM_PROJ,),
        in_specs=[pl.BlockSpec((TM_PROJ, D_MODEL), lambda i: (i, 0)),
                  pl.BlockSpec((1, D_MODEL), lambda i: (0, 0)),
                  pl.BlockSpec((1, 1, D_MODEL), lambda i: (i // per_b, 0, 0)),
                  pl.BlockSpec((1, 1, D_MODEL), lambda i: (i // per_b, 0, 0)),
                  pl.BlockSpec((D_MODEL, p), lambda i: (0, 0), pipeline_mode=pl.Buffered(1))],
        out_specs=pl.BlockSpec((TM_PROJ, p), lambda i: (i, 0)),
        out_shape=jax.ShapeDtypeStruct((N_TOK, p), F32),
        compiler_params=_cparams(("parallel",), 56 * 2**20),
        name=name,
    )(x, gain.reshape(1, D_MODEL), scale, shift, w_bf16)


def _mla_up_kernel(ql_ref, kvl_ref, kr_ref, qn_ref, kvn_ref, wq_ref, wkv_ref, cs_ref, q_ref, k_ref, v_ref):
    scale = (MLA_NOPE + MLA_ROPE) ** -0.5 * math.log2(math.e)
    hq = _rms(ql_ref[...], qn_ref[...]).astype(BF16)
    hkv = _rms(kvl_ref[...], kvn_ref[...]).astype(BF16)
    yq = jnp.dot(hq, wq_ref[...], preferred_element_type=F32) * scale
    ykv = jnp.dot(hkv, wkv_ref[...], preferred_element_type=F32)
    cs = cs_ref[...]
    lane = lax.broadcasted_iota(I32, cs.shape, 1)

    def rope_sum(blk):
        z = blk * cs
        return z + pltpu.roll(z, MLA_ROPE, 1)

    kr = jnp.where(lane < MLA_ROPE, rope_sum(kr_ref[...]), 0.0).astype(BF16)
    ones_col = jnp.where(lane == 0, 1.0, 0.0).astype(BF16)
    for h in range(MLA_HEADS):
        c0 = h * MLA_QK
        q_ref[0, h, :, 0:MLA_NOPE] = yq[:, c0:c0 + MLA_NOPE].astype(BF16)
        q_ref[0, h, :, MLA_NOPE:MLA_QK] = rope_sum(yq[:, c0 + MLA_NOPE:c0 + MLA_QK]).astype(BF16)
        k_ref[0, h, :, 0:MLA_NOPE] = ykv[:, c0:c0 + MLA_NOPE].astype(BF16)
        k_ref[0, h, :, MLA_NOPE:MLA_QK] = kr
        v_ref[0, h, :, 0:MLA_V] = ykv[:, c0 + MLA_NOPE:c0 + MLA_QK].astype(BF16)
        v_ref[0, h, :, MLA_V:2 * MLA_V] = ones_col


def _mla_up(p0, q_norm, kv_norm, wq, wkv, cs_tab):
    per_b = SEQ // TM_UP
    qk_shape = jax.ShapeDtypeStruct((BATCH, MLA_HEADS, SEQ, MLA_QK), BF16)
    return pl.pallas_call(
        _mla_up_kernel,
        grid=(N_TOK // TM_UP,),
        in_specs=[pl.BlockSpec((TM_UP, MLA_Q_RANK), lambda i: (i, 0)),
                  pl.BlockSpec((TM_UP, MLA_KV_RANK), lambda i: (i, 1)),
                  pl.BlockSpec((TM_UP, LANES), lambda i: (i, 24)),
                  pl.BlockSpec((1, MLA_Q_RANK), lambda i: (0, 0)),
                  pl.BlockSpec((1, MLA_KV_RANK), lambda i: (0, 0)),
                  pl.BlockSpec((MLA_Q_RANK, MLA_HEADS * MLA_QK), lambda i: (0, 0)),
                  pl.BlockSpec((MLA_KV_RANK, MLA_HEADS * MLA_QK), lambda i: (0, 0)),
                  pl.BlockSpec((TM_UP, LANES), lambda i: (i % per_b, 0))],
        out_specs=[pl.BlockSpec((1, MLA_HEADS, TM_UP, MLA_QK), lambda i: (i // per_b, 0, i % per_b, 0)),
                   pl.BlockSpec((1, MLA_HEADS, TM_UP, MLA_QK), lambda i: (i // per_b, 0, i % per_b, 0)),
                   pl.BlockSpec((1, MLA_HEADS, TM_UP, 2 * MLA_V), lambda i: (i // per_b, 0, i % per_b, 0))],
        out_shape=[qk_shape, qk_shape, jax.ShapeDtypeStruct((BATCH, MLA_HEADS, SEQ, 2 * MLA_V), BF16)],
        compiler_params=_cparams(("parallel",)),
        name="mla_up",
    )(p0, p0, p0, q_norm.reshape(1, -1), kv_norm.reshape(1, -1), wq, wkv, cs_tab)


def _mla_attn_kernel(q_ref, k_ref, v_ref, o_ref):
    k = k_ref[0, 0]
    v = v_ref[0, 0]
    for r in range(TQ_MLA // TQ_SUB):
        rows = pl.ds(r * TQ_SUB, TQ_SUB)
        s = lax.dot_general(q_ref[0, 0, rows, :], k, (((1,), (1,)), ((), ())), preferred_element_type=F32)
        p = jnp.exp2(s - jnp.max(s, axis=-1, keepdims=True))
        o = jnp.dot(p.astype(BF16), v, preferred_element_type=F32)
        o_ref[rows, :] = (o[:, 0:MLA_V] / o[:, MLA_V:MLA_V + 1]).astype(BF16)


def _mla_attention(q, k, v):
    nq = SEQ // TQ_MLA
    return pl.pallas_call(
        _mla_attn_kernel,
        grid=(BATCH, MLA_HEADS, nq),
        in_specs=[pl.BlockSpec((1, 1, TQ_MLA, MLA_QK), lambda b, h, i: (b, h, i, 0)),
                  pl.BlockSpec((1, 1, SEQ, MLA_QK), lambda b, h, i: (b, h, 0, 0)),
                  pl.BlockSpec((1, 1, SEQ, 2 * MLA_V), lambda b, h, i: (b, h, 0, 0))],
        out_specs=pl.BlockSpec((TQ_MLA, MLA_V), lambda b, h, i: (b * nq + i, h)),
        out_shape=jax.ShapeDtypeStruct((N_TOK, HALF), BF16),
        compiler_params=_cparams(("parallel", "parallel", "parallel")),
        name="mla_attn",
    )(q, k, v)


def _lru_kernel(x_ref, gate_ref, cw_ref, cb_ref, wg_ref, bg_ref, lam_ref, o_ref,
                af_ref, uf_ref, ab_ref, ub_ref, hf_ref, pf_ref, hb_ref, pb_ref, hs_ref):
    x = x_ref[...]
    row = lax.broadcasted_iota(I32, x.shape, 0)

    def shifted(d):
        r = pltpu.roll(x, (-d) % SEQ, 0)
        return jnp.where((row + d >= 0) & (row + d < SEQ), r, 0.0)

    cw = cw_ref[...]
    left = LRU_CONV // 2
    xc = cb_ref[...]
    for kk in range(LRU_CONV):
        d = kk - left
        xc = xc + cw[kk:kk + 1] * (x if d == 0 else shifted(d))

    gates = jnp.dot(xc.astype(BF16), wg_ref[0], preferred_element_type=F32) + bg_ref[0]
    lam = lam_ref[...]
    z = -lam
    sp = jnp.maximum(z, 0.0) + jnp.log1p(jnp.exp(-jnp.abs(z)))
    pad_rows = LRU_ROWS - SEQ
    for d, (a_ref, u_ref) in enumerate(((af_ref, uf_ref), (ab_ref, ub_ref))):
        r = _sigmoid(gates[:, d * 256:d * 256 + LRU_BS])
        i = _sigmoid(gates[:, d * 256 + LRU_BS:(d + 1) * 256])
        a = jnp.exp(r * (-LRU_C * sp[d:d + 1]))
        a_ref[0:SEQ] = a
        u_ref[0:SEQ] = jnp.sqrt(1.0 - a * a) * (i * xc)
        a_ref[SEQ:LRU_ROWS] = jnp.zeros((pad_rows, LANES), F32)
        u_ref[SEQ:LRU_ROWS] = jnp.zeros((pad_rows, LANES), F32)

    ones = jnp.ones((SUBLANES, LANES), F32)
    zeros = jnp.zeros((SUBLANES, LANES), F32)

    def seg(t):
        return pl.ds(t, SUBLANES, stride=LRU_SEG)

    def local_scan(s, carry):
        p_f, h_f, p_b, h_b = carry
        tf = s
        tb = LRU_SEG - 1 - s
        a = af_ref[seg(tf)]
        h_f = a * h_f + uf_ref[seg(tf)]
        p_f = a * p_f
        hf_ref[seg(tf)] = h_f
        pf_ref[seg(tf)] = p_f
        a = ab_ref[seg(tb)]
        h_b = a * h_b + ub_ref[seg(tb)]
        p_b = a * p_b
        hb_ref[seg(tb)] = h_b
        pb_ref[seg(tb)] = p_b
        return p_f, h_f, p_b, h_b

    p_f, h_f, p_b, h_b = lax.fori_loop(0, LRU_SEG, local_scan, (ones, zeros, ones, zeros), unroll=4)

    rows_f = []
    c = jnp.zeros((1, LANES), F32)
    for j in range(SUBLANES):
        rows_f.append(c)
        c = p_f[j:j + 1] * c + h_f[j:j + 1]
    rows_b = [None] * SUBLANES
    c = jnp.zeros((1, LANES), F32)
    for j in range(SUBLANES - 1, -1, -1):
        rows_b[j] = c
        c = p_b[j:j + 1] * c + h_b[j:j + 1]
    sub = lax.broadcasted_iota(I32, (SUBLANES, LANES), 0)
    c_f = zeros
    c_b = zeros
    for j in range(SUBLANES):
        c_f = jnp.where(sub == j, rows_f[j], c_f)
        c_b = jnp.where(sub == j, rows_b[j], c_b)

    def fixup(t, _):
        hs_ref[seg(t)] = (hf_ref[seg(t)] + pf_ref[seg(t)] * c_f) + (hb_ref[seg(t)] + pb_ref[seg(t)] * c_b)
        return 0

    lax.fori_loop(0, LRU_SEG, fixup, 0, unroll=4)

    g = gate_ref[...]
    gelu = 0.5 * g * (1.0 + jnp.tanh(math.sqrt(2.0 / math.pi) * (g + 0.044715 * (g * g * g))))
    o_ref[...] = (gelu * hs_ref[0:SEQ]).astype(BF16)


def _rglru(p0, conv_w, conv_b, w_gates, b_gates, lam):
    scan_buf = pltpu.VMEM((LRU_ROWS, LANES), F32)
    return pl.pallas_call(
        _lru_kernel,
        grid=(BATCH, LRU_BLOCKS),
        in_specs=[pl.BlockSpec((SEQ, LRU_BS), lambda b, g: (b, 8 + g)),
                  pl.BlockSpec((SEQ, LRU_BS), lambda b, g: (b, 16 + g)),
                  pl.BlockSpec((LRU_CONV, LRU_BS), lambda b, g: (0, g)),
                  pl.BlockSpec((1, LRU_BS), lambda b, g: (0, g)),
                  pl.BlockSpec((1, LRU_BS, 4 * LRU_BS), lambda b, g: (g, 0, 0)),
                  pl.BlockSpec((1, 1, 4 * LRU_BS), lambda b, g: (g, 0, 0)),
                  pl.BlockSpec((2, LRU_BS), lambda b, g: (0, g))],
        out_specs=pl.BlockSpec((SEQ, LRU_BS), lambda b, g: (b, g)),
        out_shape=jax.ShapeDtypeStruct((N_TOK, LRU_WIDTH), BF16),
        scratch_shapes=[scan_buf] * 9,
        compiler_params=_cparams(("parallel", "parallel")),
        name="rglru",
    )(p0, p0, conv_w, conv_b.reshape(1, -1), w_gates, b_gates, lam)


def _ret_kernel(lg_ref, q_ref, k_ref, v_ref, g_ref, cq_ref, sq_ref, ck_ref, sk_ref, gn_ref, o_ref, ks_ref, vs_ref):
    h = pl.program_id(1)
    qi = pl.program_id(2)
    half = RET_DK // 2

    def rope(t, c, s):
        t1, t2 = t[:, :half], t[:, half:]
        return jnp.concatenate([t1 * c - t2 * s, t2 * c + t1 * s], axis=1)

    @pl.when(qi == 0)
    def _():
        ks_ref[...] = (rope(k_ref[...], ck_ref[...], sk_ref[...]) * (RET_DK ** -0.5)).astype(BF16)
        vs_ref[...] = v_ref[...].astype(BF16)

    q = rope(q_ref[...], cq_ref[...], sq_ref[...]).astype(BF16)
    s = lax.dot_general(q, ks_ref[...], (((1,), (1,)), ((), ())), preferred_element_type=F32)
    n = qi * TQ_ATT + lax.broadcasted_iota(I32, s.shape, 0)
    m = lax.broadcasted_iota(I32, s.shape, 1)
    d = (n - m).astype(F32)
    lg_f = lg_ref[h]
    lg_b = lg_ref[RET_HEADS - 1 - h]
    dec = jnp.exp(jnp.where(d >= 0.0, lg_f * d, -lg_b * d))
    o = jnp.dot((s * dec).astype(BF16), vs_ref[...], preferred_element_type=F32)
    y = _rms(o, gn_ref[0])
    o_ref[...] = (_silu(g_ref[...]) * y).astype(BF16)


def _retention(p1, lg, cos_t, sin_t, ret_gn):
    nq = SEQ // TQ_ATT
    half = RET_DK // 2
    return pl.pallas_call(
        _ret_kernel,
        grid=(BATCH, RET_HEADS, nq),
        in_specs=[pl.BlockSpec(memory_space=pltpu.SMEM),
                  pl.BlockSpec((TQ_ATT, RET_DK), lambda b, h, i: (b * nq + i, h)),
                  pl.BlockSpec((SEQ, RET_DK), lambda b, h, i: (b, RET_HEADS + h)),
                  pl.BlockSpec((SEQ, RET_DV), lambda b, h, i: (b, 2 * RET_HEADS + h)),
                  pl.BlockSpec((TQ_ATT, RET_DV), lambda b, h, i: (b * nq + i, 3 * RET_HEADS + h)),
                  pl.BlockSpec((TQ_ATT, half), lambda b, h, i: (i, 0)),
                  pl.BlockSpec((TQ_ATT, half), lambda b, h, i: (i, 0)),
                  pl.BlockSpec((SEQ, half), lambda b, h, i: (0, 0)),
                  pl.BlockSpec((SEQ, half), lambda b, h, i: (0, 0)),
                  pl.BlockSpec((1, 1, RET_DV), lambda b, h, i: (h, 0, 0))],
        out_specs=pl.BlockSpec((TQ_ATT, RET_DV), lambda b, h, i: (b * nq + i, h)),
        out_shape=jax.ShapeDtypeStruct((N_TOK, HALF), BF16),
        scratch_shapes=[pltpu.VMEM((SEQ, RET_DK), BF16), pltpu.VMEM((SEQ, RET_DV), BF16)],
        compiler_params=_cparams(("parallel", "parallel", "arbitrary")),
        name="retention",
    )(lg, p1, p1, p1, p1, cos_t, sin_t, cos_t, sin_t, ret_gn.reshape(RET_HEADS, 1, RET_DV))


def _swa_kernel(sink_ref, q_ref, k_ref, v_ref, bias_ref, o_ref):
    kv = pl.program_id(1)
    nb = SEQ // WINDOW
    w = WINDOW

    def rows(ref, blk):
        return ref[pl.ds(pl.multiple_of(blk * w, w), w), :].astype(BF16)

    col = lax.broadcasted_iota(I32, (w, 3 * w), 1)
    for j in range(SWA_BLOCKS):
        n = pl.program_id(2) * SWA_BLOCKS + j
        prev = jnp.maximum(n - 1, 0)
        nxt = jnp.minimum(n + 1, nb - 1)
        kw = jnp.concatenate([rows(k_ref, prev), rows(k_ref, n), rows(k_ref, nxt)], axis=0)
        vw = jnp.concatenate([rows(v_ref, prev), rows(v_ref, n), rows(v_ref, nxt)], axis=0)
        qb = q_ref[j * w:(j + 1) * w, :]
        q4 = jnp.concatenate([qb[:, g * SWA_HD:(g + 1) * SWA_HD] for g in range(SWA_G)], axis=0).astype(BF16)
        s = lax.dot_general(q4, kw, (((1,), (1,)), ((), ())), preferred_element_type=F32) * (SWA_HD ** -0.5)
        outside = ((col < w) & (n == 0)) | ((col >= 2 * w) & (n == nb - 1))
        for g in range(SWA_G):
            sg = jnp.where(outside, NEG_BIG, s[g * w:(g + 1) * w] + bias_ref[g])
            sink = sink_ref[kv * SWA_G + g]
            m = jnp.maximum(jnp.max(sg, axis=-1, keepdims=True), sink)
            p = jnp.exp(sg - m)
            denom = jnp.sum(p, axis=-1, keepdims=True) + jnp.exp(sink - m)
            o = jnp.dot((p / denom).astype(BF16), vw, preferred_element_type=F32)
            o_ref[j * w:(j + 1) * w, g * SWA_HD:(g + 1) * SWA_HD] = o.astype(BF16)


def _swa(p1, sinks, bias):
    nb = SEQ // WINDOW
    qcols = SWA_G * SWA_HD
    q_blk0 = (4 * RET_HEADS * RET_DK) // qcols
    k_blk0 = (4 * RET_HEADS * RET_DK + SWA_HEADS * SWA_HD) // SWA_HD
    v_blk0 = k_blk0 + SWA_KV_HEADS
    return pl.pallas_call(
        _swa_kernel,
        grid=(BATCH, SWA_KV_HEADS, nb // SWA_BLOCKS),
        in_specs=[pl.BlockSpec(memory_space=pltpu.SMEM),
                  pl.BlockSpec((SWA_BLOCKS * WINDOW, qcols), lambda b, kv, n: (b * (nb // SWA_BLOCKS) + n, q_blk0 + kv)),
                  pl.BlockSpec((SEQ, SWA_HD), lambda b, kv, n: (b, k_blk0 + kv)),
                  pl.BlockSpec((SEQ, SWA_HD), lambda b, kv, n: (b, v_blk0 + kv)),
                  pl.BlockSpec((SWA_G, WINDOW, 3 * WINDOW), lambda b, kv, n: (kv, 0, 0))],
        out_specs=pl.BlockSpec((SWA_BLOCKS * WINDOW, qcols), lambda b, kv, n: (b * (nb // SWA_BLOCKS) + n, kv)),
        out_shape=jax.ShapeDtypeStruct((N_TOK, HALF), BF16),
        compiler_params=_cparams(("parallel", "parallel", "parallel")),
        name="swa",
    )(sinks, p1, p1, p1, bias)


def _outproj_kernel(a_ref, b_ref, wa_ref, wb_ref, x_ref, gm_ref, g_ref, sc_ref, sh_ref, wr_ref,
                    x1_ref, hp_ref, lg_ref):
    wr = wr_ref[...]
    for r in range(TM_OUT // TM_OUT_SUB):
        rows = pl.ds(r * TM_OUT_SUB, TM_OUT_SUB)
        mixed = (jnp.dot(a_ref[rows, :], wa_ref[...], preferred_element_type=F32)
                 + jnp.dot(b_ref[rows, :], wb_ref[...], preferred_element_type=F32))
        x1 = x_ref[rows, :] + gm_ref[0] * mixed
        x1_ref[rows, :] = x1
        hf = _rms(x1, g_ref[...]) * (1.0 + sc_ref[0]) + sh_ref[0]
        _store_token_tiles(hp_ref, _pack_bf16_pair(hf[:, :PACK_W], hf[:, PACK_W:]), r * TM_OUT_SUB)
        h_hi = hf.astype(BF16)
        h_lo = (hf - h_hi.astype(F32)).astype(BF16)
        t_hi = jnp.dot(h_hi, wr, preferred_element_type=F32)
        t_lo = jnp.dot(h_lo, wr, preferred_element_type=F32)
        lg_ref[rows, :] = (t_hi[:, :LANES] + t_hi[:, LANES:]) + (t_lo[:, :LANES] + t_lo[:, LANES:])


def _out_projection(a, b, w_out_bf16, x, g_m, gain, scale, shift, w_router_pad):
    per_b = SEQ // TM_OUT
    vec = pl.BlockSpec((1, 1, D_MODEL), lambda i: (i // per_b, 0, 0))
    return pl.pallas_call(
        _outproj_kernel,
        grid=(N_TOK // TM_OUT,),
        in_specs=[pl.BlockSpec((TM_OUT, HALF), lambda i: (i, 0)),
                  pl.BlockSpec((TM_OUT, HALF), lambda i: (i, 0)),
                  pl.BlockSpec((HALF, D_MODEL), lambda i: (0, 0)),
                  pl.BlockSpec((HALF, D_MODEL), lambda i: (1, 0)),
                  pl.BlockSpec((TM_OUT, D_MODEL), lambda i: (i, 0)),
                  vec,
                  pl.BlockSpec((1, D_MODEL), lambda i: (0, 0)),
                  vec, vec,
                  pl.BlockSpec((D_MODEL, 2 * LANES), lambda i: (0, 0))],
        out_specs=[pl.BlockSpec((TM_OUT, D_MODEL), lambda i: (i, 0)),
                   pl.BlockSpec((TM_OUT * TOK_ROWS, LANES), lambda i: (i, 0)),
                   pl.BlockSpec((TM_OUT, LANES), lambda i: (i, 0))],
        out_shape=[jax.ShapeDtypeStruct((N_TOK, D_MODEL), F32),
                   jax.ShapeDtypeStruct((N_TOK * TOK_ROWS, LANES), U32),
                   jax.ShapeDtypeStruct((N_TOK, LANES), F32)],
        compiler_params=_cparams(("parallel",)),
        name="out_proj",
    )(a, b, w_out_bf16, w_out_bf16, x, g_m, gain.reshape(1, D_MODEL), scale, shift, w_router_pad)


def _route_kernel(lg_ref, bias_ref, lslot_ref, w_ref, tab_ref, cnt_ref):
    step = pl.program_id(0)

    @pl.when(step == 0)
    def _():
        cnt_ref[...] = jnp.zeros(cnt_ref.shape, F32)

    t = T_ROUTE
    scores = jax.nn.sigmoid(lg_ref[...].T[:N_EXPERTS])
    biased = scores + bias_ref[...]
    sub = lax.broadcasted_iota(I32, (GROUP_SIZE, t), 0).astype(F32)
    ninf = -jnp.inf

    def first_argmax(v, idx, n):
        m = jnp.max(v, axis=0, keepdims=True)
        return m, jnp.min(jnp.where(v == m, idx, float(n)), axis=0, keepdims=True)

    gs = []
    for g in range(N_GROUPS):
        bg = biased[g * GROUP_SIZE:(g + 1) * GROUP_SIZE]
        m1, i1 = first_argmax(bg, sub, GROUP_SIZE)
        m2 = jnp.max(jnp.where(sub == i1, ninf, bg), axis=0, keepdims=True)
        gs.append(m1 + m2)
    cur = jnp.concatenate(gs, axis=0)

    gmask = jnp.zeros((N_GROUPS, t), F32)
    for _ in range(TOP_GROUPS):
        _, i = first_argmax(cur, sub, N_GROUPS)
        pick = sub == i
        gmask = jnp.where(pick, 1.0, gmask)
        cur = jnp.where(pick, ninf, cur)

    eid = lax.broadcasted_iota(I32, (N_EXPERTS, t), 0).astype(F32)
    emask = jnp.concatenate([jnp.broadcast_to(gmask[g:g + 1], (GROUP_SIZE, t)) for g in range(N_GROUPS)], axis=0)
    cur = jnp.where(emask > 0.5, biased, ninf)
    sels, ws = [], []
    onehot = jnp.zeros((N_EXPERTS, t), F32)
    for _ in range(TOP_K):
        _, i = first_argmax(cur, eid, N_EXPERTS)
        pick = eid == i
        sels.append(pick)
        ws.append(jnp.sum(jnp.where(pick, scores, 0.0), axis=0, keepdims=True))
        onehot = jnp.where(pick, 1.0, onehot)
        cur = jnp.where(pick, ninf, cur)
    wsum = ws[0]
    for k in range(1, TOP_K):
        wsum = wsum + ws[k]

    r = lax.broadcasted_iota(I32, (t, t), 0)
    c = lax.broadcasted_iota(I32, (t, t), 1)
    tri = (r < c).astype(BF16)
    earlier = jnp.dot(onehot.astype(BF16), tri, preferred_element_type=F32)
    tile_cnt = jnp.broadcast_to(jnp.sum(onehot, axis=1, keepdims=True), (N_EXPERTS, LANES))
    er = lax.broadcasted_iota(I32, (N_EXPERTS, N_EXPERTS), 0)
    ec = lax.broadcasted_iota(I32, (N_EXPERTS, N_EXPERTS), 1)
    run_start = jnp.dot((ec < er).astype(BF16), tile_cnt.astype(BF16), preferred_element_type=F32)
    pos = earlier + run_start[:, 0:1]
    lslots = [jnp.sum(jnp.where(sels[k], pos, 0.0), axis=0, keepdims=True) for k in range(TOP_K)]

    lslot_ref[...] = jnp.concatenate(lslots, axis=0).astype(I32)
    w_ref[...] = jnp.concatenate([w / wsum * ROUTE_SCALE for w in ws], axis=0)
    tab_ref[0, 0] = tile_cnt
    tab_ref[0, 1] = cnt_ref[...]
    tab_ref[0, 2] = run_start
    cnt_ref[...] = cnt_ref[...] + tile_cnt


def _route(logits, router_bias):
    ntiles = N_TOK // T_ROUTE
    return pl.pallas_call(
        _route_kernel,
        grid=(ntiles,),
        in_specs=[pl.BlockSpec((T_ROUTE, LANES), lambda i: (i, 0)),
                  pl.BlockSpec((N_EXPERTS, 1), lambda i: (0, 0))],
        out_specs=[pl.BlockSpec((TOP_K, T_ROUTE), lambda i: (0, i)),
                   pl.BlockSpec((TOP_K, T_ROUTE), lambda i: (0, i)),
                   pl.BlockSpec((1, 3, N_EXPERTS, LANES), lambda i: (i, 0, 0, 0)),
                   pl.BlockSpec((N_EXPERTS, LANES), lambda i: (0, 0))],
        out_shape=[jax.ShapeDtypeStruct((TOP_K, N_TOK), I32),
                   jax.ShapeDtypeStruct((TOP_K, N_TOK), F32),
                   jax.ShapeDtypeStruct((ntiles, 3, N_EXPERTS, LANES), F32),
                   jax.ShapeDtypeStruct((N_EXPERTS, LANES), F32)],
        compiler_params=_cparams(("arbitrary",)),
        name="route",
    )(logits, router_bias.reshape(N_EXPERTS, 1))


def _dispatch_kernel(lslot_ref, rcnt_ref, rloc_ref, rglb_ref, zstart_ref, zlen_ref, hp_ref, xs_ref,
                     zero_ref, loc_ref, sem, zsem):
    step = pl.program_id(0)
    nsteps = pl.num_programs(0)

    def zero_copy(z):
        start = pl.multiple_of(zstart_ref[z] * TOK_ROWS, TOK_ROWS)
        n = zlen_ref[z] * TOK_ROWS
        return pltpu.make_async_copy(zero_ref.at[pl.ds(0, n)], xs_ref.at[pl.ds(start, n)], zsem)

    @pl.when(step == 0)
    def _():
        zero_ref[...] = jnp.zeros(zero_ref.shape, U32)

        def fill(z, _):
            @pl.when(zlen_ref[z] > 0)
            def _():
                zero_copy(z).start()
            return 0

        lax.fori_loop(0, N_ZERO_RANGES, fill, 0)

    @pl.when(step == nsteps - 1)
    def _():
        def fill_wait(z, _):
            @pl.when(zlen_ref[z] > 0)
            def _():
                zero_copy(z).wait()
            return 0

        lax.fori_loop(0, N_ZERO_RANGES, fill_wait, 0)

    cur = step % 2

    def tile_wait(slot):
        half = pl.ds(pl.multiple_of(slot * LOCAL_ROWS, LOCAL_ROWS), LOCAL_ROWS)
        pltpu.make_async_copy(loc_ref.at[half], xs_ref.at[pl.ds(0, LOCAL_ROWS)], sem.at[slot]).wait()

    @pl.when(step >= 2)
    def _():
        tile_wait(cur)

    base = step * T_DISP

    def place(t, _):
        row = hp_ref[pl.ds(pl.multiple_of(t * TOK_ROWS, TOK_ROWS), TOK_ROWS), :]
        for k in range(TOP_K):
            dst = pl.multiple_of(lslot_ref[(base + t) * TOP_K + k], TOK_ROWS)
            loc_ref[pl.ds(dst, TOK_ROWS), :] = row
        return 0

    lax.fori_loop(0, T_DISP, place, 0, unroll=2)

    def run(e, _):
        n = rcnt_ref[step * N_EXPERTS + e]

        @pl.when(n > 0)
        def _():
            src = pl.multiple_of(rloc_ref[step * N_EXPERTS + e], TOK_ROWS)
            dst = pl.multiple_of(rglb_ref[step * N_EXPERTS + e] * TOK_ROWS, TOK_ROWS)
            pltpu.make_async_copy(loc_ref.at[pl.ds(src, n * TOK_ROWS)],
                                  xs_ref.at[pl.ds(dst, n * TOK_ROWS)], sem.at[cur]).start()
        return 0

    lax.fori_loop(0, N_EXPERTS, run, 0)

    @pl.when(step == nsteps - 1)
    def _():
        tile_wait(cur)

        @pl.when(nsteps > 1)
        def _():
            tile_wait(1 - cur)


def _dispatch(lslot_flat, run_cnt, run_loc, run_glb, zero_start, zero_len, hp):
    return pl.pallas_call(
        _dispatch_kernel,
        grid_spec=pltpu.PrefetchScalarGridSpec(
            num_scalar_prefetch=6,
            grid=(N_TILES,),
            in_specs=[pl.BlockSpec((T_DISP * TOK_ROWS, LANES), lambda i, *_: (i, 0))],
            out_specs=pl.BlockSpec(memory_space=pl.ANY),
            scratch_shapes=[pltpu.VMEM((TM_EXP * TOK_ROWS, LANES), U32),
                            pltpu.VMEM((2 * LOCAL_ROWS, LANES), U32),
                            pltpu.SemaphoreType.DMA((2,)),
                            pltpu.SemaphoreType.DMA]),
        out_shape=jax.ShapeDtypeStruct((P_ROWS * TOK_ROWS, LANES), U32),
        compiler_params=_cparams(("arbitrary",)),
        name="dispatch",
    )(lslot_flat, run_cnt, run_loc, run_glb, zero_start, zero_len, hp)


def _expert_kernel(be_ref, bv_ref, nx_ref, par_ref, xs_ref, wg_hbm, wu_hbm, wd_hbm, ys_ref,
                   wgs_ref, wus_ref, wds_ref, wgb_ref, wub_ref, wdb_ref, sem, *, layer):
    i = pl.program_id(0)
    e = be_ref[i]
    par = par_ref[i]
    changed = jnp.logical_or(i == 0, e != be_ref[jnp.maximum(i - 1, 0)])

    def stage(expert, slot):
        return (pltpu.make_async_copy(wg_hbm.at[layer, expert], wgs_ref, sem.at[0]),
                pltpu.make_async_copy(wu_hbm.at[layer, expert], wus_ref.at[slot], sem.at[1]),
                pltpu.make_async_copy(wd_hbm.at[layer, expert], wds_ref.at[slot], sem.at[2]))

    @pl.when(changed)
    def _():
        @pl.when(i == 0)
        def _():
            for c in stage(e, par):
                c.start()

        for c in stage(e, par):
            c.wait()
        rows = CAST_VREGS * SUBLANES * LANES // EXPERT_FF

        def cast_piece(c, _):
            sl = pl.ds(pl.multiple_of(c * rows, rows), rows)
            wgb_ref[sl, :] = wgs_ref[sl, :].astype(BF16)
            return 0

        lax.fori_loop(0, D_MODEL // rows, cast_piece, 0, unroll=2)

        @pl.when(nx_ref[i] >= 0)
        def _():
            for c in stage(nx_ref[i], 1 - par):
                c.start()

    def cast_up_down():
        for src, dst in ((wus_ref, wub_ref), (wds_ref, wdb_ref)):
            rows = src.shape[1] // CAST_PIECES
            for c in range(CAST_PIECES):
                dst[c * rows:(c + 1) * rows, :] = src[par, c * rows:(c + 1) * rows, :].astype(BF16)

    def sub_block(row0, nrows):
        xs_sub = xs_ref.at[pl.ds(row0 * TOK_ROWS, nrows * TOK_ROWS)]
        lo, hi = _unpack_bf16_pair(_load_token_tiles(xs_sub, nrows))
        lo = lo.astype(BF16)
        hi = hi.astype(BF16)
        hg = (jnp.dot(lo, wgb_ref[0:PACK_W], preferred_element_type=F32)
              + jnp.dot(hi, wgb_ref[PACK_W:D_MODEL], preferred_element_type=F32))
        hu = (jnp.dot(lo, wub_ref[0:PACK_W], preferred_element_type=F32)
              + jnp.dot(hi, wub_ref[PACK_W:D_MODEL], preferred_element_type=F32))
        act = (_silu(hg) * hu).astype(BF16)
        y = jnp.dot(act, wdb_ref[...], preferred_element_type=F32)
        _store_token_tiles(ys_ref, _pack_bf16_pair(y[:, :PACK_W], y[:, PACK_W:]), row0)

    valid_rows = bv_ref[i]
    full = valid_rows > TM_EXP - TM_EXP_TAIL
    unchanged = jnp.logical_not(changed)

    @pl.when(jnp.logical_and(full, changed))
    def _():
        cast_up_down()
        sub_block(0, TM_EXP_SUB)
        sub_block(TM_EXP_SUB, TM_EXP_SUB)

    @pl.when(jnp.logical_and(full, unchanged))
    def _():
        sub_block(0, TM_EXP_SUB)
        sub_block(TM_EXP_SUB, TM_EXP_SUB)

    @pl.when(jnp.logical_not(full))
    def _():
        @pl.when(changed)
        def _():
            cast_up_down()
            sub_block(0, TM_EXP_TAIL)

        for r in range(TM_EXP // TM_EXP_TAIL):
            if r == 0:
                compute = jnp.logical_and(unchanged, valid_rows > 0)
                skip = jnp.logical_and(unchanged, valid_rows <= 0)
            else:
                compute = valid_rows > r * TM_EXP_TAIL
                skip = valid_rows <= r * TM_EXP_TAIL

            @pl.when(compute)
            def _():
                sub_block(r * TM_EXP_TAIL, TM_EXP_TAIL)

            @pl.when(skip)
            def _():
                ys_ref[pl.ds(r * TM_EXP_TAIL * TOK_ROWS, TM_EXP_TAIL * TOK_ROWS), :] = jnp.zeros(
                    (TM_EXP_TAIL * TOK_ROWS, LANES), U32)


def _experts(layer, block_expert, block_rows, next_expert, block_parity, xs, w_gate, w_up, w_down):
    return pl.pallas_call(
        functools.partial(_expert_kernel, layer=layer),
        grid_spec=pltpu.PrefetchScalarGridSpec(
            num_scalar_prefetch=4,
            grid=(NB_EXP,),
            in_specs=[pl.BlockSpec((TM_EXP * TOK_ROWS, LANES), lambda i, be, bv, *_: (jnp.where(bv[i] > 0, i, 0), 0)),
                      pl.BlockSpec(memory_space=pl.ANY),
                      pl.BlockSpec(memory_space=pl.ANY),
                      pl.BlockSpec(memory_space=pl.ANY)],
            out_specs=pl.BlockSpec((TM_EXP * TOK_ROWS, LANES), lambda i, *_: (i, 0)),
            scratch_shapes=[pltpu.VMEM((D_MODEL, EXPERT_FF), F32),
                            pltpu.VMEM((2, D_MODEL, EXPERT_FF), F32),
                            pltpu.VMEM((2, EXPERT_FF, D_MODEL), F32),
                            pltpu.VMEM((D_MODEL, EXPERT_FF), BF16),
                            pltpu.VMEM((D_MODEL, EXPERT_FF), BF16),
                            pltpu.VMEM((EXPERT_FF, D_MODEL), BF16),
                            pltpu.SemaphoreType.DMA((3,))]),
        out_shape=jax.ShapeDtypeStruct((P_ROWS * TOK_ROWS, LANES), U32),
        compiler_params=_cparams(("arbitrary",)),
        name="experts",
    )(block_expert, block_rows, next_expert, block_parity, xs, w_gate, w_up, w_down)


def _combine_kernel(lslot_ref, rcnt_ref, rloc_ref, rglb_ref, w_ref, ys_ref, hp_ref, x1_ref, gf_ref,
                    wsg_ref, wsu_ref, wsd_ref, fn_ref, o_ref, buf_ref, mlo_ref, mhi_ref, sem, *, final_norm):
    i = pl.program_id(0)
    nsteps = pl.num_programs(0)

    def issue(step, slot):
        def run(e, _):
            n = rcnt_ref[step * N_EXPERTS + e]

            @pl.when(n > 0)
            def _():
                src = pl.multiple_of(rglb_ref[step * N_EXPERTS + e] * TOK_ROWS, TOK_ROWS)
                dst = pl.multiple_of(rloc_ref[step * N_EXPERTS + e], TOK_ROWS)
                pltpu.make_async_copy(ys_ref.at[pl.ds(src, n * TOK_ROWS)],
                                      buf_ref.at[pl.ds(dst, n * TOK_ROWS)], sem.at[slot]).start()
            return 0

        lax.fori_loop(0, N_EXPERTS, run, 0)

    @pl.when(i == 0)
    def _():
        issue(0, 0)

    @pl.when(i + 1 < nsteps)
    def _():
        issue(i + 1, (i + 1) % 2)

    cur = i % 2
    cur_half = pl.ds(pl.multiple_of(cur * LOCAL_ROWS, LOCAL_ROWS), LOCAL_ROWS)
    pltpu.make_async_copy(ys_ref.at[pl.ds(0, LOCAL_ROWS)], buf_ref.at[cur_half], sem.at[cur]).wait()

    base = i * T_DISP

    def token(t, _):
        acc_lo = jnp.zeros((TOK_ROWS, LANES), F32)
        acc_hi = jnp.zeros((TOK_ROWS, LANES), F32)
        for k in range(TOP_K):
            idx = (base + t) * TOP_K + k
            src = pl.multiple_of(lslot_ref[idx], TOK_ROWS)
            lo, hi = _unpack_bf16_pair(buf_ref[pl.ds(src, TOK_ROWS), :])
            wk = w_ref[idx]
            acc_lo = acc_lo + wk * lo
            acc_hi = acc_hi + wk * hi
        dst = pl.ds(pl.multiple_of(t * TOK_ROWS, TOK_ROWS), TOK_ROWS)
        mlo_ref[dst, :] = acc_lo
        mhi_ref[dst, :] = acc_hi
        return 0

    lax.fori_loop(0, T_DISP, token, 0, unroll=2)
    moe_lo = _load_token_tiles(mlo_ref, T_DISP)
    moe_hi = _load_token_tiles(mhi_ref, T_DISP)

    hlo, hhi = _unpack_bf16_pair(_load_token_tiles(hp_ref, T_DISP))
    hlo = hlo.astype(BF16)
    hhi = hhi.astype(BF16)
    sg = (jnp.dot(hlo, wsg_ref[0:PACK_W], preferred_element_type=F32)
          + jnp.dot(hhi, wsg_ref[PACK_W:D_MODEL], preferred_element_type=F32))
    su = (jnp.dot(hlo, wsu_ref[0:PACK_W], preferred_element_type=F32)
          + jnp.dot(hhi, wsu_ref[PACK_W:D_MODEL], preferred_element_type=F32))
    shared = jnp.dot((_silu(sg) * su).astype(BF16), wsd_ref[...], preferred_element_type=F32)
    moe = jnp.concatenate([moe_lo, moe_hi], axis=1)
    out = x1_ref[...] + gf_ref[0] * (moe + shared)
    if final_norm:
        out = _rms(out, fn_ref[...])
    o_ref[...] = out


def _combine(lslot_flat, run_cnt, run_loc, run_glb, w_flat, ys, hp, x1, g_f, wsg, wsu, wsd, final_gain, final_norm):
    per_b = SEQ // T_DISP
    return pl.pallas_call(
        functools.partial(_combine_kernel, final_norm=final_norm),
        grid_spec=pltpu.PrefetchScalarGridSpec(
            num_scalar_prefetch=4,
            grid=(N_TILES,),
            in_specs=[pl.BlockSpec(memory_space=pltpu.SMEM),
                      pl.BlockSpec(memory_space=pl.ANY),
                      pl.BlockSpec((T_DISP * TOK_ROWS, LANES), lambda i, *_: (i, 0)),
                      pl.BlockSpec((T_DISP, D_MODEL), lambda i, *_: (i, 0)),
                      pl.BlockSpec((1, 1, D_MODEL), lambda i, *_: (i // per_b, 0, 0)),
                      pl.BlockSpec((D_MODEL, SHARED_FF), lambda i, *_: (0, 0)),
                      pl.BlockSpec((D_MODEL, SHARED_FF), lambda i, *_: (0, 0)),
                      pl.BlockSpec((SHARED_FF, D_MODEL), lambda i, *_: (0, 0)),
                      pl.BlockSpec((1, D_MODEL), lambda i, *_: (0, 0))],
            out_specs=pl.BlockSpec((T_DISP, D_MODEL), lambda i, *_: (i, 0)),
            scratch_shapes=[pltpu.VMEM((2 * LOCAL_ROWS, LANES), U32),
                            pltpu.VMEM((T_DISP * TOK_ROWS, LANES), F32),
                            pltpu.VMEM((T_DISP * TOK_ROWS, LANES), F32),
                            pltpu.SemaphoreType.DMA((2,))]),
        out_shape=jax.ShapeDtypeStruct((N_TOK, D_MODEL), F32),
        compiler_params=_cparams(("arbitrary",)),
        name="combine",
    )(lslot_flat, run_cnt, run_loc, run_glb, w_flat, ys, hp, x1, g_f, wsg, wsu, wsd, final_gain.reshape(1, D_MODEL))


def _moe_layer(layer, hp, logits, x1, g_f, router_bias, w_gate, w_up, w_down, ws_gate, ws_up, ws_down,
               final_gain, final_norm):
    lslot, w_k, tables, counts = _route(logits, router_bias)
    cnt = counts[:, 0].astype(I32)
    padded = ((cnt + TM_EXP - 1) // TM_EXP) * TM_EXP
    ends = jnp.cumsum(padded)
    offsets = ends - padded
    tables = tables[:, :, :, 0].astype(I32)
    run_cnt = tables[:, 0].reshape(-1)
    run_glb = (offsets[None, :] + tables[:, 1]).reshape(-1)
    half_row0 = (jnp.arange(N_TILES, dtype=I32) % 2) * LOCAL_ROWS
    run_loc = (tables[:, 2] * TOK_ROWS + half_row0[:, None]).reshape(-1)
    lslot_flat = (lslot * TOK_ROWS + jnp.repeat(half_row0, T_DISP)[None, :]).T.reshape(N_SLOTS)
    blk_start = jnp.arange(NB_EXP, dtype=I32) * TM_EXP
    expert_ids = jnp.arange(N_EXPERTS, dtype=I32)
    nonempty = cnt > 0
    last_nonempty = jnp.max(jnp.where(nonempty, expert_ids, 0))
    block_expert = jnp.minimum(jnp.sum((blk_start[:, None] >= ends[None, :]).astype(I32), axis=1), last_nonempty)
    block_valid = (blk_start < ends[-1]).astype(I32)
    block_rows = jnp.clip((offsets + cnt)[block_expert] - blk_start, 0, TM_EXP) * block_valid
    following = jnp.where(nonempty, expert_ids, N_EXPERTS)
    following = lax.cummin(following, reverse=True)
    following = jnp.concatenate([following[1:], jnp.full((1,), N_EXPERTS, I32)])
    next_expert = jnp.where(following < N_EXPERTS, following, -1)[block_expert]
    zero_start = jnp.concatenate([offsets + cnt, blk_start])
    zero_len = jnp.concatenate([padded - cnt, (1 - block_valid) * TM_EXP])
    xs = _dispatch(lslot_flat, run_cnt, run_loc, run_glb, zero_start, zero_len, hp)
    block_parity = ((jnp.cumsum(nonempty.astype(I32)) - 1) % 2)[block_expert]
    ys = _experts(layer, block_expert, block_rows, next_expert, block_parity, xs, w_gate, w_up, w_down)
    return _combine(lslot_flat, run_cnt, run_loc, run_glb, w_k.T.reshape(N_SLOTS), ys, hp, x1, g_f,
                    ws_gate.astype(BF16), ws_up.astype(BF16), ws_down.astype(BF16), final_gain, final_norm)


def _rope_tables(dim):
    inv = ROPE_THETA ** (-jnp.arange(0, dim, 2, dtype=F32) / dim)
    ang = jnp.arange(SEQ, dtype=F32)[:, None] * inv[None, :]
    return jnp.cos(ang), jnp.sin(ang)


def _rot_half_cols(w):
    half = w.shape[-1] // 2
    return jnp.concatenate([-w[..., half:], w[..., :half]], axis=-1)


def _t5_bucket(rel):
    half = REL_BUCKETS // 2
    max_exact = half // 2
    ret = (rel > 0).astype(I32) * half
    n = jnp.abs(rel)
    nf = jnp.maximum(n, 1).astype(F32)
    large = max_exact + (jnp.log(nf / max_exact) / math.log(REL_MAX_DIST / max_exact)
                         * (half - max_exact)).astype(I32)
    large = jnp.minimum(large, half - 1)
    return ret + jnp.where(n < max_exact, n, large)


def _swa_bias_table(rel_bias):
    qi = jnp.arange(WINDOW)[:, None]
    kj = jnp.arange(3 * WINDOW)[None, :]
    rel = kj - WINDOW - qi
    onehot = (_t5_bucket(rel)[:, :, None] == jnp.arange(REL_BUCKETS)).astype(F32)
    bias = jnp.einsum('qjb,bh->hqj', onehot, rel_bias.astype(F32), precision=lax.Precision.HIGHEST)
    return jnp.where((jnp.abs(rel) <= WINDOW)[None], bias, NEG_BIG)


def kernel(x, c, w_mod, b_mod, norm_mix, norm_ffn, final_norm, w_in_ab, q_lat_norm, kv_lat_norm, w_uq, w_ukv, conv_w, conv_b, lru_w_a, lru_b_a, lru_w_x, lru_b_x, lru_lambda, w_out_ab, w_in_cd, ret_gn, swa_sinks, w_out_cd, rel_bias, w_router, router_bias, w_gate, w_up, w_down, ws_gate, ws_up, ws_down):
    xf = x.reshape(N_TOK, D_MODEL)
    mod = _modulation(c, w_mod, b_mod)
    cos_r, sin_r = _rope_tables(MLA_ROPE)
    cs_tab = jnp.concatenate([cos_r, cos_r, sin_r, sin_r], axis=1)
    cos_t, sin_t = _rope_tables(RET_DK)
    lg_ret = jnp.log1p(-(2.0 ** (-5.0 - jnp.arange(RET_HEADS, dtype=F32))))

    for layer in range(DEPTH):
        sh_m, sc_m, g_m, sh_f, sc_f, g_f = [m.reshape(BATCH, 1, D_MODEL) for m in jnp.split(mod[layer], 6, axis=-1)]
        i = layer // 2
        if layer % 2 == 0:
            w = w_in_ab[i]
            o1, o2, o3, o4 = np.cumsum((MLA_Q_RANK, MLA_KV_RANK, MLA_ROPE, LRU_WIDTH)).tolist()
            w_kr = w[:, o2:o3]
            w_in = jnp.concatenate([w[:, :o2], w[:, o3:], w_kr, _rot_half_cols(w_kr)], axis=1).astype(BF16)
            p0 = _in_projection(xf, norm_mix[layer], sc_m, sh_m, w_in, "in_proj_ab")
            wq = w_uq[i].reshape(MLA_Q_RANK, MLA_HEADS, MLA_NOPE + MLA_ROPE)
            wq_r = wq[:, :, MLA_NOPE:]
            wq = jnp.concatenate([wq, _rot_half_cols(wq_r)], axis=-1).reshape(MLA_Q_RANK, MLA_HEADS * MLA_QK)
            q, k, v = _mla_up(p0, q_lat_norm[i], kv_lat_norm[i], wq.astype(BF16), w_ukv[i].astype(BF16), cs_tab)
            a_out = _mla_attention(q, k, v)
            w_gates = jnp.concatenate([lru_w_a[i, 0], lru_w_x[i, 0], lru_w_a[i, 1], lru_w_x[i, 1]], axis=-1).astype(BF16)
            b_gates = jnp.concatenate([b.reshape(LRU_BLOCKS, 1, LRU_BS) for b in
                                       (lru_b_a[i, 0], lru_b_x[i, 0], lru_b_a[i, 1], lru_b_x[i, 1])], axis=-1)
            b_out = _rglru(p0, conv_w[i], conv_b[i], w_gates, b_gates, lru_lambda[i])
            w_out = w_out_ab[i].astype(BF16)
        else:
            p1 = _in_projection(xf, norm_mix[layer], sc_m, sh_m, w_in_cd[i].astype(BF16), "in_proj_cd")
            a_out = _retention(p1, lg_ret, cos_t, sin_t, ret_gn[i])
            b_out = _swa(p1, swa_sinks[i], _swa_bias_table(rel_bias))
            w_out = w_out_cd[i].astype(BF16)
        w_r = jnp.pad(w_router[layer], ((0, 0), (0, LANES - N_EXPERTS)))
        w_r_hi = w_r.astype(BF16)
        w_router_pad = jnp.concatenate([w_r_hi, (w_r - w_r_hi.astype(F32)).astype(BF16)], axis=1)
        x1, hp, logits = _out_projection(a_out, b_out, w_out, xf, g_m, norm_ffn[layer], sc_f, sh_f, w_router_pad)
        xf = _moe_layer(layer, hp, logits, x1, g_f, router_bias[layer], w_gate, w_up, w_down,
                        ws_gate[layer], ws_up[layer], ws_down[layer], final_norm, layer == DEPTH - 1)
    return xf.reshape(BATCH, SEQ, D_MODEL)
```

```python
import functools
import math

import numpy as np
import jax
import jax.numpy as jnp
from jax import lax
from jax.experimental import pallas as pl
from jax.experimental.pallas import tpu as pltpu

F32 = jnp.float32
BF16 = jnp.bfloat16
I32 = jnp.int32
U32 = jnp.uint32

D_MODEL = 2048
BATCH = 4
SEQ = 2048
DEPTH = 2
N_TOK = BATCH * SEQ
HALF = D_MODEL // 2
MLA_NOPE = 128
MLA_ROPE = 64
MLA_V = 128
MLA_HEADS = HALF // MLA_V
MLA_Q_RANK = D_MODEL // 4
MLA_KV_RANK = D_MODEL // 4
MLA_QK = 256
LRU_WIDTH = HALF
LRU_BLOCKS = 8
LRU_BS = LRU_WIDTH // LRU_BLOCKS
LRU_CONV = 4
LRU_C = 8.0
RET_DK = 256
RET_DV = 256
RET_HEADS = HALF // RET_DV
SWA_HD = 128
SWA_HEADS = HALF // SWA_HD
SWA_KV_HEADS = 2
SWA_G = SWA_HEADS // SWA_KV_HEADS
WINDOW = 128
REL_BUCKETS = 32
REL_MAX_DIST = 128
N_EXPERTS = 64
TOP_K = 8
N_GROUPS = 8
GROUP_SIZE = N_EXPERTS // N_GROUPS
TOP_GROUPS = 4
EXPERT_FF = D_MODEL // 4
SHARED_FF = D_MODEL // 4
ROUTE_SCALE = 2.5
ROPE_THETA = 10000.0
EPS = 1e-6
NEG_BIG = -1e30

LANES = 128
SUBLANES = 8
VMEM_LIMIT = 52 * 2**20

TM_PROJ = 256
TM_PROJ_SUB = 128
TM_UP = 512
TQ_ATT = 1024
TQ_MLA = 2048
TQ_SUB = 256
SWA_BLOCKS = 4
TM_OUT = 512
TM_OUT_SUB = 256
T_DISP = 256
T_ROUTE = T_DISP
N_TILES = N_TOK // T_DISP
TILE_ROWS = T_DISP * TOP_K
TM_EXP = 512
TM_EXP_SUB = 256
TM_EXP_TAIL = 128
assert TM_EXP == 2 * TM_EXP_SUB and TM_EXP % TM_EXP_TAIL == 0
CAST_VREGS = 32
CAST_PIECES = 16
N_SLOTS = N_TOK * TOP_K
NB_EXP = N_SLOTS // TM_EXP + N_EXPERTS
P_ROWS = NB_EXP * TM_EXP
N_ZERO_RANGES = N_EXPERTS + NB_EXP
PACK_W = D_MODEL // 2
TOK_ROWS = PACK_W // LANES
assert TOK_ROWS == SUBLANES
LOCAL_ROWS = TILE_ROWS * TOK_ROWS

LRU_SEG = 260
LRU_ROWS = SUBLANES * LRU_SEG
assert LRU_ROWS >= SEQ and LRU_SEG % 8 == 4


def _cparams(sem, vmem=VMEM_LIMIT):
    return pltpu.CompilerParams(dimension_semantics=sem, vmem_limit_bytes=vmem)


def _sigmoid(x):
    return 0.5 * jnp.tanh(0.5 * x) + 0.5


def _silu(x):
    return x * _sigmoid(x)


def _rms(x, g):
    return x * lax.rsqrt(jnp.mean(x * x, axis=-1, keepdims=True) + EPS) * g


def _pack_bf16_pair(lo, hi):
    lo_b = lax.bitcast_convert_type(lo.astype(BF16).astype(F32), U32)
    hi_b = lax.bitcast_convert_type(hi.astype(BF16).astype(F32), U32)
    return (hi_b & jnp.uint32(0xFFFF0000)) | (lo_b >> 16)


def _unpack_bf16_pair(w):
    lo = lax.bitcast_convert_type(w << 16, F32)
    hi = lax.bitcast_convert_type(w & jnp.uint32(0xFFFF0000), F32)
    return lo, hi


def _store_token_tiles(ref, packed, tok0=0):
    t = packed.shape[0]
    for s in range(TOK_ROWS):
        ref[pl.ds(tok0 * TOK_ROWS + s, t, stride=TOK_ROWS), :] = packed[:, s * LANES:(s + 1) * LANES]


def _load_token_tiles(ref, t):
    return jnp.concatenate([ref[pl.ds(s, t, stride=TOK_ROWS), :] for s in range(TOK_ROWS)], axis=1)


def _mod_kernel(c_ref, w_ref, b_ref, o_ref):
    c = c_ref[...]
    ca = _silu(c).astype(BF16)
    o_ref[0] = jnp.dot(ca, w_ref[0].astype(BF16), preferred_element_type=F32) + b_ref[0]


def _modulation(c, w_mod, b_mod):
    tn = 1024
    cp = jnp.pad(c, ((0, SUBLANES - BATCH), (0, 0)))
    out = pl.pallas_call(
        _mod_kernel,
        grid=(DEPTH, 6 * D_MODEL // tn),
        in_specs=[pl.BlockSpec((SUBLANES, D_MODEL), lambda l, j: (0, 0)),
                  pl.BlockSpec((1, D_MODEL, tn), lambda l, j: (l, 0, j)),
                  pl.BlockSpec((1, 1, tn), lambda l, j: (l, 0, j))],
        out_specs=pl.BlockSpec((1, SUBLANES, tn), lambda l, j: (l, 0, j)),
        out_shape=jax.ShapeDtypeStruct((DEPTH, SUBLANES, 6 * D_MODEL), F32),
        compiler_params=_cparams(("parallel", "parallel")),
        name="adaln_mod",
    )(cp, w_mod, b_mod.reshape(DEPTH, 1, 6 * D_MODEL))
    return out[:, :BATCH]


def _inproj_kernel(x_ref, g_ref, sc_ref, sh_ref, w_ref, o_ref):
    for r in range(TM_PROJ // TM_PROJ_SUB):
        rows = pl.ds(r * TM_PROJ_SUB, TM_PROJ_SUB)
        y = _rms(x_ref[rows, :], g_ref[...])
        h = (y * (1.0 + sc_ref[0]) + sh_ref[0]).astype(BF16)
        o_ref[rows, :] = jnp.dot(h, w_ref[...], preferred_element_type=F32)


def _in_projection(x, gain, scale, shift, w_bf16, name):
    p = w_bf16.shape[1]
    per_b = SEQ // TM_PROJ
    return pl.pallas_call(
        _inproj_kernel,
        grid=(N_TOK // TM_PROJ,),
        in_specs=[pl.BlockSpec((TM_PROJ, D_MODEL), lambda i: (i, 0)),
                  pl.BlockSpec((1, D_MODEL), lambda i: (0, 0)),
                  pl.BlockSpec((1, 1, D_MODEL), lambda i: (i // per_b, 0, 0)),
                  pl.BlockSpec((1, 1, D_MODEL), lambda i: (i // per_b, 0, 0)),
                  pl.BlockSpec((D_MODEL, p), lambda i: (0, 0), pipeline_mode=pl.Buffered(1))],
        out_specs=pl.BlockSpec((TM_PROJ, p), lambda i: (i, 0)),
        out_shape=jax.ShapeDtypeStruct((N_TOK, p), F32),
        compiler_params=_cparams(("parallel",), 56 * 2**20),
        name=name,
    )(x, gain.reshape(1, D_MODEL), scale, shift, w_bf16)


def _mla_up_kernel(ql_ref, kvl_ref, kr_ref, qn_ref, kvn_ref, wq_ref, wkv_ref, cs_ref, q_ref, k_ref, v_ref):
    scale = (MLA_NOPE + MLA_ROPE) ** -0.5 * math.log2(math.e)
    hq = _rms(ql_ref[...], qn_ref[...]).astype(BF16)
    hkv = _rms(kvl_ref[...], kvn_ref[...]).astype(BF16)
    yq = jnp.dot(hq, wq_ref[...], preferred_element_type=F32) * scale
    ykv = jnp.dot(hkv, wkv_ref[...], preferred_element_type=F32)
    cs = cs_ref[...]
    lane = lax.broadcasted_iota(I32, cs.shape, 1)

    def rope_sum(blk):
        z = blk * cs
        return z + pltpu.roll(z, MLA_ROPE, 1)

    kr = jnp.where(lane < MLA_ROPE, rope_sum(kr_ref[...]), 0.0).astype(BF16)
    ones_col = jnp.where(lane == 0, 1.0, 0.0).astype(BF16)
    for h in range(MLA_HEADS):
        c0 = h * MLA_QK
        q_ref[0, h, :, 0:MLA_NOPE] = yq[:, c0:c0 + MLA_NOPE].astype(BF16)
        q_ref[0, h, :, MLA_NOPE:MLA_QK] = rope_sum(yq[:, c0 + MLA_NOPE:c0 + MLA_QK]).astype(BF16)
        k_ref[0, h, :, 0:MLA_NOPE] = ykv[:, c0:c0 + MLA_NOPE].astype(BF16)
        k_ref[0, h, :, MLA_NOPE:MLA_QK] = kr
        v_ref[0, h, :, 0:MLA_V] = ykv[:, c0 + MLA_NOPE:c0 + MLA_QK].astype(BF16)
        v_ref[0, h, :, MLA_V:2 * MLA_V] = ones_col


def _mla_up(p0, q_norm, kv_norm, wq, wkv, cs_tab):
    per_b = SEQ // TM_UP
    qk_shape = jax.ShapeDtypeStruct((BATCH, MLA_HEADS, SEQ, MLA_QK), BF16)
    return pl.pallas_call(
        _mla_up_kernel,
        grid=(N_TOK // TM_UP,),
        in_specs=[pl.BlockSpec((TM_UP, MLA_Q_RANK), lambda i: (i, 0)),
                  pl.BlockSpec((TM_UP, MLA_KV_RANK), lambda i: (i, 1)),
                  pl.BlockSpec((TM_UP, LANES), lambda i: (i, 24)),
                  pl.BlockSpec((1, MLA_Q_RANK), lambda i: (0, 0)),
                  pl.BlockSpec((1, MLA_KV_RANK), lambda i: (0, 0)),
                  pl.BlockSpec((MLA_Q_RANK, MLA_HEADS * MLA_QK), lambda i: (0, 0)),
                  pl.BlockSpec((MLA_KV_RANK, MLA_HEADS * MLA_QK), lambda i: (0, 0)),
                  pl.BlockSpec((TM_UP, LANES), lambda i: (i % per_b, 0))],
        out_specs=[pl.BlockSpec((1, MLA_HEADS, TM_UP, MLA_QK), lambda i: (i // per_b, 0, i % per_b, 0)),
                   pl.BlockSpec((1, MLA_HEADS, TM_UP, MLA_QK), lambda i: (i // per_b, 0, i % per_b, 0)),
                   pl.BlockSpec((1, MLA_HEADS, TM_UP, 2 * MLA_V), lambda i: (i // per_b, 0, i % per_b, 0))],
        out_shape=[qk_shape, qk_shape, jax.ShapeDtypeStruct((BATCH, MLA_HEADS, SEQ, 2 * MLA_V), BF16)],
        compiler_params=_cparams(("parallel",)),
        name="mla_up",
    )(p0, p0, p0, q_norm.reshape(1, -1), kv_norm.reshape(1, -1), wq, wkv, cs_tab)


def _mla_attn_kernel(q_ref, k_ref, v_ref, o_ref):
    k = k_ref[0, 0]
    v = v_ref[0, 0]
    for r in range(TQ_MLA // TQ_SUB):
        rows = pl.ds(r * TQ_SUB, TQ_SUB)
        s = lax.dot_general(q_ref[0, 0, rows, :], k, (((1,), (1,)), ((), ())), preferred_element_type=F32)
        p = jnp.exp2(s - jnp.max(s, axis=-1, keepdims=True))
        o = jnp.dot(p.astype(BF16), v, preferred_element_type=F32)
        o_ref[rows, :] = (o[:, 0:MLA_V] / o[:, MLA_V:MLA_V + 1]).astype(BF16)


def _mla_attention(q, k, v):
    nq = SEQ // TQ_MLA
    return pl.pallas_call(
        _mla_attn_kernel,
        grid=(BATCH, MLA_HEADS, nq),
        in_specs=[pl.BlockSpec((1, 1, TQ_MLA, MLA_QK), lambda b, h, i: (b, h, i, 0)),
                  pl.BlockSpec((1, 1, SEQ, MLA_QK), lambda b, h, i: (b, h, 0, 0)),
                  pl.BlockSpec((1, 1, SEQ, 2 * MLA_V), lambda b, h, i: (b, h, 0, 0))],
        out_specs=pl.BlockSpec((TQ_MLA, MLA_V), lambda b, h, i: (b * nq + i, h)),
        out_shape=jax.ShapeDtypeStruct((N_TOK, HALF), BF16),
        compiler_params=_cparams(("parallel", "parallel", "parallel")),
        name="mla_attn",
    )(q, k, v)


def _lru_kernel(x_ref, gate_ref, cw_ref, cb_ref, wg_ref, bg_ref, lam_ref, o_ref,
                af_ref, uf_ref, ab_ref, ub_ref, hf_ref, pf_ref, hb_ref, pb_ref, hs_ref):
    x = x_ref[...]
    row = lax.broadcasted_iota(I32, x.shape, 0)

    def shifted(d):
        r = pltpu.roll(x, (-d) % SEQ, 0)
        return jnp.where((row + d >= 0) & (row + d < SEQ), r, 0.0)

    cw = cw_ref[...]
    left = LRU_CONV // 2
    xc = cb_ref[...]
    for kk in range(LRU_CONV):
        d = kk - left
        xc = xc + cw[kk:kk + 1] * (x if d == 0 else shifted(d))

    gates = jnp.dot(xc.astype(BF16), wg_ref[0], preferred_element_type=F32) + bg_ref[0]
    lam = lam_ref[...]
    z = -lam
    sp = jnp.maximum(z, 0.0) + jnp.log1p(jnp.exp(-jnp.abs(z)))
    pad_rows = LRU_ROWS - SEQ
    for d, (a_ref, u_ref) in enumerate(((af_ref, uf_ref), (ab_ref, ub_ref))):
        r = _sigmoid(gates[:, d * 256:d * 256 + LRU_BS])
        i = _sigmoid(gates[:, d * 256 + LRU_BS:(d + 1) * 256])
        a = jnp.exp(r * (-LRU_C * sp[d:d + 1]))
        a_ref[0:SEQ] = a
        u_ref[0:SEQ] = jnp.sqrt(1.0 - a * a) * (i * xc)
        a_ref[SEQ:LRU_ROWS] = jnp.zeros((pad_rows, LANES), F32)
        u_ref[SEQ:LRU_ROWS] = jnp.zeros((pad_rows, LANES), F32)

    ones = jnp.ones((SUBLANES, LANES), F32)
    zeros = jnp.zeros((SUBLANES, LANES), F32)

    def seg(t):
        return pl.ds(t, SUBLANES, stride=LRU_SEG)

    def local_scan(s, carry):
        p_f, h_f, p_b, h_b = carry
        tf = s
        tb = LRU_SEG - 1 - s
        a = af_ref[seg(tf)]
        h_f = a * h_f + uf_ref[seg(tf)]
        p_f = a * p_f
        hf_ref[seg(tf)] = h_f
        pf_ref[seg(tf)] = p_f
        a = ab_ref[seg(tb)]
        h_b = a * h_b + ub_ref[seg(tb)]
        p_b = a * p_b
        hb_ref[seg(tb)] = h_b
        pb_ref[seg(tb)] = p_b
        return p_f, h_f, p_b, h_b

    p_f, h_f, p_b, h_b = lax.fori_loop(0, LRU_SEG, local_scan, (ones, zeros, ones, zeros), unroll=4)

    rows_f = []
    c = jnp.zeros((1, LANES), F32)
    for j in range(SUBLANES):
        rows_f.append(c)
        c = p_f[j:j + 1] * c + h_f[j:j + 1]
    rows_b = [None] * SUBLANES
    c = jnp.zeros((1, LANES), F32)
    for j in range(SUBLANES - 1, -1, -1):
        rows_b[j] = c
        c = p_b[j:j + 1] * c + h_b[j:j + 1]
    sub = lax.broadcasted_iota(I32, (SUBLANES, LANES), 0)
    c_f = zeros
    c_b = zeros
    for j in range(SUBLANES):
        c_f = jnp.where(sub == j, rows_f[j], c_f)
        c_b = jnp.where(sub == j, rows_b[j], c_b)

    def fixup(t, _):
        hs_ref[seg(t)] = (hf_ref[seg(t)] + pf_ref[seg(t)] * c_f) + (hb_ref[seg(t)] + pb_ref[seg(t)] * c_b)
        return 0

    lax.fori_loop(0, LRU_SEG, fixup, 0, unroll=4)

    g = gate_ref[...]
    gelu = 0.5 * g * (1.0 + jnp.tanh(math.sqrt(2.0 / math.pi) * (g + 0.044715 * (g * g * g))))
    o_ref[...] = (gelu * hs_ref[0:SEQ]).astype(BF16)


def _rglru(p0, conv_w, conv_b, w_gates, b_gates, lam):
    scan_buf = pltpu.VMEM((LRU_ROWS, LANES), F32)
    return pl.pallas_call(
        _lru_kernel,
        grid=(BATCH, LRU_BLOCKS),
        in_specs=[pl.BlockSpec((SEQ, LRU_BS), lambda b, g: (b, 8 + g)),
                  pl.BlockSpec((SEQ, LRU_BS), lambda b, g: (b, 16 + g)),
                  pl.BlockSpec((LRU_CONV, LRU_BS), lambda b, g: (0, g)),
                  pl.BlockSpec((1, LRU_BS), lambda b, g: (0, g)),
                  pl.BlockSpec((1, LRU_BS, 4 * LRU_BS), lambda b, g: (g, 0, 0)),
                  pl.BlockSpec((1, 1, 4 * LRU_BS), lambda b, g: (g, 0, 0)),
                  pl.BlockSpec((2, LRU_BS), lambda b, g: (0, g))],
        out_specs=pl.BlockSpec((SEQ, LRU_BS), lambda b, g: (b, g)),
        out_shape=jax.ShapeDtypeStruct((N_TOK, LRU_WIDTH), BF16),
        scratch_shapes=[scan_buf] * 9,
        compiler_params=_cparams(("parallel", "parallel")),
        name="rglru",
    )(p0, p0, conv_w, conv_b.reshape(1, -1), w_gates, b_gates, lam)


def _ret_kernel(lg_ref, q_ref, k_ref, v_ref, g_ref, cq_ref, sq_ref, ck_ref, sk_ref, gn_ref, o_ref, ks_ref, vs_ref):
    h = pl.program_id(1)
    qi = pl.program_id(2)
    half = RET_DK // 2

    def rope(t, c, s):
        t1, t2 = t[:, :half], t[:, half:]
        return jnp.concatenate([t1 * c - t2 * s, t2 * c + t1 * s], axis=1)

    @pl.when(qi == 0)
    def _():
        ks_ref[...] = (rope(k_ref[...], ck_ref[...], sk_ref[...]) * (RET_DK ** -0.5)).astype(BF16)
        vs_ref[...] = v_ref[...].astype(BF16)

    q = rope(q_ref[...], cq_ref[...], sq_ref[...]).astype(BF16)
    s = lax.dot_general(q, ks_ref[...], (((1,), (1,)), ((), ())), preferred_element_type=F32)
    n = (qi * TQ_ATT + lax.broadcasted_iota(I32, (TQ_ATT, 1), 0)).astype(F32)
    m = lax.broadcasted_iota(I32, (1, SEQ), 1).astype(F32)
    c_f = lg_ref[h] * math.log2(math.e)
    c_b = -lg_ref[RET_HEADS - 1 - h] * math.log2(math.e)
    dec = jnp.exp2(jnp.minimum(c_f * n - c_f * m, c_b * n - c_b * m))
    o = jnp.dot((s * dec).astype(BF16), vs_ref[...], preferred_element_type=F32)
    y = _rms(o, gn_ref[0])
    o_ref[...] = (_silu(g_ref[...]) * y).astype(BF16)


def _retention(p1, lg, cos_t, sin_t, ret_gn):
    nq = SEQ // TQ_ATT
    half = RET_DK // 2
    return pl.pallas_call(
        _ret_kernel,
        grid=(BATCH, RET_HEADS, nq),
        in_specs=[pl.BlockSpec(memory_space=pltpu.SMEM),
                  pl.BlockSpec((TQ_ATT, RET_DK), lambda b, h, i: (b * nq + i, h)),
                  pl.BlockSpec((SEQ, RET_DK), lambda b, h, i: (b, RET_HEADS + h)),
                  pl.BlockSpec((SEQ, RET_DV), lambda b, h, i: (b, 2 * RET_HEADS + h)),
                  pl.BlockSpec((TQ_ATT, RET_DV), lambda b, h, i: (b * nq + i, 3 * RET_HEADS + h)),
                  pl.BlockSpec((TQ_ATT, half), lambda b, h, i: (i, 0)),
                  pl.BlockSpec((TQ_ATT, half), lambda b, h, i: (i, 0)),
                  pl.BlockSpec((SEQ, half), lambda b, h, i: (0, 0)),
                  pl.BlockSpec((SEQ, half), lambda b, h, i: (0, 0)),
                  pl.BlockSpec((1, 1, RET_DV), lambda b, h, i: (h, 0, 0))],
        out_specs=pl.BlockSpec((TQ_ATT, RET_DV), lambda b, h, i: (b * nq + i, h)),
        out_shape=jax.ShapeDtypeStruct((N_TOK, HALF), BF16),
        scratch_shapes=[pltpu.VMEM((SEQ, RET_DK), BF16), pltpu.VMEM((SEQ, RET_DV), BF16)],
        compiler_params=_cparams(("parallel", "parallel", "arbitrary")),
        name="retention",
    )(lg, p1, p1, p1, p1, cos_t, sin_t, cos_t, sin_t, ret_gn.reshape(RET_HEADS, 1, RET_DV))


def _swa_kernel(sink_ref, q_ref, k_ref, v_ref, bias_ref, o_ref):
    kv = pl.program_id(1)
    nb = SEQ // WINDOW
    w = WINDOW

    def rows(ref, blk):
        return ref[pl.ds(pl.multiple_of(blk * w, w), w), :].astype(BF16)

    col = lax.broadcasted_iota(I32, (w, 3 * w), 1)
    for j in range(SWA_BLOCKS):
        n = pl.program_id(2) * SWA_BLOCKS + j
        prev = jnp.maximum(n - 1, 0)
        nxt = jnp.minimum(n + 1, nb - 1)
        kw = jnp.concatenate([rows(k_ref, prev), rows(k_ref, n), rows(k_ref, nxt)], axis=0)
        vw = jnp.concatenate([rows(v_ref, prev), rows(v_ref, n), rows(v_ref, nxt)], axis=0)
        qb = q_ref[j * w:(j + 1) * w, :]
        q4 = jnp.concatenate([qb[:, g * SWA_HD:(g + 1) * SWA_HD] for g in range(SWA_G)], axis=0).astype(BF16)
        s = lax.dot_general(q4, kw, (((1,), (1,)), ((), ())), preferred_element_type=F32) * (SWA_HD ** -0.5)
        outside = ((col < w) & (n == 0)) | ((col >= 2 * w) & (n == nb - 1))
        for g in range(SWA_G):
            sg = jnp.where(outside, NEG_BIG, s[g * w:(g + 1) * w] + bias_ref[g])
            sink = sink_ref[kv * SWA_G + g]
            m = jnp.maximum(jnp.max(sg, axis=-1, keepdims=True), sink)
            p = jnp.exp(sg - m)
            denom = jnp.sum(p, axis=-1, keepdims=True) + jnp.exp(sink - m)
            o = jnp.dot((p / denom).astype(BF16), vw, preferred_element_type=F32)
            o_ref[j * w:(j + 1) * w, g * SWA_HD:(g + 1) * SWA_HD] = o.astype(BF16)


def _swa(p1, sinks, bias):
    nb = SEQ // WINDOW
    qcols = SWA_G * SWA_HD
    q_blk0 = (4 * RET_HEADS * RET_DK) // qcols
    k_blk0 = (4 * RET_HEADS * RET_DK + SWA_HEADS * SWA_HD) // SWA_HD
    v_blk0 = k_blk0 + SWA_KV_HEADS
    return pl.pallas_call(
        _swa_kernel,
        grid=(BATCH, SWA_KV_HEADS, nb // SWA_BLOCKS),
        in_specs=[pl.BlockSpec(memory_space=pltpu.SMEM),
                  pl.BlockSpec((SWA_BLOCKS * WINDOW, qcols), lambda b, kv, n: (b * (nb // SWA_BLOCKS) + n, q_blk0 + kv)),
                  pl.BlockSpec((SEQ, SWA_HD), lambda b, kv, n: (b, k_blk0 + kv)),
                  pl.BlockSpec((SEQ, SWA_HD), lambda b, kv, n: (b, v_blk0 + kv)),
                  pl.BlockSpec((SWA_G, WINDOW, 3 * WINDOW), lambda b, kv, n: (kv, 0, 0))],
        out_specs=pl.BlockSpec((SWA_BLOCKS * WINDOW, qcols), lambda b, kv, n: (b * (nb // SWA_BLOCKS) + n, kv)),
        out_shape=jax.ShapeDtypeStruct((N_TOK, HALF), BF16),
        compiler_params=_cparams(("parallel", "parallel", "parallel")),
        name="swa",
    )(sinks, p1, p1, p1, bias)


def _outproj_kernel(a_ref, b_ref, wa_ref, wb_ref, x_ref, gm_ref, g_ref, sc_ref, sh_ref, wr_ref,
                    x1_ref, hp_ref, lg_ref):
    wr = wr_ref[...]
    for r in range(TM_OUT // TM_OUT_SUB):
        rows = pl.ds(r * TM_OUT_SUB, TM_OUT_SUB)
        mixed = (jnp.dot(a_ref[rows, :], wa_ref[...], preferred_element_type=F32)
                 + jnp.dot(b_ref[rows, :], wb_ref[...], preferred_element_type=F32))
        x1 = x_ref[rows, :] + gm_ref[0] * mixed
        x1_ref[rows, :] = x1
        hf = _rms(x1, g_ref[...]) * (1.0 + sc_ref[0]) + sh_ref[0]
        _store_token_tiles(hp_ref, _pack_bf16_pair(hf[:, :PACK_W], hf[:, PACK_W:]), r * TM_OUT_SUB)
        h_hi = hf.astype(BF16)
        h_lo = (hf - h_hi.astype(F32)).astype(BF16)
        t_hi = jnp.dot(h_hi, wr, preferred_element_type=F32)
        t_lo = jnp.dot(h_lo, wr, preferred_element_type=F32)
        lg_ref[rows, :] = (t_hi[:, :LANES] + t_hi[:, LANES:]) + (t_lo[:, :LANES] + t_lo[:, LANES:])


def _out_projection(a, b, w_out_bf16, x, g_m, gain, scale, shift, w_router_pad):
    per_b = SEQ // TM_OUT
    vec = pl.BlockSpec((1, 1, D_MODEL), lambda i: (i // per_b, 0, 0))
    return pl.pallas_call(
        _outproj_kernel,
        grid=(N_TOK // TM_OUT,),
        in_specs=[pl.BlockSpec((TM_OUT, HALF), lambda i: (i, 0)),
                  pl.BlockSpec((TM_OUT, HALF), lambda i: (i, 0)),
                  pl.BlockSpec((HALF, D_MODEL), lambda i: (0, 0)),
                  pl.BlockSpec((HALF, D_MODEL), lambda i: (1, 0)),
                  pl.BlockSpec((TM_OUT, D_MODEL), lambda i: (i, 0)),
                  vec,
                  pl.BlockSpec((1, D_MODEL), lambda i: (0, 0)),
                  vec, vec,
                  pl.BlockSpec((D_MODEL, 2 * LANES), lambda i: (0, 0))],
        out_specs=[pl.BlockSpec((TM_OUT, D_MODEL), lambda i: (i, 0)),
                   pl.BlockSpec((TM_OUT * TOK_ROWS, LANES), lambda i: (i, 0)),
                   pl.BlockSpec((TM_OUT, LANES), lambda i: (i, 0))],
        out_shape=[jax.ShapeDtypeStruct((N_TOK, D_MODEL), F32),
                   jax.ShapeDtypeStruct((N_TOK * TOK_ROWS, LANES), U32),
                   jax.ShapeDtypeStruct((N_TOK, LANES), F32)],
        compiler_params=_cparams(("parallel",)),
        name="out_proj",
    )(a, b, w_out_bf16, w_out_bf16, x, g_m, gain.reshape(1, D_MODEL), scale, shift, w_router_pad)


def _route_kernel(lg_ref, bias_ref, lslot_ref, w_ref, tab_ref, cnt_ref):
    step = pl.program_id(0)

    @pl.when(step == 0)
    def _():
        cnt_ref[...] = jnp.zeros(cnt_ref.shape, F32)

    t = T_ROUTE
    scores = jax.nn.sigmoid(lg_ref[...].T[:N_EXPERTS])
    biased = scores + bias_ref[...]
    sub = lax.broadcasted_iota(I32, (GROUP_SIZE, t), 0).astype(F32)
    ninf = -jnp.inf

    def first_argmax(v, idx, n):
        m = jnp.max(v, axis=0, keepdims=True)
        return m, jnp.min(jnp.where(v == m, idx, float(n)), axis=0, keepdims=True)

    gs = []
    for g in range(N_GROUPS):
        bg = biased[g * GROUP_SIZE:(g + 1) * GROUP_SIZE]
        m1, i1 = first_argmax(bg, sub, GROUP_SIZE)
        m2 = jnp.max(jnp.where(sub == i1, ninf, bg), axis=0, keepdims=True)
        gs.append(m1 + m2)
    cur = jnp.concatenate(gs, axis=0)

    gmask = jnp.zeros((N_GROUPS, t), F32)
    for _ in range(TOP_GROUPS):
        _, i = first_argmax(cur, sub, N_GROUPS)
        pick = sub == i
        gmask = jnp.where(pick, 1.0, gmask)
        cur = jnp.where(pick, ninf, cur)

    eid = lax.broadcasted_iota(I32, (N_EXPERTS, t), 0).astype(F32)
    emask = jnp.concatenate([jnp.broadcast_to(gmask[g:g + 1], (GROUP_SIZE, t)) for g in range(N_GROUPS)], axis=0)
    cur = jnp.where(emask > 0.5, biased, ninf)
    sels, ws = [], []
    onehot = jnp.zeros((N_EXPERTS, t), F32)
    for _ in range(TOP_K):
        _, i = first_argmax(cur, eid, N_EXPERTS)
        pick = eid == i
        sels.append(pick)
        ws.append(jnp.sum(jnp.where(pick, scores, 0.0), axis=0, keepdims=True))
        onehot = jnp.where(pick, 1.0, onehot)
        cur = jnp.where(pick, ninf, cur)
    wsum = ws[0]
    for k in range(1, TOP_K):
        wsum = wsum + ws[k]

    r = lax.broadcasted_iota(I32, (t, t), 0)
    c = lax.broadcasted_iota(I32, (t, t), 1)
    tri = (r < c).astype(BF16)
    earlier = jnp.dot(onehot.astype(BF16), tri, preferred_element_type=F32)
    tile_cnt = jnp.broadcast_to(jnp.sum(onehot, axis=1, keepdims=True), (N_EXPERTS, LANES))
    er = lax.broadcasted_iota(I32, (N_EXPERTS, N_EXPERTS), 0)
    ec = lax.broadcasted_iota(I32, (N_EXPERTS, N_EXPERTS), 1)
    run_start = jnp.dot((ec < er).astype(BF16), tile_cnt.astype(BF16), preferred_element_type=F32)
    pos = earlier + run_start[:, 0:1]
    lslots = [jnp.sum(jnp.where(sels[k], pos, 0.0), axis=0, keepdims=True) for k in range(TOP_K)]

    lslot_ref[...] = jnp.concatenate(lslots, axis=0).astype(I32)
    w_ref[...] = jnp.concatenate([w / wsum * ROUTE_SCALE for w in ws], axis=0)
    tab_ref[0, 0] = tile_cnt
    tab_ref[0, 1] = cnt_ref[...]
    tab_ref[0, 2] = run_start
    cnt_ref[...] = cnt_ref[...] + tile_cnt


def _route(logits, router_bias):
    ntiles = N_TOK // T_ROUTE
    return pl.pallas_call(
        _route_kernel,
        grid=(ntiles,),
        in_specs=[pl.BlockSpec((T_ROUTE, LANES), lambda i: (i, 0)),
                  pl.BlockSpec((N_EXPERTS, 1), lambda i: (0, 0))],
        out_specs=[pl.BlockSpec((TOP_K, T_ROUTE), lambda i: (0, i)),
                   pl.BlockSpec((TOP_K, T_ROUTE), lambda i: (0, i)),
                   pl.BlockSpec((1, 3, N_EXPERTS, LANES), lambda i: (i, 0, 0, 0)),
                   pl.BlockSpec((N_EXPERTS, LANES), lambda i: (0, 0))],
        out_shape=[jax.ShapeDtypeStruct((TOP_K, N_TOK), I32),
                   jax.ShapeDtypeStruct((TOP_K, N_TOK), F32),
                   jax.ShapeDtypeStruct((ntiles, 3, N_EXPERTS, LANES), F32),
                   jax.ShapeDtypeStruct((N_EXPERTS, LANES), F32)],
        compiler_params=_cparams(("arbitrary",)),
        name="route",
    )(logits, router_bias.reshape(N_EXPERTS, 1))


def _dispatch_kernel(lslot_ref, rcnt_ref, rloc_ref, rglb_ref, zstart_ref, zlen_ref, hp_ref, xs_ref,
                     zero_ref, loc_ref, sem, zsem):
    step = pl.program_id(0)
    nsteps = pl.num_programs(0)

    def zero_copy(z):
        start = pl.multiple_of(zstart_ref[z] * TOK_ROWS, TOK_ROWS)
        n = zlen_ref[z] * TOK_ROWS
        return pltpu.make_async_copy(zero_ref.at[pl.ds(0, n)], xs_ref.at[pl.ds(start, n)], zsem)

    @pl.when(step == 0)
    def _():
        zero_ref[...] = jnp.zeros(zero_ref.shape, U32)

        def fill(z, _):
            @pl.when(zlen_ref[z] > 0)
            def _():
                zero_copy(z).start()
            return 0

        lax.fori_loop(0, N_ZERO_RANGES, fill, 0)

    @pl.when(step == nsteps - 1)
    def _():
        def fill_wait(z, _):
            @pl.when(zlen_ref[z] > 0)
            def _():
                zero_copy(z).wait()
            return 0

        lax.fori_loop(0, N_ZERO_RANGES, fill_wait, 0)

    cur = step % 2

    def tile_wait(slot):
        half = pl.ds(pl.multiple_of(slot * LOCAL_ROWS, LOCAL_ROWS), LOCAL_ROWS)
        pltpu.make_async_copy(loc_ref.at[half], xs_ref.at[pl.ds(0, LOCAL_ROWS)], sem.at[slot]).wait()

    @pl.when(step >= 2)
    def _():
        tile_wait(cur)

    base = step * T_DISP

    def place(t, _):
        row = hp_ref[pl.ds(pl.multiple_of(t * TOK_ROWS, TOK_ROWS), TOK_ROWS), :]
        for k in range(TOP_K):
            dst = pl.multiple_of(lslot_ref[(base + t) * TOP_K + k], TOK_ROWS)
            loc_ref[pl.ds(dst, TOK_ROWS), :] = row
        return 0

    lax.fori_loop(0, T_DISP, place, 0, unroll=2)

    def run(e, _):
        n = rcnt_ref[step * N_EXPERTS + e]

        @pl.when(n > 0)
        def _():
            src = pl.multiple_of(rloc_ref[step * N_EXPERTS + e], TOK_ROWS)
            dst = pl.multiple_of(rglb_ref[step * N_EXPERTS + e] * TOK_ROWS, TOK_ROWS)
            pltpu.make_async_copy(loc_ref.at[pl.ds(src, n * TOK_ROWS)],
                                  xs_ref.at[pl.ds(dst, n * TOK_ROWS)], sem.at[cur]).start()
        return 0

    lax.fori_loop(0, N_EXPERTS, run, 0)

    @pl.when(step == nsteps - 1)
    def _():
        tile_wait(cur)

        @pl.when(nsteps > 1)
        def _():
            tile_wait(1 - cur)


def _dispatch(lslot_flat, run_cnt, run_loc, run_glb, zero_start, zero_len, hp):
    return pl.pallas_call(
        _dispatch_kernel,
        grid_spec=pltpu.PrefetchScalarGridSpec(
            num_scalar_prefetch=6,
            grid=(N_TILES,),
            in_specs=[pl.BlockSpec((T_DISP * TOK_ROWS, LANES), lambda i, *_: (i, 0))],
            out_specs=pl.BlockSpec(memory_space=pl.ANY),
            scratch_shapes=[pltpu.VMEM((TM_EXP * TOK_ROWS, LANES), U32),
                            pltpu.VMEM((2 * LOCAL_ROWS, LANES), U32),
                            pltpu.SemaphoreType.DMA((2,)),
                            pltpu.SemaphoreType.DMA]),
        out_shape=jax.ShapeDtypeStruct((P_ROWS * TOK_ROWS, LANES), U32),
        compiler_params=_cparams(("arbitrary",)),
        name="dispatch",
    )(lslot_flat, run_cnt, run_loc, run_glb, zero_start, zero_len, hp)


def _expert_kernel(be_ref, bv_ref, nx_ref, par_ref, xs_ref, wg_hbm, wu_hbm, wd_hbm, ys_ref,
                   wgs_ref, wus_ref, wds_ref, wgb_ref, wub_ref, wdb_ref, sem, *, layer):
    i = pl.program_id(0)
    e = be_ref[i]
    par = par_ref[i]
    changed = jnp.logical_or(i == 0, e != be_ref[jnp.maximum(i - 1, 0)])

    def stage(expert, slot):
        return (pltpu.make_async_copy(wg_hbm.at[layer, expert], wgs_ref, sem.at[0]),
                pltpu.make_async_copy(wu_hbm.at[layer, expert], wus_ref.at[slot], sem.at[1]),
                pltpu.make_async_copy(wd_hbm.at[layer, expert], wds_ref.at[slot], sem.at[2]))

    @pl.when(changed)
    def _():
        @pl.when(i == 0)
        def _():
            for c in stage(e, par):
                c.start()

        for c in stage(e, par):
            c.wait()
        rows = CAST_VREGS * SUBLANES * LANES // EXPERT_FF

        def cast_piece(c, _):
            sl = pl.ds(pl.multiple_of(c * rows, rows), rows)
            wgb_ref[sl, :] = wgs_ref[sl, :].astype(BF16)
            return 0

        lax.fori_loop(0, D_MODEL // rows, cast_piece, 0, unroll=2)

        @pl.when(nx_ref[i] >= 0)
        def _():
            for c in stage(nx_ref[i], 1 - par):
                c.start()

    def cast_up_down():
        for src, dst in ((wus_ref, wub_ref), (wds_ref, wdb_ref)):
            rows = src.shape[1] // CAST_PIECES
            for c in range(CAST_PIECES):
                dst[c * rows:(c + 1) * rows, :] = src[par, c * rows:(c + 1) * rows, :].astype(BF16)

    def sub_block(row0, nrows):
        xs_sub = xs_ref.at[pl.ds(row0 * TOK_ROWS, nrows * TOK_ROWS)]
        lo, hi = _unpack_bf16_pair(_load_token_tiles(xs_sub, nrows))
        lo = lo.astype(BF16)
        hi = hi.astype(BF16)
        hg = (jnp.dot(lo, wgb_ref[0:PACK_W], preferred_element_type=F32)
              + jnp.dot(hi, wgb_ref[PACK_W:D_MODEL], preferred_element_type=F32))
        hu = (jnp.dot(lo, wub_ref[0:PACK_W], preferred_element_type=F32)
              + jnp.dot(hi, wub_ref[PACK_W:D_MODEL], preferred_element_type=F32))
        act = (_silu(hg) * hu).astype(BF16)
        y = jnp.dot(act, wdb_ref[...], preferred_element_type=F32)
        _store_token_tiles(ys_ref, _pack_bf16_pair(y[:, :PACK_W], y[:, PACK_W:]), row0)

    valid_rows = bv_ref[i]
    full = valid_rows > TM_EXP - TM_EXP_TAIL
    unchanged = jnp.logical_not(changed)

    @pl.when(jnp.logical_and(full, changed))
    def _():
        cast_up_down()
        sub_block(0, TM_EXP_SUB)
        sub_block(TM_EXP_SUB, TM_EXP_SUB)

    @pl.when(jnp.logical_and(full, unchanged))
    def _():
        sub_block(0, TM_EXP_SUB)
        sub_block(TM_EXP_SUB, TM_EXP_SUB)

    @pl.when(jnp.logical_not(full))
    def _():
        @pl.when(changed)
        def _():
            cast_up_down()
            sub_block(0, TM_EXP_TAIL)

        for r in range(TM_EXP // TM_EXP_TAIL):
            if r == 0:
                compute = jnp.logical_and(unchanged, valid_rows > 0)
                skip = jnp.logical_and(unchanged, valid_rows <= 0)
            else:
                compute = valid_rows > r * TM_EXP_TAIL
                skip = valid_rows <= r * TM_EXP_TAIL

            @pl.when(compute)
            def _():
                sub_block(r * TM_EXP_TAIL, TM_EXP_TAIL)

            @pl.when(skip)
            def _():
                ys_ref[pl.ds(r * TM_EXP_TAIL * TOK_ROWS, TM_EXP_TAIL * TOK_ROWS), :] = jnp.zeros(
                    (TM_EXP_TAIL * TOK_ROWS, LANES), U32)


def _experts(layer, block_expert, block_rows, next_expert, block_parity, xs, w_gate, w_up, w_down):
    return pl.pallas_call(
        functools.partial(_expert_kernel, layer=layer),
        grid_spec=pltpu.PrefetchScalarGridSpec(
            num_scalar_prefetch=4,
            grid=(NB_EXP,),
            in_specs=[pl.BlockSpec((TM_EXP * TOK_ROWS, LANES), lambda i, be, bv, *_: (jnp.where(bv[i] > 0, i, 0), 0)),
                      pl.BlockSpec(memory_space=pl.ANY),
                      pl.BlockSpec(memory_space=pl.ANY),
                      pl.BlockSpec(memory_space=pl.ANY)],
            out_specs=pl.BlockSpec((TM_EXP * TOK_ROWS, LANES), lambda i, *_: (i, 0)),
            scratch_shapes=[pltpu.VMEM((D_MODEL, EXPERT_FF), F32),
                            pltpu.VMEM((2, D_MODEL, EXPERT_FF), F32),
                            pltpu.VMEM((2, EXPERT_FF, D_MODEL), F32),
                            pltpu.VMEM((D_MODEL, EXPERT_FF), BF16),
                            pltpu.VMEM((D_MODEL, EXPERT_FF), BF16),
                            pltpu.VMEM((EXPERT_FF, D_MODEL), BF16),
                            pltpu.SemaphoreType.DMA((3,))]),
        out_shape=jax.ShapeDtypeStruct((P_ROWS * TOK_ROWS, LANES), U32),
        compiler_params=_cparams(("arbitrary",)),
        name="experts",
    )(block_expert, block_rows, next_expert, block_parity, xs, w_gate, w_up, w_down)


def _combine_kernel(lslot_ref, rcnt_ref, rloc_ref, rglb_ref, w_ref, ys_ref, hp_ref, x1_ref, gf_ref,
                    wsg_ref, wsu_ref, wsd_ref, fn_ref, o_ref, buf_ref, mlo_ref, mhi_ref, sem, *, final_norm):
    i = pl.program_id(0)
    nsteps = pl.num_programs(0)

    def issue(step, slot):
        def run(e, _):
            n = rcnt_ref[step * N_EXPERTS + e]

            @pl.when(n > 0)
            def _():
                src = pl.multiple_of(rglb_ref[step * N_EXPERTS + e] * TOK_ROWS, TOK_ROWS)
                dst = pl.multiple_of(rloc_ref[step * N_EXPERTS + e], TOK_ROWS)
                pltpu.make_async_copy(ys_ref.at[pl.ds(src, n * TOK_ROWS)],
                                      buf_ref.at[pl.ds(dst, n * TOK_ROWS)], sem.at[slot]).start()
            return 0

        lax.fori_loop(0, N_EXPERTS, run, 0)

    @pl.when(i == 0)
    def _():
        issue(0, 0)

    @pl.when(i + 1 < nsteps)
    def _():
        issue(i + 1, (i + 1) % 2)

    cur = i % 2
    cur_half = pl.ds(pl.multiple_of(cur * LOCAL_ROWS, LOCAL_ROWS), LOCAL_ROWS)
    pltpu.make_async_copy(ys_ref.at[pl.ds(0, LOCAL_ROWS)], buf_ref.at[cur_half], sem.at[cur]).wait()

    base = i * T_DISP

    def token(t, _):
        acc_lo = jnp.zeros((TOK_ROWS, LANES), F32)
        acc_hi = jnp.zeros((TOK_ROWS, LANES), F32)
        for k in range(TOP_K):
            idx = (base + t) * TOP_K + k
            src = pl.multiple_of(lslot_ref[idx], TOK_ROWS)
            lo, hi = _unpack_bf16_pair(buf_ref[pl.ds(src, TOK_ROWS), :])
            wk = w_ref[idx]
            acc_lo = acc_lo + wk * lo
            acc_hi = acc_hi + wk * hi
        dst = pl.ds(pl.multiple_of(t * TOK_ROWS, TOK_ROWS), TOK_ROWS)
        mlo_ref[dst, :] = acc_lo
        mhi_ref[dst, :] = acc_hi
        return 0

    lax.fori_loop(0, T_DISP, token, 0, unroll=2)
    moe_lo = _load_token_tiles(mlo_ref, T_DISP)
    moe_hi = _load_token_tiles(mhi_ref, T_DISP)

    hlo, hhi = _unpack_bf16_pair(_load_token_tiles(hp_ref, T_DISP))
    hlo = hlo.astype(BF16)
    hhi = hhi.astype(BF16)
    sg = (jnp.dot(hlo, wsg_ref[0:PACK_W], preferred_element_type=F32)
          + jnp.dot(hhi, wsg_ref[PACK_W:D_MODEL], preferred_element_type=F32))
    su = (jnp.dot(hlo, wsu_ref[0:PACK_W], preferred_element_type=F32)
          + jnp.dot(hhi, wsu_ref[PACK_W:D_MODEL], preferred_element_type=F32))
    shared = jnp.dot((_silu(sg) * su).astype(BF16), wsd_ref[...], preferred_element_type=F32)
    moe = jnp.concatenate([moe_lo, moe_hi], axis=1)
    out = x1_ref[...] + gf_ref[0] * (moe + shared)
    if final_norm:
        out = _rms(out, fn_ref[...])
    o_ref[...] = out


def _combine(lslot_flat, run_cnt, run_loc, run_glb, w_flat, ys, hp, x1, g_f, wsg, wsu, wsd, final_gain, final_norm):
    per_b = SEQ // T_DISP
    return pl.pallas_call(
        functools.partial(_combine_kernel, final_norm=final_norm),
        grid_spec=pltpu.PrefetchScalarGridSpec(
            num_scalar_prefetch=4,
            grid=(N_TILES,),
            in_specs=[pl.BlockSpec(memory_space=pltpu.SMEM),
                      pl.BlockSpec(memory_space=pl.ANY),
                      pl.BlockSpec((T_DISP * TOK_ROWS, LANES), lambda i, *_: (i, 0)),
                      pl.BlockSpec((T_DISP, D_MODEL), lambda i, *_: (i, 0)),
                      pl.BlockSpec((1, 1, D_MODEL), lambda i, *_: (i // per_b, 0, 0)),
                      pl.BlockSpec((D_MODEL, SHARED_FF), lambda i, *_: (0, 0)),
                      pl.BlockSpec((D_MODEL, SHARED_FF), lambda i, *_: (0, 0)),
                      pl.BlockSpec((SHARED_FF, D_MODEL), lambda i, *_: (0, 0)),
                      pl.BlockSpec((1, D_MODEL), lambda i, *_: (0, 0))],
            out_specs=pl.BlockSpec((T_DISP, D_MODEL), lambda i, *_: (i, 0)),
            scratch_shapes=[pltpu.VMEM((2 * LOCAL_ROWS, LANES), U32),
                            pltpu.VMEM((T_DISP * TOK_ROWS, LANES), F32),
                            pltpu.VMEM((T_DISP * TOK_ROWS, LANES), F32),
                            pltpu.SemaphoreType.DMA((2,))]),
        out_shape=jax.ShapeDtypeStruct((N_TOK, D_MODEL), F32),
        compiler_params=_cparams(("arbitrary",)),
        name="combine",
    )(lslot_flat, run_cnt, run_loc, run_glb, w_flat, ys, hp, x1, g_f, wsg, wsu, wsd, final_gain.reshape(1, D_MODEL))


def _moe_layer(layer, hp, logits, x1, g_f, router_bias, w_gate, w_up, w_down, ws_gate, ws_up, ws_down,
               final_gain, final_norm):
    lslot, w_k, tables, counts = _route(logits, router_bias)
    cnt = counts[:, 0].astype(I32)
    padded = ((cnt + TM_EXP - 1) // TM_EXP) * TM_EXP
    ends = jnp.cumsum(padded)
    offsets = ends - padded
    tables = tables[:, :, :, 0].astype(I32)
    run_cnt = tables[:, 0].reshape(-1)
    run_glb = (offsets[None, :] + tables[:, 1]).reshape(-1)
    half_row0 = (jnp.arange(N_TILES, dtype=I32) % 2) * LOCAL_ROWS
    run_loc = (tables[:, 2] * TOK_ROWS + half_row0[:, None]).reshape(-1)
    lslot_flat = (lslot * TOK_ROWS + jnp.repeat(half_row0, T_DISP)[None, :]).T.reshape(N_SLOTS)
    blk_start = jnp.arange(NB_EXP, dtype=I32) * TM_EXP
    expert_ids = jnp.arange(N_EXPERTS, dtype=I32)
    nonempty = cnt > 0
    last_nonempty = jnp.max(jnp.where(nonempty, expert_ids, 0))
    block_expert = jnp.minimum(jnp.sum((blk_start[:, None] >= ends[None, :]).astype(I32), axis=1), last_nonempty)
    block_valid = (blk_start < ends[-1]).astype(I32)
    block_rows = jnp.clip((offsets + cnt)[block_expert] - blk_start, 0, TM_EXP) * block_valid
    following = jnp.where(nonempty, expert_ids, N_EXPERTS)
    following = lax.cummin(following, reverse=True)
    following = jnp.concatenate([following[1:], jnp.full((1,), N_EXPERTS, I32)])
    next_expert = jnp.where(following < N_EXPERTS, following, -1)[block_expert]
    zero_start = jnp.concatenate([offsets + cnt, blk_start])
    zero_len = jnp.concatenate([padded - cnt, (1 - block_valid) * TM_EXP])
    xs = _dispatch(lslot_flat, run_cnt, run_loc, run_glb, zero_start, zero_len, hp)
    block_parity = ((jnp.cumsum(nonempty.astype(I32)) - 1) % 2)[block_expert]
    ys = _experts(layer, block_expert, block_rows, next_expert, block_parity, xs, w_gate, w_up, w_down)
    return _combine(lslot_flat, run_cnt, run_loc, run_glb, w_k.T.reshape(N_SLOTS), ys, hp, x1, g_f,
                    ws_gate.astype(BF16), ws_up.astype(BF16), ws_down.astype(BF16), final_gain, final_norm)


def _rope_tables(dim):
    inv = ROPE_THETA ** (-jnp.arange(0, dim, 2, dtype=F32) / dim)
    ang = jnp.arange(SEQ, dtype=F32)[:, None] * inv[None, :]
    return jnp.cos(ang), jnp.sin(ang)


def _rot_half_cols(w):
    half = w.shape[-1] // 2
    return jnp.concatenate([-w[..., half:], w[..., :half]], axis=-1)


def _t5_bucket(rel):
    half = REL_BUCKETS // 2
    max_exact = half // 2
    ret = (rel > 0).astype(I32) * half
    n = jnp.abs(rel)
    nf = jnp.maximum(n, 1).astype(F32)
    large = max_exact + (jnp.log(nf / max_exact) / math.log(REL_MAX_DIST / max_exact)
                         * (half - max_exact)).astype(I32)
    large = jnp.minimum(large, half - 1)
    return ret + jnp.where(n < max_exact, n, large)


def _swa_bias_table(rel_bias):
    qi = jnp.arange(WINDOW)[:, None]
    kj = jnp.arange(3 * WINDOW)[None, :]
    rel = kj - WINDOW - qi
    onehot = (_t5_bucket(rel)[:, :, None] == jnp.arange(REL_BUCKETS)).astype(F32)
    bias = jnp.einsum('qjb,bh->hqj', onehot, rel_bias.astype(F32), precision=lax.Precision.HIGHEST)
    return jnp.where((jnp.abs(rel) <= WINDOW)[None], bias, NEG_BIG)


def kernel(x, c, w_mod, b_mod, norm_mix, norm_ffn, final_norm, w_in_ab, q_lat_norm, kv_lat_norm, w_uq, w_ukv, conv_w, conv_b, lru_w_a, lru_b_a, lru_w_x, lru_b_x, lru_lambda, w_out_ab, w_in_cd, ret_gn, swa_sinks, w_out_cd, rel_bias, w_router, router_bias, w_gate, w_up, w_down, ws_gate, ws_up, ws_down):
    xf = x.reshape(N_TOK, D_MODEL)
    mod = _modulation(c, w_mod, b_mod)
    cos_r, sin_r = _rope_tables(MLA_ROPE)
    cs_tab = jnp.concatenate([cos_r, cos_r, sin_r, sin_r], axis=1)
    cos_t, sin_t = _rope_tables(RET_DK)
    lg_ret = jnp.log1p(-(2.0 ** (-5.0 - jnp.arange(RET_HEADS, dtype=F32))))

    for layer in range(DEPTH):
        sh_m, sc_m, g_m, sh_f, sc_f, g_f = [m.reshape(BATCH, 1, D_MODEL) for m in jnp.split(mod[layer], 6, axis=-1)]
        i = layer // 2
        if layer % 2 == 0:
            w = w_in_ab[i]
            o1, o2, o3, o4 = np.cumsum((MLA_Q_RANK, MLA_KV_RANK, MLA_ROPE, LRU_WIDTH)).tolist()
            w_kr = w[:, o2:o3]
            w_in = jnp.concatenate([w[:, :o2], w[:, o3:], w_kr, _rot_half_cols(w_kr)], axis=1).astype(BF16)
            p0 = _in_projection(xf, norm_mix[layer], sc_m, sh_m, w_in, "in_proj_ab")
            wq = w_uq[i].reshape(MLA_Q_RANK, MLA_HEADS, MLA_NOPE + MLA_ROPE)
            wq_r = wq[:, :, MLA_NOPE:]
            wq = jnp.concatenate([wq, _rot_half_cols(wq_r)], axis=-1).reshape(MLA_Q_RANK, MLA_HEADS * MLA_QK)
            q, k, v = _mla_up(p0, q_lat_norm[i], kv_lat_norm[i], wq.astype(BF16), w_ukv[i].astype(BF16), cs_tab)
            a_out = _mla_attention(q, k, v)
            w_gates = jnp.concatenate([lru_w_a[i, 0], lru_w_x[i, 0], lru_w_a[i, 1], lru_w_x[i, 1]], axis=-1).astype(BF16)
            b_gates = jnp.concatenate([b.reshape(LRU_BLOCKS, 1, LRU_BS) for b in
                                       (lru_b_a[i, 0], lru_b_x[i, 0], lru_b_a[i, 1], lru_b_x[i, 1])], axis=-1)
            b_out = _rglru(p0, conv_w[i], conv_b[i], w_gates, b_gates, lru_lambda[i])
            w_out = w_out_ab[i].astype(BF16)
        else:
            p1 = _in_projection(xf, norm_mix[layer], sc_m, sh_m, w_in_cd[i].astype(BF16), "in_proj_cd")
            a_out = _retention(p1, lg_ret, cos_t, sin_t, ret_gn[i])
            b_out = _swa(p1, swa_sinks[i], _swa_bias_table(rel_bias))
            w_out = w_out_cd[i].astype(BF16)
        w_r = jnp.pad(w_router[layer], ((0, 0), (0, LANES - N_EXPERTS)))
        w_r_hi = w_r.astype(BF16)
        w_router_pad = jnp.concatenate([w_r_hi, (w_r - w_r_hi.astype(F32)).astype(BF16)], axis=1)
        x1, hp, logits = _out_projection(a_out, b_out, w_out, xf, g_m, norm_ffn[layer], sc_f, sh_f, w_router_pad)
        xf = _moe_layer(layer, hp, logits, x1, g_f, router_bias[layer], w_gate, w_up, w_down,
                        ws_gate[layer], ws_up[layer], ws_down[layer], final_norm, layer == DEPTH - 1)
    return xf.reshape(BATCH, SEQ, D_MODEL)
```

```python
import functools
import math

import numpy as np
import jax
import jax.numpy as jnp
from jax import lax
from jax.experimental import pallas as pl
from jax.experimental.pallas import tpu as pltpu

F32 = jnp.float32
BF16 = jnp.bfloat16
I32 = jnp.int32
U32 = jnp.uint32

D_MODEL = 2048
BATCH = 4
SEQ = 2048
DEPTH = 2
N_TOK = BATCH * SEQ
HALF = D_MODEL // 2
MLA_NOPE = 128
MLA_ROPE = 64
MLA_V = 128
MLA_HEADS = HALF // MLA_V
MLA_Q_RANK = D_MODEL // 4
MLA_KV_RANK = D_MODEL // 4
MLA_QK = 256
LRU_WIDTH = HALF
LRU_BLOCKS = 8
LRU_BS = LRU_WIDTH // LRU_BLOCKS
LRU_CONV = 4
LRU_C = 8.0
RET_DK = 256
RET_DV = 256
RET_HEADS = HALF // RET_DV
SWA_HD = 128
SWA_HEADS = HALF // SWA_HD
SWA_KV_HEADS = 2
SWA_G = SWA_HEADS // SWA_KV_HEADS
WINDOW = 128
REL_BUCKETS = 32
REL_MAX_DIST = 128
N_EXPERTS = 64
TOP_K = 8
N_GROUPS = 8
GROUP_SIZE = N_EXPERTS // N_GROUPS
TOP_GROUPS = 4
EXPERT_FF = D_MODEL // 4
SHARED_FF = D_MODEL // 4
ROUTE_SCALE = 2.5
ROPE_THETA = 10000.0
EPS = 1e-6
NEG_BIG = -1e30

LANES = 128
SUBLANES = 8
VMEM_LIMIT = 52 * 2**20

TM_PROJ = 256
TM_PROJ_SUB = 128
TM_UP = 512
TQ_ATT = 1024
TQ_MLA = 2048
TQ_SUB = 256
SWA_BLOCKS = 8
TM_OUT = 512
TM_OUT_SUB = 256
T_DISP = 256
T_ROUTE = T_DISP
ROUTE_TILES = 4
N_TILES = N_TOK // T_DISP
TILE_ROWS = T_DISP * TOP_K
TM_EXP = 512
TM_EXP_SUB = 256
TM_EXP_TAIL = 128
assert TM_EXP == 2 * TM_EXP_SUB and TM_EXP % TM_EXP_TAIL == 0
CAST_VREGS = 32
CAST_PIECES = 16
N_SLOTS = N_TOK * TOP_K
NB_EXP = N_SLOTS // TM_EXP + N_EXPERTS
P_ROWS = NB_EXP * TM_EXP
N_ZERO_RANGES = N_EXPERTS + NB_EXP
PACK_W = D_MODEL // 2
TOK_ROWS = PACK_W // LANES
assert TOK_ROWS == SUBLANES
LOCAL_ROWS = TILE_ROWS * TOK_ROWS

LRU_SEG = 260
LRU_ROWS = SUBLANES * LRU_SEG
assert LRU_ROWS >= SEQ and LRU_SEG % 8 == 4


def _cparams(sem, vmem=VMEM_LIMIT):
    return pltpu.CompilerParams(dimension_semantics=sem, vmem_limit_bytes=vmem)


def _sigmoid(x):
    return 0.5 * jnp.tanh(0.5 * x) + 0.5


def _silu(x):
    return x * _sigmoid(x)


def _rms(x, g):
    return x * lax.rsqrt(jnp.mean(x * x, axis=-1, keepdims=True) + EPS) * g


def _pack_bf16_pair(lo, hi):
    lo_b = lax.bitcast_convert_type(lo.astype(BF16).astype(F32), U32)
    hi_b = lax.bitcast_convert_type(hi.astype(BF16).astype(F32), U32)
    return (hi_b & jnp.uint32(0xFFFF0000)) | (lo_b >> 16)


def _unpack_bf16_pair(w):
    lo = lax.bitcast_convert_type(w << 16, F32)
    hi = lax.bitcast_convert_type(w & jnp.uint32(0xFFFF0000), F32)
    return lo, hi


def _store_token_tiles(ref, packed, tok0=0):
    t = packed.shape[0]
    for s in range(TOK_ROWS):
        ref[pl.ds(tok0 * TOK_ROWS + s, t, stride=TOK_ROWS), :] = packed[:, s * LANES:(s + 1) * LANES]


def _load_token_tiles(ref, t):
    return jnp.concatenate([ref[pl.ds(s, t, stride=TOK_ROWS), :] for s in range(TOK_ROWS)], axis=1)


def _mod_kernel(c_ref, w_ref, b_ref, o_ref):
    c = c_ref[...]
    ca = _silu(c).astype(BF16)
    o_ref[0] = jnp.dot(ca, w_ref[0].astype(BF16), preferred_element_type=F32) + b_ref[0]


def _modulation(c, w_mod, b_mod):
    tn = 1024
    cp = jnp.pad(c, ((0, SUBLANES - BATCH), (0, 0)))
    out = pl.pallas_call(
        _mod_kernel,
        grid=(DEPTH, 6 * D_MODEL // tn),
        in_specs=[pl.BlockSpec((SUBLANES, D_MODEL), lambda l, j: (0, 0)),
                  pl.BlockSpec((1, D_MODEL, tn), lambda l, j: (l, 0, j)),
                  pl.BlockSpec((1, 1, tn), lambda l, j: (l, 0, j))],
        out_specs=pl.BlockSpec((1, SUBLANES, tn), lambda l, j: (l, 0, j)),
        out_shape=jax.ShapeDtypeStruct((DEPTH, SUBLANES, 6 * D_MODEL), F32),
        compiler_params=_cparams(("parallel", "parallel")),
        name="adaln_mod",
    )(cp, w_mod, b_mod.reshape(DEPTH, 1, 6 * D_MODEL))
    return out[:, :BATCH]


def _inproj_kernel(x_ref, g_ref, sc_ref, sh_ref, w_ref, o_ref):
    for r in range(TM_PROJ // TM_PROJ_SUB):
        rows = pl.ds(r * TM_PROJ_SUB, TM_PROJ_SUB)
        y = _rms(x_ref[rows, :], g_ref[...])
        h = (y * (1.0 + sc_ref[0]) + sh_ref[0]).astype(BF16)
        o_ref[rows, :] = jnp.dot(h, w_ref[...], preferred_element_type=F32)


def _in_projection(x, gain, scale, shift, w_bf16, name):
    p = w_bf16.shape[1]
    per_b = SEQ // TM_PROJ
    return pl.pallas_call(
        _inproj_kernel,
        grid=(N_TOK // TM_PROJ,),
        in_specs=[pl.BlockSpec((TM_PROJ, D_MODEL), lambda i: (i, 0)),
                  pl.BlockSpec((1, D_MODEL), lambda i: (0, 0)),
                  pl.BlockSpec((1, 1, D_MODEL), lambda i: (i // per_b, 0, 0)),
                  pl.BlockSpec((1, 1, D_MODEL), lambda i: (i // per_b, 0, 0)),
                  pl.BlockSpec((D_MODEL, p), lambda i: (0, 0), pipeline_mode=pl.Buffered(1))],
        out_specs=pl.BlockSpec((TM_PROJ, p), lambda i: (i, 0)),
        out_shape=jax.ShapeDtypeStruct((N_TOK, p), F32),
        compiler_params=_cparams(("parallel",), 56 * 2**20),
        name=name,
    )(x, gain.reshape(1, D_MODEL), scale, shift, w_bf16)


def _mla_up_kernel(ql_ref, kvl_ref, kr_ref, qn_ref, kvn_ref, wq_ref, wkv_ref, cs_ref, q_ref, k_ref, v_ref):
    scale = (MLA_NOPE + MLA_ROPE) ** -0.5 * math.log2(math.e)
    hq = _rms(ql_ref[...], qn_ref[...]).astype(BF16)
    hkv = _rms(kvl_ref[...], kvn_ref[...]).astype(BF16)
    yq = jnp.dot(hq, wq_ref[...], preferred_element_type=F32) * scale
    ykv = jnp.dot(hkv, wkv_ref[...], preferred_element_type=F32)
    cs = cs_ref[...]
    lane = lax.broadcasted_iota(I32, cs.shape, 1)

    def rope_sum(blk):
        z = blk * cs
        return z + pltpu.roll(z, MLA_ROPE, 1)

    kr = jnp.where(lane < MLA_ROPE, rope_sum(kr_ref[...]), 0.0).astype(BF16)
    ones_col = jnp.where(lane == 0, 1.0, 0.0).astype(BF16)
    for h in range(MLA_HEADS):
        c0 = h * MLA_QK
        q_ref[0, h, :, 0:MLA_NOPE] = yq[:, c0:c0 + MLA_NOPE].astype(BF16)
        q_ref[0, h, :, MLA_NOPE:MLA_QK] = rope_sum(yq[:, c0 + MLA_NOPE:c0 + MLA_QK]).astype(BF16)
        k_ref[0, h, :, 0:MLA_NOPE] = ykv[:, c0:c0 + MLA_NOPE].astype(BF16)
        k_ref[0, h, :, MLA_NOPE:MLA_QK] = kr
        v_ref[0, h, :, 0:MLA_V] = ykv[:, c0 + MLA_NOPE:c0 + MLA_QK].astype(BF16)
        v_ref[0, h, :, MLA_V:2 * MLA_V] = ones_col


def _mla_up(p0, q_norm, kv_norm, wq, wkv, cs_tab):
    per_b = SEQ // TM_UP
    qk_shape = jax.ShapeDtypeStruct((BATCH, MLA_HEADS, SEQ, MLA_QK), BF16)
    return pl.pallas_call(
        _mla_up_kernel,
        grid=(N_TOK // TM_UP,),
        in_specs=[pl.BlockSpec((TM_UP, MLA_Q_RANK), lambda i: (i, 0)),
                  pl.BlockSpec((TM_UP, MLA_KV_RANK), lambda i: (i, 1)),
                  pl.BlockSpec((TM_UP, LANES), lambda i: (i, 24)),
                  pl.BlockSpec((1, MLA_Q_RANK), lambda i: (0, 0)),
                  pl.BlockSpec((1, MLA_KV_RANK), lambda i: (0, 0)),
                  pl.BlockSpec((MLA_Q_RANK, MLA_HEADS * MLA_QK), lambda i: (0, 0)),
                  pl.BlockSpec((MLA_KV_RANK, MLA_HEADS * MLA_QK), lambda i: (0, 0)),
                  pl.BlockSpec((TM_UP, LANES), lambda i: (i % per_b, 0))],
        out_specs=[pl.BlockSpec((1, MLA_HEADS, TM_UP, MLA_QK), lambda i: (i // per_b, 0, i % per_b, 0)),
                   pl.BlockSpec((1, MLA_HEADS, TM_UP, MLA_QK), lambda i: (i // per_b, 0, i % per_b, 0)),
                   pl.BlockSpec((1, MLA_HEADS, TM_UP, 2 * MLA_V), lambda i: (i // per_b, 0, i % per_b, 0))],
        out_shape=[qk_shape, qk_shape, jax.ShapeDtypeStruct((BATCH, MLA_HEADS, SEQ, 2 * MLA_V), BF16)],
        compiler_params=_cparams(("parallel",)),
        name="mla_up",
    )(p0, p0, p0, q_norm.reshape(1, -1), kv_norm.reshape(1, -1), wq, wkv, cs_tab)


def _mla_attn_kernel(q_ref, k_ref, v_ref, o_ref):
    k = k_ref[0, 0]
    v = v_ref[0, 0]
    for r in range(TQ_MLA // TQ_SUB):
        rows = pl.ds(r * TQ_SUB, TQ_SUB)
        s = lax.dot_general(q_ref[0, 0, rows, :], k, (((1,), (1,)), ((), ())), preferred_element_type=F32)
        p = jnp.exp2(s - jnp.max(s, axis=-1, keepdims=True))
        o = jnp.dot(p.astype(BF16), v, preferred_element_type=F32)
        o_ref[rows, :] = (o[:, 0:MLA_V] / o[:, MLA_V:MLA_V + 1]).astype(BF16)


def _mla_attention(q, k, v):
    nq = SEQ // TQ_MLA
    return pl.pallas_call(
        _mla_attn_kernel,
        grid=(BATCH, MLA_HEADS, nq),
        in_specs=[pl.BlockSpec((1, 1, TQ_MLA, MLA_QK), lambda b, h, i: (b, h, i, 0)),
                  pl.BlockSpec((1, 1, SEQ, MLA_QK), lambda b, h, i: (b, h, 0, 0)),
                  pl.BlockSpec((1, 1, SEQ, 2 * MLA_V), lambda b, h, i: (b, h, 0, 0))],
        out_specs=pl.BlockSpec((TQ_MLA, MLA_V), lambda b, h, i: (b * nq + i, h)),
        out_shape=jax.ShapeDtypeStruct((N_TOK, HALF), BF16),
        compiler_params=_cparams(("parallel", "parallel", "parallel")),
        name="mla_attn",
    )(q, k, v)


def _lru_kernel(x_ref, gate_ref, cw_ref, cb_ref, wg_ref, bg_ref, lam_ref, o_ref,
                af_ref, uf_ref, ab_ref, ub_ref, hf_ref, pf_ref, hb_ref, pb_ref, hs_ref):
    x = x_ref[...]
    row = lax.broadcasted_iota(I32, x.shape, 0)

    def shifted(d):
        r = pltpu.roll(x, (-d) % SEQ, 0)
        return jnp.where((row + d >= 0) & (row + d < SEQ), r, 0.0)

    cw = cw_ref[...]
    left = LRU_CONV // 2
    xc = cb_ref[...]
    for kk in range(LRU_CONV):
        d = kk - left
        xc = xc + cw[kk:kk + 1] * (x if d == 0 else shifted(d))

    gates = jnp.dot(xc.astype(BF16), wg_ref[0], preferred_element_type=F32) + bg_ref[0]
    lam = lam_ref[...]
    z = -lam
    sp = jnp.maximum(z, 0.0) + jnp.log1p(jnp.exp(-jnp.abs(z)))
    pad_rows = LRU_ROWS - SEQ
    for d, (a_ref, u_ref) in enumerate(((af_ref, uf_ref), (ab_ref, ub_ref))):
        r = _sigmoid(gates[:, d * 256:d * 256 + LRU_BS])
        i = _sigmoid(gates[:, d * 256 + LRU_BS:(d + 1) * 256])
        a = jnp.exp(r * (-LRU_C * sp[d:d + 1]))
        a_ref[0:SEQ] = a
        u_ref[0:SEQ] = jnp.sqrt(1.0 - a * a) * (i * xc)
        a_ref[SEQ:LRU_ROWS] = jnp.zeros((pad_rows, LANES), F32)
        u_ref[SEQ:LRU_ROWS] = jnp.zeros((pad_rows, LANES), F32)

    ones = jnp.ones((SUBLANES, LANES), F32)
    zeros = jnp.zeros((SUBLANES, LANES), F32)

    def seg(t):
        return pl.ds(t, SUBLANES, stride=LRU_SEG)

    def local_scan(s, carry):
        p_f, h_f, p_b, h_b = carry
        tf = s
        tb = LRU_SEG - 1 - s
        a = af_ref[seg(tf)]
        h_f = a * h_f + uf_ref[seg(tf)]
        p_f = a * p_f
        hf_ref[seg(tf)] = h_f
        pf_ref[seg(tf)] = p_f
        a = ab_ref[seg(tb)]
        h_b = a * h_b + ub_ref[seg(tb)]
        p_b = a * p_b
        hb_ref[seg(tb)] = h_b
        pb_ref[seg(tb)] = p_b
        return p_f, h_f, p_b, h_b

    p_f, h_f, p_b, h_b = lax.fori_loop(0, LRU_SEG, local_scan, (ones, zeros, ones, zeros), unroll=4)

    rows_f = []
    c = jnp.zeros((1, LANES), F32)
    for j in range(SUBLANES):
        rows_f.append(c)
        c = p_f[j:j + 1] * c + h_f[j:j + 1]
    rows_b = [None] * SUBLANES
    c = jnp.zeros((1, LANES), F32)
    for j in range(SUBLANES - 1, -1, -1):
        rows_b[j] = c
        c = p_b[j:j + 1] * c + h_b[j:j + 1]
    sub = lax.broadcasted_iota(I32, (SUBLANES, LANES), 0)
    c_f = zeros
    c_b = zeros
    for j in range(SUBLANES):
        c_f = jnp.where(sub == j, rows_f[j], c_f)
        c_b = jnp.where(sub == j, rows_b[j], c_b)

    def fixup(t, _):
        hs_ref[seg(t)] = (hf_ref[seg(t)] + pf_ref[seg(t)] * c_f) + (hb_ref[seg(t)] + pb_ref[seg(t)] * c_b)
        return 0

    lax.fori_loop(0, LRU_SEG, fixup, 0, unroll=4)

    g = gate_ref[...]
    gelu = 0.5 * g * (1.0 + jnp.tanh(math.sqrt(2.0 / math.pi) * (g + 0.044715 * (g * g * g))))
    o_ref[...] = (gelu * hs_ref[0:SEQ]).astype(BF16)


def _rglru(p0, conv_w, conv_b, w_gates, b_gates, lam):
    scan_buf = pltpu.VMEM((LRU_ROWS, LANES), F32)
    return pl.pallas_call(
        _lru_kernel,
        grid=(BATCH, LRU_BLOCKS),
        in_specs=[pl.BlockSpec((SEQ, LRU_BS), lambda b, g: (b, 8 + g)),
                  pl.BlockSpec((SEQ, LRU_BS), lambda b, g: (b, 16 + g)),
                  pl.BlockSpec((LRU_CONV, LRU_BS), lambda b, g: (0, g)),
                  pl.BlockSpec((1, LRU_BS), lambda b, g: (0, g)),
                  pl.BlockSpec((1, LRU_BS, 4 * LRU_BS), lambda b, g: (g, 0, 0)),
                  pl.BlockSpec((1, 1, 4 * LRU_BS), lambda b, g: (g, 0, 0)),
                  pl.BlockSpec((2, LRU_BS), lambda b, g: (0, g))],
        out_specs=pl.BlockSpec((SEQ, LRU_BS), lambda b, g: (b, g)),
        out_shape=jax.ShapeDtypeStruct((N_TOK, LRU_WIDTH), BF16),
        scratch_shapes=[scan_buf] * 9,
        compiler_params=_cparams(("parallel", "parallel")),
        name="rglru",
    )(p0, p0, conv_w, conv_b.reshape(1, -1), w_gates, b_gates, lam)


def _ret_kernel(lg_ref, q_ref, k_ref, v_ref, g_ref, cq_ref, sq_ref, ck_ref, sk_ref, gn_ref, o_ref, ks_ref, vs_ref):
    h = pl.program_id(1)
    qi = pl.program_id(2)
    half = RET_DK // 2

    def rope(t, c, s):
        t1, t2 = t[:, :half], t[:, half:]
        return jnp.concatenate([t1 * c - t2 * s, t2 * c + t1 * s], axis=1)

    @pl.when(qi == 0)
    def _():
        ks_ref[...] = (rope(k_ref[...], ck_ref[...], sk_ref[...]) * (RET_DK ** -0.5)).astype(BF16)
        vs_ref[...] = v_ref[...].astype(BF16)

    q = rope(q_ref[...], cq_ref[...], sq_ref[...]).astype(BF16)
    s = lax.dot_general(q, ks_ref[...], (((1,), (1,)), ((), ())), preferred_element_type=F32)
    n = (qi * TQ_ATT + lax.broadcasted_iota(I32, (TQ_ATT, 1), 0)).astype(F32)
    m = lax.broadcasted_iota(I32, (1, SEQ), 1).astype(F32)
    c_f = lg_ref[h] * math.log2(math.e)
    c_b = -lg_ref[RET_HEADS - 1 - h] * math.log2(math.e)
    dec = jnp.exp2(jnp.minimum(c_f * n - c_f * m, c_b * n - c_b * m))
    o = jnp.dot((s * dec).astype(BF16), vs_ref[...], preferred_element_type=F32)
    y = _rms(o, gn_ref[0])
    o_ref[...] = (_silu(g_ref[...]) * y).astype(BF16)


def _retention(p1, lg, cos_t, sin_t, ret_gn):
    nq = SEQ // TQ_ATT
    half = RET_DK // 2
    return pl.pallas_call(
        _ret_kernel,
        grid=(BATCH, RET_HEADS, nq),
        in_specs=[pl.BlockSpec(memory_space=pltpu.SMEM),
                  pl.BlockSpec((TQ_ATT, RET_DK), lambda b, h, i: (b * nq + i, h)),
                  pl.BlockSpec((SEQ, RET_DK), lambda b, h, i: (b, RET_HEADS + h)),
                  pl.BlockSpec((SEQ, RET_DV), lambda b, h, i: (b, 2 * RET_HEADS + h)),
                  pl.BlockSpec((TQ_ATT, RET_DV), lambda b, h, i: (b * nq + i, 3 * RET_HEADS + h)),
                  pl.BlockSpec((TQ_ATT, half), lambda b, h, i: (i, 0)),
                  pl.BlockSpec((TQ_ATT, half), lambda b, h, i: (i, 0)),
                  pl.BlockSpec((SEQ, half), lambda b, h, i: (0, 0)),
                  pl.BlockSpec((SEQ, half), lambda b, h, i: (0, 0)),
                  pl.BlockSpec((1, 1, RET_DV), lambda b, h, i: (h, 0, 0))],
        out_specs=pl.BlockSpec((TQ_ATT, RET_DV), lambda b, h, i: (b * nq + i, h)),
        out_shape=jax.ShapeDtypeStruct((N_TOK, HALF), BF16),
        scratch_shapes=[pltpu.VMEM((SEQ, RET_DK), BF16), pltpu.VMEM((SEQ, RET_DV), BF16)],
        compiler_params=_cparams(("parallel", "parallel", "arbitrary")),
        name="retention",
    )(lg, p1, p1, p1, p1, cos_t, sin_t, cos_t, sin_t, ret_gn.reshape(RET_HEADS, 1, RET_DV))


def _swa_kernel(sink_ref, q_ref, k_ref, v_ref, bias_ref, o_ref):
    kv = pl.program_id(1)
    nb = SEQ // WINDOW
    w = WINDOW

    def rows(ref, blk):
        return ref[pl.ds(pl.multiple_of(blk * w, w), w), :].astype(BF16)

    col = lax.broadcasted_iota(I32, (w, 3 * w), 1)
    for j in range(SWA_BLOCKS):
        n = pl.program_id(2) * SWA_BLOCKS + j
        prev = jnp.maximum(n - 1, 0)
        nxt = jnp.minimum(n + 1, nb - 1)
        kw = jnp.concatenate([rows(k_ref, prev), rows(k_ref, n), rows(k_ref, nxt)], axis=0)
        vw = jnp.concatenate([rows(v_ref, prev), rows(v_ref, n), rows(v_ref, nxt)], axis=0)
        qb = q_ref[j * w:(j + 1) * w, :]
        q4 = jnp.concatenate([qb[:, g * SWA_HD:(g + 1) * SWA_HD] for g in range(SWA_G)], axis=0).astype(BF16)
        s = lax.dot_general(q4, kw, (((1,), (1,)), ((), ())), preferred_element_type=F32) * (SWA_HD ** -0.5)
        outside = ((col < w) & (n == 0)) | ((col >= 2 * w) & (n == nb - 1))
        for g in range(SWA_G):
            sg = jnp.where(outside, NEG_BIG, s[g * w:(g + 1) * w] + bias_ref[g])
            sink = sink_ref[kv * SWA_G + g]
            m = jnp.maximum(jnp.max(sg, axis=-1, keepdims=True), sink)
            p = jnp.exp(sg - m)
            denom = jnp.sum(p, axis=-1, keepdims=True) + jnp.exp(sink - m)
            o = jnp.dot((p / denom).astype(BF16), vw, preferred_element_type=F32)
            o_ref[j * w:(j + 1) * w, g * SWA_HD:(g + 1) * SWA_HD] = o.astype(BF16)


def _swa(p1, sinks, bias):
    nb = SEQ // WINDOW
    qcols = SWA_G * SWA_HD
    q_blk0 = (4 * RET_HEADS * RET_DK) // qcols
    k_blk0 = (4 * RET_HEADS * RET_DK + SWA_HEADS * SWA_HD) // SWA_HD
    v_blk0 = k_blk0 + SWA_KV_HEADS
    return pl.pallas_call(
        _swa_kernel,
        grid=(BATCH, SWA_KV_HEADS, nb // SWA_BLOCKS),
        in_specs=[pl.BlockSpec(memory_space=pltpu.SMEM),
                  pl.BlockSpec((SWA_BLOCKS * WINDOW, qcols), lambda b, kv, n: (b * (nb // SWA_BLOCKS) + n, q_blk0 + kv)),
                  pl.BlockSpec((SEQ, SWA_HD), lambda b, kv, n: (b, k_blk0 + kv)),
                  pl.BlockSpec((SEQ, SWA_HD), lambda b, kv, n: (b, v_blk0 + kv)),
                  pl.BlockSpec((SWA_G, WINDOW, 3 * WINDOW), lambda b, kv, n: (kv, 0, 0))],
        out_specs=pl.BlockSpec((SWA_BLOCKS * WINDOW, qcols), lambda b, kv, n: (b * (nb // SWA_BLOCKS) + n, kv)),
        out_shape=jax.ShapeDtypeStruct((N_TOK, HALF), BF16),
        compiler_params=_cparams(("parallel", "parallel", "parallel")),
        name="swa",
    )(sinks, p1, p1, p1, bias)


def _outproj_kernel(a_ref, b_ref, wa_ref, wb_ref, x_ref, gm_ref, g_ref, sc_ref, sh_ref, wr_ref,
                    x1_ref, hp_ref, lg_ref):
    wr = wr_ref[...]
    for r in range(TM_OUT // TM_OUT_SUB):
        rows = pl.ds(r * TM_OUT_SUB, TM_OUT_SUB)
        mixed = (jnp.dot(a_ref[rows, :], wa_ref[...], preferred_element_type=F32)
                 + jnp.dot(b_ref[rows, :], wb_ref[...], preferred_element_type=F32))
        x1 = x_ref[rows, :] + gm_ref[0] * mixed
        x1_ref[rows, :] = x1
        hf = _rms(x1, g_ref[...]) * (1.0 + sc_ref[0]) + sh_ref[0]
        _store_token_tiles(hp_ref, _pack_bf16_pair(hf[:, :PACK_W], hf[:, PACK_W:]), r * TM_OUT_SUB)
        h_hi = hf.astype(BF16)
        h_lo = (hf - h_hi.astype(F32)).astype(BF16)
        t_hi = jnp.dot(h_hi, wr, preferred_element_type=F32)
        t_lo = jnp.dot(h_lo, wr, preferred_element_type=F32)
        lg_ref[rows, :] = (t_hi[:, :LANES] + t_hi[:, LANES:]) + (t_lo[:, :LANES] + t_lo[:, LANES:])


def _out_projection(a, b, w_out_bf16, x, g_m, gain, scale, shift, w_router_pad):
    per_b = SEQ // TM_OUT
    vec = pl.BlockSpec((1, 1, D_MODEL), lambda i: (i // per_b, 0, 0))
    return pl.pallas_call(
        _outproj_kernel,
        grid=(N_TOK // TM_OUT,),
        in_specs=[pl.BlockSpec((TM_OUT, HALF), lambda i: (i, 0)),
                  pl.BlockSpec((TM_OUT, HALF), lambda i: (i, 0)),
                  pl.BlockSpec((HALF, D_MODEL), lambda i: (0, 0)),
                  pl.BlockSpec((HALF, D_MODEL), lambda i: (1, 0)),
                  pl.BlockSpec((TM_OUT, D_MODEL), lambda i: (i, 0)),
                  vec,
                  pl.BlockSpec((1, D_MODEL), lambda i: (0, 0)),
                  vec, vec,
                  pl.BlockSpec((D_MODEL, 2 * LANES), lambda i: (0, 0))],
        out_specs=[pl.BlockSpec((TM_OUT, D_MODEL), lambda i: (i, 0)),
                   pl.BlockSpec((TM_OUT * TOK_ROWS, LANES), lambda i: (i, 0)),
                   pl.BlockSpec((TM_OUT, LANES), lambda i: (i, 0))],
        out_shape=[jax.ShapeDtypeStruct((N_TOK, D_MODEL), F32),
                   jax.ShapeDtypeStruct((N_TOK * TOK_ROWS, LANES), U32),
                   jax.ShapeDtypeStruct((N_TOK, LANES), F32)],
        compiler_params=_cparams(("parallel",)),
        name="out_proj",
    )(a, b, w_out_bf16, w_out_bf16, x, g_m, gain.reshape(1, D_MODEL), scale, shift, w_router_pad)


def _route_kernel(lg_ref, bias_ref, lslot_ref, w_ref, tab_ref, cnt_ref):
    step = pl.program_id(0)

    @pl.when(step == 0)
    def _():
        cnt_ref[...] = jnp.zeros(cnt_ref.shape, F32)

    for j in range(ROUTE_TILES):
        _route_tile(j, lg_ref, bias_ref, lslot_ref, w_ref, tab_ref, cnt_ref)


def _route_tile(j, lg_ref, bias_ref, lslot_ref, w_ref, tab_ref, cnt_ref):
    t = T_ROUTE
    cols = slice(j * t, (j + 1) * t)
    scores = jax.nn.sigmoid(lg_ref[cols, :].T[:N_EXPERTS])
    biased = scores + bias_ref[...]
    sub = lax.broadcasted_iota(I32, (GROUP_SIZE, t), 0).astype(F32)
    ninf = -jnp.inf

    def first_argmax(v, idx, n):
        m = jnp.max(v, axis=0, keepdims=True)
        return m, jnp.min(jnp.where(v == m, idx, float(n)), axis=0, keepdims=True)

    gs = []
    for g in range(N_GROUPS):
        bg = biased[g * GROUP_SIZE:(g + 1) * GROUP_SIZE]
        m1, i1 = first_argmax(bg, sub, GROUP_SIZE)
        m2 = jnp.max(jnp.where(sub == i1, ninf, bg), axis=0, keepdims=True)
        gs.append(m1 + m2)
    cur = jnp.concatenate(gs, axis=0)

    gmask = jnp.zeros((N_GROUPS, t), F32)
    for _ in range(TOP_GROUPS):
        _, i = first_argmax(cur, sub, N_GROUPS)
        pick = sub == i
        gmask = jnp.where(pick, 1.0, gmask)
        cur = jnp.where(pick, ninf, cur)

    eid = lax.broadcasted_iota(I32, (N_EXPERTS, t), 0).astype(F32)
    emask = jnp.concatenate([jnp.broadcast_to(gmask[g:g + 1], (GROUP_SIZE, t)) for g in range(N_GROUPS)], axis=0)
    cur = jnp.where(emask > 0.5, biased, ninf)
    sels, ws = [], []
    onehot = jnp.zeros((N_EXPERTS, t), F32)
    for _ in range(TOP_K):
        _, i = first_argmax(cur, eid, N_EXPERTS)
        pick = eid == i
        sels.append(pick)
        ws.append(jnp.sum(jnp.where(pick, scores, 0.0), axis=0, keepdims=True))
        onehot = jnp.where(pick, 1.0, onehot)
        cur = jnp.where(pick, ninf, cur)
    wsum = ws[0]
    for k in range(1, TOP_K):
        wsum = wsum + ws[k]

    r = lax.broadcasted_iota(I32, (t, t), 0)
    c = lax.broadcasted_iota(I32, (t, t), 1)
    tri = (r < c).astype(BF16)
    earlier = jnp.dot(onehot.astype(BF16), tri, preferred_element_type=F32)
    tile_cnt = jnp.broadcast_to(jnp.sum(onehot, axis=1, keepdims=True), (N_EXPERTS, LANES))
    er = lax.broadcasted_iota(I32, (N_EXPERTS, N_EXPERTS), 0)
    ec = lax.broadcasted_iota(I32, (N_EXPERTS, N_EXPERTS), 1)
    run_start = jnp.dot((ec < er).astype(BF16), tile_cnt.astype(BF16), preferred_element_type=F32)
    pos = earlier + run_start[:, 0:1]
    lslots = [jnp.sum(jnp.where(sels[k], pos, 0.0), axis=0, keepdims=True) for k in range(TOP_K)]

    lslot_ref[:, cols] = jnp.concatenate(lslots, axis=0).astype(I32)
    w_ref[:, cols] = jnp.concatenate([w / wsum * ROUTE_SCALE for w in ws], axis=0)
    tab_ref[j, 0] = tile_cnt
    tab_ref[j, 1] = cnt_ref[...]
    tab_ref[j, 2] = run_start
    cnt_ref[...] = cnt_ref[...] + tile_cnt


def _route(logits, router_bias):
    ntiles = N_TOK // T_ROUTE
    return pl.pallas_call(
        _route_kernel,
        grid=(ntiles // ROUTE_TILES,),
        in_specs=[pl.BlockSpec((ROUTE_TILES * T_ROUTE, LANES), lambda i: (i, 0)),
                  pl.BlockSpec((N_EXPERTS, 1), lambda i: (0, 0))],
        out_specs=[pl.BlockSpec((TOP_K, ROUTE_TILES * T_ROUTE), lambda i: (0, i)),
                   pl.BlockSpec((TOP_K, ROUTE_TILES * T_ROUTE), lambda i: (0, i)),
                   pl.BlockSpec((ROUTE_TILES, 3, N_EXPERTS, LANES), lambda i: (i, 0, 0, 0)),
                   pl.BlockSpec((N_EXPERTS, LANES), lambda i: (0, 0))],
        out_shape=[jax.ShapeDtypeStruct((TOP_K, N_TOK), I32),
                   jax.ShapeDtypeStruct((TOP_K, N_TOK), F32),
                   jax.ShapeDtypeStruct((ntiles, 3, N_EXPERTS, LANES), F32),
                   jax.ShapeDtypeStruct((N_EXPERTS, LANES), F32)],
        compiler_params=_cparams(("arbitrary",)),
        name="route",
    )(logits, router_bias.reshape(N_EXPERTS, 1))


def _dispatch_kernel(lslot_ref, rcnt_ref, rloc_ref, rglb_ref, zstart_ref, zlen_ref, hp_ref, xs_ref,
                     zero_ref, loc_ref, sem, zsem):
    step = pl.program_id(0)
    nsteps = pl.num_programs(0)

    def zero_copy(z):
        start = pl.multiple_of(zstart_ref[z] * TOK_ROWS, TOK_ROWS)
        n = zlen_ref[z] * TOK_ROWS
        return pltpu.make_async_copy(zero_ref.at[pl.ds(0, n)], xs_ref.at[pl.ds(start, n)], zsem)

    @pl.when(step == 0)
    def _():
        zero_ref[...] = jnp.zeros(zero_ref.shape, U32)

        def fill(z, _):
            @pl.when(zlen_ref[z] > 0)
            def _():
                zero_copy(z).start()
            return 0

        lax.fori_loop(0, N_ZERO_RANGES, fill, 0)

    @pl.when(step == nsteps - 1)
    def _():
        def fill_wait(z, _):
            @pl.when(zlen_ref[z] > 0)
            def _():
                zero_copy(z).wait()
            return 0

        lax.fori_loop(0, N_ZERO_RANGES, fill_wait, 0)

    cur = step % 2

    def tile_wait(slot):
        half = pl.ds(pl.multiple_of(slot * LOCAL_ROWS, LOCAL_ROWS), LOCAL_ROWS)
        pltpu.make_async_copy(loc_ref.at[half], xs_ref.at[pl.ds(0, LOCAL_ROWS)], sem.at[slot]).wait()

    @pl.when(step >= 2)
    def _():
        tile_wait(cur)

    base = step * T_DISP

    def place(t, _):
        row = hp_ref[pl.ds(pl.multiple_of(t * TOK_ROWS, TOK_ROWS), TOK_ROWS), :]
        for k in range(TOP_K):
            dst = pl.multiple_of(lslot_ref[(base + t) * TOP_K + k], TOK_ROWS)
            loc_ref[pl.ds(dst, TOK_ROWS), :] = row
        return 0

    lax.fori_loop(0, T_DISP, place, 0, unroll=2)

    def run(e, _):
        n = rcnt_ref[step * N_EXPERTS + e]

        @pl.when(n > 0)
        def _():
            src = pl.multiple_of(rloc_ref[step * N_EXPERTS + e], TOK_ROWS)
            dst = pl.multiple_of(rglb_ref[step * N_EXPERTS + e] * TOK_ROWS, TOK_ROWS)
            pltpu.make_async_copy(loc_ref.at[pl.ds(src, n * TOK_ROWS)],
                                  xs_ref.at[pl.ds(dst, n * TOK_ROWS)], sem.at[cur]).start()
        return 0

    lax.fori_loop(0, N_EXPERTS, run, 0)

    @pl.when(step == nsteps - 1)
    def _():
        tile_wait(cur)

        @pl.when(nsteps > 1)
        def _():
            tile_wait(1 - cur)


def _dispatch(lslot_flat, run_cnt, run_loc, run_glb, zero_start, zero_len, hp):
    return pl.pallas_call(
        _dispatch_kernel,
        grid_spec=pltpu.PrefetchScalarGridSpec(
            num_scalar_prefetch=6,
            grid=(N_TILES,),
            in_specs=[pl.BlockSpec((T_DISP * TOK_ROWS, LANES), lambda i, *_: (i, 0))],
            out_specs=pl.BlockSpec(memory_space=pl.ANY),
            scratch_shapes=[pltpu.VMEM((TM_EXP * TOK_ROWS, LANES), U32),
                            pltpu.VMEM((2 * LOCAL_ROWS, LANES), U32),
                            pltpu.SemaphoreType.DMA((2,)),
                            pltpu.SemaphoreType.DMA]),
        out_shape=jax.ShapeDtypeStruct((P_ROWS * TOK_ROWS, LANES), U32),
        compiler_params=_cparams(("arbitrary",)),
        name="dispatch",
    )(lslot_flat, run_cnt, run_loc, run_glb, zero_start, zero_len, hp)


def _expert_kernel(be_ref, bv_ref, nx_ref, par_ref, xs_ref, wg_hbm, wu_hbm, wd_hbm, ys_ref,
                   wgs_ref, wus_ref, wds_ref, wgb_ref, wub_ref, wdb_ref, sem, *, layer):
    i = pl.program_id(0)
    e = be_ref[i]
    par = par_ref[i]
    changed = jnp.logical_or(i == 0, e != be_ref[jnp.maximum(i - 1, 0)])

    def stage(expert, slot):
        return (pltpu.make_async_copy(wg_hbm.at[layer, expert], wgs_ref, sem.at[0]),
                pltpu.make_async_copy(wu_hbm.at[layer, expert], wus_ref.at[slot], sem.at[1]),
                pltpu.make_async_copy(wd_hbm.at[layer, expert], wds_ref.at[slot], sem.at[2]))

    @pl.when(changed)
    def _():
        @pl.when(i == 0)
        def _():
            for c in stage(e, par):
                c.start()

        for c in stage(e, par):
            c.wait()
        rows = CAST_VREGS * SUBLANES * LANES // EXPERT_FF

        def cast_piece(c, _):
            sl = pl.ds(pl.multiple_of(c * rows, rows), rows)
            wgb_ref[sl, :] = wgs_ref[sl, :].astype(BF16)
            return 0

        lax.fori_loop(0, D_MODEL // rows, cast_piece, 0, unroll=2)

        @pl.when(nx_ref[i] >= 0)
        def _():
            for c in stage(nx_ref[i], 1 - par):
                c.start()

    def cast_up_down():
        for src, dst in ((wus_ref, wub_ref), (wds_ref, wdb_ref)):
            rows = src.shape[1] // CAST_PIECES
            for c in range(CAST_PIECES):
                dst[c * rows:(c + 1) * rows, :] = src[par, c * rows:(c + 1) * rows, :].astype(BF16)

    def sub_block(row0, nrows):
        xs_sub = xs_ref.at[pl.ds(row0 * TOK_ROWS, nrows * TOK_ROWS)]
        lo, hi = _unpack_bf16_pair(_load_token_tiles(xs_sub, nrows))
        lo = lo.astype(BF16)
        hi = hi.astype(BF16)
        hg = (jnp.dot(lo, wgb_ref[0:PACK_W], preferred_element_type=F32)
              + jnp.dot(hi, wgb_ref[PACK_W:D_MODEL], preferred_element_type=F32))
        hu = (jnp.dot(lo, wub_ref[0:PACK_W], preferred_element_type=F32)
              + jnp.dot(hi, wub_ref[PACK_W:D_MODEL], preferred_element_type=F32))
        act = (_silu(hg) * hu).astype(BF16)
        y = jnp.dot(act, wdb_ref[...], preferred_element_type=F32)
        _store_token_tiles(ys_ref, _pack_bf16_pair(y[:, :PACK_W], y[:, PACK_W:]), row0)

    valid_rows = bv_ref[i]
    full = valid_rows > TM_EXP - TM_EXP_TAIL
    unchanged = jnp.logical_not(changed)

    @pl.when(jnp.logical_and(full, changed))
    def _():
        cast_up_down()
        sub_block(0, TM_EXP_SUB)
        sub_block(TM_EXP_SUB, TM_EXP_SUB)

    @pl.when(jnp.logical_and(full, unchanged))
    def _():
        sub_block(0, TM_EXP_SUB)
        sub_block(TM_EXP_SUB, TM_EXP_SUB)

    @pl.when(jnp.logical_not(full))
    def _():
        @pl.when(changed)
        def _():
            cast_up_down()
            sub_block(0, TM_EXP_TAIL)

        for r in range(TM_EXP // TM_EXP_TAIL):
            if r == 0:
                compute = jnp.logical_and(unchanged, valid_rows > 0)
                skip = jnp.logical_and(unchanged, valid_rows <= 0)
            else:
                compute = valid_rows > r * TM_EXP_TAIL
                skip = valid_rows <= r * TM_EXP_TAIL

            @pl.when(compute)
            def _():
                sub_block(r * TM_EXP_TAIL, TM_EXP_TAIL)

            @pl.when(skip)
            def _():
                ys_ref[pl.ds(r * TM_EXP_TAIL * TOK_ROWS, TM_EXP_TAIL * TOK_ROWS), :] = jnp.zeros(
                    (TM_EXP_TAIL * TOK_ROWS, LANES), U32)


def _experts(layer, block_expert, block_rows, next_expert, block_parity, xs, w_gate, w_up, w_down):
    return pl.pallas_call(
        functools.partial(_expert_kernel, layer=layer),
        grid_spec=pltpu.PrefetchScalarGridSpec(
            num_scalar_prefetch=4,
            grid=(NB_EXP,),
            in_specs=[pl.BlockSpec((TM_EXP * TOK_ROWS, LANES), lambda i, be, bv, *_: (jnp.where(bv[i] > 0, i, 0), 0)),
                      pl.BlockSpec(memory_space=pl.ANY),
                      pl.BlockSpec(memory_space=pl.ANY),
                      pl.BlockSpec(memory_space=pl.ANY)],
            out_specs=pl.BlockSpec((TM_EXP * TOK_ROWS, LANES), lambda i, *_: (i, 0)),
            scratch_shapes=[pltpu.VMEM((D_MODEL, EXPERT_FF), F32),
                            pltpu.VMEM((2, D_MODEL, EXPERT_FF), F32),
                            pltpu.VMEM((2, EXPERT_FF, D_MODEL), F32),
                            pltpu.VMEM((D_MODEL, EXPERT_FF), BF16),
                            pltpu.VMEM((D_MODEL, EXPERT_FF), BF16),
                            pltpu.VMEM((EXPERT_FF, D_MODEL), BF16),
                            pltpu.SemaphoreType.DMA((3,))]),
        out_shape=jax.ShapeDtypeStruct((P_ROWS * TOK_ROWS, LANES), U32),
        compiler_params=_cparams(("arbitrary",)),
        name="experts",
    )(block_expert, block_rows, next_expert, block_parity, xs, w_gate, w_up, w_down)


def _combine_kernel(lslot_ref, rcnt_ref, rloc_ref, rglb_ref, w_ref, ys_ref, hp_ref, x1_ref, gf_ref,
                    wsg_ref, wsu_ref, wsd_ref, fn_ref, o_ref, buf_ref, mlo_ref, mhi_ref, sem, *, final_norm):
    i = pl.program_id(0)
    nsteps = pl.num_programs(0)

    def issue(step, slot):
        def run(e, _):
            n = rcnt_ref[step * N_EXPERTS + e]

            @pl.when(n > 0)
            def _():
                src = pl.multiple_of(rglb_ref[step * N_EXPERTS + e] * TOK_ROWS, TOK_ROWS)
                dst = pl.multiple_of(rloc_ref[step * N_EXPERTS + e], TOK_ROWS)
                pltpu.make_async_copy(ys_ref.at[pl.ds(src, n * TOK_ROWS)],
                                      buf_ref.at[pl.ds(dst, n * TOK_ROWS)], sem.at[slot]).start()
            return 0

        lax.fori_loop(0, N_EXPERTS, run, 0)

    @pl.when(i == 0)
    def _():
        issue(0, 0)

    @pl.when(i + 1 < nsteps)
    def _():
        issue(i + 1, (i + 1) % 2)

    cur = i % 2
    cur_half = pl.ds(pl.multiple_of(cur * LOCAL_ROWS, LOCAL_ROWS), LOCAL_ROWS)
    pltpu.make_async_copy(ys_ref.at[pl.ds(0, LOCAL_ROWS)], buf_ref.at[cur_half], sem.at[cur]).wait()

    base = i * T_DISP

    def token(t, _):
        acc_lo = jnp.zeros((TOK_ROWS, LANES), F32)
        acc_hi = jnp.zeros((TOK_ROWS, LANES), F32)
        for k in range(TOP_K):
            idx = (base + t) * TOP_K + k
            src = pl.multiple_of(lslot_ref[idx], TOK_ROWS)
            lo, hi = _unpack_bf16_pair(buf_ref[pl.ds(src, TOK_ROWS), :])
            wk = w_ref[idx]
            acc_lo = acc_lo + wk * lo
            acc_hi = acc_hi + wk * hi
        dst = pl.ds(pl.multiple_of(t * TOK_ROWS, TOK_ROWS), TOK_ROWS)
        mlo_ref[dst, :] = acc_lo
        mhi_ref[dst, :] = acc_hi
        return 0

    lax.fori_loop(0, T_DISP, token, 0, unroll=2)
    moe_lo = _load_token_tiles(mlo_ref, T_DISP)
    moe_hi = _load_token_tiles(mhi_ref, T_DISP)

    hlo, hhi = _unpack_bf16_pair(_load_token_tiles(hp_ref, T_DISP))
    hlo = hlo.astype(BF16)
    hhi = hhi.astype(BF16)
    sg = (jnp.dot(hlo, wsg_ref[0:PACK_W], preferred_element_type=F32)
          + jnp.dot(hhi, wsg_ref[PACK_W:D_MODEL], preferred_element_type=F32))
    su = (jnp.dot(hlo, wsu_ref[0:PACK_W], preferred_element_type=F32)
          + jnp.dot(hhi, wsu_ref[PACK_W:D_MODEL], preferred_element_type=F32))
    shared = jnp.dot((_silu(sg) * su).astype(BF16), wsd_ref[...], preferred_element_type=F32)
    moe = jnp.concatenate([moe_lo, moe_hi], axis=1)
    out = x1_ref[...] + gf_ref[0] * (moe + shared)
    if final_norm:
        out = _rms(out, fn_ref[...])
    o_ref[...] = out


def _combine(lslot_flat, run_cnt, run_loc, run_glb, w_flat, ys, hp, x1, g_f, wsg, wsu, wsd, final_gain, final_norm):
    per_b = SEQ // T_DISP
    return pl.pallas_call(
        functools.partial(_combine_kernel, final_norm=final_norm),
        grid_spec=pltpu.PrefetchScalarGridSpec(
            num_scalar_prefetch=4,
            grid=(N_TILES,),
            in_specs=[pl.BlockSpec(memory_space=pltpu.SMEM),
                      pl.BlockSpec(memory_space=pl.ANY),
                      pl.BlockSpec((T_DISP * TOK_ROWS, LANES), lambda i, *_: (i, 0)),
                      pl.BlockSpec((T_DISP, D_MODEL), lambda i, *_: (i, 0)),
                      pl.BlockSpec((1, 1, D_MODEL), lambda i, *_: (i // per_b, 0, 0)),
                      pl.BlockSpec((D_MODEL, SHARED_FF), lambda i, *_: (0, 0)),
                      pl.BlockSpec((D_MODEL, SHARED_FF), lambda i, *_: (0, 0)),
                      pl.BlockSpec((SHARED_FF, D_MODEL), lambda i, *_: (0, 0)),
                      pl.BlockSpec((1, D_MODEL), lambda i, *_: (0, 0))],
            out_specs=pl.BlockSpec((T_DISP, D_MODEL), lambda i, *_: (i, 0)),
            scratch_shapes=[pltpu.VMEM((2 * LOCAL_ROWS, LANES), U32),
                            pltpu.VMEM((T_DISP * TOK_ROWS, LANES), F32),
                            pltpu.VMEM((T_DISP * TOK_ROWS, LANES), F32),
                            pltpu.SemaphoreType.DMA((2,))]),
        out_shape=jax.ShapeDtypeStruct((N_TOK, D_MODEL), F32),
        compiler_params=_cparams(("arbitrary",)),
        name="combine",
    )(lslot_flat, run_cnt, run_loc, run_glb, w_flat, ys, hp, x1, g_f, wsg, wsu, wsd, final_gain.reshape(1, D_MODEL))


def _moe_layer(layer, hp, logits, x1, g_f, router_bias, w_gate, w_up, w_down, ws_gate, ws_up, ws_down,
               final_gain, final_norm):
    lslot, w_k, tables, counts = _route(logits, router_bias)
    cnt = counts[:, 0].astype(I32)
    padded = ((cnt + TM_EXP - 1) // TM_EXP) * TM_EXP
    ends = jnp.cumsum(padded)
    offsets = ends - padded
    tables = tables[:, :, :, 0].astype(I32)
    run_cnt = tables[:, 0].reshape(-1)
    run_glb = (offsets[None, :] + tables[:, 1]).reshape(-1)
    half_row0 = (jnp.arange(N_TILES, dtype=I32) % 2) * LOCAL_ROWS
    run_loc = (tables[:, 2] * TOK_ROWS + half_row0[:, None]).reshape(-1)
    lslot_flat = (lslot * TOK_ROWS + jnp.repeat(half_row0, T_DISP)[None, :]).T.reshape(N_SLOTS)
    blk_start = jnp.arange(NB_EXP, dtype=I32) * TM_EXP
    expert_ids = jnp.arange(N_EXPERTS, dtype=I32)
    nonempty = cnt > 0
    last_nonempty = jnp.max(jnp.where(nonempty, expert_ids, 0))
    block_expert = jnp.minimum(jnp.sum((blk_start[:, None] >= ends[None, :]).astype(I32), axis=1), last_nonempty)
    block_valid = (blk_start < ends[-1]).astype(I32)
    block_rows = jnp.clip((offsets + cnt)[block_expert] - blk_start, 0, TM_EXP) * block_valid
    following = jnp.where(nonempty, expert_ids, N_EXPERTS)
    following = lax.cummin(following, reverse=True)
    following = jnp.concatenate([following[1:], jnp.full((1,), N_EXPERTS, I32)])
    next_expert = jnp.where(following < N_EXPERTS, following, -1)[block_expert]
    zero_start = jnp.concatenate([offsets + cnt, blk_start])
    zero_len = jnp.concatenate([padded - cnt, (1 - block_valid) * TM_EXP])
    xs = _dispatch(lslot_flat, run_cnt, run_loc, run_glb, zero_start, zero_len, hp)
    block_parity = ((jnp.cumsum(nonempty.astype(I32)) - 1) % 2)[block_expert]
    ys = _experts(layer, block_expert, block_rows, next_expert, block_parity, xs, w_gate, w_up, w_down)
    return _combine(lslot_flat, run_cnt, run_loc, run_glb, w_k.T.reshape(N_SLOTS), ys, hp, x1, g_f,
                    ws_gate.astype(BF16), ws_up.astype(BF16), ws_down.astype(BF16), final_gain, final_norm)


def _rope_tables(dim):
    inv = ROPE_THETA ** (-jnp.arange(0, dim, 2, dtype=F32) / dim)
    ang = jnp.arange(SEQ, dtype=F32)[:, None] * inv[None, :]
    return jnp.cos(ang), jnp.sin(ang)


def _rot_half_cols(w):
    half = w.shape[-1] // 2
    return jnp.concatenate([-w[..., half:], w[..., :half]], axis=-1)


def _t5_bucket(rel):
    half = REL_BUCKETS // 2
    max_exact = half // 2
    ret = (rel > 0).astype(I32) * half
    n = jnp.abs(rel)
    nf = jnp.maximum(n, 1).astype(F32)
    large = max_exact + (jnp.log(nf / max_exact) / math.log(REL_MAX_DIST / max_exact)
                         * (half - max_exact)).astype(I32)
    large = jnp.minimum(large, half - 1)
    return ret + jnp.where(n < max_exact, n, large)


def _swa_bias_table(rel_bias):
    qi = jnp.arange(WINDOW)[:, None]
    kj = jnp.arange(3 * WINDOW)[None, :]
    rel = kj - WINDOW - qi
    onehot = (_t5_bucket(rel)[:, :, None] == jnp.arange(REL_BUCKETS)).astype(F32)
    bias = jnp.einsum('qjb,bh->hqj', onehot, rel_bias.astype(F32), precision=lax.Precision.HIGHEST)
    return jnp.where((jnp.abs(rel) <= WINDOW)[None], bias, NEG_BIG)


def kernel(x, c, w_mod, b_mod, norm_mix, norm_ffn, final_norm, w_in_ab, q_lat_norm, kv_lat_norm, w_uq, w_ukv, conv_w, conv_b, lru_w_a, lru_b_a, lru_w_x, lru_b_x, lru_lambda, w_out_ab, w_in_cd, ret_gn, swa_sinks, w_out_cd, rel_bias, w_router, router_bias, w_gate, w_up, w_down, ws_gate, ws_up, ws_down):
    xf = x.reshape(N_TOK, D_MODEL)
    mod = _modulation(c, w_mod, b_mod)
    cos_r, sin_r = _rope_tables(MLA_ROPE)
    cs_tab = jnp.concatenate([cos_r, cos_r, sin_r, sin_r], axis=1)
    cos_t, sin_t = _rope_tables(RET_DK)
    lg_ret = jnp.log1p(-(2.0 ** (-5.0 - jnp.arange(RET_HEADS, dtype=F32))))

    for layer in range(DEPTH):
        sh_m, sc_m, g_m, sh_f, sc_f, g_f = [m.reshape(BATCH, 1, D_MODEL) for m in jnp.split(mod[layer], 6, axis=-1)]
        i = layer // 2
        if layer % 2 == 0:
            w = w_in_ab[i]
            o1, o2, o3, o4 = np.cumsum((MLA_Q_RANK, MLA_KV_RANK, MLA_ROPE, LRU_WIDTH)).tolist()
            w_kr = w[:, o2:o3]
            w_in = jnp.concatenate([w[:, :o2], w[:, o3:], w_kr, _rot_half_cols(w_kr)], axis=1).astype(BF16)
            p0 = _in_projection(xf, norm_mix[layer], sc_m, sh_m, w_in, "in_proj_ab")
            wq = w_uq[i].reshape(MLA_Q_RANK, MLA_HEADS, MLA_NOPE + MLA_ROPE)
            wq_r = wq[:, :, MLA_NOPE:]
            wq = jnp.concatenate([wq, _rot_half_cols(wq_r)], axis=-1).reshape(MLA_Q_RANK, MLA_HEADS * MLA_QK)
            q, k, v = _mla_up(p0, q_lat_norm[i], kv_lat_norm[i], wq.astype(BF16), w_ukv[i].astype(BF16), cs_tab)
            a_out = _mla_attention(q, k, v)
            w_gates = jnp.concatenate([lru_w_a[i, 0], lru_w_x[i, 0], lru_w_a[i, 1], lru_w_x[i, 1]], axis=-1).astype(BF16)
            b_gates = jnp.concatenate([b.reshape(LRU_BLOCKS, 1, LRU_BS) for b in
                                       (lru_b_a[i, 0], lru_b_x[i, 0], lru_b_a[i, 1], lru_b_x[i, 1])], axis=-1)
            b_out = _rglru(p0, conv_w[i], conv_b[i], w_gates, b_gates, lru_lambda[i])
            w_out = w_out_ab[i].astype(BF16)
        else:
            p1 = _in_projection(xf, norm_mix[layer], sc_m, sh_m, w_in_cd[i].astype(BF16), "in_proj_cd")
            a_out = _retention(p1, lg_ret, cos_t, sin_t, ret_gn[i])
            b_out = _swa(p1, swa_sinks[i], _swa_bias_table(rel_bias))
            w_out = w_out_cd[i].astype(BF16)
        w_r = jnp.pad(w_router[layer], ((0, 0), (0, LANES - N_EXPERTS)))
        w_r_hi = w_r.astype(BF16)
        w_router_pad = jnp.concatenate([w_r_hi, (w_r - w_r_hi.astype(F32)).astype(BF16)], axis=1)
        x1, hp, logits = _out_projection(a_out, b_out, w_out, xf, g_m, norm_ffn[layer], sc_f, sh_f, w_router_pad)
        xf = _moe_layer(layer, hp, logits, x1, g_f, router_bias[layer], w_gate, w_up, w_down,
                        ws_gate[layer], ws_up[layer], ws_down[layer], final_norm, layer == DEPTH - 1)
    return xf.reshape(BATCH, SEQ, D_MODEL)
```

```python
import functools
import math

import numpy as np
import jax
import jax.numpy as jnp
from jax import lax
from jax.experimental import pallas as pl
from jax.experimental.pallas import tpu as pltpu

F32 = jnp.float32
BF16 = jnp.bfloat16
I32 = jnp.int32
U32 = jnp.uint32

D_MODEL = 2048
BATCH = 4
SEQ = 2048
DEPTH = 2
N_TOK = BATCH * SEQ
HALF = D_MODEL // 2
MLA_NOPE = 128
MLA_ROPE = 64
MLA_V = 128
MLA_HEADS = HALF // MLA_V
MLA_Q_RANK = D_MODEL // 4
MLA_KV_RANK = D_MODEL // 4
MLA_QK = 256
LRU_WIDTH = HALF
LRU_BLOCKS = 8
LRU_BS = LRU_WIDTH // LRU_BLOCKS
LRU_CONV = 4
LRU_C = 8.0
RET_DK = 256
RET_DV = 256
RET_HEADS = HALF // RET_DV
SWA_HD = 128
SWA_HEADS = HALF // SWA_HD
SWA_KV_HEADS = 2
SWA_G = SWA_HEADS // SWA_KV_HEADS
WINDOW = 128
REL_BUCKETS = 32
REL_MAX_DIST = 128
N_EXPERTS = 64
TOP_K = 8
N_GROUPS = 8
GROUP_SIZE = N_EXPERTS // N_GROUPS
TOP_GROUPS = 4
EXPERT_FF = D_MODEL // 4
SHARED_FF = D_MODEL // 4
ROUTE_SCALE = 2.5
ROPE_THETA = 10000.0
EPS = 1e-6
NEG_BIG = -1e30

LANES = 128
SUBLANES = 8
VMEM_LIMIT = 52 * 2**20

TM_PROJ_AB = 512
TM_PROJ_CD = 256
TM_UP = 512
TQ_ATT = 1024
TQ_MLA = 2048
TQ_SUB = 256
SWA_BLOCKS = 8
TM_OUT = 512
TM_OUT_SUB = 256
T_DISP = 256
T_ROUTE = T_DISP
ROUTE_TILES = 4
N_TILES = N_TOK // T_DISP
TILE_ROWS = T_DISP * TOP_K
TM_EXP = 512
TM_EXP_SUB = 256
TM_EXP_TAIL = 128
assert TM_EXP == 2 * TM_EXP_SUB and TM_EXP % TM_EXP_TAIL == 0
CAST_VREGS = 32
CAST_PIECES = 16
N_SLOTS = N_TOK * TOP_K
NB_EXP = N_SLOTS // TM_EXP + N_EXPERTS
P_ROWS = NB_EXP * TM_EXP
N_ZERO_RANGES = N_EXPERTS + NB_EXP
PACK_W = D_MODEL // 2
TOK_ROWS = PACK_W // LANES
assert TOK_ROWS == SUBLANES
LOCAL_ROWS = TILE_ROWS * TOK_ROWS

LRU_SEG = 260
LRU_ROWS = SUBLANES * LRU_SEG
assert LRU_ROWS >= SEQ and LRU_SEG % 8 == 4


def _cparams(sem, vmem=VMEM_LIMIT):
    return pltpu.CompilerParams(dimension_semantics=sem, vmem_limit_bytes=vmem)


def _sigmoid(x):
    return 0.5 * jnp.tanh(0.5 * x) + 0.5


def _silu(x):
    return x * _sigmoid(x)


def _rms(x, g):
    return x * lax.rsqrt(jnp.mean(x * x, axis=-1, keepdims=True) + EPS) * g


def _pack_bf16_pair(lo, hi):
    lo_b = lax.bitcast_convert_type(lo.astype(BF16).astype(F32), U32)
    hi_b = lax.bitcast_convert_type(hi.astype(BF16).astype(F32), U32)
    return (hi_b & jnp.uint32(0xFFFF0000)) | (lo_b >> 16)


def _unpack_bf16_pair(w):
    lo = lax.bitcast_convert_type(w << 16, F32)
    hi = lax.bitcast_convert_type(w & jnp.uint32(0xFFFF0000), F32)
    return lo, hi


def _store_token_tiles(ref, packed, tok0=0):
    t = packed.shape[0]
    for s in range(TOK_ROWS):
        ref[pl.ds(tok0 * TOK_ROWS + s, t, stride=TOK_ROWS), :] = packed[:, s * LANES:(s + 1) * LANES]


def _load_token_tiles(ref, t):
    return jnp.concatenate([ref[pl.ds(s, t, stride=TOK_ROWS), :] for s in range(TOK_ROWS)], axis=1)


def _mod_kernel(c_ref, w_ref, b_ref, o_ref):
    c = c_ref[...]
    ca = _silu(c).astype(BF16)
    o_ref[0] = jnp.dot(ca, w_ref[0].astype(BF16), preferred_element_type=F32) + b_ref[0]


def _modulation(c, w_mod, b_mod):
    tn = 1024
    cp = jnp.pad(c, ((0, SUBLANES - BATCH), (0, 0)))
    out = pl.pallas_call(
        _mod_kernel,
        grid=(DEPTH, 6 * D_MODEL // tn),
        in_specs=[pl.BlockSpec((SUBLANES, D_MODEL), lambda l, j: (0, 0)),
                  pl.BlockSpec((1, D_MODEL, tn), lambda l, j: (l, 0, j)),
                  pl.BlockSpec((1, 1, tn), lambda l, j: (l, 0, j))],
        out_specs=pl.BlockSpec((1, SUBLANES, tn), lambda l, j: (l, 0, j)),
        out_shape=jax.ShapeDtypeStruct((DEPTH, SUBLANES, 6 * D_MODEL), F32),
        compiler_params=_cparams(("parallel", "parallel")),
        name="adaln_mod",
    )(cp, w_mod, b_mod.reshape(DEPTH, 1, 6 * D_MODEL))
    return out[:, :BATCH]


def _inproj_kernel(x_ref, g_ref, sc_ref, sh_ref, w_ref, o_ref):
    sub = x_ref.shape[0] // 2
    for r in range(2):
        rows = pl.ds(r * sub, sub)
        y = _rms(x_ref[rows, :], g_ref[...])
        h = (y * (1.0 + sc_ref[0]) + sh_ref[0]).astype(BF16)
        o_ref[rows, :] = jnp.dot(h, w_ref[...], preferred_element_type=F32)


def _in_projection(x, gain, scale, shift, w_bf16, tm, name):
    p = w_bf16.shape[1]
    per_b = SEQ // tm
    return pl.pallas_call(
        _inproj_kernel,
        grid=(N_TOK // tm,),
        in_specs=[pl.BlockSpec((tm, D_MODEL), lambda i: (i, 0)),
                  pl.BlockSpec((1, D_MODEL), lambda i: (0, 0)),
                  pl.BlockSpec((1, 1, D_MODEL), lambda i: (i // per_b, 0, 0)),
                  pl.BlockSpec((1, 1, D_MODEL), lambda i: (i // per_b, 0, 0)),
                  pl.BlockSpec((D_MODEL, p), lambda i: (0, 0), pipeline_mode=pl.Buffered(1))],
        out_specs=pl.BlockSpec((tm, p), lambda i: (i, 0)),
        out_shape=jax.ShapeDtypeStruct((N_TOK, p), F32),
        compiler_params=_cparams(("parallel",), 56 * 2**20),
        name=name,
    )(x, gain.reshape(1, D_MODEL), scale, shift, w_bf16)


def _mla_up_kernel(ql_ref, kvl_ref, kr_ref, qn_ref, kvn_ref, wq_ref, wkv_ref, cs_ref, q_ref, k_ref, v_ref):
    scale = (MLA_NOPE + MLA_ROPE) ** -0.5 * math.log2(math.e)
    hq = _rms(ql_ref[...], qn_ref[...]).astype(BF16)
    hkv = _rms(kvl_ref[...], kvn_ref[...]).astype(BF16)
    yq = jnp.dot(hq, wq_ref[...], preferred_element_type=F32) * scale
    ykv = jnp.dot(hkv, wkv_ref[...], preferred_element_type=F32)
    cs = cs_ref[...]
    lane = lax.broadcasted_iota(I32, cs.shape, 1)

    def rope_sum(blk):
        z = blk * cs
        return z + pltpu.roll(z, MLA_ROPE, 1)

    kr = jnp.where(lane < MLA_ROPE, rope_sum(kr_ref[...]), 0.0).astype(BF16)
    ones_col = jnp.where(lane == 0, 1.0, 0.0).astype(BF16)
    for h in range(MLA_HEADS):
        c0 = h * MLA_QK
        q_ref[0, h, :, 0:MLA_NOPE] = yq[:, c0:c0 + MLA_NOPE].astype(BF16)
        q_ref[0, h, :, MLA_NOPE:MLA_QK] = rope_sum(yq[:, c0 + MLA_NOPE:c0 + MLA_QK]).astype(BF16)
        k_ref[0, h, :, 0:MLA_NOPE] = ykv[:, c0:c0 + MLA_NOPE].astype(BF16)
        k_ref[0, h, :, MLA_NOPE:MLA_QK] = kr
        v_ref[0, h, :, 0:MLA_V] = ykv[:, c0 + MLA_NOPE:c0 + MLA_QK].astype(BF16)
        v_ref[0, h, :, MLA_V:2 * MLA_V] = ones_col


def _mla_up(p0, q_norm, kv_norm, wq, wkv, cs_tab):
    per_b = SEQ // TM_UP
    qk_shape = jax.ShapeDtypeStruct((BATCH, MLA_HEADS, SEQ, MLA_QK), BF16)
    return pl.pallas_call(
        _mla_up_kernel,
        grid=(N_TOK // TM_UP,),
        in_specs=[pl.BlockSpec((TM_UP, MLA_Q_RANK), lambda i: (i, 0)),
                  pl.BlockSpec((TM_UP, MLA_KV_RANK), lambda i: (i, 1)),
                  pl.BlockSpec((TM_UP, LANES), lambda i: (i, 24)),
                  pl.BlockSpec((1, MLA_Q_RANK), lambda i: (0, 0)),
                  pl.BlockSpec((1, MLA_KV_RANK), lambda i: (0, 0)),
                  pl.BlockSpec((MLA_Q_RANK, MLA_HEADS * MLA_QK), lambda i: (0, 0)),
                  pl.BlockSpec((MLA_KV_RANK, MLA_HEADS * MLA_QK), lambda i: (0, 0)),
                  pl.BlockSpec((TM_UP, LANES), lambda i: (i % per_b, 0))],
        out_specs=[pl.BlockSpec((1, MLA_HEADS, TM_UP, MLA_QK), lambda i: (i // per_b, 0, i % per_b, 0)),
                   pl.BlockSpec((1, MLA_HEADS, TM_UP, MLA_QK), lambda i: (i // per_b, 0, i % per_b, 0)),
                   pl.BlockSpec((1, MLA_HEADS, TM_UP, 2 * MLA_V), lambda i: (i // per_b, 0, i % per_b, 0))],
        out_shape=[qk_shape, qk_shape, jax.ShapeDtypeStruct((BATCH, MLA_HEADS, SEQ, 2 * MLA_V), BF16)],
        compiler_params=_cparams(("parallel",)),
        name="mla_up",
    )(p0, p0, p0, q_norm.reshape(1, -1), kv_norm.reshape(1, -1), wq, wkv, cs_tab)


def _mla_attn_kernel(q_ref, k_ref, v_ref, o_ref):
    k = k_ref[0, 0]
    v = v_ref[0, 0]
    for r in range(TQ_MLA // TQ_SUB):
        rows = pl.ds(r * TQ_SUB, TQ_SUB)
        s = lax.dot_general(q_ref[0, 0, rows, :], k, (((1,), (1,)), ((), ())), preferred_element_type=F32)
        p = jnp.exp2(s - jnp.max(s, axis=-1, keepdims=True))
        o = jnp.dot(p.astype(BF16), v, preferred_element_type=F32)
        o_ref[rows, :] = (o[:, 0:MLA_V] / o[:, MLA_V:MLA_V + 1]).astype(BF16)


def _mla_attention(q, k, v):
    nq = SEQ // TQ_MLA
    return pl.pallas_call(
        _mla_attn_kernel,
        grid=(BATCH, MLA_HEADS, nq),
        in_specs=[pl.BlockSpec((1, 1, TQ_MLA, MLA_QK), lambda b, h, i: (b, h, i, 0)),
                  pl.BlockSpec((1, 1, SEQ, MLA_QK), lambda b, h, i: (b, h, 0, 0)),
                  pl.BlockSpec((1, 1, SEQ, 2 * MLA_V), lambda b, h, i: (b, h, 0, 0))],
        out_specs=pl.BlockSpec((TQ_MLA, MLA_V), lambda b, h, i: (b * nq + i, h)),
        out_shape=jax.ShapeDtypeStruct((N_TOK, HALF), BF16),
        compiler_params=_cparams(("parallel", "parallel", "parallel")),
        name="mla_attn",
    )(q, k, v)


def _lru_kernel(x_ref, gate_ref, cw_ref, cb_ref, wg_ref, bg_ref, lam_ref, o_ref,
                af_ref, uf_ref, ab_ref, ub_ref, hf_ref, pf_ref, hb_ref, pb_ref, hs_ref):
    x = x_ref[...]
    row = lax.broadcasted_iota(I32, x.shape, 0)

    def shifted(d):
        r = pltpu.roll(x, (-d) % SEQ, 0)
        return jnp.where((row + d >= 0) & (row + d < SEQ), r, 0.0)

    cw = cw_ref[...]
    left = LRU_CONV // 2
    xc = cb_ref[...]
    for kk in range(LRU_CONV):
        d = kk - left
        xc = xc + cw[kk:kk + 1] * (x if d == 0 else shifted(d))

    gates = jnp.dot(xc.astype(BF16), wg_ref[0], preferred_element_type=F32) + bg_ref[0]
    lam = lam_ref[...]
    z = -lam
    sp = jnp.maximum(z, 0.0) + jnp.log1p(jnp.exp(-jnp.abs(z)))
    pad_rows = LRU_ROWS - SEQ
    for d, (a_ref, u_ref) in enumerate(((af_ref, uf_ref), (ab_ref, ub_ref))):
        r = _sigmoid(gates[:, d * 256:d * 256 + LRU_BS])
        i = _sigmoid(gates[:, d * 256 + LRU_BS:(d + 1) * 256])
        a = jnp.exp(r * (-LRU_C * sp[d:d + 1]))
        a_ref[0:SEQ] = a
        u_ref[0:SEQ] = jnp.sqrt(1.0 - a * a) * (i * xc)
        a_ref[SEQ:LRU_ROWS] = jnp.zeros((pad_rows, LANES), F32)
        u_ref[SEQ:LRU_ROWS] = jnp.zeros((pad_rows, LANES), F32)

    ones = jnp.ones((SUBLANES, LANES), F32)
    zeros = jnp.zeros((SUBLANES, LANES), F32)

    def seg(t):
        return pl.ds(t, SUBLANES, stride=LRU_SEG)

    def local_scan(s, carry):
        p_f, h_f, p_b, h_b = carry
        tf = s
        tb = LRU_SEG - 1 - s
        a = af_ref[seg(tf)]
        h_f = a * h_f + uf_ref[seg(tf)]
        p_f = a * p_f
        hf_ref[seg(tf)] = h_f
        pf_ref[seg(tf)] = p_f
        a = ab_ref[seg(tb)]
        h_b = a * h_b + ub_ref[seg(tb)]
        p_b = a * p_b
        hb_ref[seg(tb)] = h_b
        pb_ref[seg(tb)] = p_b
        return p_f, h_f, p_b, h_b

    p_f, h_f, p_b, h_b = lax.fori_loop(0, LRU_SEG, local_scan, (ones, zeros, ones, zeros), unroll=4)

    rows_f = []
    c = jnp.zeros((1, LANES), F32)
    for j in range(SUBLANES):
        rows_f.append(c)
        c = p_f[j:j + 1] * c + h_f[j:j + 1]
    rows_b = [None] * SUBLANES
    c = jnp.zeros((1, LANES), F32)
    for j in range(SUBLANES - 1, -1, -1):
        rows_b[j] = c
        c = p_b[j:j + 1] * c + h_b[j:j + 1]
    sub = lax.broadcasted_iota(I32, (SUBLANES, LANES), 0)
    c_f = zeros
    c_b = zeros
    for j in range(SUBLANES):
        c_f = jnp.where(sub == j, rows_f[j], c_f)
        c_b = jnp.where(sub == j, rows_b[j], c_b)

    def fixup(t, _):
        hs_ref[seg(t)] = (hf_ref[seg(t)] + pf_ref[seg(t)] * c_f) + (hb_ref[seg(t)] + pb_ref[seg(t)] * c_b)
        return 0

    lax.fori_loop(0, LRU_SEG, fixup, 0, unroll=4)

    g = gate_ref[...]
    gelu = 0.5 * g * (1.0 + jnp.tanh(math.sqrt(2.0 / math.pi) * (g + 0.044715 * (g * g * g))))
    o_ref[...] = (gelu * hs_ref[0:SEQ]).astype(BF16)


def _rglru(p0, conv_w, conv_b, w_gates, b_gates, lam):
    scan_buf = pltpu.VMEM((LRU_ROWS, LANES), F32)
    return pl.pallas_call(
        _lru_kernel,
        grid=(BATCH, LRU_BLOCKS),
        in_specs=[pl.BlockSpec((SEQ, LRU_BS), lambda b, g: (b, 8 + g)),
                  pl.BlockSpec((SEQ, LRU_BS), lambda b, g: (b, 16 + g)),
                  pl.BlockSpec((LRU_CONV, LRU_BS), lambda b, g: (0, g)),
                  pl.BlockSpec((1, LRU_BS), lambda b, g: (0, g)),
                  pl.BlockSpec((1, LRU_BS, 4 * LRU_BS), lambda b, g: (g, 0, 0)),
                  pl.BlockSpec((1, 1, 4 * LRU_BS), lambda b, g: (g, 0, 0)),
                  pl.BlockSpec((2, LRU_BS), lambda b, g: (0, g))],
        out_specs=pl.BlockSpec((SEQ, LRU_BS), lambda b, g: (b, g)),
        out_shape=jax.ShapeDtypeStruct((N_TOK, LRU_WIDTH), BF16),
        scratch_shapes=[scan_buf] * 9,
        compiler_params=_cparams(("parallel", "parallel")),
        name="rglru",
    )(p0, p0, conv_w, conv_b.reshape(1, -1), w_gates, b_gates, lam)


def _ret_kernel(lg_ref, q_ref, k_ref, v_ref, g_ref, cq_ref, sq_ref, ck_ref, sk_ref, gn_ref, o_ref, ks_ref, vs_ref):
    h = pl.program_id(1)
    qi = pl.program_id(2)
    half = RET_DK // 2

    def rope(t, c, s):
        t1, t2 = t[:, :half], t[:, half:]
        return jnp.concatenate([t1 * c - t2 * s, t2 * c + t1 * s], axis=1)

    @pl.when(qi == 0)
    def _():
        ks_ref[...] = (rope(k_ref[...], ck_ref[...], sk_ref[...]) * (RET_DK ** -0.5)).astype(BF16)
        vs_ref[...] = v_ref[...].astype(BF16)

    q = rope(q_ref[...], cq_ref[...], sq_ref[...]).astype(BF16)
    s = lax.dot_general(q, ks_ref[...], (((1,), (1,)), ((), ())), preferred_element_type=F32)
    n = (qi * TQ_ATT + lax.broadcasted_iota(I32, (TQ_ATT, 1), 0)).astype(F32)
    m = lax.broadcasted_iota(I32, (1, SEQ), 1).astype(F32)
    c_f = lg_ref[h] * math.log2(math.e)
    c_b = -lg_ref[RET_HEADS - 1 - h] * math.log2(math.e)
    dec = jnp.exp2(jnp.minimum(c_f * n - c_f * m, c_b * n - c_b * m))
    o = jnp.dot((s * dec).astype(BF16), vs_ref[...], preferred_element_type=F32)
    y = _rms(o, gn_ref[0])
    o_ref[...] = (_silu(g_ref[...]) * y).astype(BF16)


def _retention(p1, lg, cos_t, sin_t, ret_gn):
    nq = SEQ // TQ_ATT
    half = RET_DK // 2
    return pl.pallas_call(
        _ret_kernel,
        grid=(BATCH, RET_HEADS, nq),
        in_specs=[pl.BlockSpec(memory_space=pltpu.SMEM),
                  pl.BlockSpec((TQ_ATT, RET_DK), lambda b, h, i: (b * nq + i, h)),
                  pl.BlockSpec((SEQ, RET_DK), lambda b, h, i: (b, RET_HEADS + h)),
                  pl.BlockSpec((SEQ, RET_DV), lambda b, h, i: (b, 2 * RET_HEADS + h)),
                  pl.BlockSpec((TQ_ATT, RET_DV), lambda b, h, i: (b * nq + i, 3 * RET_HEADS + h)),
                  pl.BlockSpec((TQ_ATT, half), lambda b, h, i: (i, 0)),
                  pl.BlockSpec((TQ_ATT, half), lambda b, h, i: (i, 0)),
                  pl.BlockSpec((SEQ, half), lambda b, h, i: (0, 0)),
                  pl.BlockSpec((SEQ, half), lambda b, h, i: (0, 0)),
                  pl.BlockSpec((1, 1, RET_DV), lambda b, h, i: (h, 0, 0))],
        out_specs=pl.BlockSpec((TQ_ATT, RET_DV), lambda b, h, i: (b * nq + i, h)),
        out_shape=jax.ShapeDtypeStruct((N_TOK, HALF), BF16),
        scratch_shapes=[pltpu.VMEM((SEQ, RET_DK), BF16), pltpu.VMEM((SEQ, RET_DV), BF16)],
        compiler_params=_cparams(("parallel", "parallel", "arbitrary")),
        name="retention",
    )(lg, p1, p1, p1, p1, cos_t, sin_t, cos_t, sin_t, ret_gn.reshape(RET_HEADS, 1, RET_DV))


def _swa_kernel(sink_ref, q_ref, k_ref, v_ref, bias_ref, o_ref):
    kv = pl.program_id(1)
    nb = SEQ // WINDOW
    w = WINDOW

    def rows(ref, blk):
        return ref[pl.ds(pl.multiple_of(blk * w, w), w), :].astype(BF16)

    col = lax.broadcasted_iota(I32, (w, 3 * w), 1)
    for j in range(SWA_BLOCKS):
        n = pl.program_id(2) * SWA_BLOCKS + j
        prev = jnp.maximum(n - 1, 0)
        nxt = jnp.minimum(n + 1, nb - 1)
        kw = jnp.concatenate([rows(k_ref, prev), rows(k_ref, n), rows(k_ref, nxt)], axis=0)
        vw = jnp.concatenate([rows(v_ref, prev), rows(v_ref, n), rows(v_ref, nxt)], axis=0)
        qb = q_ref[j * w:(j + 1) * w, :]
        q4 = jnp.concatenate([qb[:, g * SWA_HD:(g + 1) * SWA_HD] for g in range(SWA_G)], axis=0).astype(BF16)
        s = lax.dot_general(q4, kw, (((1,), (1,)), ((), ())), preferred_element_type=F32) * (SWA_HD ** -0.5)
        outside = ((col < w) & (n == 0)) | ((col >= 2 * w) & (n == nb - 1))
        for g in range(SWA_G):
            sg = jnp.where(outside, NEG_BIG, s[g * w:(g + 1) * w] + bias_ref[g])
            sink = sink_ref[kv * SWA_G + g]
            m = jnp.maximum(jnp.max(sg, axis=-1, keepdims=True), sink)
            p = jnp.exp(sg - m)
            denom = jnp.sum(p, axis=-1, keepdims=True) + jnp.exp(sink - m)
            o = jnp.dot((p / denom).astype(BF16), vw, preferred_element_type=F32)
            o_ref[j * w:(j + 1) * w, g * SWA_HD:(g + 1) * SWA_HD] = o.astype(BF16)


def _swa(p1, sinks, bias):
    nb = SEQ // WINDOW
    qcols = SWA_G * SWA_HD
    q_blk0 = (4 * RET_HEADS * RET_DK) // qcols
    k_blk0 = (4 * RET_HEADS * RET_DK + SWA_HEADS * SWA_HD) // SWA_HD
    v_blk0 = k_blk0 + SWA_KV_HEADS
    return pl.pallas_call(
        _swa_kernel,
        grid=(BATCH, SWA_KV_HEADS, nb // SWA_BLOCKS),
        in_specs=[pl.BlockSpec(memory_space=pltpu.SMEM),
                  pl.BlockSpec((SWA_BLOCKS * WINDOW, qcols), lambda b, kv, n: (b * (nb // SWA_BLOCKS) + n, q_blk0 + kv)),
                  pl.BlockSpec((SEQ, SWA_HD), lambda b, kv, n: (b, k_blk0 + kv)),
                  pl.BlockSpec((SEQ, SWA_HD), lambda b, kv, n: (b, v_blk0 + kv)),
                  pl.BlockSpec((SWA_G, WINDOW, 3 * WINDOW), lambda b, kv, n: (kv, 0, 0))],
        out_specs=pl.BlockSpec((SWA_BLOCKS * WINDOW, qcols), lambda b, kv, n: (b * (nb // SWA_BLOCKS) + n, kv)),
        out_shape=jax.ShapeDtypeStruct((N_TOK, HALF), BF16),
        compiler_params=_cparams(("parallel", "parallel", "parallel")),
        name="swa",
    )(sinks, p1, p1, p1, bias)


def _outproj_kernel(a_ref, b_ref, wa_ref, wb_ref, x_ref, gm_ref, g_ref, sc_ref, sh_ref, wr_ref,
                    x1_ref, hp_ref, lg_ref):
    wr = wr_ref[...]
    for r in range(TM_OUT // TM_OUT_SUB):
        rows = pl.ds(r * TM_OUT_SUB, TM_OUT_SUB)
        mixed = (jnp.dot(a_ref[rows, :], wa_ref[...], preferred_element_type=F32)
                 + jnp.dot(b_ref[rows, :], wb_ref[...], preferred_element_type=F32))
        x1 = x_ref[rows, :] + gm_ref[0] * mixed
        x1_ref[rows, :] = x1
        hf = _rms(x1, g_ref[...]) * (1.0 + sc_ref[0]) + sh_ref[0]
        _store_token_tiles(hp_ref, _pack_bf16_pair(hf[:, :PACK_W], hf[:, PACK_W:]), r * TM_OUT_SUB)
        h_hi = hf.astype(BF16)
        h_lo = (hf - h_hi.astype(F32)).astype(BF16)
        t_hi = jnp.dot(h_hi, wr, preferred_element_type=F32)
        t_lo = jnp.dot(h_lo, wr, preferred_element_type=F32)
        lg_ref[rows, :] = (t_hi[:, :LANES] + t_hi[:, LANES:]) + (t_lo[:, :LANES] + t_lo[:, LANES:])


def _out_projection(a, b, w_out_bf16, x, g_m, gain, scale, shift, w_router_pad):
    per_b = SEQ // TM_OUT
    vec = pl.BlockSpec((1, 1, D_MODEL), lambda i: (i // per_b, 0, 0))
    return pl.pallas_call(
        _outproj_kernel,
        grid=(N_TOK // TM_OUT,),
        in_specs=[pl.BlockSpec((TM_OUT, HALF), lambda i: (i, 0)),
                  pl.BlockSpec((TM_OUT, HALF), lambda i: (i, 0)),
                  pl.BlockSpec((HALF, D_MODEL), lambda i: (0, 0)),
                  pl.BlockSpec((HALF, D_MODEL), lambda i: (1, 0)),
                  pl.BlockSpec((TM_OUT, D_MODEL), lambda i: (i, 0)),
                  vec,
                  pl.BlockSpec((1, D_MODEL), lambda i: (0, 0)),
                  vec, vec,
                  pl.BlockSpec((D_MODEL, 2 * LANES), lambda i: (0, 0))],
        out_specs=[pl.BlockSpec((TM_OUT, D_MODEL), lambda i: (i, 0)),
                   pl.BlockSpec((TM_OUT * TOK_ROWS, LANES), lambda i: (i, 0)),
                   pl.BlockSpec((TM_OUT, LANES), lambda i: (i, 0))],
        out_shape=[jax.ShapeDtypeStruct((N_TOK, D_MODEL), F32),
                   jax.ShapeDtypeStruct((N_TOK * TOK_ROWS, LANES), U32),
                   jax.ShapeDtypeStruct((N_TOK, LANES), F32)],
        compiler_params=_cparams(("parallel",)),
        name="out_proj",
    )(a, b, w_out_bf16, w_out_bf16, x, g_m, gain.reshape(1, D_MODEL), scale, shift, w_router_pad)


def _route_kernel(lg_ref, bias_ref, lslot_ref, w_ref, tab_ref, cnt_ref):
    step = pl.program_id(0)

    @pl.when(step == 0)
    def _():
        cnt_ref[...] = jnp.zeros(cnt_ref.shape, F32)

    for j in range(ROUTE_TILES):
        _route_tile(j, lg_ref, bias_ref, lslot_ref, w_ref, tab_ref, cnt_ref)


def _route_tile(j, lg_ref, bias_ref, lslot_ref, w_ref, tab_ref, cnt_ref):
    t = T_ROUTE
    cols = slice(j * t, (j + 1) * t)
    scores = jax.nn.sigmoid(lg_ref[cols, :].T[:N_EXPERTS])
    biased = scores + bias_ref[...]
    sub = lax.broadcasted_iota(I32, (GROUP_SIZE, t), 0).astype(F32)
    ninf = -jnp.inf

    def first_argmax(v, idx, n):
        m = jnp.max(v, axis=0, keepdims=True)
        return m, jnp.min(jnp.where(v == m, idx, float(n)), axis=0, keepdims=True)

    gs = []
    for g in range(N_GROUPS):
        bg = biased[g * GROUP_SIZE:(g + 1) * GROUP_SIZE]
        m1, i1 = first_argmax(bg, sub, GROUP_SIZE)
        m2 = jnp.max(jnp.where(sub == i1, ninf, bg), axis=0, keepdims=True)
        gs.append(m1 + m2)
    cur = jnp.concatenate(gs, axis=0)

    gmask = jnp.zeros((N_GROUPS, t), F32)
    for _ in range(TOP_GROUPS):
        _, i = first_argmax(cur, sub, N_GROUPS)
        pick = sub == i
        gmask = jnp.where(pick, 1.0, gmask)
        cur = jnp.where(pick, ninf, cur)

    eid = lax.broadcasted_iota(I32, (N_EXPERTS, t), 0).astype(F32)
    emask = jnp.concatenate([jnp.broadcast_to(gmask[g:g + 1], (GROUP_SIZE, t)) for g in range(N_GROUPS)], axis=0)
    cur = jnp.where(emask > 0.5, biased, ninf)
    sels, ws = [], []
    onehot = jnp.zeros((N_EXPERTS, t), F32)
    for _ in range(TOP_K):
        _, i = first_argmax(cur, eid, N_EXPERTS)
        pick = eid == i
        sels.append(pick)
        ws.append(jnp.sum(jnp.where(pick, scores, 0.0), axis=0, keepdims=True))
        onehot = jnp.where(pick, 1.0, onehot)
        cur = jnp.where(pick, ninf, cur)
    wsum = ws[0]
    for k in range(1, TOP_K):
        wsum = wsum + ws[k]

    r = lax.broadcasted_iota(I32, (t, t), 0)
    c = lax.broadcasted_iota(I32, (t, t), 1)
    tri = (r < c).astype(BF16)
    earlier = jnp.dot(onehot.astype(BF16), tri, preferred_element_type=F32)
    tile_cnt = jnp.broadcast_to(jnp.sum(onehot, axis=1, keepdims=True), (N_EXPERTS, LANES))
    er = lax.broadcasted_iota(I32, (N_EXPERTS, N_EXPERTS), 0)
    ec = lax.broadcasted_iota(I32, (N_EXPERTS, N_EXPERTS), 1)
    run_start = jnp.dot((ec < er).astype(BF16), tile_cnt.astype(BF16), preferred_element_type=F32)
    pos = earlier + run_start[:, 0:1]
    lslots = [jnp.sum(jnp.where(sels[k], pos, 0.0), axis=0, keepdims=True) for k in range(TOP_K)]

    lslot_ref[:, cols] = jnp.concatenate(lslots, axis=0).astype(I32)
    w_ref[:, cols] = jnp.concatenate([w / wsum * ROUTE_SCALE for w in ws], axis=0)
    tab_ref[j, 0] = tile_cnt
    tab_ref[j, 1] = cnt_ref[...]
    tab_ref[j, 2] = run_start
    cnt_ref[...] = cnt_ref[...] + tile_cnt


def _route(logits, router_bias):
    ntiles = N_TOK // T_ROUTE
    return pl.pallas_call(
        _route_kernel,
        grid=(ntiles // ROUTE_TILES,),
        in_specs=[pl.BlockSpec((ROUTE_TILES * T_ROUTE, LANES), lambda i: (i, 0)),
                  pl.BlockSpec((N_EXPERTS, 1), lambda i: (0, 0))],
        out_specs=[pl.BlockSpec((TOP_K, ROUTE_TILES * T_ROUTE), lambda i: (0, i)),
                   pl.BlockSpec((TOP_K, ROUTE_TILES * T_ROUTE), lambda i: (0, i)),
                   pl.BlockSpec((ROUTE_TILES, 3, N_EXPERTS, LANES), lambda i: (i, 0, 0, 0)),
                   pl.BlockSpec((N_EXPERTS, LANES), lambda i: (0, 0))],
        out_shape=[jax.ShapeDtypeStruct((TOP_K, N_TOK), I32),
                   jax.ShapeDtypeStruct((TOP_K, N_TOK), F32),
                   jax.ShapeDtypeStruct((ntiles, 3, N_EXPERTS, LANES), F32),
                   jax.ShapeDtypeStruct((N_EXPERTS, LANES), F32)],
        compiler_params=_cparams(("arbitrary",)),
        name="route",
    )(logits, router_bias.reshape(N_EXPERTS, 1))


def _dispatch_kernel(lslot_ref, rcnt_ref, rloc_ref, rglb_ref, zstart_ref, zlen_ref, hp_ref, xs_ref,
                     zero_ref, loc_ref, sem, zsem):
    step = pl.program_id(0)
    nsteps = pl.num_programs(0)

    def zero_copy(z):
        start = pl.multiple_of(zstart_ref[z] * TOK_ROWS, TOK_ROWS)
        n = zlen_ref[z] * TOK_ROWS
        return pltpu.make_async_copy(zero_ref.at[pl.ds(0, n)], xs_ref.at[pl.ds(start, n)], zsem)

    @pl.when(step == 0)
    def _():
        zero_ref[...] = jnp.zeros(zero_ref.shape, U32)

        def fill(z, _):
            @pl.when(zlen_ref[z] > 0)
            def _():
                zero_copy(z).start()
            return 0

        lax.fori_loop(0, N_ZERO_RANGES, fill, 0)

    @pl.when(step == nsteps - 1)
    def _():
        def fill_wait(z, _):
            @pl.when(zlen_ref[z] > 0)
            def _():
                zero_copy(z).wait()
            return 0

        lax.fori_loop(0, N_ZERO_RANGES, fill_wait, 0)

    cur = step % 2

    def tile_wait(slot):
        half = pl.ds(pl.multiple_of(slot * LOCAL_ROWS, LOCAL_ROWS), LOCAL_ROWS)
        pltpu.make_async_copy(loc_ref.at[half], xs_ref.at[pl.ds(0, LOCAL_ROWS)], sem.at[slot]).wait()

    @pl.when(step >= 2)
    def _():
        tile_wait(cur)

    base = step * T_DISP

    def place(t, _):
        row = hp_ref[pl.ds(pl.multiple_of(t * TOK_ROWS, TOK_ROWS), TOK_ROWS), :]
        for k in range(TOP_K):
            dst = pl.multiple_of(lslot_ref[(base + t) * TOP_K + k], TOK_ROWS)
            loc_ref[pl.ds(dst, TOK_ROWS), :] = row
        return 0

    lax.fori_loop(0, T_DISP, place, 0, unroll=2)

    def run(e, _):
        n = rcnt_ref[step * N_EXPERTS + e]

        @pl.when(n > 0)
        def _():
            src = pl.multiple_of(rloc_ref[step * N_EXPERTS + e], TOK_ROWS)
            dst = pl.multiple_of(rglb_ref[step * N_EXPERTS + e] * TOK_ROWS, TOK_ROWS)
            pltpu.make_async_copy(loc_ref.at[pl.ds(src, n * TOK_ROWS)],
                                  xs_ref.at[pl.ds(dst, n * TOK_ROWS)], sem.at[cur]).start()
        return 0

    lax.fori_loop(0, N_EXPERTS, run, 0)

    @pl.when(step == nsteps - 1)
    def _():
        tile_wait(cur)

        @pl.when(nsteps > 1)
        def _():
            tile_wait(1 - cur)


def _dispatch(lslot_flat, run_cnt, run_loc, run_glb, zero_start, zero_len, hp):
    return pl.pallas_call(
        _dispatch_kernel,
        grid_spec=pltpu.PrefetchScalarGridSpec(
            num_scalar_prefetch=6,
            grid=(N_TILES,),
            in_specs=[pl.BlockSpec((T_DISP * TOK_ROWS, LANES), lambda i, *_: (i, 0))],
            out_specs=pl.BlockSpec(memory_space=pl.ANY),
            scratch_shapes=[pltpu.VMEM((TM_EXP * TOK_ROWS, LANES), U32),
                            pltpu.VMEM((2 * LOCAL_ROWS, LANES), U32),
                            pltpu.SemaphoreType.DMA((2,)),
                            pltpu.SemaphoreType.DMA]),
        out_shape=jax.ShapeDtypeStruct((P_ROWS * TOK_ROWS, LANES), U32),
        compiler_params=_cparams(("arbitrary",)),
        name="dispatch",
    )(lslot_flat, run_cnt, run_loc, run_glb, zero_start, zero_len, hp)


def _expert_kernel(be_ref, bv_ref, nx_ref, par_ref, xs_ref, wg_hbm, wu_hbm, wd_hbm, ys_ref,
                   wgs_ref, wus_ref, wds_ref, wgb_ref, wub_ref, wdb_ref, sem, *, layer):
    i = pl.program_id(0)
    e = be_ref[i]
    par = par_ref[i]
    changed = jnp.logical_or(i == 0, e != be_ref[jnp.maximum(i - 1, 0)])

    def stage(expert, slot):
        return (pltpu.make_async_copy(wg_hbm.at[layer, expert], wgs_ref, sem.at[0]),
                pltpu.make_async_copy(wu_hbm.at[layer, expert], wus_ref.at[slot], sem.at[1]),
                pltpu.make_async_copy(wd_hbm.at[layer, expert], wds_ref.at[slot], sem.at[2]))

    @pl.when(changed)
    def _():
        @pl.when(i == 0)
        def _():
            for c in stage(e, par):
                c.start()

        for c in stage(e, par):
            c.wait()
        rows = CAST_VREGS * SUBLANES * LANES // EXPERT_FF

        def cast_piece(c, _):
            sl = pl.ds(pl.multiple_of(c * rows, rows), rows)
            wgb_ref[sl, :] = wgs_ref[sl, :].astype(BF16)
            return 0

        lax.fori_loop(0, D_MODEL // rows, cast_piece, 0, unroll=2)

        @pl.when(nx_ref[i] >= 0)
        def _():
            for c in stage(nx_ref[i], 1 - par):
                c.start()

    def cast_up_down():
        for src, dst in ((wus_ref, wub_ref), (wds_ref, wdb_ref)):
            rows = src.shape[1] // CAST_PIECES
            for c in range(CAST_PIECES):
                dst[c * rows:(c + 1) * rows, :] = src[par, c * rows:(c + 1) * rows, :].astype(BF16)

    def sub_block(row0, nrows):
        xs_sub = xs_ref.at[pl.ds(row0 * TOK_ROWS, nrows * TOK_ROWS)]
        lo, hi = _unpack_bf16_pair(_load_token_tiles(xs_sub, nrows))
        lo = lo.astype(BF16)
        hi = hi.astype(BF16)
        hg = (jnp.dot(lo, wgb_ref[0:PACK_W], preferred_element_type=F32)
              + jnp.dot(hi, wgb_ref[PACK_W:D_MODEL], preferred_element_type=F32))
        hu = (jnp.dot(lo, wub_ref[0:PACK_W], preferred_element_type=F32)
              + jnp.dot(hi, wub_ref[PACK_W:D_MODEL], preferred_element_type=F32))
        act = (_silu(hg) * hu).astype(BF16)
        y = jnp.dot(act, wdb_ref[...], preferred_element_type=F32)
        _store_token_tiles(ys_ref, _pack_bf16_pair(y[:, :PACK_W], y[:, PACK_W:]), row0)

    valid_rows = bv_ref[i]
    full = valid_rows > TM_EXP - TM_EXP_TAIL
    unchanged = jnp.logical_not(changed)

    @pl.when(jnp.logical_and(full, changed))
    def _():
        cast_up_down()
        sub_block(0, TM_EXP_SUB)
        sub_block(TM_EXP_SUB, TM_EXP_SUB)

    @pl.when(jnp.logical_and(full, unchanged))
    def _():
        sub_block(0, TM_EXP_SUB)
        sub_block(TM_EXP_SUB, TM_EXP_SUB)

    @pl.when(jnp.logical_not(full))
    def _():
        @pl.when(changed)
        def _():
            cast_up_down()
            sub_block(0, TM_EXP_TAIL)

        for r in range(TM_EXP // TM_EXP_TAIL):
            if r == 0:
                compute = jnp.logical_and(unchanged, valid_rows > 0)
                skip = jnp.logical_and(unchanged, valid_rows <= 0)
            else:
                compute = valid_rows > r * TM_EXP_TAIL
                skip = valid_rows <= r * TM_EXP_TAIL

            @pl.when(compute)
            def _():
                sub_block(r * TM_EXP_TAIL, TM_EXP_TAIL)

            @pl.when(skip)
            def _():
                ys_ref[pl.ds(r * TM_EXP_TAIL * TOK_ROWS, TM_EXP_TAIL * TOK_ROWS), :] = jnp.zeros(
                    (TM_EXP_TAIL * TOK_ROWS, LANES), U32)


def _experts(layer, block_expert, block_rows, next_expert, block_parity, xs, w_gate, w_up, w_down):
    return pl.pallas_call(
        functools.partial(_expert_kernel, layer=layer),
        grid_spec=pltpu.PrefetchScalarGridSpec(
            num_scalar_prefetch=4,
            grid=(NB_EXP,),
            in_specs=[pl.BlockSpec((TM_EXP * TOK_ROWS, LANES), lambda i, be, bv, *_: (jnp.where(bv[i] > 0, i, 0), 0)),
                      pl.BlockSpec(memory_space=pl.ANY),
                      pl.BlockSpec(memory_space=pl.ANY),
                      pl.BlockSpec(memory_space=pl.ANY)],
            out_specs=pl.BlockSpec((TM_EXP * TOK_ROWS, LANES), lambda i, *_: (i, 0)),
            scratch_shapes=[pltpu.VMEM((D_MODEL, EXPERT_FF), F32),
                            pltpu.VMEM((2, D_MODEL, EXPERT_FF), F32),
                            pltpu.VMEM((2, EXPERT_FF, D_MODEL), F32),
                            pltpu.VMEM((D_MODEL, EXPERT_FF), BF16),
                            pltpu.VMEM((D_MODEL, EXPERT_FF), BF16),
                            pltpu.VMEM((EXPERT_FF, D_MODEL), BF16),
                            pltpu.SemaphoreType.DMA((3,))]),
        out_shape=jax.ShapeDtypeStruct((P_ROWS * TOK_ROWS, LANES), U32),
        compiler_params=_cparams(("arbitrary",)),
        name="experts",
    )(block_expert, block_rows, next_expert, block_parity, xs, w_gate, w_up, w_down)


def _combine_kernel(lslot_ref, rcnt_ref, rloc_ref, rglb_ref, w_ref, ys_ref, hp_ref, x1_ref, gf_ref,
                    wsg_ref, wsu_ref, wsd_ref, fn_ref, o_ref, buf_ref, mlo_ref, mhi_ref, sem, *, final_norm):
    i = pl.program_id(0)
    nsteps = pl.num_programs(0)

    def issue(step, slot):
        def run(e, _):
            n = rcnt_ref[step * N_EXPERTS + e]

            @pl.when(n > 0)
            def _():
                src = pl.multiple_of(rglb_ref[step * N_EXPERTS + e] * TOK_ROWS, TOK_ROWS)
                dst = pl.multiple_of(rloc_ref[step * N_EXPERTS + e], TOK_ROWS)
                pltpu.make_async_copy(ys_ref.at[pl.ds(src, n * TOK_ROWS)],
                                      buf_ref.at[pl.ds(dst, n * TOK_ROWS)], sem.at[slot]).start()
            return 0

        lax.fori_loop(0, N_EXPERTS, run, 0)

    @pl.when(i == 0)
    def _():
        issue(0, 0)

    @pl.when(i + 1 < nsteps)
    def _():
        issue(i + 1, (i + 1) % 2)

    cur = i % 2
    cur_half = pl.ds(pl.multiple_of(cur * LOCAL_ROWS, LOCAL_ROWS), LOCAL_ROWS)
    pltpu.make_async_copy(ys_ref.at[pl.ds(0, LOCAL_ROWS)], buf_ref.at[cur_half], sem.at[cur]).wait()

    base = i * T_DISP

    def token(t, _):
        acc_lo = jnp.zeros((TOK_ROWS, LANES), F32)
        acc_hi = jnp.zeros((TOK_ROWS, LANES), F32)
        for k in range(TOP_K):
            idx = (base + t) * TOP_K + k
            src = pl.multiple_of(lslot_ref[idx], TOK_ROWS)
            lo, hi = _unpack_bf16_pair(buf_ref[pl.ds(src, TOK_ROWS), :])
            wk = w_ref[idx]
            acc_lo = acc_lo + wk * lo
            acc_hi = acc_hi + wk * hi
        dst = pl.ds(pl.multiple_of(t * TOK_ROWS, TOK_ROWS), TOK_ROWS)
        mlo_ref[dst, :] = acc_lo
        mhi_ref[dst, :] = acc_hi
        return 0

    lax.fori_loop(0, T_DISP, token, 0, unroll=2)
    moe_lo = _load_token_tiles(mlo_ref, T_DISP)
    moe_hi = _load_token_tiles(mhi_ref, T_DISP)

    hlo, hhi = _unpack_bf16_pair(_load_token_tiles(hp_ref, T_DISP))
    hlo = hlo.astype(BF16)
    hhi = hhi.astype(BF16)
    sg = (jnp.dot(hlo, wsg_ref[0:PACK_W], preferred_element_type=F32)
          + jnp.dot(hhi, wsg_ref[PACK_W:D_MODEL], preferred_element_type=F32))
    su = (jnp.dot(hlo, wsu_ref[0:PACK_W], preferred_element_type=F32)
          + jnp.dot(hhi, wsu_ref[PACK_W:D_MODEL], preferred_element_type=F32))
    shared = jnp.dot((_silu(sg) * su).astype(BF16), wsd_ref[...], preferred_element_type=F32)
    moe = jnp.concatenate([moe_lo, moe_hi], axis=1)
    out = x1_ref[...] + gf_ref[0] * (moe + shared)
    if final_norm:
        out = _rms(out, fn_ref[...])
    o_ref[...] = out


def _combine(lslot_flat, run_cnt, run_loc, run_glb, w_flat, ys, hp, x1, g_f, wsg, wsu, wsd, final_gain, final_norm):
    per_b = SEQ // T_DISP
    return pl.pallas_call(
        functools.partial(_combine_kernel, final_norm=final_norm),
        grid_spec=pltpu.PrefetchScalarGridSpec(
            num_scalar_prefetch=4,
            grid=(N_TILES,),
            in_specs=[pl.BlockSpec(memory_space=pltpu.SMEM),
                      pl.BlockSpec(memory_space=pl.ANY),
                      pl.BlockSpec((T_DISP * TOK_ROWS, LANES), lambda i, *_: (i, 0)),
                      pl.BlockSpec((T_DISP, D_MODEL), lambda i, *_: (i, 0)),
                      pl.BlockSpec((1, 1, D_MODEL), lambda i, *_: (i // per_b, 0, 0)),
                      pl.BlockSpec((D_MODEL, SHARED_FF), lambda i, *_: (0, 0)),
                      pl.BlockSpec((D_MODEL, SHARED_FF), lambda i, *_: (0, 0)),
                      pl.BlockSpec((SHARED_FF, D_MODEL), lambda i, *_: (0, 0)),
                      pl.BlockSpec((1, D_MODEL), lambda i, *_: (0, 0))],
            out_specs=pl.BlockSpec((T_DISP, D_MODEL), lambda i, *_: (i, 0)),
            scratch_shapes=[pltpu.VMEM((2 * LOCAL_ROWS, LANES), U32),
                            pltpu.VMEM((T_DISP * TOK_ROWS, LANES), F32),
                            pltpu.VMEM((T_DISP * TOK_ROWS, LANES), F32),
                            pltpu.SemaphoreType.DMA((2,))]),
        out_shape=jax.ShapeDtypeStruct((N_TOK, D_MODEL), F32),
        compiler_params=_cparams(("arbitrary",)),
        name="combine",
    )(lslot_flat, run_cnt, run_loc, run_glb, w_flat, ys, hp, x1, g_f, wsg, wsu, wsd, final_gain.reshape(1, D_MODEL))


def _moe_layer(layer, hp, logits, x1, g_f, router_bias, w_gate, w_up, w_down, ws_gate, ws_up, ws_down,
               final_gain, final_norm):
    lslot, w_k, tables, counts = _route(logits, router_bias)
    cnt = counts[:, 0].astype(I32)
    padded = ((cnt + TM_EXP - 1) // TM_EXP) * TM_EXP
    ends = jnp.cumsum(padded)
    offsets = ends - padded
    tables = tables[:, :, :, 0].astype(I32)
    run_cnt = tables[:, 0].reshape(-1)
    run_glb = (offsets[None, :] + tables[:, 1]).reshape(-1)
    half_row0 = (jnp.arange(N_TILES, dtype=I32) % 2) * LOCAL_ROWS
    run_loc = (tables[:, 2] * TOK_ROWS + half_row0[:, None]).reshape(-1)
    lslot_flat = (lslot * TOK_ROWS + jnp.repeat(half_row0, T_DISP)[None, :]).T.reshape(N_SLOTS)
    blk_start = jnp.arange(NB_EXP, dtype=I32) * TM_EXP
    expert_ids = jnp.arange(N_EXPERTS, dtype=I32)
    nonempty = cnt > 0
    last_nonempty = jnp.max(jnp.where(nonempty, expert_ids, 0))
    block_expert = jnp.minimum(jnp.sum((blk_start[:, None] >= ends[None, :]).astype(I32), axis=1), last_nonempty)
    block_valid = (blk_start < ends[-1]).astype(I32)
    block_rows = jnp.clip((offsets + cnt)[block_expert] - blk_start, 0, TM_EXP) * block_valid
    following = jnp.where(nonempty, expert_ids, N_EXPERTS)
    following = lax.cummin(following, reverse=True)
    following = jnp.concatenate([following[1:], jnp.full((1,), N_EXPERTS, I32)])
    next_expert = jnp.where(following < N_EXPERTS, following, -1)[block_expert]
    zero_start = jnp.concatenate([offsets + cnt, blk_start])
    zero_len = jnp.concatenate([padded - cnt, (1 - block_valid) * TM_EXP])
    xs = _dispatch(lslot_flat, run_cnt, run_loc, run_glb, zero_start, zero_len, hp)
    block_parity = ((jnp.cumsum(nonempty.astype(I32)) - 1) % 2)[block_expert]
    ys = _experts(layer, block_expert, block_rows, next_expert, block_parity, xs, w_gate, w_up, w_down)
    return _combine(lslot_flat, run_cnt, run_loc, run_glb, w_k.T.reshape(N_SLOTS), ys, hp, x1, g_f,
                    ws_gate.astype(BF16), ws_up.astype(BF16), ws_down.astype(BF16), final_gain, final_norm)


def _rope_tables(dim):
    inv = ROPE_THETA ** (-jnp.arange(0, dim, 2, dtype=F32) / dim)
    ang = jnp.arange(SEQ, dtype=F32)[:, None] * inv[None, :]
    return jnp.cos(ang), jnp.sin(ang)


def _rot_half_cols(w):
    half = w.shape[-1] // 2
    return jnp.concatenate([-w[..., half:], w[..., :half]], axis=-1)


def _t5_bucket(rel):
    half = REL_BUCKETS // 2
    max_exact = half // 2
    ret = (rel > 0).astype(I32) * half
    n = jnp.abs(rel)
    nf = jnp.maximum(n, 1).astype(F32)
    large = max_exact + (jnp.log(nf / max_exact) / math.log(REL_MAX_DIST / max_exact)
                         * (half - max_exact)).astype(I32)
    large = jnp.minimum(large, half - 1)
    return ret + jnp.where(n < max_exact, n, large)


def _swa_bias_table(rel_bias):
    qi = jnp.arange(WINDOW)[:, None]
    kj = jnp.arange(3 * WINDOW)[None, :]
    rel = kj - WINDOW - qi
    onehot = (_t5_bucket(rel)[:, :, None] == jnp.arange(REL_BUCKETS)).astype(F32)
    bias = jnp.einsum('qjb,bh->hqj', onehot, rel_bias.astype(F32), precision=lax.Precision.HIGHEST)
    return jnp.where((jnp.abs(rel) <= WINDOW)[None], bias, NEG_BIG)


def kernel(x, c, w_mod, b_mod, norm_mix, norm_ffn, final_norm, w_in_ab, q_lat_norm, kv_lat_norm, w_uq, w_ukv, conv_w, conv_b, lru_w_a, lru_b_a, lru_w_x, lru_b_x, lru_lambda, w_out_ab, w_in_cd, ret_gn, swa_sinks, w_out_cd, rel_bias, w_router, router_bias, w_gate, w_up, w_down, ws_gate, ws_up, ws_down):
    xf = x.reshape(N_TOK, D_MODEL)
    mod = _modulation(c, w_mod, b_mod)
    cos_r, sin_r = _rope_tables(MLA_ROPE)
    cs_tab = jnp.concatenate([cos_r, cos_r, sin_r, sin_r], axis=1)
    cos_t, sin_t = _rope_tables(RET_DK)
    lg_ret = jnp.log1p(-(2.0 ** (-5.0 - jnp.arange(RET_HEADS, dtype=F32))))

    for layer in range(DEPTH):
        sh_m, sc_m, g_m, sh_f, sc_f, g_f = [m.reshape(BATCH, 1, D_MODEL) for m in jnp.split(mod[layer], 6, axis=-1)]
        i = layer // 2
        if layer % 2 == 0:
            w = w_in_ab[i]
            o1, o2, o3, o4 = np.cumsum((MLA_Q_RANK, MLA_KV_RANK, MLA_ROPE, LRU_WIDTH)).tolist()
            w_kr = w[:, o2:o3]
            w_in = jnp.concatenate([w[:, :o2], w[:, o3:], w_kr, _rot_half_cols(w_kr)], axis=1).astype(BF16)
            p0 = _in_projection(xf, norm_mix[layer], sc_m, sh_m, w_in, TM_PROJ_AB, "in_proj_ab")
            wq = w_uq[i].reshape(MLA_Q_RANK, MLA_HEADS, MLA_NOPE + MLA_ROPE)
            wq_r = wq[:, :, MLA_NOPE:]
            wq = jnp.concatenate([wq, _rot_half_cols(wq_r)], axis=-1).reshape(MLA_Q_RANK, MLA_HEADS * MLA_QK)
            q, k, v = _mla_up(p0, q_lat_norm[i], kv_lat_norm[i], wq.astype(BF16), w_ukv[i].astype(BF16), cs_tab)
            a_out = _mla_attention(q, k, v)
            w_gates = jnp.concatenate([lru_w_a[i, 0], lru_w_x[i, 0], lru_w_a[i, 1], lru_w_x[i, 1]], axis=-1).astype(BF16)
            b_gates = jnp.concatenate([b.reshape(LRU_BLOCKS, 1, LRU_BS) for b in
                                       (lru_b_a[i, 0], lru_b_x[i, 0], lru_b_a[i, 1], lru_b_x[i, 1])], axis=-1)
            b_out = _rglru(p0, conv_w[i], conv_b[i], w_gates, b_gates, lru_lambda[i])
            w_out = w_out_ab[i].astype(BF16)
        else:
            p1 = _in_projection(xf, norm_mix[layer], sc_m, sh_m, w_in_cd[i].astype(BF16), TM_PROJ_CD, "in_proj_cd")
            a_out = _retention(p1, lg_ret, cos_t, sin_t, ret_gn[i])
            b_out = _swa(p1, swa_sinks[i], _swa_bias_table(rel_bias))
            w_out = w_out_cd[i].astype(BF16)
        w_r = jnp.pad(w_router[layer], ((0, 0), (0, LANES - N_EXPERTS)))
        w_r_hi = w_r.astype(BF16)
        w_router_pad = jnp.concatenate([w_r_hi, (w_r - w_r_hi.astype(F32)).astype(BF16)], axis=1)
        x1, hp, logits = _out_projection(a_out, b_out, w_out, xf, g_m, norm_ffn[layer], sc_f, sh_f, w_router_pad)
        xf = _moe_layer(layer, hp, logits, x1, g_f, router_bias[layer], w_gate, w_up, w_down,
                        ws_gate[layer], ws_up[layer], ws_down[layer], final_norm, layer == DEPTH - 1)
    return xf.reshape(BATCH, SEQ, D_MODEL)
```

```python
import functools
import math

import numpy as np
import jax
import jax.numpy as jnp
from jax import lax
from jax.experimental import pallas as pl
from jax.experimental.pallas import tpu as pltpu

F32 = jnp.float32
BF16 = jnp.bfloat16
I32 = jnp.int32
U32 = jnp.uint32

D_MODEL = 2048
BATCH = 4
SEQ = 2048
DEPTH = 2
N_TOK = BATCH * SEQ
HALF = D_MODEL // 2
MLA_NOPE = 128
MLA_ROPE = 64
MLA_V = 128
MLA_HEADS = HALF // MLA_V
MLA_Q_RANK = D_MODEL // 4
MLA_KV_RANK = D_MODEL // 4
MLA_QK = 256
LRU_WIDTH = HALF
LRU_BLOCKS = 8
LRU_BS = LRU_WIDTH // LRU_BLOCKS
LRU_CONV = 4
LRU_C = 8.0
RET_DK = 256
RET_DV = 256
RET_HEADS = HALF // RET_DV
SWA_HD = 128
SWA_HEADS = HALF // SWA_HD
SWA_KV_HEADS = 2
SWA_G = SWA_HEADS // SWA_KV_HEADS
WINDOW = 128
REL_BUCKETS = 32
REL_MAX_DIST = 128
N_EXPERTS = 64
TOP_K = 8
N_GROUPS = 8
GROUP_SIZE = N_EXPERTS // N_GROUPS
TOP_GROUPS = 4
EXPERT_FF = D_MODEL // 4
SHARED_FF = D_MODEL // 4
ROUTE_SCALE = 2.5
ROPE_THETA = 10000.0
EPS = 1e-6
NEG_BIG = -1e30

LANES = 128
SUBLANES = 8
VMEM_LIMIT = 52 * 2**20

TM_PROJ_AB = 512
TM_PROJ_CD = 256
TM_UP = 512
TQ_ATT = 1024
TQ_MLA = 2048
TQ_SUB = 256
SWA_BLOCKS = 8
TM_OUT = 512
TM_OUT_SUB = 256
T_DISP = 256
T_ROUTE = T_DISP
ROUTE_TILES = 4
N_TILES = N_TOK // T_DISP
TILE_ROWS = T_DISP * TOP_K
TM_EXP = 512
TM_EXP_SUB = 256
TM_EXP_TAIL = 128
assert TM_EXP == 2 * TM_EXP_SUB and TM_EXP % TM_EXP_TAIL == 0
CAST_VREGS = 32
CAST_PIECES = 16
N_SLOTS = N_TOK * TOP_K
NB_EXP = N_SLOTS // TM_EXP + N_EXPERTS
P_ROWS = NB_EXP * TM_EXP
N_ZERO_RANGES = N_EXPERTS + NB_EXP
PACK_W = D_MODEL // 2
TOK_ROWS = PACK_W // LANES
assert TOK_ROWS == SUBLANES
LOCAL_ROWS = TILE_ROWS * TOK_ROWS

LRU_SEG = 260
LRU_ROWS = SUBLANES * LRU_SEG
assert LRU_ROWS >= SEQ and LRU_SEG % 8 == 4


def _cparams(sem, vmem=VMEM_LIMIT):
    return pltpu.CompilerParams(dimension_semantics=sem, vmem_limit_bytes=vmem)


def _sigmoid(x):
    return 0.5 * jnp.tanh(0.5 * x) + 0.5


def _silu(x):
    return x * _sigmoid(x)


def _rms(x, g):
    return x * lax.rsqrt(jnp.mean(x * x, axis=-1, keepdims=True) + EPS) * g


def _pack_bf16_pair(lo, hi):
    lo_b = lax.bitcast_convert_type(lo.astype(BF16).astype(F32), U32)
    hi_b = lax.bitcast_convert_type(hi.astype(BF16).astype(F32), U32)
    return (hi_b & jnp.uint32(0xFFFF0000)) | (lo_b >> 16)


def _unpack_bf16_pair(w):
    lo = lax.bitcast_convert_type(w << 16, F32)
    hi = lax.bitcast_convert_type(w & jnp.uint32(0xFFFF0000), F32)
    return lo, hi


def _store_token_tiles(ref, packed, tok0=0):
    t = packed.shape[0]
    for s in range(TOK_ROWS):
        ref[pl.ds(tok0 * TOK_ROWS + s, t, stride=TOK_ROWS), :] = packed[:, s * LANES:(s + 1) * LANES]


def _load_token_tiles(ref, t):
    return jnp.concatenate([ref[pl.ds(s, t, stride=TOK_ROWS), :] for s in range(TOK_ROWS)], axis=1)


def _mod_kernel(c_ref, w_ref, b_ref, o_ref):
    c = c_ref[...]
    ca = _silu(c).astype(BF16)
    o_ref[0] = jnp.dot(ca, w_ref[0].astype(BF16), preferred_element_type=F32) + b_ref[0]


def _modulation(c, w_mod, b_mod):
    tn = 1024
    cp = jnp.pad(c, ((0, SUBLANES - BATCH), (0, 0)))
    out = pl.pallas_call(
        _mod_kernel,
        grid=(DEPTH, 6 * D_MODEL // tn),
        in_specs=[pl.BlockSpec((SUBLANES, D_MODEL), lambda l, j: (0, 0)),
                  pl.BlockSpec((1, D_MODEL, tn), lambda l, j: (l, 0, j)),
                  pl.BlockSpec((1, 1, tn), lambda l, j: (l, 0, j))],
        out_specs=pl.BlockSpec((1, SUBLANES, tn), lambda l, j: (l, 0, j)),
        out_shape=jax.ShapeDtypeStruct((DEPTH, SUBLANES, 6 * D_MODEL), F32),
        compiler_params=_cparams(("parallel", "parallel")),
        name="adaln_mod",
    )(cp, w_mod, b_mod.reshape(DEPTH, 1, 6 * D_MODEL))
    return out[:, :BATCH]


def _inproj_kernel(x_ref, g_ref, sc_ref, sh_ref, w_ref, o_ref):
    sub = x_ref.shape[0] // 2
    for r in range(2):
        rows = pl.ds(r * sub, sub)
        y = _rms(x_ref[rows, :], g_ref[...])
        h = (y * (1.0 + sc_ref[0]) + sh_ref[0]).astype(BF16)
        o_ref[rows, :] = jnp.dot(h, w_ref[...], preferred_element_type=F32)


def _in_projection(x, gain, scale, shift, w_bf16, tm, name):
    p = w_bf16.shape[1]
    per_b = SEQ // tm
    return pl.pallas_call(
        _inproj_kernel,
        grid=(N_TOK // tm,),
        in_specs=[pl.BlockSpec((tm, D_MODEL), lambda i: (i, 0)),
                  pl.BlockSpec((1, D_MODEL), lambda i: (0, 0)),
                  pl.BlockSpec((1, 1, D_MODEL), lambda i: (i // per_b, 0, 0)),
                  pl.BlockSpec((1, 1, D_MODEL), lambda i: (i // per_b, 0, 0)),
                  pl.BlockSpec((D_MODEL, p), lambda i: (0, 0), pipeline_mode=pl.Buffered(1))],
        out_specs=pl.BlockSpec((tm, p), lambda i: (i, 0)),
        out_shape=jax.ShapeDtypeStruct((N_TOK, p), F32),
        compiler_params=_cparams(("parallel",), 56 * 2**20),
        name=name,
    )(x, gain.reshape(1, D_MODEL), scale, shift, w_bf16)


def _mla_up_kernel(ql_ref, kvl_ref, kr_ref, qn_ref, kvn_ref, wq_ref, wkv_ref, cs_ref, q_ref, k_ref, v_ref):
    scale = (MLA_NOPE + MLA_ROPE) ** -0.5 * math.log2(math.e)
    hq = _rms(ql_ref[...], qn_ref[...]).astype(BF16)
    hkv = _rms(kvl_ref[...], kvn_ref[...]).astype(BF16)
    yq = jnp.dot(hq, wq_ref[...], preferred_element_type=F32) * scale
    ykv = jnp.dot(hkv, wkv_ref[...], preferred_element_type=F32)
    cs = cs_ref[...]
    lane = lax.broadcasted_iota(I32, cs.shape, 1)

    def rope_sum(blk):
        z = blk * cs
        return z + pltpu.roll(z, MLA_ROPE, 1)

    kr = jnp.where(lane < MLA_ROPE, rope_sum(kr_ref[...]), 0.0).astype(BF16)
    ones_col = jnp.where(lane == 0, 1.0, 0.0).astype(BF16)
    for h in range(MLA_HEADS):
        c0 = h * MLA_QK
        q_ref[0, h, :, 0:MLA_NOPE] = yq[:, c0:c0 + MLA_NOPE].astype(BF16)
        q_ref[0, h, :, MLA_NOPE:MLA_QK] = rope_sum(yq[:, c0 + MLA_NOPE:c0 + MLA_QK]).astype(BF16)
        k_ref[0, h, :, 0:MLA_NOPE] = ykv[:, c0:c0 + MLA_NOPE].astype(BF16)
        k_ref[0, h, :, MLA_NOPE:MLA_QK] = kr
        v_ref[0, h, :, 0:MLA_V] = ykv[:, c0 + MLA_NOPE:c0 + MLA_QK].astype(BF16)
        v_ref[0, h, :, MLA_V:2 * MLA_V] = ones_col


def _mla_up(p0, q_norm, kv_norm, wq, wkv, cs_tab):
    per_b = SEQ // TM_UP
    qk_shape = jax.ShapeDtypeStruct((BATCH, MLA_HEADS, SEQ, MLA_QK), BF16)
    return pl.pallas_call(
        _mla_up_kernel,
        grid=(N_TOK // TM_UP,),
        in_specs=[pl.BlockSpec((TM_UP, MLA_Q_RANK), lambda i: (i, 0)),
                  pl.BlockSpec((TM_UP, MLA_KV_RANK), lambda i: (i, 1)),
                  pl.BlockSpec((TM_UP, LANES), lambda i: (i, 24)),
                  pl.BlockSpec((1, MLA_Q_RANK), lambda i: (0, 0)),
                  pl.BlockSpec((1, MLA_KV_RANK), lambda i: (0, 0)),
                  pl.BlockSpec((MLA_Q_RANK, MLA_HEADS * MLA_QK), lambda i: (0, 0)),
                  pl.BlockSpec((MLA_KV_RANK, MLA_HEADS * MLA_QK), lambda i: (0, 0)),
                  pl.BlockSpec((TM_UP, LANES), lambda i: (i % per_b, 0))],
        out_specs=[pl.BlockSpec((1, MLA_HEADS, TM_UP, MLA_QK), lambda i: (i // per_b, 0, i % per_b, 0)),
                   pl.BlockSpec((1, MLA_HEADS, TM_UP, MLA_QK), lambda i: (i // per_b, 0, i % per_b, 0)),
                   pl.BlockSpec((1, MLA_HEADS, TM_UP, 2 * MLA_V), lambda i: (i // per_b, 0, i % per_b, 0))],
        out_shape=[qk_shape, qk_shape, jax.ShapeDtypeStruct((BATCH, MLA_HEADS, SEQ, 2 * MLA_V), BF16)],
        compiler_params=_cparams(("parallel",)),
        name="mla_up",
    )(p0, p0, p0, q_norm.reshape(1, -1), kv_norm.reshape(1, -1), wq, wkv, cs_tab)


def _mla_attn_kernel(q_ref, k_ref, v_ref, o_ref):
    k = k_ref[0, 0]
    v = v_ref[0, 0]
    for r in range(TQ_MLA // TQ_SUB):
        rows = pl.ds(r * TQ_SUB, TQ_SUB)
        s = lax.dot_general(q_ref[0, 0, rows, :], k, (((1,), (1,)), ((), ())), preferred_element_type=F32)
        p = jnp.exp2(s - jnp.max(s, axis=-1, keepdims=True))
        o = jnp.dot(p.astype(BF16), v, preferred_element_type=F32)
        o_ref[rows, :] = (o[:, 0:MLA_V] / o[:, MLA_V:MLA_V + 1]).astype(BF16)


def _mla_attention(q, k, v):
    nq = SEQ // TQ_MLA
    return pl.pallas_call(
        _mla_attn_kernel,
        grid=(BATCH, MLA_HEADS, nq),
        in_specs=[pl.BlockSpec((1, 1, TQ_MLA, MLA_QK), lambda b, h, i: (b, h, i, 0)),
                  pl.BlockSpec((1, 1, SEQ, MLA_QK), lambda b, h, i: (b, h, 0, 0)),
                  pl.BlockSpec((1, 1, SEQ, 2 * MLA_V), lambda b, h, i: (b, h, 0, 0))],
        out_specs=pl.BlockSpec((TQ_MLA, MLA_V), lambda b, h, i: (b * nq + i, h)),
        out_shape=jax.ShapeDtypeStruct((N_TOK, HALF), BF16),
        compiler_params=_cparams(("parallel", "parallel", "parallel")),
        name="mla_attn",
    )(q, k, v)


def _lru_kernel(x_ref, gate_ref, cw_ref, cb_ref, wg_ref, bg_ref, lam_ref, o_ref,
                af_ref, uf_ref, ab_ref, ub_ref, hf_ref, pf_ref, hb_ref, pb_ref, hs_ref):
    x = x_ref[...]
    row = lax.broadcasted_iota(I32, x.shape, 0)

    def shifted(d):
        r = pltpu.roll(x, (-d) % SEQ, 0)
        return jnp.where((row + d >= 0) & (row + d < SEQ), r, 0.0)

    cw = cw_ref[...]
    left = LRU_CONV // 2
    xc = cb_ref[...]
    for kk in range(LRU_CONV):
        d = kk - left
        xc = xc + cw[kk:kk + 1] * (x if d == 0 else shifted(d))

    gates = jnp.dot(xc.astype(BF16), wg_ref[0], preferred_element_type=F32) + bg_ref[0]
    lam = lam_ref[...]
    z = -lam
    sp = jnp.maximum(z, 0.0) + jnp.log1p(jnp.exp(-jnp.abs(z)))
    pad_rows = LRU_ROWS - SEQ
    for d, (a_ref, u_ref) in enumerate(((af_ref, uf_ref), (ab_ref, ub_ref))):
        r = _sigmoid(gates[:, d * 256:d * 256 + LRU_BS])
        i = _sigmoid(gates[:, d * 256 + LRU_BS:(d + 1) * 256])
        a = jnp.exp(r * (-LRU_C * sp[d:d + 1]))
        a_ref[0:SEQ] = a
        u_ref[0:SEQ] = jnp.sqrt(1.0 - a * a) * (i * xc)
        a_ref[SEQ:LRU_ROWS] = jnp.zeros((pad_rows, LANES), F32)
        u_ref[SEQ:LRU_ROWS] = jnp.zeros((pad_rows, LANES), F32)

    ones = jnp.ones((SUBLANES, LANES), F32)
    zeros = jnp.zeros((SUBLANES, LANES), F32)

    def seg(t):
        return pl.ds(t, SUBLANES, stride=LRU_SEG)

    def local_scan(s, carry):
        p_f, h_f, p_b, h_b = carry
        tf = s
        tb = LRU_SEG - 1 - s
        a = af_ref[seg(tf)]
        h_f = a * h_f + uf_ref[seg(tf)]
        p_f = a * p_f
        hf_ref[seg(tf)] = h_f
        pf_ref[seg(tf)] = p_f
        a = ab_ref[seg(tb)]
        h_b = a * h_b + ub_ref[seg(tb)]
        p_b = a * p_b
        hb_ref[seg(tb)] = h_b
        pb_ref[seg(tb)] = p_b
        return p_f, h_f, p_b, h_b

    p_f, h_f, p_b, h_b = lax.fori_loop(0, LRU_SEG, local_scan, (ones, zeros, ones, zeros), unroll=4)

    rows_f = []
    c = jnp.zeros((1, LANES), F32)
    for j in range(SUBLANES):
        rows_f.append(c)
        c = p_f[j:j + 1] * c + h_f[j:j + 1]
    rows_b = [None] * SUBLANES
    c = jnp.zeros((1, LANES), F32)
    for j in range(SUBLANES - 1, -1, -1):
        rows_b[j] = c
        c = p_b[j:j + 1] * c + h_b[j:j + 1]
    sub = lax.broadcasted_iota(I32, (SUBLANES, LANES), 0)
    c_f = zeros
    c_b = zeros
    for j in range(SUBLANES):
        c_f = jnp.where(sub == j, rows_f[j], c_f)
        c_b = jnp.where(sub == j, rows_b[j], c_b)

    def fixup(t, _):
        hs_ref[seg(t)] = (hf_ref[seg(t)] + pf_ref[seg(t)] * c_f) + (hb_ref[seg(t)] + pb_ref[seg(t)] * c_b)
        return 0

    lax.fori_loop(0, LRU_SEG, fixup, 0, unroll=4)

    g = gate_ref[...]
    gelu = 0.5 * g * (1.0 + jnp.tanh(math.sqrt(2.0 / math.pi) * (g + 0.044715 * (g * g * g))))
    o_ref[...] = (gelu * hs_ref[0:SEQ]).astype(BF16)


def _rglru(p0, conv_w, conv_b, w_gates, b_gates, lam):
    scan_buf = pltpu.VMEM((LRU_ROWS, LANES), F32)
    return pl.pallas_call(
        _lru_kernel,
        grid=(BATCH, LRU_BLOCKS),
        in_specs=[pl.BlockSpec((SEQ, LRU_BS), lambda b, g: (b, 8 + g)),
                  pl.BlockSpec((SEQ, LRU_BS), lambda b, g: (b, 16 + g)),
                  pl.BlockSpec((LRU_CONV, LRU_BS), lambda b, g: (0, g)),
                  pl.BlockSpec((1, LRU_BS), lambda b, g: (0, g)),
                  pl.BlockSpec((1, LRU_BS, 4 * LRU_BS), lambda b, g: (g, 0, 0)),
                  pl.BlockSpec((1, 1, 4 * LRU_BS), lambda b, g: (g, 0, 0)),
                  pl.BlockSpec((2, LRU_BS), lambda b, g: (0, g))],
        out_specs=pl.BlockSpec((SEQ, LRU_BS), lambda b, g: (b, g)),
        out_shape=jax.ShapeDtypeStruct((N_TOK, LRU_WIDTH), BF16),
        scratch_shapes=[scan_buf] * 9,
        compiler_params=_cparams(("parallel", "parallel")),
        name="rglru",
    )(p0, p0, conv_w, conv_b.reshape(1, -1), w_gates, b_gates, lam)


def _ret_kernel(lg_ref, q_ref, k_ref, v_ref, g_ref, cq_ref, sq_ref, ck_ref, sk_ref, gn_ref, o_ref, ks_ref, vs_ref):
    h = pl.program_id(1)
    qi = pl.program_id(2)
    half = RET_DK // 2

    def rope(t, c, s):
        t1, t2 = t[:, :half], t[:, half:]
        return jnp.concatenate([t1 * c - t2 * s, t2 * c + t1 * s], axis=1)

    @pl.when(qi == 0)
    def _():
        ks_ref[...] = (rope(k_ref[...], ck_ref[...], sk_ref[...]) * (RET_DK ** -0.5)).astype(BF16)
        vs_ref[...] = v_ref[...].astype(BF16)

    q = rope(q_ref[...], cq_ref[...], sq_ref[...]).astype(BF16)
    s = lax.dot_general(q, ks_ref[...], (((1,), (1,)), ((), ())), preferred_element_type=F32)
    n = (qi * TQ_ATT + lax.broadcasted_iota(I32, (TQ_ATT, 1), 0)).astype(F32)
    m = lax.broadcasted_iota(I32, (1, SEQ), 1).astype(F32)
    c_f = lg_ref[h] * math.log2(math.e)
    c_b = -lg_ref[RET_HEADS - 1 - h] * math.log2(math.e)
    dec = jnp.exp2(jnp.minimum(c_f * n - c_f * m, c_b * n - c_b * m))
    o = jnp.dot((s * dec).astype(BF16), vs_ref[...], preferred_element_type=F32)
    y = _rms(o, gn_ref[0])
    o_ref[...] = (_silu(g_ref[...]) * y).astype(BF16)


def _retention(p1, lg, cos_t, sin_t, ret_gn):
    nq = SEQ // TQ_ATT
    half = RET_DK // 2
    return pl.pallas_call(
        _ret_kernel,
        grid=(BATCH, RET_HEADS, nq),
        in_specs=[pl.BlockSpec(memory_space=pltpu.SMEM),
                  pl.BlockSpec((TQ_ATT, RET_DK), lambda b, h, i: (b * nq + i, h)),
                  pl.BlockSpec((SEQ, RET_DK), lambda b, h, i: (b, RET_HEADS + h)),
                  pl.BlockSpec((SEQ, RET_DV), lambda b, h, i: (b, 2 * RET_HEADS + h)),
                  pl.BlockSpec((TQ_ATT, RET_DV), lambda b, h, i: (b * nq + i, 3 * RET_HEADS + h)),
                  pl.BlockSpec((TQ_ATT, half), lambda b, h, i: (i, 0)),
                  pl.BlockSpec((TQ_ATT, half), lambda b, h, i: (i, 0)),
                  pl.BlockSpec((SEQ, half), lambda b, h, i: (0, 0)),
                  pl.BlockSpec((SEQ, half), lambda b, h, i: (0, 0)),
                  pl.BlockSpec((1, 1, RET_DV), lambda b, h, i: (h, 0, 0))],
        out_specs=pl.BlockSpec((TQ_ATT, RET_DV), lambda b, h, i: (b * nq + i, h)),
        out_shape=jax.ShapeDtypeStruct((N_TOK, HALF), BF16),
        scratch_shapes=[pltpu.VMEM((SEQ, RET_DK), BF16), pltpu.VMEM((SEQ, RET_DV), BF16)],
        compiler_params=_cparams(("parallel", "parallel", "arbitrary")),
        name="retention",
    )(lg, p1, p1, p1, p1, cos_t, sin_t, cos_t, sin_t, ret_gn.reshape(RET_HEADS, 1, RET_DV))


def _swa_kernel(sink_ref, q_ref, k_ref, v_ref, bias_ref, o_ref):
    kv = pl.program_id(1)
    nb = SEQ // WINDOW
    w = WINDOW

    def rows(ref, blk):
        return ref[pl.ds(pl.multiple_of(blk * w, w), w), :].astype(BF16)

    col = lax.broadcasted_iota(I32, (w, 3 * w), 1)
    for j in range(SWA_BLOCKS):
        n = pl.program_id(2) * SWA_BLOCKS + j
        prev = jnp.maximum(n - 1, 0)
        nxt = jnp.minimum(n + 1, nb - 1)
        kw = jnp.concatenate([rows(k_ref, prev), rows(k_ref, n), rows(k_ref, nxt)], axis=0)
        vw = jnp.concatenate([rows(v_ref, prev), rows(v_ref, n), rows(v_ref, nxt)], axis=0)
        qb = q_ref[j * w:(j + 1) * w, :]
        q4 = jnp.concatenate([qb[:, g * SWA_HD:(g + 1) * SWA_HD] for g in range(SWA_G)], axis=0).astype(BF16)
        s = lax.dot_general(q4, kw, (((1,), (1,)), ((), ())), preferred_element_type=F32) * (SWA_HD ** -0.5)
        outside = ((col < w) & (n == 0)) | ((col >= 2 * w) & (n == nb - 1))
        for g in range(SWA_G):
            sg = jnp.where(outside, NEG_BIG, s[g * w:(g + 1) * w] + bias_ref[g])
            sink = sink_ref[kv * SWA_G + g]
            m = jnp.maximum(jnp.max(sg, axis=-1, keepdims=True), sink)
            p = jnp.exp(sg - m)
            denom = jnp.sum(p, axis=-1, keepdims=True) + jnp.exp(sink - m)
            o = jnp.dot((p / denom).astype(BF16), vw, preferred_element_type=F32)
            o_ref[j * w:(j + 1) * w, g * SWA_HD:(g + 1) * SWA_HD] = o.astype(BF16)


def _swa(p1, sinks, bias):
    nb = SEQ // WINDOW
    qcols = SWA_G * SWA_HD
    q_blk0 = (4 * RET_HEADS * RET_DK) // qcols
    k_blk0 = (4 * RET_HEADS * RET_DK + SWA_HEADS * SWA_HD) // SWA_HD
    v_blk0 = k_blk0 + SWA_KV_HEADS
    return pl.pallas_call(
        _swa_kernel,
        grid=(BATCH, SWA_KV_HEADS, nb // SWA_BLOCKS),
        in_specs=[pl.BlockSpec(memory_space=pltpu.SMEM),
                  pl.BlockSpec((SWA_BLOCKS * WINDOW, qcols), lambda b, kv, n: (b * (nb // SWA_BLOCKS) + n, q_blk0 + kv)),
                  pl.BlockSpec((SEQ, SWA_HD), lambda b, kv, n: (b, k_blk0 + kv)),
                  pl.BlockSpec((SEQ, SWA_HD), lambda b, kv, n: (b, v_blk0 + kv)),
                  pl.BlockSpec((SWA_G, WINDOW, 3 * WINDOW), lambda b, kv, n: (kv, 0, 0))],
        out_specs=pl.BlockSpec((SWA_BLOCKS * WINDOW, qcols), lambda b, kv, n: (b * (nb // SWA_BLOCKS) + n, kv)),
        out_shape=jax.ShapeDtypeStruct((N_TOK, HALF), BF16),
        compiler_params=_cparams(("parallel", "parallel", "parallel")),
        name="swa",
    )(sinks, p1, p1, p1, bias)


def _outproj_kernel(a_ref, b_ref, wa_ref, wb_ref, x_ref, gm_ref, g_ref, sc_ref, sh_ref, wr_ref,
                    x1_ref, hp_ref, lg_ref):
    wr = wr_ref[...]
    for r in range(TM_OUT // TM_OUT_SUB):
        rows = pl.ds(r * TM_OUT_SUB, TM_OUT_SUB)
        mixed = (jnp.dot(a_ref[rows, :], wa_ref[...], preferred_element_type=F32)
                 + jnp.dot(b_ref[rows, :], wb_ref[...], preferred_element_type=F32))
        x1 = x_ref[rows, :] + gm_ref[0] * mixed
        x1_ref[rows, :] = x1
        hf = _rms(x1, g_ref[...]) * (1.0 + sc_ref[0]) + sh_ref[0]
        _store_token_tiles(hp_ref, _pack_bf16_pair(hf[:, :PACK_W], hf[:, PACK_W:]), r * TM_OUT_SUB)
        h_hi = hf.astype(BF16)
        h_lo = (hf - h_hi.astype(F32)).astype(BF16)
        t_hi = jnp.dot(h_hi, wr, preferred_element_type=F32)
        t_lo = jnp.dot(h_lo, wr, preferred_element_type=F32)
        lg_ref[rows, :] = (t_hi[:, :LANES] + t_hi[:, LANES:]) + (t_lo[:, :LANES] + t_lo[:, LANES:])


def _out_projection(a, b, w_out_bf16, x, g_m, gain, scale, shift, w_router_pad):
    per_b = SEQ // TM_OUT
    vec = pl.BlockSpec((1, 1, D_MODEL), lambda i: (i // per_b, 0, 0))
    return pl.pallas_call(
        _outproj_kernel,
        grid=(N_TOK // TM_OUT,),
        in_specs=[pl.BlockSpec((TM_OUT, HALF), lambda i: (i, 0)),
                  pl.BlockSpec((TM_OUT, HALF), lambda i: (i, 0)),
                  pl.BlockSpec((HALF, D_MODEL), lambda i: (0, 0)),
                  pl.BlockSpec((HALF, D_MODEL), lambda i: (1, 0)),
                  pl.BlockSpec((TM_OUT, D_MODEL), lambda i: (i, 0)),
                  vec,
                  pl.BlockSpec((1, D_MODEL), lambda i: (0, 0)),
                  vec, vec,
                  pl.BlockSpec((D_MODEL, 2 * LANES), lambda i: (0, 0))],
        out_specs=[pl.BlockSpec((TM_OUT, D_MODEL), lambda i: (i, 0)),
                   pl.BlockSpec((TM_OUT * TOK_ROWS, LANES), lambda i: (i, 0)),
                   pl.BlockSpec((TM_OUT, LANES), lambda i: (i, 0))],
        out_shape=[jax.ShapeDtypeStruct((N_TOK, D_MODEL), F32),
                   jax.ShapeDtypeStruct((N_TOK * TOK_ROWS, LANES), U32),
                   jax.ShapeDtypeStruct((N_TOK, LANES), F32)],
        compiler_params=_cparams(("parallel",)),
        name="out_proj",
    )(a, b, w_out_bf16, w_out_bf16, x, g_m, gain.reshape(1, D_MODEL), scale, shift, w_router_pad)


def _route_kernel(lg_ref, bias_ref, lslot_ref, w_ref, tab_ref, cnt_ref):
    step = pl.program_id(0)

    @pl.when(step == 0)
    def _():
        cnt_ref[...] = jnp.zeros(cnt_ref.shape, F32)

    for j in range(ROUTE_TILES):
        _route_tile(j, lg_ref, bias_ref, lslot_ref, w_ref, tab_ref, cnt_ref)


def _route_tile(j, lg_ref, bias_ref, lslot_ref, w_ref, tab_ref, cnt_ref):
    t = T_ROUTE
    cols = slice(j * t, (j + 1) * t)
    scores = jax.nn.sigmoid(lg_ref[cols, :].T[:N_EXPERTS])
    biased = scores + bias_ref[...]
    sub = lax.broadcasted_iota(I32, (GROUP_SIZE, t), 0).astype(F32)
    ninf = -jnp.inf

    def first_argmax(v, idx, n):
        m = jnp.max(v, axis=0, keepdims=True)
        return m, jnp.min(jnp.where(v == m, idx, float(n)), axis=0, keepdims=True)

    gs = []
    for g in range(N_GROUPS):
        bg = biased[g * GROUP_SIZE:(g + 1) * GROUP_SIZE]
        m1, i1 = first_argmax(bg, sub, GROUP_SIZE)
        m2 = jnp.max(jnp.where(sub == i1, ninf, bg), axis=0, keepdims=True)
        gs.append(m1 + m2)
    cur = jnp.concatenate(gs, axis=0)

    gmask = jnp.zeros((N_GROUPS, t), F32)
    for _ in range(TOP_GROUPS):
        _, i = first_argmax(cur, sub, N_GROUPS)
        pick = sub == i
        gmask = jnp.where(pick, 1.0, gmask)
        cur = jnp.where(pick, ninf, cur)

    eid = lax.broadcasted_iota(I32, (N_EXPERTS, t), 0).astype(F32)
    emask = jnp.concatenate([jnp.broadcast_to(gmask[g:g + 1], (GROUP_SIZE, t)) for g in range(N_GROUPS)], axis=0)
    cur = jnp.where(emask > 0.5, biased, ninf)
    sels, ws = [], []
    onehot = jnp.zeros((N_EXPERTS, t), F32)
    for _ in range(TOP_K):
        _, i = first_argmax(cur, eid, N_EXPERTS)
        pick = eid == i
        sels.append(pick)
        ws.append(jnp.sum(jnp.where(pick, scores, 0.0), axis=0, keepdims=True))
        onehot = jnp.where(pick, 1.0, onehot)
        cur = jnp.where(pick, ninf, cur)
    wsum = ws[0]
    for k in range(1, TOP_K):
        wsum = wsum + ws[k]

    r = lax.broadcasted_iota(I32, (t, t), 0)
    c = lax.broadcasted_iota(I32, (t, t), 1)
    tri = (r < c).astype(BF16)
    earlier = jnp.dot(onehot.astype(BF16), tri, preferred_element_type=F32)
    tile_cnt = jnp.broadcast_to(jnp.sum(onehot, axis=1, keepdims=True), (N_EXPERTS, LANES))
    er = lax.broadcasted_iota(I32, (N_EXPERTS, N_EXPERTS), 0)
    ec = lax.broadcasted_iota(I32, (N_EXPERTS, N_EXPERTS), 1)
    run_start = jnp.dot((ec < er).astype(BF16), tile_cnt.astype(BF16), preferred_element_type=F32)
    pos = earlier + run_start[:, 0:1]
    lslots = [jnp.sum(jnp.where(sels[k], pos, 0.0), axis=0, keepdims=True) for k in range(TOP_K)]

    lslot_ref[:, cols] = jnp.concatenate(lslots, axis=0).astype(I32)
    w_ref[:, cols] = jnp.concatenate([w / wsum * ROUTE_SCALE for w in ws], axis=0)
    tab_ref[j, 0] = tile_cnt
    tab_ref[j, 1] = cnt_ref[...]
    tab_ref[j, 2] = run_start
    cnt_ref[...] = cnt_ref[...] + tile_cnt


def _route(logits, router_bias):
    ntiles = N_TOK // T_ROUTE
    return pl.pallas_call(
        _route_kernel,
        grid=(ntiles // ROUTE_TILES,),
        in_specs=[pl.BlockSpec((ROUTE_TILES * T_ROUTE, LANES), lambda i: (i, 0)),
                  pl.BlockSpec((N_EXPERTS, 1), lambda i: (0, 0))],
        out_specs=[pl.BlockSpec((TOP_K, ROUTE_TILES * T_ROUTE), lambda i: (0, i)),
                   pl.BlockSpec((TOP_K, ROUTE_TILES * T_ROUTE), lambda i: (0, i)),
                   pl.BlockSpec((ROUTE_TILES, 3, N_EXPERTS, LANES), lambda i: (i, 0, 0, 0)),
                   pl.BlockSpec((N_EXPERTS, LANES), lambda i: (0, 0))],
        out_shape=[jax.ShapeDtypeStruct((TOP_K, N_TOK), I32),
                   jax.ShapeDtypeStruct((TOP_K, N_TOK), F32),
                   jax.ShapeDtypeStruct((ntiles, 3, N_EXPERTS, LANES), F32),
                   jax.ShapeDtypeStruct((N_EXPERTS, LANES), F32)],
        compiler_params=_cparams(("arbitrary",)),
        name="route",
    )(logits, router_bias.reshape(N_EXPERTS, 1))


def _dispatch_kernel(lslot_ref, rcnt_ref, rloc_ref, rglb_ref, zstart_ref, zlen_ref, hp_ref, xs_ref,
                     zero_ref, loc_ref, sem, zsem):
    step = pl.program_id(0)
    nsteps = pl.num_programs(0)

    def zero_copy(z):
        start = pl.multiple_of(zstart_ref[z] * TOK_ROWS, TOK_ROWS)
        n = zlen_ref[z] * TOK_ROWS
        return pltpu.make_async_copy(zero_ref.at[pl.ds(0, n)], xs_ref.at[pl.ds(start, n)], zsem)

    @pl.when(step == 0)
    def _():
        zero_ref[...] = jnp.zeros(zero_ref.shape, U32)

        def fill(z, _):
            @pl.when(zlen_ref[z] > 0)
            def _():
                zero_copy(z).start()
            return 0

        lax.fori_loop(0, N_ZERO_RANGES, fill, 0)

    @pl.when(step == nsteps - 1)
    def _():
        def fill_wait(z, _):
            @pl.when(zlen_ref[z] > 0)
            def _():
                zero_copy(z).wait()
            return 0

        lax.fori_loop(0, N_ZERO_RANGES, fill_wait, 0)

    cur = step % 2

    def tile_wait(slot):
        half = pl.ds(pl.multiple_of(slot * LOCAL_ROWS, LOCAL_ROWS), LOCAL_ROWS)
        pltpu.make_async_copy(loc_ref.at[half], xs_ref.at[pl.ds(0, LOCAL_ROWS)], sem.at[slot]).wait()

    @pl.when(step >= 2)
    def _():
        tile_wait(cur)

    base = step * T_DISP

    def place(t, _):
        row = hp_ref[pl.ds(pl.multiple_of(t * TOK_ROWS, TOK_ROWS), TOK_ROWS), :]
        for k in range(TOP_K):
            dst = pl.multiple_of(lslot_ref[(base + t) * TOP_K + k], TOK_ROWS)
            loc_ref[pl.ds(dst, TOK_ROWS), :] = row
        return 0

    lax.fori_loop(0, T_DISP, place, 0, unroll=4)

    def run(e, _):
        n = rcnt_ref[step * N_EXPERTS + e]

        @pl.when(n > 0)
        def _():
            src = pl.multiple_of(rloc_ref[step * N_EXPERTS + e], TOK_ROWS)
            dst = pl.multiple_of(rglb_ref[step * N_EXPERTS + e] * TOK_ROWS, TOK_ROWS)
            pltpu.make_async_copy(loc_ref.at[pl.ds(src, n * TOK_ROWS)],
                                  xs_ref.at[pl.ds(dst, n * TOK_ROWS)], sem.at[cur]).start()
        return 0

    lax.fori_loop(0, N_EXPERTS, run, 0)

    @pl.when(step == nsteps - 1)
    def _():
        tile_wait(cur)

        @pl.when(nsteps > 1)
        def _():
            tile_wait(1 - cur)


def _dispatch(lslot_flat, run_cnt, run_loc, run_glb, zero_start, zero_len, hp):
    return pl.pallas_call(
        _dispatch_kernel,
        grid_spec=pltpu.PrefetchScalarGridSpec(
            num_scalar_prefetch=6,
            grid=(N_TILES,),
            in_specs=[pl.BlockSpec((T_DISP * TOK_ROWS, LANES), lambda i, *_: (i, 0))],
            out_specs=pl.BlockSpec(memory_space=pl.ANY),
            scratch_shapes=[pltpu.VMEM((TM_EXP * TOK_ROWS, LANES), U32),
                            pltpu.VMEM((2 * LOCAL_ROWS, LANES), U32),
                            pltpu.SemaphoreType.DMA((2,)),
                            pltpu.SemaphoreType.DMA]),
        out_shape=jax.ShapeDtypeStruct((P_ROWS * TOK_ROWS, LANES), U32),
        compiler_params=_cparams(("arbitrary",)),
        name="dispatch",
    )(lslot_flat, run_cnt, run_loc, run_glb, zero_start, zero_len, hp)


def _expert_kernel(be_ref, bv_ref, nx_ref, par_ref, xs_ref, wg_hbm, wu_hbm, wd_hbm, ys_ref,
                   wgs_ref, wus_ref, wds_ref, wgb_ref, wub_ref, wdb_ref, sem, *, layer):
    i = pl.program_id(0)
    e = be_ref[i]
    par = par_ref[i]
    changed = jnp.logical_or(i == 0, e != be_ref[jnp.maximum(i - 1, 0)])

    def stage(expert, slot):
        return (pltpu.make_async_copy(wg_hbm.at[layer, expert], wgs_ref, sem.at[0]),
                pltpu.make_async_copy(wu_hbm.at[layer, expert], wus_ref.at[slot], sem.at[1]),
                pltpu.make_async_copy(wd_hbm.at[layer, expert], wds_ref.at[slot], sem.at[2]))

    @pl.when(changed)
    def _():
        @pl.when(i == 0)
        def _():
            for c in stage(e, par):
                c.start()

        for c in stage(e, par):
            c.wait()
        rows = CAST_VREGS * SUBLANES * LANES // EXPERT_FF

        def cast_piece(c, _):
            sl = pl.ds(pl.multiple_of(c * rows, rows), rows)
            wgb_ref[sl, :] = wgs_ref[sl, :].astype(BF16)
            return 0

        lax.fori_loop(0, D_MODEL // rows, cast_piece, 0, unroll=2)

        @pl.when(nx_ref[i] >= 0)
        def _():
            for c in stage(nx_ref[i], 1 - par):
                c.start()

    def cast_up_down():
        for src, dst in ((wus_ref, wub_ref), (wds_ref, wdb_ref)):
            rows = src.shape[1] // CAST_PIECES
            for c in range(CAST_PIECES):
                dst[c * rows:(c + 1) * rows, :] = src[par, c * rows:(c + 1) * rows, :].astype(BF16)

    def sub_block(row0, nrows):
        xs_sub = xs_ref.at[pl.ds(row0 * TOK_ROWS, nrows * TOK_ROWS)]
        lo, hi = _unpack_bf16_pair(_load_token_tiles(xs_sub, nrows))
        lo = lo.astype(BF16)
        hi = hi.astype(BF16)
        hg = (jnp.dot(lo, wgb_ref[0:PACK_W], preferred_element_type=F32)
              + jnp.dot(hi, wgb_ref[PACK_W:D_MODEL], preferred_element_type=F32))
        hu = (jnp.dot(lo, wub_ref[0:PACK_W], preferred_element_type=F32)
              + jnp.dot(hi, wub_ref[PACK_W:D_MODEL], preferred_element_type=F32))
        act = (_silu(hg) * hu).astype(BF16)
        y = jnp.dot(act, wdb_ref[...], preferred_element_type=F32)
        _store_token_tiles(ys_ref, _pack_bf16_pair(y[:, :PACK_W], y[:, PACK_W:]), row0)

    valid_rows = bv_ref[i]
    full = valid_rows > TM_EXP - TM_EXP_TAIL
    unchanged = jnp.logical_not(changed)

    @pl.when(jnp.logical_and(full, changed))
    def _():
        cast_up_down()
        sub_block(0, TM_EXP_SUB)
        sub_block(TM_EXP_SUB, TM_EXP_SUB)

    @pl.when(jnp.logical_and(full, unchanged))
    def _():
        sub_block(0, TM_EXP_SUB)
        sub_block(TM_EXP_SUB, TM_EXP_SUB)

    @pl.when(jnp.logical_not(full))
    def _():
        @pl.when(changed)
        def _():
            cast_up_down()
            sub_block(0, TM_EXP_TAIL)

        for r in range(TM_EXP // TM_EXP_TAIL):
            if r == 0:
                compute = jnp.logical_and(unchanged, valid_rows > 0)
                skip = jnp.logical_and(unchanged, valid_rows <= 0)
            else:
                compute = valid_rows > r * TM_EXP_TAIL
                skip = valid_rows <= r * TM_EXP_TAIL

            @pl.when(compute)
            def _():
                sub_block(r * TM_EXP_TAIL, TM_EXP_TAIL)

            @pl.when(skip)
            def _():
                ys_ref[pl.ds(r * TM_EXP_TAIL * TOK_ROWS, TM_EXP_TAIL * TOK_ROWS), :] = jnp.zeros(
                    (TM_EXP_TAIL * TOK_ROWS, LANES), U32)


def _experts(layer, block_expert, block_rows, next_expert, block_parity, xs, w_gate, w_up, w_down):
    return pl.pallas_call(
        functools.partial(_expert_kernel, layer=layer),
        grid_spec=pltpu.PrefetchScalarGridSpec(
            num_scalar_prefetch=4,
            grid=(NB_EXP,),
            in_specs=[pl.BlockSpec((TM_EXP * TOK_ROWS, LANES), lambda i, be, bv, *_: (jnp.where(bv[i] > 0, i, 0), 0)),
                      pl.BlockSpec(memory_space=pl.ANY),
                      pl.BlockSpec(memory_space=pl.ANY),
                      pl.BlockSpec(memory_space=pl.ANY)],
            out_specs=pl.BlockSpec((TM_EXP * TOK_ROWS, LANES), lambda i, *_: (i, 0)),
            scratch_shapes=[pltpu.VMEM((D_MODEL, EXPERT_FF), F32),
                            pltpu.VMEM((2, D_MODEL, EXPERT_FF), F32),
                            pltpu.VMEM((2, EXPERT_FF, D_MODEL), F32),
                            pltpu.VMEM((D_MODEL, EXPERT_FF), BF16),
                            pltpu.VMEM((D_MODEL, EXPERT_FF), BF16),
                            pltpu.VMEM((EXPERT_FF, D_MODEL), BF16),
                            pltpu.SemaphoreType.DMA((3,))]),
        out_shape=jax.ShapeDtypeStruct((P_ROWS * TOK_ROWS, LANES), U32),
        compiler_params=_cparams(("arbitrary",)),
        name="experts",
    )(block_expert, block_rows, next_expert, block_parity, xs, w_gate, w_up, w_down)


def _combine_kernel(lslot_ref, rcnt_ref, rloc_ref, rglb_ref, w_ref, ys_ref, hp_ref, x1_ref, gf_ref,
                    wsg_ref, wsu_ref, wsd_ref, fn_ref, o_ref, buf_ref, mlo_ref, mhi_ref, sem, *, final_norm):
    i = pl.program_id(0)
    nsteps = pl.num_programs(0)

    def issue(step, slot):
        def run(e, _):
            n = rcnt_ref[step * N_EXPERTS + e]

            @pl.when(n > 0)
            def _():
                src = pl.multiple_of(rglb_ref[step * N_EXPERTS + e] * TOK_ROWS, TOK_ROWS)
                dst = pl.multiple_of(rloc_ref[step * N_EXPERTS + e], TOK_ROWS)
                pltpu.make_async_copy(ys_ref.at[pl.ds(src, n * TOK_ROWS)],
                                      buf_ref.at[pl.ds(dst, n * TOK_ROWS)], sem.at[slot]).start()
            return 0

        lax.fori_loop(0, N_EXPERTS, run, 0)

    @pl.when(i == 0)
    def _():
        issue(0, 0)

    @pl.when(i + 1 < nsteps)
    def _():
        issue(i + 1, (i + 1) % 2)

    cur = i % 2
    cur_half = pl.ds(pl.multiple_of(cur * LOCAL_ROWS, LOCAL_ROWS), LOCAL_ROWS)
    pltpu.make_async_copy(ys_ref.at[pl.ds(0, LOCAL_ROWS)], buf_ref.at[cur_half], sem.at[cur]).wait()

    base = i * T_DISP

    def token(t, _):
        acc_lo = jnp.zeros((TOK_ROWS, LANES), F32)
        acc_hi = jnp.zeros((TOK_ROWS, LANES), F32)
        for k in range(TOP_K):
            idx = (base + t) * TOP_K + k
            src = pl.multiple_of(lslot_ref[idx], TOK_ROWS)
            lo, hi = _unpack_bf16_pair(buf_ref[pl.ds(src, TOK_ROWS), :])
            wk = w_ref[idx]
            acc_lo = acc_lo + wk * lo
            acc_hi = acc_hi + wk * hi
        dst = pl.ds(pl.multiple_of(t * TOK_ROWS, TOK_ROWS), TOK_ROWS)
        mlo_ref[dst, :] = acc_lo
        mhi_ref[dst, :] = acc_hi
        return 0

    lax.fori_loop(0, T_DISP, token, 0, unroll=2)
    moe_lo = _load_token_tiles(mlo_ref, T_DISP)
    moe_hi = _load_token_tiles(mhi_ref, T_DISP)

    hlo, hhi = _unpack_bf16_pair(_load_token_tiles(hp_ref, T_DISP))
    hlo = hlo.astype(BF16)
    hhi = hhi.astype(BF16)
    sg = (jnp.dot(hlo, wsg_ref[0:PACK_W], preferred_element_type=F32)
          + jnp.dot(hhi, wsg_ref[PACK_W:D_MODEL], preferred_element_type=F32))
    su = (jnp.dot(hlo, wsu_ref[0:PACK_W], preferred_element_type=F32)
          + jnp.dot(hhi, wsu_ref[PACK_W:D_MODEL], preferred_element_type=F32))
    shared = jnp.dot((_silu(sg) * su).astype(BF16), wsd_ref[...], preferred_element_type=F32)
    moe = jnp.concatenate([moe_lo, moe_hi], axis=1)
    out = x1_ref[...] + gf_ref[0] * (moe + shared)
    if final_norm:
        out = _rms(out, fn_ref[...])
    o_ref[...] = out


def _combine(lslot_flat, run_cnt, run_loc, run_glb, w_flat, ys, hp, x1, g_f, wsg, wsu, wsd, final_gain, final_norm):
    per_b = SEQ // T_DISP
    return pl.pallas_call(
        functools.partial(_combine_kernel, final_norm=final_norm),
        grid_spec=pltpu.PrefetchScalarGridSpec(
            num_scalar_prefetch=4,
            grid=(N_TILES,),
            in_specs=[pl.BlockSpec(memory_space=pltpu.SMEM),
                      pl.BlockSpec(memory_space=pl.ANY),
                      pl.BlockSpec((T_DISP * TOK_ROWS, LANES), lambda i, *_: (i, 0)),
                      pl.BlockSpec((T_DISP, D_MODEL), lambda i, *_: (i, 0)),
                      pl.BlockSpec((1, 1, D_MODEL), lambda i, *_: (i // per_b, 0, 0)),
                      pl.BlockSpec((D_MODEL, SHARED_FF), lambda i, *_: (0, 0)),
                      pl.BlockSpec((D_MODEL, SHARED_FF), lambda i, *_: (0, 0)),
                      pl.BlockSpec((SHARED_FF, D_MODEL), lambda i, *_: (0, 0)),
                      pl.BlockSpec((1, D_MODEL), lambda i, *_: (0, 0))],
            out_specs=pl.BlockSpec((T_DISP, D_MODEL), lambda i, *_: (i, 0)),
            scratch_shapes=[pltpu.VMEM((2 * LOCAL_ROWS, LANES), U32),
                            pltpu.VMEM((T_DISP * TOK_ROWS, LANES), F32),
                            pltpu.VMEM((T_DISP * TOK_ROWS, LANES), F32),
                            pltpu.SemaphoreType.DMA((2,))]),
        out_shape=jax.ShapeDtypeStruct((N_TOK, D_MODEL), F32),
        compiler_params=_cparams(("arbitrary",)),
        name="combine",
    )(lslot_flat, run_cnt, run_loc, run_glb, w_flat, ys, hp, x1, g_f, wsg, wsu, wsd, final_gain.reshape(1, D_MODEL))


def _moe_layer(layer, hp, logits, x1, g_f, router_bias, w_gate, w_up, w_down, ws_gate, ws_up, ws_down,
               final_gain, final_norm):
    lslot, w_k, tables, counts = _route(logits, router_bias)
    cnt = counts[:, 0].astype(I32)
    padded = ((cnt + TM_EXP - 1) // TM_EXP) * TM_EXP
    ends = jnp.cumsum(padded)
    offsets = ends - padded
    tables = tables[:, :, :, 0].astype(I32)
    run_cnt = tables[:, 0].reshape(-1)
    run_glb = (offsets[None, :] + tables[:, 1]).reshape(-1)
    half_row0 = (jnp.arange(N_TILES, dtype=I32) % 2) * LOCAL_ROWS
    run_loc = (tables[:, 2] * TOK_ROWS + half_row0[:, None]).reshape(-1)
    lslot_flat = (lslot * TOK_ROWS + jnp.repeat(half_row0, T_DISP)[None, :]).T.reshape(N_SLOTS)
    blk_start = jnp.arange(NB_EXP, dtype=I32) * TM_EXP
    expert_ids = jnp.arange(N_EXPERTS, dtype=I32)
    nonempty = cnt > 0
    last_nonempty = jnp.max(jnp.where(nonempty, expert_ids, 0))
    block_expert = jnp.minimum(jnp.sum((blk_start[:, None] >= ends[None, :]).astype(I32), axis=1), last_nonempty)
    block_valid = (blk_start < ends[-1]).astype(I32)
    block_rows = jnp.clip((offsets + cnt)[block_expert] - blk_start, 0, TM_EXP) * block_valid
    following = jnp.where(nonempty, expert_ids, N_EXPERTS)
    following = lax.cummin(following, reverse=True)
    following = jnp.concatenate([following[1:], jnp.full((1,), N_EXPERTS, I32)])
    next_expert = jnp.where(following < N_EXPERTS, following, -1)[block_expert]
    zero_start = jnp.concatenate([offsets + cnt, blk_start])
    zero_len = jnp.concatenate([padded - cnt, (1 - block_valid) * TM_EXP])
    xs = _dispatch(lslot_flat, run_cnt, run_loc, run_glb, zero_start, zero_len, hp)
    block_parity = ((jnp.cumsum(nonempty.astype(I32)) - 1) % 2)[block_expert]
    ys = _experts(layer, block_expert, block_rows, next_expert, block_parity, xs, w_gate, w_up, w_down)
    return _combine(lslot_flat, run_cnt, run_loc, run_glb, w_k.T.reshape(N_SLOTS), ys, hp, x1, g_f,
                    ws_gate.astype(BF16), ws_up.astype(BF16), ws_down.astype(BF16), final_gain, final_norm)


def _rope_tables(dim):
    inv = ROPE_THETA ** (-jnp.arange(0, dim, 2, dtype=F32) / dim)
    ang = jnp.arange(SEQ, dtype=F32)[:, None] * inv[None, :]
    return jnp.cos(ang), jnp.sin(ang)


def _rot_half_cols(w):
    half = w.shape[-1] // 2
    return jnp.concatenate([-w[..., half:], w[..., :half]], axis=-1)


def _t5_bucket(rel):
    half = REL_BUCKETS // 2
    max_exact = half // 2
    ret = (rel > 0).astype(I32) * half
    n = jnp.abs(rel)
    nf = jnp.maximum(n, 1).astype(F32)
    large = max_exact + (jnp.log(nf / max_exact) / math.log(REL_MAX_DIST / max_exact)
                         * (half - max_exact)).astype(I32)
    large = jnp.minimum(large, half - 1)
    return ret + jnp.where(n < max_exact, n, large)


def _swa_bias_table(rel_bias):
    qi = jnp.arange(WINDOW)[:, None]
    kj = jnp.arange(3 * WINDOW)[None, :]
    rel = kj - WINDOW - qi
    onehot = (_t5_bucket(rel)[:, :, None] == jnp.arange(REL_BUCKETS)).astype(F32)
    bias = jnp.einsum('qjb,bh->hqj', onehot, rel_bias.astype(F32), precision=lax.Precision.HIGHEST)
    return jnp.where((jnp.abs(rel) <= WINDOW)[None], bias, NEG_BIG)


def kernel(x, c, w_mod, b_mod, norm_mix, norm_ffn, final_norm, w_in_ab, q_lat_norm, kv_lat_norm, w_uq, w_ukv, conv_w, conv_b, lru_w_a, lru_b_a, lru_w_x, lru_b_x, lru_lambda, w_out_ab, w_in_cd, ret_gn, swa_sinks, w_out_cd, rel_bias, w_router, router_bias, w_gate, w_up, w_down, ws_gate, ws_up, ws_down):
    xf = x.reshape(N_TOK, D_MODEL)
    mod = _modulation(c, w_mod, b_mod)
    cos_r, sin_r = _rope_tables(MLA_ROPE)
    cs_tab = jnp.concatenate([cos_r, cos_r, sin_r, sin_r], axis=1)
    cos_t, sin_t = _rope_tables(RET_DK)
    lg_ret = jnp.log1p(-(2.0 ** (-5.0 - jnp.arange(RET_HEADS, dtype=F32))))

    for layer in range(DEPTH):
        sh_m, sc_m, g_m, sh_f, sc_f, g_f = [m.reshape(BATCH, 1, D_MODEL) for m in jnp.split(mod[layer], 6, axis=-1)]
        i = layer // 2
        if layer % 2 == 0:
            w = w_in_ab[i]
            o1, o2, o3, o4 = np.cumsum((MLA_Q_RANK, MLA_KV_RANK, MLA_ROPE, LRU_WIDTH)).tolist()
            w_kr = w[:, o2:o3]
            w_in = jnp.concatenate([w[:, :o2], w[:, o3:], w_kr, _rot_half_cols(w_kr)], axis=1).astype(BF16)
            p0 = _in_projection(xf, norm_mix[layer], sc_m, sh_m, w_in, TM_PROJ_AB, "in_proj_ab")
            wq = w_uq[i].reshape(MLA_Q_RANK, MLA_HEADS, MLA_NOPE + MLA_ROPE)
            wq_r = wq[:, :, MLA_NOPE:]
            wq = jnp.concatenate([wq, _rot_half_cols(wq_r)], axis=-1).reshape(MLA_Q_RANK, MLA_HEADS * MLA_QK)
            q, k, v = _mla_up(p0, q_lat_norm[i], kv_lat_norm[i], wq.astype(BF16), w_ukv[i].astype(BF16), cs_tab)
            a_out = _mla_attention(q, k, v)
            w_gates = jnp.concatenate([lru_w_a[i, 0], lru_w_x[i, 0], lru_w_a[i, 1], lru_w_x[i, 1]], axis=-1).astype(BF16)
            b_gates = jnp.concatenate([b.reshape(LRU_BLOCKS, 1, LRU_BS) for b in
                                       (lru_b_a[i, 0], lru_b_x[i, 0], lru_b_a[i, 1], lru_b_x[i, 1])], axis=-1)
            b_out = _rglru(p0, conv_w[i], conv_b[i], w_gates, b_gates, lru_lambda[i])
            w_out = w_out_ab[i].astype(BF16)
        else:
            p1 = _in_projection(xf, norm_mix[layer], sc_m, sh_m, w_in_cd[i].astype(BF16), TM_PROJ_CD, "in_proj_cd")
            a_out = _retention(p1, lg_ret, cos_t, sin_t, ret_gn[i])
            b_out = _swa(p1, swa_sinks[i], _swa_bias_table(rel_bias))
            w_out = w_out_cd[i].astype(BF16)
        w_r = jnp.pad(w_router[layer], ((0, 0), (0, LANES - N_EXPERTS)))
        w_r_hi = w_r.astype(BF16)
        w_router_pad = jnp.concatenate([w_r_hi, (w_r - w_r_hi.astype(F32)).astype(BF16)], axis=1)
        x1, hp, logits = _out_projection(a_out, b_out, w_out, xf, g_m, norm_ffn[layer], sc_f, sh_f, w_router_pad)
        xf = _moe_layer(layer, hp, logits, x1, g_f, router_bias[layer], w_gate, w_up, w_down,
                        ws_gate[layer], ws_up[layer], ws_down[layer], final_norm, layer == DEPTH - 1)
    return xf.reshape(BATCH, SEQ, D_MODEL)
```

```python
import functools
import math

import numpy as np
import jax
import jax.numpy as jnp
from jax import lax
from jax.experimental import pallas as pl
from jax.experimental.pallas import tpu as pltpu

F32 = jnp.float32
BF16 = jnp.bfloat16
I32 = jnp.int32
U32 = jnp.uint32

D_MODEL = 2048
BATCH = 4
SEQ = 2048
DEPTH = 2
N_TOK = BATCH * SEQ
HALF = D_MODEL // 2
MLA_NOPE = 128
MLA_ROPE = 64
MLA_V = 128
MLA_HEADS = HALF // MLA_V
MLA_Q_RANK = D_MODEL // 4
MLA_KV_RANK = D_MODEL // 4
MLA_QK = 256
LRU_WIDTH = HALF
LRU_BLOCKS = 8
LRU_BS = LRU_WIDTH // LRU_BLOCKS
LRU_CONV = 4
LRU_C = 8.0
RET_DK = 256
RET_DV = 256
RET_HEADS = HALF // RET_DV
SWA_HD = 128
SWA_HEADS = HALF // SWA_HD
SWA_KV_HEADS = 2
SWA_G = SWA_HEADS // SWA_KV_HEADS
WINDOW = 128
REL_BUCKETS = 32
REL_MAX_DIST = 128
N_EXPERTS = 64
TOP_K = 8
N_GROUPS = 8
GROUP_SIZE = N_EXPERTS // N_GROUPS
TOP_GROUPS = 4
EXPERT_FF = D_MODEL // 4
SHARED_FF = D_MODEL // 4
ROUTE_SCALE = 2.5
ROPE_THETA = 10000.0
EPS = 1e-6
NEG_BIG = -1e30

LANES = 128
SUBLANES = 8
VMEM_LIMIT = 52 * 2**20

TM_PROJ_AB = 512
TM_PROJ_CD = 256
TM_UP = 512
TQ_ATT = 1024
TQ_MLA = 2048
TQ_SUB = 256
SWA_BLOCKS = 8
TM_OUT = 512
TM_OUT_SUB = 256
T_DISP = 256
T_ROUTE = T_DISP
ROUTE_TILES = 4
N_TILES = N_TOK // T_DISP
TILE_ROWS = T_DISP * TOP_K
TM_EXP = 512
TM_EXP_SUB = 256
TM_EXP_TAIL = 128
assert TM_EXP == 2 * TM_EXP_SUB and TM_EXP % TM_EXP_TAIL == 0
CAST_VREGS = 32
CAST_PIECES = 16
N_SLOTS = N_TOK * TOP_K
NB_EXP = N_SLOTS // TM_EXP + N_EXPERTS
P_ROWS = NB_EXP * TM_EXP
N_ZERO_RANGES = N_EXPERTS + NB_EXP
PACK_W = D_MODEL // 2
TOK_ROWS = PACK_W // LANES
assert TOK_ROWS == SUBLANES
LOCAL_ROWS = TILE_ROWS * TOK_ROWS

LRU_SEG = 260
LRU_ROWS = SUBLANES * LRU_SEG
assert LRU_ROWS >= SEQ and LRU_SEG % 8 == 4


def _cparams(sem, vmem=VMEM_LIMIT):
    return pltpu.CompilerParams(dimension_semantics=sem, vmem_limit_bytes=vmem)


def _sigmoid(x):
    return 0.5 * jnp.tanh(0.5 * x) + 0.5


def _silu(x):
    return x * _sigmoid(x)


def _rms(x, g):
    return x * lax.rsqrt(jnp.mean(x * x, axis=-1, keepdims=True) + EPS) * g


def _pack_bf16_pair(lo, hi):
    lo_b = lax.bitcast_convert_type(lo.astype(BF16).astype(F32), U32)
    hi_b = lax.bitcast_convert_type(hi.astype(BF16).astype(F32), U32)
    return (hi_b & jnp.uint32(0xFFFF0000)) | (lo_b >> 16)


def _unpack_bf16_pair(w):
    lo = lax.bitcast_convert_type(w << 16, F32)
    hi = lax.bitcast_convert_type(w & jnp.uint32(0xFFFF0000), F32)
    return lo, hi


def _store_token_tiles(ref, packed, tok0=0):
    t = packed.shape[0]
    for s in range(TOK_ROWS):
        ref[pl.ds(tok0 * TOK_ROWS + s, t, stride=TOK_ROWS), :] = packed[:, s * LANES:(s + 1) * LANES]


def _load_token_tiles(ref, t):
    return jnp.concatenate([ref[pl.ds(s, t, stride=TOK_ROWS), :] for s in range(TOK_ROWS)], axis=1)


def _mod_kernel(c_ref, w_ref, b_ref, o_ref):
    c = c_ref[...]
    ca = _silu(c).astype(BF16)
    o_ref[0] = jnp.dot(ca, w_ref[0].astype(BF16), preferred_element_type=F32) + b_ref[0]


def _modulation(c, w_mod, b_mod):
    tn = 1024
    cp = jnp.pad(c, ((0, SUBLANES - BATCH), (0, 0)))
    out = pl.pallas_call(
        _mod_kernel,
        grid=(DEPTH, 6 * D_MODEL // tn),
        in_specs=[pl.BlockSpec((SUBLANES, D_MODEL), lambda l, j: (0, 0)),
                  pl.BlockSpec((1, D_MODEL, tn), lambda l, j: (l, 0, j)),
                  pl.BlockSpec((1, 1, tn), lambda l, j: (l, 0, j))],
        out_specs=pl.BlockSpec((1, SUBLANES, tn), lambda l, j: (l, 0, j)),
        out_shape=jax.ShapeDtypeStruct((DEPTH, SUBLANES, 6 * D_MODEL), F32),
        compiler_params=_cparams(("parallel", "parallel")),
        name="adaln_mod",
    )(cp, w_mod, b_mod.reshape(DEPTH, 1, 6 * D_MODEL))
    return out[:, :BATCH]


def _inproj_kernel(x_ref, g_ref, sc_ref, sh_ref, w_ref, o_ref):
    sub = x_ref.shape[0] // 2
    for r in range(2):
        rows = pl.ds(r * sub, sub)
        y = _rms(x_ref[rows, :], g_ref[...])
        h = (y * (1.0 + sc_ref[0]) + sh_ref[0]).astype(BF16)
        o_ref[rows, :] = jnp.dot(h, w_ref[...], preferred_element_type=F32)


def _in_projection(x, gain, scale, shift, w_bf16, tm, name):
    p = w_bf16.shape[1]
    per_b = SEQ // tm
    return pl.pallas_call(
        _inproj_kernel,
        grid=(N_TOK // tm,),
        in_specs=[pl.BlockSpec((tm, D_MODEL), lambda i: (i, 0)),
                  pl.BlockSpec((1, D_MODEL), lambda i: (0, 0)),
                  pl.BlockSpec((1, 1, D_MODEL), lambda i: (i // per_b, 0, 0)),
                  pl.BlockSpec((1, 1, D_MODEL), lambda i: (i // per_b, 0, 0)),
                  pl.BlockSpec((D_MODEL, p), lambda i: (0, 0), pipeline_mode=pl.Buffered(1))],
        out_specs=pl.BlockSpec((tm, p), lambda i: (i, 0)),
        out_shape=jax.ShapeDtypeStruct((N_TOK, p), F32),
        compiler_params=_cparams(("parallel",), 56 * 2**20),
        name=name,
    )(x, gain.reshape(1, D_MODEL), scale, shift, w_bf16)


def _mla_up_kernel(ql_ref, kvl_ref, kr_ref, qn_ref, kvn_ref, wq_ref, wkv_ref, cs_ref, q_ref, k_ref, v_ref):
    scale = (MLA_NOPE + MLA_ROPE) ** -0.5 * math.log2(math.e)
    hq = _rms(ql_ref[...], qn_ref[...]).astype(BF16)
    hkv = _rms(kvl_ref[...], kvn_ref[...]).astype(BF16)
    yq = jnp.dot(hq, wq_ref[...], preferred_element_type=F32) * scale
    ykv = jnp.dot(hkv, wkv_ref[...], preferred_element_type=F32)
    cs = cs_ref[...]
    lane = lax.broadcasted_iota(I32, cs.shape, 1)

    def rope_sum(blk):
        z = blk * cs
        return z + pltpu.roll(z, MLA_ROPE, 1)

    kr = jnp.where(lane < MLA_ROPE, rope_sum(kr_ref[...]), 0.0).astype(BF16)
    ones_col = jnp.where(lane == 0, 1.0, 0.0).astype(BF16)
    for h in range(MLA_HEADS):
        c0 = h * MLA_QK
        q_ref[0, h, :, 0:MLA_NOPE] = yq[:, c0:c0 + MLA_NOPE].astype(BF16)
        q_ref[0, h, :, MLA_NOPE:MLA_QK] = rope_sum(yq[:, c0 + MLA_NOPE:c0 + MLA_QK]).astype(BF16)
        k_ref[0, h, :, 0:MLA_NOPE] = ykv[:, c0:c0 + MLA_NOPE].astype(BF16)
        k_ref[0, h, :, MLA_NOPE:MLA_QK] = kr
        v_ref[0, h, :, 0:MLA_V] = ykv[:, c0 + MLA_NOPE:c0 + MLA_QK].astype(BF16)
        v_ref[0, h, :, MLA_V:2 * MLA_V] = ones_col


def _mla_up(p0, q_norm, kv_norm, wq, wkv, cs_tab):
    per_b = SEQ // TM_UP
    qk_shape = jax.ShapeDtypeStruct((BATCH, MLA_HEADS, SEQ, MLA_QK), BF16)
    return pl.pallas_call(
        _mla_up_kernel,
        grid=(N_TOK // TM_UP,),
        in_specs=[pl.BlockSpec((TM_UP, MLA_Q_RANK), lambda i: (i, 0)),
                  pl.BlockSpec((TM_UP, MLA_KV_RANK), lambda i: (i, 1)),
                  pl.BlockSpec((TM_UP, LANES), lambda i: (i, 24)),
                  pl.BlockSpec((1, MLA_Q_RANK), lambda i: (0, 0)),
                  pl.BlockSpec((1, MLA_KV_RANK), lambda i: (0, 0)),
                  pl.BlockSpec((MLA_Q_RANK, MLA_HEADS * MLA_QK), lambda i: (0, 0)),
                  pl.BlockSpec((MLA_KV_RANK, MLA_HEADS * MLA_QK), lambda i: (0, 0)),
                  pl.BlockSpec((TM_UP, LANES), lambda i: (i % per_b, 0))],
        out_specs=[pl.BlockSpec((1, MLA_HEADS, TM_UP, MLA_QK), lambda i: (i // per_b, 0, i % per_b, 0)),
                   pl.BlockSpec((1, MLA_HEADS, TM_UP, MLA_QK), lambda i: (i // per_b, 0, i % per_b, 0)),
                   pl.BlockSpec((1, MLA_HEADS, TM_UP, 2 * MLA_V), lambda i: (i // per_b, 0, i % per_b, 0))],
        out_shape=[qk_shape, qk_shape, jax.ShapeDtypeStruct((BATCH, MLA_HEADS, SEQ, 2 * MLA_V), BF16)],
        compiler_params=_cparams(("parallel",)),
        name="mla_up",
    )(p0, p0, p0, q_norm.reshape(1, -1), kv_norm.reshape(1, -1), wq, wkv, cs_tab)


def _mla_attn_kernel(q_ref, k_ref, v_ref, o_ref):
    k = k_ref[0, 0]
    v = v_ref[0, 0]
    for r in range(TQ_MLA // TQ_SUB):
        rows = pl.ds(r * TQ_SUB, TQ_SUB)
        s = lax.dot_general(q_ref[0, 0, rows, :], k, (((1,), (1,)), ((), ())), preferred_element_type=F32)
        p = jnp.exp2(s - jnp.max(s, axis=-1, keepdims=True))
        o = jnp.dot(p.astype(BF16), v, preferred_element_type=F32)
        o_ref[rows, :] = (o[:, 0:MLA_V] / o[:, MLA_V:MLA_V + 1]).astype(BF16)


def _mla_attention(q, k, v):
    nq = SEQ // TQ_MLA
    return pl.pallas_call(
        _mla_attn_kernel,
        grid=(BATCH, MLA_HEADS, nq),
        in_specs=[pl.BlockSpec((1, 1, TQ_MLA, MLA_QK), lambda b, h, i: (b, h, i, 0)),
                  pl.BlockSpec((1, 1, SEQ, MLA_QK), lambda b, h, i: (b, h, 0, 0)),
                  pl.BlockSpec((1, 1, SEQ, 2 * MLA_V), lambda b, h, i: (b, h, 0, 0))],
        out_specs=pl.BlockSpec((TQ_MLA, MLA_V), lambda b, h, i: (b * nq + i, h)),
        out_shape=jax.ShapeDtypeStruct((N_TOK, HALF), BF16),
        compiler_params=_cparams(("parallel", "parallel", "parallel")),
        name="mla_attn",
    )(q, k, v)


def _lru_kernel(x_ref, gate_ref, cw_ref, cb_ref, wg_ref, bg_ref, lam_ref, o_ref,
                af_ref, uf_ref, ab_ref, ub_ref, hf_ref, pf_ref, hb_ref, pb_ref, hs_ref):
    x = x_ref[...]
    row = lax.broadcasted_iota(I32, x.shape, 0)

    def shifted(d):
        r = pltpu.roll(x, (-d) % SEQ, 0)
        return jnp.where((row + d >= 0) & (row + d < SEQ), r, 0.0)

    cw = cw_ref[...]
    left = LRU_CONV // 2
    xc = cb_ref[...]
    for kk in range(LRU_CONV):
        d = kk - left
        xc = xc + cw[kk:kk + 1] * (x if d == 0 else shifted(d))

    gates = jnp.dot(xc.astype(BF16), wg_ref[0], preferred_element_type=F32) + bg_ref[0]
    lam = lam_ref[...]
    z = -lam
    sp = jnp.maximum(z, 0.0) + jnp.log1p(jnp.exp(-jnp.abs(z)))
    pad_rows = LRU_ROWS - SEQ
    for d, (a_ref, u_ref) in enumerate(((af_ref, uf_ref), (ab_ref, ub_ref))):
        r = _sigmoid(gates[:, d * 256:d * 256 + LRU_BS])
        i = _sigmoid(gates[:, d * 256 + LRU_BS:(d + 1) * 256])
        a = jnp.exp(r * (-LRU_C * sp[d:d + 1]))
        a_ref[0:SEQ] = a
        u_ref[0:SEQ] = jnp.sqrt(1.0 - a * a) * (i * xc)
        a_ref[SEQ:LRU_ROWS] = jnp.zeros((pad_rows, LANES), F32)
        u_ref[SEQ:LRU_ROWS] = jnp.zeros((pad_rows, LANES), F32)

    ones = jnp.ones((SUBLANES, LANES), F32)
    zeros = jnp.zeros((SUBLANES, LANES), F32)

    def seg(t):
        return pl.ds(t, SUBLANES, stride=LRU_SEG)

    def local_scan(s, carry):
        p_f, h_f, p_b, h_b = carry
        tf = s
        tb = LRU_SEG - 1 - s
        a = af_ref[seg(tf)]
        h_f = a * h_f + uf_ref[seg(tf)]
        p_f = a * p_f
        hf_ref[seg(tf)] = h_f
        pf_ref[seg(tf)] = p_f
        a = ab_ref[seg(tb)]
        h_b = a * h_b + ub_ref[seg(tb)]
        p_b = a * p_b
        hb_ref[seg(tb)] = h_b
        pb_ref[seg(tb)] = p_b
        return p_f, h_f, p_b, h_b

    p_f, h_f, p_b, h_b = lax.fori_loop(0, LRU_SEG, local_scan, (ones, zeros, ones, zeros), unroll=4)

    rows_f = []
    c = jnp.zeros((1, LANES), F32)
    for j in range(SUBLANES):
        rows_f.append(c)
        c = p_f[j:j + 1] * c + h_f[j:j + 1]
    rows_b = [None] * SUBLANES
    c = jnp.zeros((1, LANES), F32)
    for j in range(SUBLANES - 1, -1, -1):
        rows_b[j] = c
        c = p_b[j:j + 1] * c + h_b[j:j + 1]
    sub = lax.broadcasted_iota(I32, (SUBLANES, LANES), 0)
    c_f = zeros
    c_b = zeros
    for j in range(SUBLANES):
        c_f = jnp.where(sub == j, rows_f[j], c_f)
        c_b = jnp.where(sub == j, rows_b[j], c_b)

    def fixup(t, _):
        hs_ref[seg(t)] = (hf_ref[seg(t)] + pf_ref[seg(t)] * c_f) + (hb_ref[seg(t)] + pb_ref[seg(t)] * c_b)
        return 0

    lax.fori_loop(0, LRU_SEG, fixup, 0, unroll=4)

    g = gate_ref[...]
    gelu = 0.5 * g * (1.0 + jnp.tanh(math.sqrt(2.0 / math.pi) * (g + 0.044715 * (g * g * g))))
    o_ref[...] = (gelu * hs_ref[0:SEQ]).astype(BF16)


def _rglru(p0, conv_w, conv_b, w_gates, b_gates, lam):
    scan_buf = pltpu.VMEM((LRU_ROWS, LANES), F32)
    return pl.pallas_call(
        _lru_kernel,
        grid=(BATCH, LRU_BLOCKS),
        in_specs=[pl.BlockSpec((SEQ, LRU_BS), lambda b, g: (b, 8 + g)),
                  pl.BlockSpec((SEQ, LRU_BS), lambda b, g: (b, 16 + g)),
                  pl.BlockSpec((LRU_CONV, LRU_BS), lambda b, g: (0, g)),
                  pl.BlockSpec((1, LRU_BS), lambda b, g: (0, g)),
                  pl.BlockSpec((1, LRU_BS, 4 * LRU_BS), lambda b, g: (g, 0, 0)),
                  pl.BlockSpec((1, 1, 4 * LRU_BS), lambda b, g: (g, 0, 0)),
                  pl.BlockSpec((2, LRU_BS), lambda b, g: (0, g))],
        out_specs=pl.BlockSpec((SEQ, LRU_BS), lambda b, g: (b, g)),
        out_shape=jax.ShapeDtypeStruct((N_TOK, LRU_WIDTH), BF16),
        scratch_shapes=[scan_buf] * 9,
        compiler_params=_cparams(("parallel", "parallel")),
        name="rglru",
    )(p0, p0, conv_w, conv_b.reshape(1, -1), w_gates, b_gates, lam)


def _ret_kernel(lg_ref, q_ref, k_ref, v_ref, g_ref, cq_ref, sq_ref, ck_ref, sk_ref, gn_ref, o_ref, ks_ref, vs_ref):
    h = pl.program_id(1)
    qi = pl.program_id(2)
    half = RET_DK // 2

    def rope(t, c, s):
        t1, t2 = t[:, :half], t[:, half:]
        return jnp.concatenate([t1 * c - t2 * s, t2 * c + t1 * s], axis=1)

    @pl.when(qi == 0)
    def _():
        ks_ref[...] = (rope(k_ref[...], ck_ref[...], sk_ref[...]) * (RET_DK ** -0.5)).astype(BF16)
        vs_ref[...] = v_ref[...].astype(BF16)

    q = rope(q_ref[...], cq_ref[...], sq_ref[...]).astype(BF16)
    s = lax.dot_general(q, ks_ref[...], (((1,), (1,)), ((), ())), preferred_element_type=F32)
    n = (qi * TQ_ATT + lax.broadcasted_iota(I32, (TQ_ATT, 1), 0)).astype(F32)
    m = lax.broadcasted_iota(I32, (1, SEQ), 1).astype(F32)
    c_f = lg_ref[h] * math.log2(math.e)
    c_b = -lg_ref[RET_HEADS - 1 - h] * math.log2(math.e)
    dec = jnp.exp2(jnp.minimum(c_f * n - c_f * m, c_b * n - c_b * m))
    o = jnp.dot((s * dec).astype(BF16), vs_ref[...], preferred_element_type=F32)
    y = _rms(o, gn_ref[0])
    o_ref[...] = (_silu(g_ref[...]) * y).astype(BF16)


def _retention(p1, lg, cos_t, sin_t, ret_gn):
    nq = SEQ // TQ_ATT
    half = RET_DK // 2
    return pl.pallas_call(
        _ret_kernel,
        grid=(BATCH, RET_HEADS, nq),
        in_specs=[pl.BlockSpec(memory_space=pltpu.SMEM),
                  pl.BlockSpec((TQ_ATT, RET_DK), lambda b, h, i: (b * nq + i, h)),
                  pl.BlockSpec((SEQ, RET_DK), lambda b, h, i: (b, RET_HEADS + h)),
                  pl.BlockSpec((SEQ, RET_DV), lambda b, h, i: (b, 2 * RET_HEADS + h)),
                  pl.BlockSpec((TQ_ATT, RET_DV), lambda b, h, i: (b * nq + i, 3 * RET_HEADS + h)),
                  pl.BlockSpec((TQ_ATT, half), lambda b, h, i: (i, 0)),
                  pl.BlockSpec((TQ_ATT, half), lambda b, h, i: (i, 0)),
                  pl.BlockSpec((SEQ, half), lambda b, h, i: (0, 0)),
                  pl.BlockSpec((SEQ, half), lambda b, h, i: (0, 0)),
                  pl.BlockSpec((1, 1, RET_DV), lambda b, h, i: (h, 0, 0))],
        out_specs=pl.BlockSpec((TQ_ATT, RET_DV), lambda b, h, i: (b * nq + i, h)),
        out_shape=jax.ShapeDtypeStruct((N_TOK, HALF), BF16),
        scratch_shapes=[pltpu.VMEM((SEQ, RET_DK), BF16), pltpu.VMEM((SEQ, RET_DV), BF16)],
        compiler_params=_cparams(("parallel", "parallel", "arbitrary")),
        name="retention",
    )(lg, p1, p1, p1, p1, cos_t, sin_t, cos_t, sin_t, ret_gn.reshape(RET_HEADS, 1, RET_DV))


def _swa_kernel(sink_ref, q_ref, k_ref, v_ref, bias_ref, o_ref):
    kv = pl.program_id(1)
    nb = SEQ // WINDOW
    w = WINDOW

    def rows(ref, blk):
        return ref[pl.ds(pl.multiple_of(blk * w, w), w), :].astype(BF16)

    col = lax.broadcasted_iota(I32, (w, 3 * w), 1)
    for j in range(SWA_BLOCKS):
        n = pl.program_id(2) * SWA_BLOCKS + j
        prev = jnp.maximum(n - 1, 0)
        nxt = jnp.minimum(n + 1, nb - 1)
        kw = jnp.concatenate([rows(k_ref, prev), rows(k_ref, n), rows(k_ref, nxt)], axis=0)
        vw = jnp.concatenate([rows(v_ref, prev), rows(v_ref, n), rows(v_ref, nxt)], axis=0)
        qb = q_ref[j * w:(j + 1) * w, :]
        q4 = jnp.concatenate([qb[:, g * SWA_HD:(g + 1) * SWA_HD] for g in range(SWA_G)], axis=0).astype(BF16)
        s = lax.dot_general(q4, kw, (((1,), (1,)), ((), ())), preferred_element_type=F32) * (SWA_HD ** -0.5)
        outside = ((col < w) & (n == 0)) | ((col >= 2 * w) & (n == nb - 1))
        for g in range(SWA_G):
            sg = jnp.where(outside, NEG_BIG, s[g * w:(g + 1) * w] + bias_ref[g])
            sink = sink_ref[kv * SWA_G + g]
            m = jnp.maximum(jnp.max(sg, axis=-1, keepdims=True), sink)
            p = jnp.exp(sg - m)
            denom = jnp.sum(p, axis=-1, keepdims=True) + jnp.exp(sink - m)
            o = jnp.dot((p / denom).astype(BF16), vw, preferred_element_type=F32)
            o_ref[j * w:(j + 1) * w, g * SWA_HD:(g + 1) * SWA_HD] = o.astype(BF16)


def _swa(p1, sinks, bias):
    nb = SEQ // WINDOW
    qcols = SWA_G * SWA_HD
    q_blk0 = (4 * RET_HEADS * RET_DK) // qcols
    k_blk0 = (4 * RET_HEADS * RET_DK + SWA_HEADS * SWA_HD) // SWA_HD
    v_blk0 = k_blk0 + SWA_KV_HEADS
    return pl.pallas_call(
        _swa_kernel,
        grid=(BATCH, SWA_KV_HEADS, nb // SWA_BLOCKS),
        in_specs=[pl.BlockSpec(memory_space=pltpu.SMEM),
                  pl.BlockSpec((SWA_BLOCKS * WINDOW, qcols), lambda b, kv, n: (b * (nb // SWA_BLOCKS) + n, q_blk0 + kv)),
                  pl.BlockSpec((SEQ, SWA_HD), lambda b, kv, n: (b, k_blk0 + kv)),
                  pl.BlockSpec((SEQ, SWA_HD), lambda b, kv, n: (b, v_blk0 + kv)),
                  pl.BlockSpec((SWA_G, WINDOW, 3 * WINDOW), lambda b, kv, n: (kv, 0, 0))],
        out_specs=pl.BlockSpec((SWA_BLOCKS * WINDOW, qcols), lambda b, kv, n: (b * (nb // SWA_BLOCKS) + n, kv)),
        out_shape=jax.ShapeDtypeStruct((N_TOK, HALF), BF16),
        compiler_params=_cparams(("parallel", "parallel", "parallel")),
        name="swa",
    )(sinks, p1, p1, p1, bias)


def _outproj_kernel(a_ref, b_ref, wa_ref, wb_ref, x_ref, gm_ref, g_ref, sc_ref, sh_ref, wr_ref,
                    x1_ref, hp_ref, lg_ref):
    wr = wr_ref[...]
    for r in range(TM_OUT // TM_OUT_SUB):
        rows = pl.ds(r * TM_OUT_SUB, TM_OUT_SUB)
        mixed = (jnp.dot(a_ref[rows, :], wa_ref[...], preferred_element_type=F32)
                 + jnp.dot(b_ref[rows, :], wb_ref[...], preferred_element_type=F32))
        x1 = x_ref[rows, :] + gm_ref[0] * mixed
        x1_ref[rows, :] = x1
        hf = _rms(x1, g_ref[...]) * (1.0 + sc_ref[0]) + sh_ref[0]
        _store_token_tiles(hp_ref, _pack_bf16_pair(hf[:, :PACK_W], hf[:, PACK_W:]), r * TM_OUT_SUB)
        h_hi = hf.astype(BF16)
        h_lo = (hf - h_hi.astype(F32)).astype(BF16)
        t_hi = jnp.dot(h_hi, wr, preferred_element_type=F32)
        t_lo = jnp.dot(h_lo, wr, preferred_element_type=F32)
        lg_ref[rows, :] = (t_hi[:, :LANES] + t_hi[:, LANES:]) + (t_lo[:, :LANES] + t_lo[:, LANES:])


def _out_projection(a, b, w_out_bf16, x, g_m, gain, scale, shift, w_router_pad):
    per_b = SEQ // TM_OUT
    vec = pl.BlockSpec((1, 1, D_MODEL), lambda i: (i // per_b, 0, 0))
    return pl.pallas_call(
        _outproj_kernel,
        grid=(N_TOK // TM_OUT,),
        in_specs=[pl.BlockSpec((TM_OUT, HALF), lambda i: (i, 0)),
                  pl.BlockSpec((TM_OUT, HALF), lambda i: (i, 0)),
                  pl.BlockSpec((HALF, D_MODEL), lambda i: (0, 0)),
                  pl.BlockSpec((HALF, D_MODEL), lambda i: (1, 0)),
                  pl.BlockSpec((TM_OUT, D_MODEL), lambda i: (i, 0)),
                  vec,
                  pl.BlockSpec((1, D_MODEL), lambda i: (0, 0)),
                  vec, vec,
                  pl.BlockSpec((D_MODEL, 2 * LANES), lambda i: (0, 0))],
        out_specs=[pl.BlockSpec((TM_OUT, D_MODEL), lambda i: (i, 0)),
                   pl.BlockSpec((TM_OUT * TOK_ROWS, LANES), lambda i: (i, 0)),
                   pl.BlockSpec((TM_OUT, LANES), lambda i: (i, 0))],
        out_shape=[jax.ShapeDtypeStruct((N_TOK, D_MODEL), F32),
                   jax.ShapeDtypeStruct((N_TOK * TOK_ROWS, LANES), U32),
                   jax.ShapeDtypeStruct((N_TOK, LANES), F32)],
        compiler_params=_cparams(("parallel",)),
        name="out_proj",
    )(a, b, w_out_bf16, w_out_bf16, x, g_m, gain.reshape(1, D_MODEL), scale, shift, w_router_pad)


def _route_kernel(lg_ref, bias_ref, lslot_ref, w_ref, tab_ref, cnt_ref):
    step = pl.program_id(0)

    @pl.when(step == 0)
    def _():
        cnt_ref[...] = jnp.zeros(cnt_ref.shape, F32)

    for j in range(ROUTE_TILES):
        _route_tile(j, lg_ref, bias_ref, lslot_ref, w_ref, tab_ref, cnt_ref)


def _route_tile(j, lg_ref, bias_ref, lslot_ref, w_ref, tab_ref, cnt_ref):
    t = T_ROUTE
    cols = slice(j * t, (j + 1) * t)
    scores = jax.nn.sigmoid(lg_ref[cols, :].T[:N_EXPERTS])
    biased = scores + bias_ref[...]
    sub = lax.broadcasted_iota(I32, (GROUP_SIZE, t), 0).astype(F32)
    ninf = -jnp.inf

    def first_argmax(v, idx, n):
        m = jnp.max(v, axis=0, keepdims=True)
        return m, jnp.min(jnp.where(v == m, idx, float(n)), axis=0, keepdims=True)

    gs = []
    for g in range(N_GROUPS):
        bg = biased[g * GROUP_SIZE:(g + 1) * GROUP_SIZE]
        m1, i1 = first_argmax(bg, sub, GROUP_SIZE)
        m2 = jnp.max(jnp.where(sub == i1, ninf, bg), axis=0, keepdims=True)
        gs.append(m1 + m2)
    cur = jnp.concatenate(gs, axis=0)

    gmask = jnp.zeros((N_GROUPS, t), F32)
    for _ in range(TOP_GROUPS):
        _, i = first_argmax(cur, sub, N_GROUPS)
        pick = sub == i
        gmask = jnp.where(pick, 1.0, gmask)
        cur = jnp.where(pick, ninf, cur)

    eid = lax.broadcasted_iota(I32, (N_EXPERTS, t), 0).astype(F32)
    emask = jnp.concatenate([jnp.broadcast_to(gmask[g:g + 1], (GROUP_SIZE, t)) for g in range(N_GROUPS)], axis=0)
    cur = jnp.where(emask > 0.5, biased, ninf)
    sels, ws = [], []
    onehot = jnp.zeros((N_EXPERTS, t), F32)
    for _ in range(TOP_K):
        _, i = first_argmax(cur, eid, N_EXPERTS)
        pick = eid == i
        sels.append(pick)
        ws.append(jnp.sum(jnp.where(pick, scores, 0.0), axis=0, keepdims=True))
        onehot = jnp.where(pick, 1.0, onehot)
        cur = jnp.where(pick, ninf, cur)
    wsum = ws[0]
    for k in range(1, TOP_K):
        wsum = wsum + ws[k]

    r = lax.broadcasted_iota(I32, (t, t), 0)
    c = lax.broadcasted_iota(I32, (t, t), 1)
    tri = (r < c).astype(BF16)
    earlier = jnp.dot(onehot.astype(BF16), tri, preferred_element_type=F32)
    tile_cnt = jnp.broadcast_to(jnp.sum(onehot, axis=1, keepdims=True), (N_EXPERTS, LANES))
    er = lax.broadcasted_iota(I32, (N_EXPERTS, N_EXPERTS), 0)
    ec = lax.broadcasted_iota(I32, (N_EXPERTS, N_EXPERTS), 1)
    run_start = jnp.dot((ec < er).astype(BF16), tile_cnt.astype(BF16), preferred_element_type=F32)
    pos = earlier + run_start[:, 0:1]
    lslots = [jnp.sum(jnp.where(sels[k], pos, 0.0), axis=0, keepdims=True) for k in range(TOP_K)]

    lslot_ref[:, cols] = jnp.concatenate(lslots, axis=0).astype(I32)
    w_ref[:, cols] = jnp.concatenate([w / wsum * ROUTE_SCALE for w in ws], axis=0)
    tab_ref[j, 0] = tile_cnt
    tab_ref[j, 1] = cnt_ref[...]
    tab_ref[j, 2] = run_start
    cnt_ref[...] = cnt_ref[...] + tile_cnt


def _route(logits, router_bias):
    ntiles = N_TOK // T_ROUTE
    return pl.pallas_call(
        _route_kernel,
        grid=(ntiles // ROUTE_TILES,),
        in_specs=[pl.BlockSpec((ROUTE_TILES * T_ROUTE, LANES), lambda i: (i, 0)),
                  pl.BlockSpec((N_EXPERTS, 1), lambda i: (0, 0))],
        out_specs=[pl.BlockSpec((TOP_K, ROUTE_TILES * T_ROUTE), lambda i: (0, i)),
                   pl.BlockSpec((TOP_K, ROUTE_TILES * T_ROUTE), lambda i: (0, i)),
                   pl.BlockSpec((ROUTE_TILES, 3, N_EXPERTS, LANES), lambda i: (i, 0, 0, 0)),
                   pl.BlockSpec((N_EXPERTS, LANES), lambda i: (0, 0))],
        out_shape=[jax.ShapeDtypeStruct((TOP_K, N_TOK), I32),
                   jax.ShapeDtypeStruct((TOP_K, N_TOK), F32),
                   jax.ShapeDtypeStruct((ntiles, 3, N_EXPERTS, LANES), F32),
                   jax.ShapeDtypeStruct((N_EXPERTS, LANES), F32)],
        compiler_params=_cparams(("arbitrary",)),
        name="route",
    )(logits, router_bias.reshape(N_EXPERTS, 1))


def _dispatch_kernel(lslot_ref, rcnt_ref, rloc_ref, rglb_ref, zstart_ref, zlen_ref, hp_ref, xs_ref,
                     zero_ref, loc_ref, sem, zsem):
    step = pl.program_id(0)
    nsteps = pl.num_programs(0)

    def zero_copy(z):
        start = pl.multiple_of(zstart_ref[z] * TOK_ROWS, TOK_ROWS)
        n = zlen_ref[z] * TOK_ROWS
        return pltpu.make_async_copy(zero_ref.at[pl.ds(0, n)], xs_ref.at[pl.ds(start, n)], zsem)

    @pl.when(step == 0)
    def _():
        zero_ref[...] = jnp.zeros(zero_ref.shape, U32)

        def fill(z, _):
            @pl.when(zlen_ref[z] > 0)
            def _():
                zero_copy(z).start()
            return 0

        lax.fori_loop(0, N_ZERO_RANGES, fill, 0)

    @pl.when(step == nsteps - 1)
    def _():
        def fill_wait(z, _):
            @pl.when(zlen_ref[z] > 0)
            def _():
                zero_copy(z).wait()
            return 0

        lax.fori_loop(0, N_ZERO_RANGES, fill_wait, 0)

    cur = step % 2

    def tile_wait(slot):
        half = pl.ds(pl.multiple_of(slot * LOCAL_ROWS, LOCAL_ROWS), LOCAL_ROWS)
        pltpu.make_async_copy(loc_ref.at[half], xs_ref.at[pl.ds(0, LOCAL_ROWS)], sem.at[slot]).wait()

    @pl.when(step >= 2)
    def _():
        tile_wait(cur)

    base = step * T_DISP

    def place(t, _):
        row = hp_ref[pl.ds(pl.multiple_of(t * TOK_ROWS, TOK_ROWS), TOK_ROWS), :]
        for k in range(TOP_K):
            dst = pl.multiple_of(lslot_ref[(base + t) * TOP_K + k], TOK_ROWS)
            loc_ref[pl.ds(dst, TOK_ROWS), :] = row
        return 0

    lax.fori_loop(0, T_DISP, place, 0, unroll=4)

    def run(pair, _):
        for prio in range(2):
            e = 2 * pair + prio
            n = rcnt_ref[step * N_EXPERTS + e]

            @pl.when(n > 0)
            def _():
                src = pl.multiple_of(rloc_ref[step * N_EXPERTS + e], TOK_ROWS)
                dst = pl.multiple_of(rglb_ref[step * N_EXPERTS + e] * TOK_ROWS, TOK_ROWS)
                pltpu.make_async_copy(loc_ref.at[pl.ds(src, n * TOK_ROWS)],
                                      xs_ref.at[pl.ds(dst, n * TOK_ROWS)], sem.at[cur]).start(priority=prio)
        return 0

    lax.fori_loop(0, N_EXPERTS // 2, run, 0)

    @pl.when(step == nsteps - 1)
    def _():
        tile_wait(cur)

        @pl.when(nsteps > 1)
        def _():
            tile_wait(1 - cur)


def _dispatch(lslot_flat, run_cnt, run_loc, run_glb, zero_start, zero_len, hp):
    return pl.pallas_call(
        _dispatch_kernel,
        grid_spec=pltpu.PrefetchScalarGridSpec(
            num_scalar_prefetch=6,
            grid=(N_TILES,),
            in_specs=[pl.BlockSpec((T_DISP * TOK_ROWS, LANES), lambda i, *_: (i, 0))],
            out_specs=pl.BlockSpec(memory_space=pl.ANY),
            scratch_shapes=[pltpu.VMEM((TM_EXP * TOK_ROWS, LANES), U32),
                            pltpu.VMEM((2 * LOCAL_ROWS, LANES), U32),
                            pltpu.SemaphoreType.DMA((2,)),
                            pltpu.SemaphoreType.DMA]),
        out_shape=jax.ShapeDtypeStruct((P_ROWS * TOK_ROWS, LANES), U32),
        compiler_params=_cparams(("arbitrary",)),
        name="dispatch",
    )(lslot_flat, run_cnt, run_loc, run_glb, zero_start, zero_len, hp)


def _expert_kernel(be_ref, bv_ref, nx_ref, par_ref, xs_ref, wg_hbm, wu_hbm, wd_hbm, ys_ref,
                   wgs_ref, wus_ref, wds_ref, wgb_ref, wub_ref, wdb_ref, sem, *, layer):
    i = pl.program_id(0)
    e = be_ref[i]
    par = par_ref[i]
    changed = jnp.logical_or(i == 0, e != be_ref[jnp.maximum(i - 1, 0)])

    def stage(expert, slot):
        return (pltpu.make_async_copy(wg_hbm.at[layer, expert], wgs_ref, sem.at[0]),
                pltpu.make_async_copy(wu_hbm.at[layer, expert], wus_ref.at[slot], sem.at[1]),
                pltpu.make_async_copy(wd_hbm.at[layer, expert], wds_ref.at[slot], sem.at[2]))

    @pl.when(changed)
    def _():
        @pl.when(i == 0)
        def _():
            for c in stage(e, par):
                c.start()

        for c in stage(e, par):
            c.wait()
        rows = CAST_VREGS * SUBLANES * LANES // EXPERT_FF

        def cast_piece(c, _):
            sl = pl.ds(pl.multiple_of(c * rows, rows), rows)
            wgb_ref[sl, :] = wgs_ref[sl, :].astype(BF16)
            return 0

        lax.fori_loop(0, D_MODEL // rows, cast_piece, 0, unroll=2)

        @pl.when(nx_ref[i] >= 0)
        def _():
            for c in stage(nx_ref[i], 1 - par):
                c.start()

    def cast_up_down():
        for src, dst in ((wus_ref, wub_ref), (wds_ref, wdb_ref)):
            rows = src.shape[1] // CAST_PIECES
            for c in range(CAST_PIECES):
                dst[c * rows:(c + 1) * rows, :] = src[par, c * rows:(c + 1) * rows, :].astype(BF16)

    def sub_block(row0, nrows):
        xs_sub = xs_ref.at[pl.ds(row0 * TOK_ROWS, nrows * TOK_ROWS)]
        lo, hi = _unpack_bf16_pair(_load_token_tiles(xs_sub, nrows))
        lo = lo.astype(BF16)
        hi = hi.astype(BF16)
        hg = (jnp.dot(lo, wgb_ref[0:PACK_W], preferred_element_type=F32)
              + jnp.dot(hi, wgb_ref[PACK_W:D_MODEL], preferred_element_type=F32))
        hu = (jnp.dot(lo, wub_ref[0:PACK_W], preferred_element_type=F32)
              + jnp.dot(hi, wub_ref[PACK_W:D_MODEL], preferred_element_type=F32))
        act = (_silu(hg) * hu).astype(BF16)
        y = jnp.dot(act, wdb_ref[...], preferred_element_type=F32)
        _store_token_tiles(ys_ref, _pack_bf16_pair(y[:, :PACK_W], y[:, PACK_W:]), row0)

    valid_rows = bv_ref[i]
    full = valid_rows > TM_EXP - TM_EXP_TAIL
    unchanged = jnp.logical_not(changed)

    @pl.when(jnp.logical_and(full, changed))
    def _():
        cast_up_down()
        sub_block(0, TM_EXP_SUB)
        sub_block(TM_EXP_SUB, TM_EXP_SUB)

    @pl.when(jnp.logical_and(full, unchanged))
    def _():
        sub_block(0, TM_EXP_SUB)
        sub_block(TM_EXP_SUB, TM_EXP_SUB)

    @pl.when(jnp.logical_not(full))
    def _():
        @pl.when(changed)
        def _():
            cast_up_down()
            sub_block(0, TM_EXP_TAIL)

        for r in range(TM_EXP // TM_EXP_TAIL):
            if r == 0:
                compute = jnp.logical_and(unchanged, valid_rows > 0)
                skip = jnp.logical_and(unchanged, valid_rows <= 0)
            else:
                compute = valid_rows > r * TM_EXP_TAIL
                skip = valid_rows <= r * TM_EXP_TAIL

            @pl.when(compute)
            def _():
                sub_block(r * TM_EXP_TAIL, TM_EXP_TAIL)

            @pl.when(skip)
            def _():
                ys_ref[pl.ds(r * TM_EXP_TAIL * TOK_ROWS, TM_EXP_TAIL * TOK_ROWS), :] = jnp.zeros(
                    (TM_EXP_TAIL * TOK_ROWS, LANES), U32)


def _experts(layer, block_expert, block_rows, next_expert, block_parity, xs, w_gate, w_up, w_down):
    return pl.pallas_call(
        functools.partial(_expert_kernel, layer=layer),
        grid_spec=pltpu.PrefetchScalarGridSpec(
            num_scalar_prefetch=4,
            grid=(NB_EXP,),
            in_specs=[pl.BlockSpec((TM_EXP * TOK_ROWS, LANES), lambda i, be, bv, *_: (jnp.where(bv[i] > 0, i, 0), 0)),
                      pl.BlockSpec(memory_space=pl.ANY),
                      pl.BlockSpec(memory_space=pl.ANY),
                      pl.BlockSpec(memory_space=pl.ANY)],
            out_specs=pl.BlockSpec((TM_EXP * TOK_ROWS, LANES), lambda i, *_: (i, 0)),
            scratch_shapes=[pltpu.VMEM((D_MODEL, EXPERT_FF), F32),
                            pltpu.VMEM((2, D_MODEL, EXPERT_FF), F32),
                            pltpu.VMEM((2, EXPERT_FF, D_MODEL), F32),
                            pltpu.VMEM((D_MODEL, EXPERT_FF), BF16),
                            pltpu.VMEM((D_MODEL, EXPERT_FF), BF16),
                            pltpu.VMEM((EXPERT_FF, D_MODEL), BF16),
                            pltpu.SemaphoreType.DMA((3,))]),
        out_shape=jax.ShapeDtypeStruct((P_ROWS * TOK_ROWS, LANES), U32),
        compiler_params=_cparams(("arbitrary",)),
        name="experts",
    )(block_expert, block_rows, next_expert, block_parity, xs, w_gate, w_up, w_down)


def _combine_kernel(lslot_ref, rcnt_ref, rloc_ref, rglb_ref, w_ref, ys_ref, hp_ref, x1_ref, gf_ref,
                    wsg_ref, wsu_ref, wsd_ref, fn_ref, o_ref, buf_ref, mlo_ref, mhi_ref, sem, *, final_norm):
    i = pl.program_id(0)
    nsteps = pl.num_programs(0)

    def issue(step, slot):
        def run(pair, _):
            for prio in range(2):
                e = 2 * pair + prio
                n = rcnt_ref[step * N_EXPERTS + e]

                @pl.when(n > 0)
                def _():
                    src = pl.multiple_of(rglb_ref[step * N_EXPERTS + e] * TOK_ROWS, TOK_ROWS)
                    dst = pl.multiple_of(rloc_ref[step * N_EXPERTS + e], TOK_ROWS)
                    pltpu.make_async_copy(ys_ref.at[pl.ds(src, n * TOK_ROWS)],
                                          buf_ref.at[pl.ds(dst, n * TOK_ROWS)], sem.at[slot]).start(priority=prio)
            return 0

        lax.fori_loop(0, N_EXPERTS // 2, run, 0)

    @pl.when(i == 0)
    def _():
        issue(0, 0)

    @pl.when(i + 1 < nsteps)
    def _():
        issue(i + 1, (i + 1) % 2)

    cur = i % 2
    cur_half = pl.ds(pl.multiple_of(cur * LOCAL_ROWS, LOCAL_ROWS), LOCAL_ROWS)
    pltpu.make_async_copy(ys_ref.at[pl.ds(0, LOCAL_ROWS)], buf_ref.at[cur_half], sem.at[cur]).wait()

    base = i * T_DISP

    def token(t, _):
        acc_lo = jnp.zeros((TOK_ROWS, LANES), F32)
        acc_hi = jnp.zeros((TOK_ROWS, LANES), F32)
        for k in range(TOP_K):
            idx = (base + t) * TOP_K + k
            src = pl.multiple_of(lslot_ref[idx], TOK_ROWS)
            lo, hi = _unpack_bf16_pair(buf_ref[pl.ds(src, TOK_ROWS), :])
            wk = w_ref[idx]
            acc_lo = acc_lo + wk * lo
            acc_hi = acc_hi + wk * hi
        dst = pl.ds(pl.multiple_of(t * TOK_ROWS, TOK_ROWS), TOK_ROWS)
        mlo_ref[dst, :] = acc_lo
        mhi_ref[dst, :] = acc_hi
        return 0

    lax.fori_loop(0, T_DISP, token, 0, unroll=2)
    moe_lo = _load_token_tiles(mlo_ref, T_DISP)
    moe_hi = _load_token_tiles(mhi_ref, T_DISP)

    hlo, hhi = _unpack_bf16_pair(_load_token_tiles(hp_ref, T_DISP))
    hlo = hlo.astype(BF16)
    hhi = hhi.astype(BF16)
    sg = (jnp.dot(hlo, wsg_ref[0:PACK_W], preferred_element_type=F32)
          + jnp.dot(hhi, wsg_ref[PACK_W:D_MODEL], preferred_element_type=F32))
    su = (jnp.dot(hlo, wsu_ref[0:PACK_W], preferred_element_type=F32)
          + jnp.dot(hhi, wsu_ref[PACK_W:D_MODEL], preferred_element_type=F32))
    shared = jnp.dot((_silu(sg) * su).astype(BF16), wsd_ref[...], preferred_element_type=F32)
    moe = jnp.concatenate([moe_lo, moe_hi], axis=1)
    out = x1_ref[...] + gf_ref[0] * (moe + shared)
    if final_norm:
        out = _rms(out, fn_ref[...])
    o_ref[...] = out


def _combine(lslot_flat, run_cnt, run_loc, run_glb, w_flat, ys, hp, x1, g_f, wsg, wsu, wsd, final_gain, final_norm):
    per_b = SEQ // T_DISP
    return pl.pallas_call(
        functools.partial(_combine_kernel, final_norm=final_norm),
        grid_spec=pltpu.PrefetchScalarGridSpec(
            num_scalar_prefetch=4,
            grid=(N_TILES,),
            in_specs=[pl.BlockSpec(memory_space=pltpu.SMEM),
                      pl.BlockSpec(memory_space=pl.ANY),
                      pl.BlockSpec((T_DISP * TOK_ROWS, LANES), lambda i, *_: (i, 0)),
                      pl.BlockSpec((T_DISP, D_MODEL), lambda i, *_: (i, 0)),
                      pl.BlockSpec((1, 1, D_MODEL), lambda i, *_: (i // per_b, 0, 0)),
                      pl.BlockSpec((D_MODEL, SHARED_FF), lambda i, *_: (0, 0)),
                      pl.BlockSpec((D_MODEL, SHARED_FF), lambda i, *_: (0, 0)),
                      pl.BlockSpec((SHARED_FF, D_MODEL), lambda i, *_: (0, 0)),
                      pl.BlockSpec((1, D_MODEL), lambda i, *_: (0, 0))],
            out_specs=pl.BlockSpec((T_DISP, D_MODEL), lambda i, *_: (i, 0)),
            scratch_shapes=[pltpu.VMEM((2 * LOCAL_ROWS, LANES), U32),
                            pltpu.VMEM((T_DISP * TOK_ROWS, LANES), F32),
                            pltpu.VMEM((T_DISP * TOK_ROWS, LANES), F32),
                            pltpu.SemaphoreType.DMA((2,))]),
        out_shape=jax.ShapeDtypeStruct((N_TOK, D_MODEL), F32),
        compiler_params=_cparams(("arbitrary",)),
        name="combine",
    )(lslot_flat, run_cnt, run_loc, run_glb, w_flat, ys, hp, x1, g_f, wsg, wsu, wsd, final_gain.reshape(1, D_MODEL))


def _moe_layer(layer, hp, logits, x1, g_f, router_bias, w_gate, w_up, w_down, ws_gate, ws_up, ws_down,
               final_gain, final_norm):
    lslot, w_k, tables, counts = _route(logits, router_bias)
    cnt = counts[:, 0].astype(I32)
    padded = ((cnt + TM_EXP - 1) // TM_EXP) * TM_EXP
    ends = jnp.cumsum(padded)
    offsets = ends - padded
    tables = tables[:, :, :, 0].astype(I32)
    run_cnt = tables[:, 0].reshape(-1)
    run_glb = (offsets[None, :] + tables[:, 1]).reshape(-1)
    half_row0 = (jnp.arange(N_TILES, dtype=I32) % 2) * LOCAL_ROWS
    run_loc = (tables[:, 2] * TOK_ROWS + half_row0[:, None]).reshape(-1)
    lslot_flat = (lslot * TOK_ROWS + jnp.repeat(half_row0, T_DISP)[None, :]).T.reshape(N_SLOTS)
    blk_start = jnp.arange(NB_EXP, dtype=I32) * TM_EXP
    expert_ids = jnp.arange(N_EXPERTS, dtype=I32)
    nonempty = cnt > 0
    last_nonempty = jnp.max(jnp.where(nonempty, expert_ids, 0))
    block_expert = jnp.minimum(jnp.sum((blk_start[:, None] >= ends[None, :]).astype(I32), axis=1), last_nonempty)
    block_valid = (blk_start < ends[-1]).astype(I32)
    block_rows = jnp.clip((offsets + cnt)[block_expert] - blk_start, 0, TM_EXP) * block_valid
    following = jnp.where(nonempty, expert_ids, N_EXPERTS)
    following = lax.cummin(following, reverse=True)
    following = jnp.concatenate([following[1:], jnp.full((1,), N_EXPERTS, I32)])
    next_expert = jnp.where(following < N_EXPERTS, following, -1)[block_expert]
    zero_start = jnp.concatenate([offsets + cnt, blk_start])
    zero_len = jnp.concatenate([padded - cnt, (1 - block_valid) * TM_EXP])
    xs = _dispatch(lslot_flat, run_cnt, run_loc, run_glb, zero_start, zero_len, hp)
    block_parity = ((jnp.cumsum(nonempty.astype(I32)) - 1) % 2)[block_expert]
    ys = _experts(layer, block_expert, block_rows, next_expert, block_parity, xs, w_gate, w_up, w_down)
    return _combine(lslot_flat, run_cnt, run_loc, run_glb, w_k.T.reshape(N_SLOTS), ys, hp, x1, g_f,
                    ws_gate.astype(BF16), ws_up.astype(BF16), ws_down.astype(BF16), final_gain, final_norm)


def _rope_tables(dim):
    inv = ROPE_THETA ** (-jnp.arange(0, dim, 2, dtype=F32) / dim)
    ang = jnp.arange(SEQ, dtype=F32)[:, None] * inv[None, :]
    return jnp.cos(ang), jnp.sin(ang)


def _rot_half_cols(w):
    half = w.shape[-1] // 2
    return jnp.concatenate([-w[..., half:], w[..., :half]], axis=-1)


def _t5_bucket(rel):
    half = REL_BUCKETS // 2
    max_exact = half // 2
    ret = (rel > 0).astype(I32) * half
    n = jnp.abs(rel)
    nf = jnp.maximum(n, 1).astype(F32)
    large = max_exact + (jnp.log(nf / max_exact) / math.log(REL_MAX_DIST / max_exact)
                         * (half - max_exact)).astype(I32)
    large = jnp.minimum(large, half - 1)
    return ret + jnp.where(n < max_exact, n, large)


def _swa_bias_table(rel_bias):
    qi = jnp.arange(WINDOW)[:, None]
    kj = jnp.arange(3 * WINDOW)[None, :]
    rel = kj - WINDOW - qi
    onehot = (_t5_bucket(rel)[:, :, None] == jnp.arange(REL_BUCKETS)).astype(F32)
    bias = jnp.einsum('qjb,bh->hqj', onehot, rel_bias.astype(F32), precision=lax.Precision.HIGHEST)
    return jnp.where((jnp.abs(rel) <= WINDOW)[None], bias, NEG_BIG)


def kernel(x, c, w_mod, b_mod, norm_mix, norm_ffn, final_norm, w_in_ab, q_lat_norm, kv_lat_norm, w_uq, w_ukv, conv_w, conv_b, lru_w_a, lru_b_a, lru_w_x, lru_b_x, lru_lambda, w_out_ab, w_in_cd, ret_gn, swa_sinks, w_out_cd, rel_bias, w_router, router_bias, w_gate, w_up, w_down, ws_gate, ws_up, ws_down):
    xf = x.reshape(N_TOK, D_MODEL)
    mod = _modulation(c, w_mod, b_mod)
    cos_r, sin_r = _rope_tables(MLA_ROPE)
    cs_tab = jnp.concatenate([cos_r, cos_r, sin_r, sin_r], axis=1)
    cos_t, sin_t = _rope_tables(RET_DK)
    lg_ret = jnp.log1p(-(2.0 ** (-5.0 - jnp.arange(RET_HEADS, dtype=F32))))

    for layer in range(DEPTH):
        sh_m, sc_m, g_m, sh_f, sc_f, g_f = [m.reshape(BATCH, 1, D_MODEL) for m in jnp.split(mod[layer], 6, axis=-1)]
        i = layer // 2
        if layer % 2 == 0:
            w = w_in_ab[i]
            o1, o2, o3, o4 = np.cumsum((MLA_Q_RANK, MLA_KV_RANK, MLA_ROPE, LRU_WIDTH)).tolist()
            w_kr = w[:, o2:o3]
            w_in = jnp.concatenate([w[:, :o2], w[:, o3:], w_kr, _rot_half_cols(w_kr)], axis=1).astype(BF16)
            p0 = _in_projection(xf, norm_mix[layer], sc_m, sh_m, w_in, TM_PROJ_AB, "in_proj_ab")
            wq = w_uq[i].reshape(MLA_Q_RANK, MLA_HEADS, MLA_NOPE + MLA_ROPE)
            wq_r = wq[:, :, MLA_NOPE:]
            wq = jnp.concatenate([wq, _rot_half_cols(wq_r)], axis=-1).reshape(MLA_Q_RANK, MLA_HEADS * MLA_QK)
            q, k, v = _mla_up(p0, q_lat_norm[i], kv_lat_norm[i], wq.astype(BF16), w_ukv[i].astype(BF16), cs_tab)
            a_out = _mla_attention(q, k, v)
            w_gates = jnp.concatenate([lru_w_a[i, 0], lru_w_x[i, 0], lru_w_a[i, 1], lru_w_x[i, 1]], axis=-1).astype(BF16)
            b_gates = jnp.concatenate([b.reshape(LRU_BLOCKS, 1, LRU_BS) for b in
                                       (lru_b_a[i, 0], lru_b_x[i, 0], lru_b_a[i, 1], lru_b_x[i, 1])], axis=-1)
            b_out = _rglru(p0, conv_w[i], conv_b[i], w_gates, b_gates, lru_lambda[i])
            w_out = w_out_ab[i].astype(BF16)
        else:
            p1 = _in_projection(xf, norm_mix[layer], sc_m, sh_m, w_in_cd[i].astype(BF16), TM_PROJ_CD, "in_proj_cd")
            a_out = _retention(p1, lg_ret, cos_t, sin_t, ret_gn[i])
            b_out = _swa(p1, swa_sinks[i], _swa_bias_table(rel_bias))
            w_out = w_out_cd[i].astype(BF16)
        w_r = jnp.pad(w_router[layer], ((0, 0), (0, LANES - N_EXPERTS)))
        w_r_hi = w_r.astype(BF16)
        w_router_pad = jnp.concatenate([w_r_hi, (w_r - w_r_hi.astype(F32)).astype(BF16)], axis=1)
        x1, hp, logits = _out_projection(a_out, b_out, w_out, xf, g_m, norm_ffn[layer], sc_f, sh_f, w_router_pad)
        xf = _moe_layer(layer, hp, logits, x1, g_f, router_bias[layer], w_gate, w_up, w_down,
                        ws_gate[layer], ws_up[layer], ws_down[layer], final_norm, layer == DEPTH - 1)
    return xf.reshape(BATCH, SEQ, D_MODEL)
```
